```python
import math
import jax
import jax.numpy as jnp
from jax import lax
import numpy as np

D_MODEL = 1024
BATCH = 4
SEQ = 4096
DEPTH = 1

CHUNK = 64
Q_BLOCK = 128
ROPE_THETA = 500000.0
EPS = 1e-6

MLA_HEADS = 8
MLA_NOPE = 64
MLA_ROPE = 32
MLA_V = 64
MLA_QK = MLA_NOPE + MLA_ROPE
MLA_Q_RANK = 256
MLA_KV_RANK = 128
MLA_V_WIDTH = MLA_HEADS * MLA_V

DIFF_HEADS = 4
DIFF_HEAD_DIM = 64
DIFF_V_DIM = 2 * DIFF_HEAD_DIM
DIFF_ROPE = DIFF_HEAD_DIM // 4
DIFF_QK_WIDTH = DIFF_HEADS * 2 * DIFF_HEAD_DIM
DIFF_V_WIDTH = DIFF_HEADS * DIFF_V_DIM

N_GROUPS = 4
EXPERTS_PER_GROUP = 8
N_EXPERTS = N_GROUPS * EXPERTS_PER_GROUP
TOP_K = 2
EXPERT_FF = 256

IN_SIZES = (MLA_Q_RANK, MLA_KV_RANK, MLA_ROPE, DIFF_QK_WIDTH, DIFF_QK_WIDTH, DIFF_V_WIDTH, D_MODEL, D_MODEL)
IN_WIDTH = sum(IN_SIZES)
IN_SPLITS = tuple(int(s) for s in np.cumsum(IN_SIZES)[:-1])

kernel_name = 'hybrid_mla_diffattn_hiermoe_block'


def rms_norm(x, gain):
    x32 = x.astype(jnp.float32)
    y = x32 * lax.rsqrt(jnp.mean(x32 * x32, axis=-1, keepdims=True) + EPS)
    return (y * gain.astype(jnp.float32)).astype(x.dtype)


def rope_tables(seq_len, rot_dim):
    pos = jnp.arange(seq_len, dtype=jnp.float32)
    inv = 1.0 / (ROPE_THETA ** (jnp.arange(0, rot_dim, 2, dtype=jnp.float32) / rot_dim))
    ang = pos[:, None] * inv[None, :]
    return jnp.cos(ang), jnp.sin(ang)


def apply_rope(x, cos, sin):
    shape = (1, cos.shape[0]) + (1,) * (x.ndim - 3) + (cos.shape[1],)
    c = cos.reshape(shape)
    s = sin.reshape(shape)
    x32 = x.astype(jnp.float32)
    x1, x2 = jnp.split(x32, 2, axis=-1)
    return jnp.concatenate([x1 * c - x2 * s, x2 * c + x1 * s], axis=-1).astype(x.dtype)


def chunk_causal_probs(q_blk, k_ctx, q_start, scale):
    n_q = q_blk.shape[1]
    n_k = k_ctx.shape[1]
    s = jnp.einsum('bqhd,bkhd->bhqk', q_blk.astype(jnp.float32), k_ctx.astype(jnp.float32)) * scale
    q_chunk = (q_start + jnp.arange(n_q)) // CHUNK
    k_chunk = jnp.arange(n_k) // CHUNK
    s = jnp.where(k_chunk[None, :] <= q_chunk[:, None], s, -jnp.inf)
    return jax.nn.softmax(s, axis=-1)


def mla_branch(q_lat, kv_lat, k_rope_in, q_lat_norm, w_uq, kv_lat_norm, w_ukv, q_gain, k_gain, cos, sin):
    b, s, _ = q_lat.shape
    q = (rms_norm(q_lat, q_lat_norm) @ w_uq).reshape(b, s, MLA_HEADS, MLA_QK)
    kv = (rms_norm(kv_lat, kv_lat_norm) @ w_ukv).reshape(b, s, MLA_HEADS, MLA_NOPE + MLA_V)
    k_nope, v = kv[..., :MLA_NOPE], kv[..., MLA_NOPE:]
    k_rope = jnp.broadcast_to(k_rope_in[:, :, None, :], (b, s, MLA_HEADS, MLA_ROPE))
    k = jnp.concatenate([k_nope, k_rope], axis=-1)
    q = rms_norm(q, q_gain)
    k = rms_norm(k, k_gain)
    q = jnp.concatenate([q[..., :MLA_NOPE], apply_rope(q[..., MLA_NOPE:], cos, sin)], axis=-1)
    k = jnp.concatenate([k[..., :MLA_NOPE], apply_rope(k[..., MLA_NOPE:], cos, sin)], axis=-1)
    scale = MLA_QK ** -0.5
    outs = []
    for start in range(0, s, Q_BLOCK):
        end = start + Q_BLOCK
        p = chunk_causal_probs(q[:, start:end], k[:, :end], start, scale)
        outs.append(jnp.einsum('bhqk,bkhd->bqhd', p, v[:, :end].astype(jnp.float32)))
    o = jnp.concatenate(outs, axis=1).astype(q_lat.dtype)
    return o.reshape(b, s, MLA_V_WIDTH)


def diff_branch(q_in, k_in, v_in, q_gain, k_gain, lq1, lk1, lq2, lk2, subln, lambda_init, cos, sin):
    b, s, _ = q_in.shape
    q = rms_norm(q_in.reshape(b, s, DIFF_HEADS, 2, DIFF_HEAD_DIM), q_gain)
    k = rms_norm(k_in.reshape(b, s, DIFF_HEADS, 2, DIFF_HEAD_DIM), k_gain)
    v = v_in.reshape(b, s, DIFF_HEADS, DIFF_V_DIM)
    q = jnp.concatenate([apply_rope(q[..., :DIFF_ROPE], cos, sin), q[..., DIFF_ROPE:]], axis=-1)
    k = jnp.concatenate([apply_rope(k[..., :DIFF_ROPE], cos, sin), k[..., DIFF_ROPE:]], axis=-1)
    q1, q2 = q[:, :, :, 0], q[:, :, :, 1]
    k1, k2 = k[:, :, :, 0], k[:, :, :, 1]
    lam = (jnp.exp(jnp.sum(lq1.astype(jnp.float32) * lk1.astype(jnp.float32)))
           - jnp.exp(jnp.sum(lq2.astype(jnp.float32) * lk2.astype(jnp.float32))) + lambda_init)
    scale = DIFF_HEAD_DIM ** -0.5
    outs = []
    for start in range(0, s, Q_BLOCK):
        end = start + Q_BLOCK
        p1 = chunk_causal_probs(q1[:, start:end], k1[:, :end], start, scale)
        p2 = chunk_causal_probs(q2[:, start:end], k2[:, :end], start, scale)
        a = p1 - lam * p2
        outs.append(jnp.einsum('bhqk,bkhd->bqhd', a, v[:, :end].astype(jnp.float32)))
    o = jnp.concatenate(outs, axis=1)
    o = rms_norm(o, subln) * (1.0 - lambda_init)
    return o.astype(q_in.dtype).reshape(b, s, DIFF_V_WIDTH)


def hier_moe(h, w_rg, b_rg, w_re, b_re, w_gate, w_up, w_down):
    b, s, d = h.shape
    t = h.reshape(b * s, d)
    n = t.shape[0]
    group_logits = (t @ w_rg).astype(jnp.float32) + b_rg.astype(jnp.float32)
    p_group = jax.nn.softmax(group_logits, axis=-1)
    pg_sel, g_idx = lax.top_k(p_group, 1)
    exp_logits = ((t @ w_re).astype(jnp.float32) + b_re.astype(jnp.float32)).reshape(n, N_GROUPS, EXPERTS_PER_GROUP)
    in_group = jnp.take_along_axis(exp_logits, g_idx[:, :, None], axis=1)[:, 0]
    p_exp = jax.nn.softmax(in_group, axis=-1)
    pe_top, e_idx = lax.top_k(p_exp, TOP_K)
    weights = pg_sel * pe_top / jnp.sum(pe_top, axis=-1, keepdims=True)
    global_idx = g_idx * EXPERTS_PER_GROUP + e_idx
    combine = jnp.sum(jax.nn.one_hot(global_idx, N_EXPERTS, dtype=jnp.float32) * weights[..., None], axis=1)
    out = jnp.zeros((n, d), jnp.float32)
    for e in range(N_EXPERTS):
        hidden = jax.nn.silu(t @ w_gate[e]) * (t @ w_up[e])
        out = out + combine[:, e:e + 1] * (hidden @ w_down[e]).astype(jnp.float32)
    return out.astype(h.dtype).reshape(b, s, d)


def setup_inputs(seed: int = 0) -> dict:
    key = jax.random.key(seed)
    ks = jax.random.split(key, 32)
    f32 = jnp.float32
    L = DEPTH

    def w(k, shape, fan_in):
        return jax.random.normal(k, shape, f32) * (fan_in ** -0.5)

    def gain(k, shape):
        return 1.0 + 0.02 * jax.random.normal(k, shape, f32)

    def small(k, shape, sc):
        return sc * jax.random.normal(k, shape, f32)

    return {
        'x': jax.random.normal(ks[0], (BATCH, SEQ, D_MODEL), f32),
        'norm_mix': gain(ks[1], (L, D_MODEL)),
        'w_in': w(ks[2], (L, D_MODEL, IN_WIDTH), D_MODEL),
        'mla_q_latent_norm': gain(ks[3], (L, MLA_Q_RANK)),
        'w_mla_uq': w(ks[4], (L, MLA_Q_RANK, MLA_HEADS * MLA_QK), MLA_Q_RANK),
        'mla_kv_latent_norm': gain(ks[5], (L, MLA_KV_RANK)),
        'w_mla_ukv': w(ks[6], (L, MLA_KV_RANK, MLA_HEADS * (MLA_NOPE + MLA_V)), MLA_KV_RANK),
        'mla_q_gain': gain(ks[7], (L, MLA_QK)),
        'mla_k_gain': gain(ks[8], (L, MLA_QK)),
        'diff_q_gain': gain(ks[9], (L, DIFF_HEAD_DIM)),
        'diff_k_gain': gain(ks[10], (L, DIFF_HEAD_DIM)),
        'lambda_q1': small(ks[11], (L, DIFF_HEAD_DIM), 0.1),
        'lambda_k1': small(ks[12], (L, DIFF_HEAD_DIM), 0.1),
        'lambda_q2': small(ks[13], (L, DIFF_HEAD_DIM), 0.1),
        'lambda_k2': small(ks[14], (L, DIFF_HEAD_DIM), 0.1),
        'diff_subln': gain(ks[15], (L, DIFF_V_DIM)),
        'w_mla_up': w(ks[16], (L, MLA_V_WIDTH, D_MODEL), MLA_V_WIDTH),
        'w_diff_up': w(ks[17], (L, DIFF_V_WIDTH, D_MODEL), DIFF_V_WIDTH),
        'w_out': w(ks[18], (L, D_MODEL, D_MODEL), D_MODEL),
        'norm_ffn': gain(ks[19], (L, D_MODEL)),
        'w_router_group': w(ks[20], (L, D_MODEL, N_GROUPS), D_MODEL),
        'b_router_group': small(ks[21], (L, N_GROUPS), 0.01),
        'w_router_expert': w(ks[22], (L, D_MODEL, N_EXPERTS), D_MODEL),
        'b_router_expert': small(ks[23], (L, N_EXPERTS), 0.01),
        'w_expert_gate': w(ks[24], (L, N_EXPERTS, D_MODEL, EXPERT_FF), D_MODEL),
        'w_expert_up': w(ks[25], (L, N_EXPERTS, D_MODEL, EXPERT_FF), D_MODEL),
        'w_expert_down': w(ks[26], (L, N_EXPERTS, EXPERT_FF, D_MODEL), EXPERT_FF),
    }


def reference(x, norm_mix, w_in, mla_q_latent_norm, w_mla_uq, mla_kv_latent_norm, w_mla_ukv,
              mla_q_gain, mla_k_gain, diff_q_gain, diff_k_gain, lambda_q1, lambda_k1, lambda_q2, lambda_k2,
              diff_subln, w_mla_up, w_diff_up, w_out, norm_ffn, w_router_group, b_router_group,
              w_router_expert, b_router_expert, w_expert_gate, w_expert_up, w_expert_down):
    seq_len = x.shape[1]
    cos_mla, sin_mla = rope_tables(seq_len, MLA_ROPE)
    cos_diff, sin_diff = rope_tables(seq_len, DIFF_ROPE)
    for l in range(DEPTH):
        lambda_init = 0.8 - 0.6 * math.exp(-0.3 * l)
        h = rms_norm(x, norm_mix[l])
        proj = h @ w_in[l]
        q_lat, kv_lat, k_rope_in, dq, dk, dv, g_mla, g_diff = jnp.split(proj, IN_SPLITS, axis=-1)
        o_mla = mla_branch(q_lat, kv_lat, k_rope_in, mla_q_latent_norm[l], w_mla_uq[l],
                           mla_kv_latent_norm[l], w_mla_ukv[l], mla_q_gain[l], mla_k_gain[l],
                           cos_mla, sin_mla)
        o_diff = diff_branch(dq, dk, dv, diff_q_gain[l], diff_k_gain[l], lambda_q1[l], lambda_k1[l],
                             lambda_q2[l], lambda_k2[l], diff_subln[l], lambda_init, cos_diff, sin_diff)
        merged = (jax.nn.sigmoid(g_mla) * (o_mla @ w_mla_up[l])
                  + jax.nn.sigmoid(g_diff) * (o_diff @ w_diff_up[l]))
        x = x + merged @ w_out[l]
        h2 = rms_norm(x, norm_ffn[l])
        x = x + hier_moe(h2, w_router_group[l], b_router_group[l], w_router_expert[l], b_router_expert[l],
                         w_expert_gate[l], w_expert_up[l], w_expert_down[l])
    return x
```

```python
import functools
import math

import jax
import jax.numpy as jnp
from jax import lax
from jax.experimental import pallas as pl
from jax.experimental.pallas import tpu as pltpu

CHUNK = 64
ROPE_THETA = 500000.0
EPS = 1e-6

MLA_HEADS = 8
MLA_NOPE = 64
MLA_ROPE = 32
MLA_V = 64
MLA_QK = MLA_NOPE + MLA_ROPE
MLA_Q_RANK = 256
MLA_KV_RANK = 128

DIFF_HEADS = 4
DIFF_HEAD_DIM = 64
DIFF_V_DIM = 2 * DIFF_HEAD_DIM
DIFF_ROPE = DIFF_HEAD_DIM // 4
DIFF_QK_WIDTH = DIFF_HEADS * 2 * DIFF_HEAD_DIM
DIFF_V_WIDTH = DIFF_HEADS * DIFF_V_DIM

N_GROUPS = 4
EXPERTS_PER_GROUP = 8
N_EXPERTS = N_GROUPS * EXPERTS_PER_GROUP
EXPERT_FF = 256

LANES = 128
VMEM_LIMIT_BYTES = 48 * 1024 * 1024

PROJ_ROWS = 512
ATTN_Q_ROWS = 256
ATTN_K_ROWS = 256
MERGE_ROWS = 512
MOE_ROWS = 1024

BF16 = jnp.bfloat16
F32 = jnp.float32


def _dot(a, b):
    return jnp.dot(a, b, preferred_element_type=F32)


def _dot_nt(a, b):
    return lax.dot_general(a, b, (((1,), (1,)), ((), ())), preferred_element_type=F32)


def _rms(x, width):
    return x * lax.rsqrt(jnp.sum(x * x, axis=-1, keepdims=True) * (1.0 / width) + EPS)


def _rotate_pairs(y, half, cos, sin):
    lane = lax.broadcasted_iota(jnp.int32, y.shape, 1)
    up = pltpu.roll(y, LANES - half, 1)
    down = pltpu.roll(y, half, 1)
    partner = jnp.where((lane // half) % 2 == 0, up, down)
    return y * cos + partner * sin


def _proj_kernel(x_ref, gmix_ref, wql_ref, wkvl_ref, wkr_ref, wdq_ref, wdk_ref, wdv_ref, wgm_ref, wgd_ref,
                 gql_ref, wuq_ref, gkvl_ref, wuk_ref, wuv_ref, gq_ref, gk_ref, gdq_ref, gdk_ref,
                 cm_ref, sm_ref, cd_ref, sd_ref,
                 qm_ref, km_ref, vm_ref, qd_ref, kd_ref, vd_ref, sgm_ref, sgd_ref):
    x = x_ref[...]
    h = (_rms(x, x.shape[-1]) * gmix_ref[...]).astype(BF16)

    cm, sm = cm_ref[...], sm_ref[...]
    cd, sd = cd_ref[...], sd_ref[...]

    ql = _rms(_dot(h, wql_ref[...]), MLA_Q_RANK) * gql_ref[...]
    q = _dot(ql.astype(BF16), wuq_ref[...])
    gq = gq_ref[...]
    for hd in range(MLA_HEADS):
        sl = slice(hd * LANES, (hd + 1) * LANES)
        y = _rms(q[:, sl], MLA_QK) * gq
        y = _rotate_pairs(y, MLA_ROPE // 2, cm, sm) * (MLA_QK ** -0.5)
        qm_ref[:, sl] = y.astype(BF16)

    kvl = (_rms(_dot(h, wkvl_ref[...]), MLA_KV_RANK) * gkvl_ref[...]).astype(BF16)
    kr = _dot(h, wkr_ref[...])
    kn = _dot(kvl, wuk_ref[...])
    vm_ref[...] = _dot(kvl, wuv_ref[...]).astype(BF16)
    gk = gk_ref[...]
    for hd in range(MLA_HEADS):
        sl = slice(hd * LANES, (hd + 1) * LANES)
        y = _rms(kn[:, sl] + kr, MLA_QK) * gk
        km_ref[:, sl] = _rotate_pairs(y, MLA_ROPE // 2, cm, sm).astype(BF16)

    def diff_qk(w_ref, g_ref, o_ref, scale):
        t = _dot(h, w_ref[...])
        g = g_ref[...]
        for hd in range(DIFF_HEADS):
            sl = slice(hd * LANES, (hd + 1) * LANES)
            th = t[:, sl]
            lane = lax.broadcasted_iota(jnp.int32, th.shape, 1)
            sq = th * th
            lo = jnp.sum(jnp.where(lane < DIFF_HEAD_DIM, sq, 0.0), axis=-1, keepdims=True)
            tot = jnp.sum(sq, axis=-1, keepdims=True)
            ss = jnp.where(lane < DIFF_HEAD_DIM, lo, tot - lo)
            y = th * lax.rsqrt(ss * (1.0 / DIFF_HEAD_DIM) + EPS) * g
            o_ref[:, sl] = (_rotate_pairs(y, DIFF_ROPE // 2, cd, sd) * scale).astype(BF16)

    diff_qk(wdq_ref, gdq_ref, qd_ref, DIFF_HEAD_DIM ** -0.5)
    diff_qk(wdk_ref, gdk_ref, kd_ref, 1.0)
    vd_ref[...] = _dot(h, wdv_ref[...]).astype(BF16)

    sgm_ref[...] = jax.nn.sigmoid(_dot(h, wgm_ref[...])).astype(BF16)
    sgd_ref[...] = jax.nn.sigmoid(_dot(h, wgd_ref[...])).astype(BF16)


def _proj_call(x2, seq_len, p):
    n, d = x2.shape
    tm = PROJ_ROWS
    pos_blocks = seq_len // tm
    row = lambda i: (i, 0)
    const = lambda i: (0, 0)
    pos = lambda i: (i % pos_blocks, 0)
    weights = [p["gmix"], p["wql"], p["wkvl"], p["wkr"], p["wdq"], p["wdk"], p["wdv"], p["wgm"], p["wgd"],
               p["gql"], p["wuq"], p["gkvl"], p["wuk"], p["wuv"], p["gq"], p["gk"], p["gdq"], p["gdk"]]
    tables = [p["cm"], p["sm"], p["cd"], p["sd"]]
    in_specs = ([pl.BlockSpec((tm, d), row)]
                + [pl.BlockSpec(w.shape, const) for w in weights]
                + [pl.BlockSpec((tm, LANES), pos) for _ in tables])
    widths = [MLA_HEADS * LANES, MLA_HEADS * LANES, MLA_HEADS * MLA_V, DIFF_QK_WIDTH, DIFF_QK_WIDTH, DIFF_V_WIDTH, d, d]
    return pl.pallas_call(
        _proj_kernel,
        grid=(n // tm,),
        in_specs=in_specs,
        out_specs=[pl.BlockSpec((tm, w), row) for w in widths],
        out_shape=[jax.ShapeDtypeStruct((n, w), BF16) for w in widths],
        compiler_params=pltpu.CompilerParams(dimension_semantics=("parallel",), vmem_limit_bytes=VMEM_LIMIT_BYTES),
        name="proj",
    )(x2, *weights, *tables)


def _chunk_mask(tq, tk):
    qc = lax.broadcasted_iota(jnp.int32, (tq, tk), 0) // CHUNK
    kc = lax.broadcasted_iota(jnp.int32, (tq, tk), 1) // CHUNK
    return kc <= qc


def _softmax_step(s, v, m_ref, l_ref, acc_ref):
    m_prev = m_ref[...]
    m_new = jnp.maximum(m_prev, jnp.max(s, axis=-1, keepdims=True))
    alpha = jnp.exp(m_prev - m_new)
    pr = jnp.exp(s - m_new)
    l_ref[...] = alpha * l_ref[...] + jnp.sum(pr, axis=-1, keepdims=True)
    acc_ref[...] = alpha * acc_ref[...] + _dot(pr.astype(BF16), v)
    m_ref[...] = m_new


def _init_state(refs):
    for m_ref, l_ref, acc_ref in refs:
        m_ref[...] = jnp.full(m_ref.shape, -jnp.inf, F32)
        l_ref[...] = jnp.zeros(l_ref.shape, F32)
        acc_ref[...] = jnp.zeros(acc_ref.shape, F32)


def _mla_kernel(q_ref, k_ref, v_ref, o_ref, m0, l0, a0, m1, l1, a1):
    i = pl.program_id(2)
    tq, tk = ATTN_Q_ROWS, ATTN_K_ROWS
    state = ((m0, l0, a0), (m1, l1, a1))
    _init_state(state)

    def step(j, masked):
        rows = pl.ds(pl.multiple_of(j * tk, tk), tk)
        v = v_ref[rows, :]
        for hd in range(2):
            sl = slice(hd * LANES, (hd + 1) * LANES)
            s = _dot_nt(q_ref[:, sl], k_ref[rows, sl])
            if masked:
                s = jnp.where(_chunk_mask(tq, tk), s, -jnp.inf)
            _softmax_step(s, v, *state[hd])

    def body(j, carry):
        step(j, False)
        return carry

    lax.fori_loop(0, i, body, 0)
    step(i, True)

    lane = lax.broadcasted_iota(jnp.int32, a0.shape, 1)
    o = jnp.where(lane < MLA_V, a0[...] / l0[...], a1[...] / l1[...])
    o_ref[...] = o.astype(BF16)


def _mla_call(qm, km, vm, batch, seq_len):
    n = qm.shape[0]
    tq = ATTN_Q_ROWS
    qt = seq_len // tq
    pairs = MLA_HEADS // 2
    state = [pltpu.VMEM((tq, 1), F32), pltpu.VMEM((tq, 1), F32), pltpu.VMEM((tq, LANES), F32)]
    return pl.pallas_call(
        _mla_kernel,
        grid=(batch, pairs, qt),
        in_specs=[pl.BlockSpec((tq, 2 * LANES), lambda b, h, i: (b * qt + i, h)),
                  pl.BlockSpec((seq_len, 2 * LANES), lambda b, h, i: (b, h)),
                  pl.BlockSpec((seq_len, LANES), lambda b, h, i: (b, h))],
        out_specs=pl.BlockSpec((tq, LANES), lambda b, h, i: (b * qt + i, h)),
        out_shape=jax.ShapeDtypeStruct((n, MLA_HEADS * MLA_V), BF16),
        scratch_shapes=state + state,
        compiler_params=pltpu.CompilerParams(dimension_semantics=("parallel", "parallel", "arbitrary"),
                                             vmem_limit_bytes=VMEM_LIMIT_BYTES),
        name="mla_attn",
    )(qm, km, vm)


def _diff_kernel(lam_init, q_ref, k_ref, v_ref, lq1_ref, lk1_ref, lq2_ref, lk2_ref, subln_ref, o_ref,
                 m0, l0, a0, m1, l1, a1):
    i = pl.program_id(2)
    tq, tk = ATTN_Q_ROWS, ATTN_K_ROWS
    state = ((m0, l0, a0), (m1, l1, a1))
    _init_state(state)

    q = q_ref[...]
    lane = lax.broadcasted_iota(jnp.int32, q.shape, 1)
    zero = jnp.zeros_like(q)
    halves = (jnp.where(lane < DIFF_HEAD_DIM, q, zero), jnp.where(lane < DIFF_HEAD_DIM, zero, q))

    def step(j, masked):
        rows = pl.ds(pl.multiple_of(j * tk, tk), tk)
        k = k_ref[rows, :]
        v = v_ref[rows, :]
        for hf in range(2):
            s = _dot_nt(halves[hf], k)
            if masked:
                s = jnp.where(_chunk_mask(tq, tk), s, -jnp.inf)
            _softmax_step(s, v, *state[hf])

    def body(j, carry):
        step(j, False)
        return carry

    lax.fori_loop(0, i, body, 0)
    step(i, True)

    lam = (jnp.exp(jnp.sum(lq1_ref[...] * lk1_ref[...], axis=-1, keepdims=True))
           - jnp.exp(jnp.sum(lq2_ref[...] * lk2_ref[...], axis=-1, keepdims=True)) + lam_init)
    o = a0[...] / l0[...] - lam * (a1[...] / l1[...])
    o = _rms(o, DIFF_V_DIM) * subln_ref[...] * (1.0 - lam_init)
    o_ref[...] = o.astype(BF16)


def _diff_call(qd, kd, vd, lq1, lk1, lq2, lk2, subln, lam_init, batch, seq_len):
    n = qd.shape[0]
    tq = ATTN_Q_ROWS
    qt = seq_len // tq
    state = [pltpu.VMEM((tq, 1), F32), pltpu.VMEM((tq, 1), F32), pltpu.VMEM((tq, LANES), F32)]
    small = lambda a: pl.BlockSpec(a.shape, lambda b, h, i: (0, 0))
    return pl.pallas_call(
        functools.partial(_diff_kernel, lam_init),
        grid=(batch, DIFF_HEADS, qt),
        in_specs=[pl.BlockSpec((tq, LANES), lambda b, h, i: (b * qt + i, h)),
                  pl.BlockSpec((seq_len, LANES), lambda b, h, i: (b, h)),
                  pl.BlockSpec((seq_len, LANES), lambda b, h, i: (b, h)),
                  small(lq1), small(lk1), small(lq2), small(lk2), small(subln)],
        out_specs=pl.BlockSpec((tq, LANES), lambda b, h, i: (b * qt + i, h)),
        out_shape=jax.ShapeDtypeStruct((n, DIFF_V_WIDTH), BF16),
        scratch_shapes=state + state,
        compiler_params=pltpu.CompilerParams(dimension_semantics=("parallel", "parallel", "arbitrary"),
                                             vmem_limit_bytes=VMEM_LIMIT_BYTES),
        name="diff_attn",
    )(qd, kd, vd, lq1, lk1, lq2, lk2, subln)


def _merge_kernel(x_ref, om_ref, od_ref, sgm_ref, sgd_ref, wmu_ref, wdu_ref, wout_ref, gffn_ref, wr_ref, br_ref,
                  x1_ref, h2_ref, comb_ref):
    merged = (sgm_ref[...].astype(F32) * _dot(om_ref[...], wmu_ref[...])
              + sgd_ref[...].astype(F32) * _dot(od_ref[...], wdu_ref[...]))
    x1 = x_ref[...] + _dot(merged.astype(BF16), wout_ref[...])
    x1_ref[...] = x1
    h2 = _rms(x1, x1.shape[-1]) * gffn_ref[...]
    h2_ref[...] = h2.astype(BF16)

    logits = jnp.dot(h2, wr_ref[...], preferred_element_type=F32, precision=lax.Precision.HIGHEST) + br_ref[...]
    lane = lax.broadcasted_iota(jnp.int32, logits.shape, 1)
    neg = -jnp.inf
    big = jnp.int32(1 << 20)

    def top(vals):
        mx = jnp.max(vals, axis=-1, keepdims=True)
        idx = jnp.min(jnp.where(vals == mx, lane, big), axis=-1, keepdims=True)
        return mx, idx

    gl = jnp.where((lane >= N_EXPERTS) & (lane < N_EXPERTS + N_GROUPS), logits, neg)
    gmax, gidx = top(gl)
    pg_sel = 1.0 / jnp.sum(jnp.exp(gl - gmax), axis=-1, keepdims=True)
    el = jnp.where((lane < N_EXPERTS) & (lane // EXPERTS_PER_GROUP == gidx - N_EXPERTS), logits, neg)
    m1, i1 = top(el)
    m2, i2 = top(jnp.where(lane == i1, neg, el))
    e2 = jnp.exp(m2 - m1)
    w1 = pg_sel / (1.0 + e2)
    comb_ref[...] = jnp.where(lane == i1, w1, 0.0) + jnp.where(lane == i2, w1 * e2, 0.0)


def _merge_call(x2, om, od, sgm, sgd, p):
    n, d = x2.shape
    tm = MERGE_ROWS
    row = lambda i: (i, 0)
    const = lambda i: (0, 0)
    weights = [p["wmu"], p["wdu"], p["wout"], p["gffn"], p["wr"], p["br"]]
    return pl.pallas_call(
        _merge_kernel,
        grid=(n // tm,),
        in_specs=([pl.BlockSpec((tm, a.shape[1]), row) for a in (x2, om, od, sgm, sgd)]
                  + [pl.BlockSpec(w.shape, const) for w in weights]),
        out_specs=[pl.BlockSpec((tm, d), row), pl.BlockSpec((tm, d), row), pl.BlockSpec((tm, LANES), row)],
        out_shape=[jax.ShapeDtypeStruct((n, d), F32), jax.ShapeDtypeStruct((n, d), BF16),
                   jax.ShapeDtypeStruct((n, LANES), F32)],
        compiler_params=pltpu.CompilerParams(dimension_semantics=("parallel",), vmem_limit_bytes=VMEM_LIMIT_BYTES),
        name="merge_router",
    )(x2, om, od, sgm, sgd, *weights)


def _moe_kernel(x1_ref, h2_ref, comb_ref, wg_ref, wu_ref, wd_ref, o_ref):
    e = pl.program_id(1)

    @pl.when(e == 0)
    def _():
        o_ref[...] = x1_ref[...]

    h2 = h2_ref[...]
    comb = comb_ref[...]
    lane = lax.broadcasted_iota(jnp.int32, comb.shape, 1)
    c = jnp.sum(jnp.where(lane == e, comb, 0.0), axis=-1, keepdims=True)
    gate = _dot(h2, wg_ref[0])
    up = _dot(h2, wu_ref[0])
    hidden = (gate * jax.nn.sigmoid(gate) * up).astype(BF16)
    o_ref[...] += c * _dot(hidden, wd_ref[0])


def _moe_call(x1, h2, comb, wg, wu, wd):
    n, d = x1.shape
    tm = MOE_ROWS
    row = lambda i, e: (i, 0)
    wsel = lambda i, e: (e, 0, 0)
    return pl.pallas_call(
        _moe_kernel,
        grid=(n // tm, N_EXPERTS),
        in_specs=[pl.BlockSpec((tm, d), row), pl.BlockSpec((tm, d), row), pl.BlockSpec((tm, LANES), row),
                  pl.BlockSpec((1, d, EXPERT_FF), wsel), pl.BlockSpec((1, d, EXPERT_FF), wsel),
                  pl.BlockSpec((1, EXPERT_FF, d), wsel)],
        out_specs=pl.BlockSpec((tm, d), row),
        out_shape=jax.ShapeDtypeStruct((n, d), F32),
        compiler_params=pltpu.CompilerParams(dimension_semantics=("parallel", "arbitrary"),
                                             vmem_limit_bytes=VMEM_LIMIT_BYTES),
        name="moe",
    )(x1, h2, comb, wg, wu, wd)


def _rope_lane_tables(seq_len, rot_dim, period, first):
    half = rot_dim // 2
    pos = jnp.arange(seq_len, dtype=F32)
    inv = 1.0 / (ROPE_THETA ** (jnp.arange(0, rot_dim, 2, dtype=F32) / rot_dim))
    ang = pos[:, None] * inv[None, :]
    cos, sin = jnp.cos(ang), jnp.sin(ang)
    lane = jnp.arange(LANES)
    rel = (lane % period) - first
    active = (rel >= 0) & (rel < rot_dim)
    idx = jnp.clip(rel, 0, rot_dim - 1) % half
    sign = jnp.where(rel < half, -1.0, 1.0)
    c = jnp.where(active[None, :], cos[:, idx], 1.0)
    s = jnp.where(active[None, :], sin[:, idx] * sign[None, :], 0.0)
    return c.astype(F32), s.astype(F32)


def _head_pad(w, heads, width):
    r = w.shape[0]
    w = w.reshape(r, heads, width)
    return jnp.pad(w, ((0, 0), (0, 0), (0, LANES - width))).reshape(r, heads * LANES)


def _layer_params(l, seq_len, norm_mix, w_in, mla_q_latent_norm, w_mla_uq, mla_kv_latent_norm, w_mla_ukv,
                  mla_q_gain, mla_k_gain, diff_q_gain, diff_k_gain, w_mla_up, w_diff_up, w_out, norm_ffn,
                  w_router_group, b_router_group, w_router_expert, b_router_expert):
    d = w_in.shape[1]
    sizes = (MLA_Q_RANK, MLA_KV_RANK, MLA_ROPE, DIFF_QK_WIDTH, DIFF_QK_WIDTH, DIFF_V_WIDTH, d, d)
    offs = [0]
    for s in sizes:
        offs.append(offs[-1] + s)
    wi = w_in[l]
    seg = [wi[:, offs[k]:offs[k + 1]] for k in range(len(sizes))]
    row = lambda g: g.astype(F32)[None, :]
    p = {}
    p["gmix"] = row(norm_mix[l])
    p["wql"] = seg[0].astype(BF16)
    p["wkvl"] = seg[1].astype(BF16)
    p["wkr"] = jnp.pad(seg[2], ((0, 0), (MLA_NOPE, LANES - MLA_QK))).astype(BF16)
    p["wdq"], p["wdk"], p["wdv"] = (s.astype(BF16) for s in seg[3:6])
    p["wgm"], p["wgd"] = seg[6].astype(BF16), seg[7].astype(BF16)
    p["gql"] = row(mla_q_latent_norm[l])
    p["wuq"] = _head_pad(w_mla_uq[l], MLA_HEADS, MLA_QK).astype(BF16)
    p["gkvl"] = row(mla_kv_latent_norm[l])
    ukv = w_mla_ukv[l].reshape(MLA_KV_RANK, MLA_HEADS, MLA_NOPE + MLA_V)
    p["wuk"] = _head_pad(ukv[:, :, :MLA_NOPE].reshape(MLA_KV_RANK, -1), MLA_HEADS, MLA_NOPE).astype(BF16)
    p["wuv"] = ukv[:, :, MLA_NOPE:].reshape(MLA_KV_RANK, -1).astype(BF16)
    p["gq"] = jnp.pad(mla_q_gain[l].astype(F32), (0, LANES - MLA_QK))[None, :]
    p["gk"] = jnp.pad(mla_k_gain[l].astype(F32), (0, LANES - MLA_QK))[None, :]
    p["gdq"] = jnp.tile(diff_q_gain[l].astype(F32), 2)[None, :]
    p["gdk"] = jnp.tile(diff_k_gain[l].astype(F32), 2)[None, :]
    p["cm"], p["sm"] = _rope_lane_tables(seq_len, MLA_ROPE, LANES, MLA_NOPE)
    p["cd"], p["sd"] = _rope_lane_tables(seq_len, DIFF_ROPE, DIFF_HEAD_DIM, 0)
    p["wmu"] = w_mla_up[l].astype(BF16)
    p["wdu"] = w_diff_up[l].astype(BF16)
    p["wout"] = w_out[l].astype(BF16)
    p["gffn"] = row(norm_ffn[l])
    wr = jnp.concatenate([w_router_expert[l], w_router_group[l]], axis=1).astype(F32)
    p["wr"] = jnp.pad(wr, ((0, 0), (0, LANES - wr.shape[1])))
    br = jnp.concatenate([b_router_expert[l], b_router_group[l]]).astype(F32)
    p["br"] = jnp.pad(br, (0, LANES - br.shape[0]))[None, :]
    return p


def kernel(x, norm_mix, w_in, mla_q_latent_norm, w_mla_uq, mla_kv_latent_norm, w_mla_ukv, mla_q_gain, mla_k_gain, diff_q_gain, diff_k_gain, lambda_q1, lambda_k1, lambda_q2, lambda_k2, diff_subln, w_mla_up, w_diff_up, w_out, norm_ffn, w_router_group, b_router_group, w_router_expert, b_router_expert, w_expert_gate, w_expert_up, w_expert_down):
    batch, seq_len, d = x.shape
    x2 = x.reshape(batch * seq_len, d)
    row = lambda g: g.astype(F32)[None, :]
    for l in range(norm_mix.shape[0]):
        lam_init = 0.8 - 0.6 * math.exp(-0.3 * l)
        p = _layer_params(l, seq_len, norm_mix, w_in, mla_q_latent_norm, w_mla_uq, mla_kv_latent_norm, w_mla_ukv,
                          mla_q_gain, mla_k_gain, diff_q_gain, diff_k_gain, w_mla_up, w_diff_up, w_out, norm_ffn,
                          w_router_group, b_router_group, w_router_expert, b_router_expert)
        qm, km, vm, qd, kd, vd, sgm, sgd = _proj_call(x2, seq_len, p)
        om = _mla_call(qm, km, vm, batch, seq_len)
        od = _diff_call(qd, kd, vd, row(lambda_q1[l]), row(lambda_k1[l]), row(lambda_q2[l]), row(lambda_k2[l]),
                        row(diff_subln[l]), lam_init, batch, seq_len)
        x1, h2, comb = _merge_call(x2, om, od, sgm, sgd, p)
        x2 = _moe_call(x1, h2, comb, w_expert_gate[l].astype(BF16), w_expert_up[l].astype(BF16),
                       w_expert_down[l].astype(BF16))
    return x2.reshape(batch, seq_len, d)
```

```python
import functools
import math

import jax
import jax.numpy as jnp
from jax import lax
from jax.experimental import pallas as pl
from jax.experimental.pallas import tpu as pltpu

CHUNK = 64
ROPE_THETA = 500000.0
EPS = 1e-6

MLA_HEADS = 8
MLA_NOPE = 64
MLA_ROPE = 32
MLA_V = 64
MLA_QK = MLA_NOPE + MLA_ROPE
MLA_Q_RANK = 256
MLA_KV_RANK = 128

DIFF_HEADS = 4
DIFF_HEAD_DIM = 64
DIFF_V_DIM = 2 * DIFF_HEAD_DIM
DIFF_ROPE = DIFF_HEAD_DIM // 4
DIFF_QK_WIDTH = DIFF_HEADS * 2 * DIFF_HEAD_DIM
DIFF_V_WIDTH = DIFF_HEADS * DIFF_V_DIM

N_GROUPS = 4
EXPERTS_PER_GROUP = 8
N_EXPERTS = N_GROUPS * EXPERTS_PER_GROUP
EXPERT_FF = 256

LANES = 128
VMEM_LIMIT_BYTES = 48 * 1024 * 1024

PROJ_ROWS = 512
ATTN_Q_ROWS = 256
ATTN_K_ROWS = 256
MERGE_ROWS = 512
MOE_ROWS = 1024
MLA_HEADS_PER_STEP = 4
DIFF_HEADS_PER_STEP = 2
LOG2E = 1.4426950408889634

BF16 = jnp.bfloat16
F32 = jnp.float32


def _dot(a, b):
    return jnp.dot(a, b, preferred_element_type=F32)


def _dot_nt(a, b):
    return lax.dot_general(a, b, (((1,), (1,)), ((), ())), preferred_element_type=F32)


def _rms(x, width):
    return x * lax.rsqrt(jnp.sum(x * x, axis=-1, keepdims=True) * (1.0 / width) + EPS)


def _rotate_pairs(y, half, cos, sin):
    lane = lax.broadcasted_iota(jnp.int32, y.shape, 1)
    up = pltpu.roll(y, LANES - half, 1)
    down = pltpu.roll(y, half, 1)
    partner = jnp.where((lane // half) % 2 == 0, up, down)
    return y * cos + partner * sin


def _store_k_tiles(o_ref, vt):
    tk = o_ref.shape[-1]
    for c in range(o_ref.shape[0]):
        o_ref[c] = vt[:, c * tk:(c + 1) * tk].astype(BF16)


def _proj_kernel(x_ref, gmix_ref, wql_ref, wkvl_ref, wkr_ref, wdq_ref, wdk_ref, wdvt_ref, wgm_ref, wgd_ref,
                 gql_ref, wuq_ref, gkvl_ref, wuk_ref, wuvt_ref, gq_ref, gk_ref, gdq_ref, gdk_ref,
                 cm_ref, sm_ref, cd_ref, sd_ref,
                 qm_ref, km_ref, vtm_ref, qd_ref, kd_ref, vtd_ref, sgm_ref, sgd_ref):
    x = x_ref[...]
    h = (_rms(x, x.shape[-1]) * gmix_ref[...]).astype(BF16)

    cm, sm = cm_ref[...], sm_ref[...]
    cd, sd = cd_ref[...], sd_ref[...]

    ql = _rms(_dot(h, wql_ref[...]), MLA_Q_RANK) * gql_ref[...]
    q = _dot(ql.astype(BF16), wuq_ref[...])
    gq = gq_ref[...]
    for hd in range(MLA_HEADS):
        sl = slice(hd * LANES, (hd + 1) * LANES)
        y = _rms(q[:, sl], MLA_QK) * gq
        y = _rotate_pairs(y, MLA_ROPE // 2, cm, sm) * (LOG2E * MLA_QK ** -0.5)
        qm_ref[:, sl] = y.astype(BF16)

    kvl = (_rms(_dot(h, wkvl_ref[...]), MLA_KV_RANK) * gkvl_ref[...]).astype(BF16)
    kr = _dot(h, wkr_ref[...])
    kn = _dot(kvl, wuk_ref[...])
    _store_k_tiles(vtm_ref, _dot_nt(wuvt_ref[...], kvl))
    gk = gk_ref[...]
    for hd in range(MLA_HEADS):
        sl = slice(hd * LANES, (hd + 1) * LANES)
        y = _rms(kn[:, sl] + kr, MLA_QK) * gk
        km_ref[:, sl] = _rotate_pairs(y, MLA_ROPE // 2, cm, sm).astype(BF16)

    def diff_qk(w_ref, g_ref, o_ref, scale):
        t = _dot(h, w_ref[...])
        g = g_ref[...]
        for hd in range(DIFF_HEADS):
            sl = slice(hd * LANES, (hd + 1) * LANES)
            th = t[:, sl]
            lane = lax.broadcasted_iota(jnp.int32, th.shape, 1)
            sq = th * th
            lo = jnp.sum(jnp.where(lane < DIFF_HEAD_DIM, sq, 0.0), axis=-1, keepdims=True)
            tot = jnp.sum(sq, axis=-1, keepdims=True)
            ss = jnp.where(lane < DIFF_HEAD_DIM, lo, tot - lo)
            y = th * lax.rsqrt(ss * (1.0 / DIFF_HEAD_DIM) + EPS) * g
            o_ref[:, sl] = (_rotate_pairs(y, DIFF_ROPE // 2, cd, sd) * scale).astype(BF16)

    diff_qk(wdq_ref, gdq_ref, qd_ref, LOG2E * DIFF_HEAD_DIM ** -0.5)
    diff_qk(wdk_ref, gdk_ref, kd_ref, 1.0)
    _store_k_tiles(vtd_ref, _dot_nt(wdvt_ref[...], h))

    sgm_ref[...] = jax.nn.sigmoid(_dot(h, wgm_ref[...])).astype(BF16)
    sgd_ref[...] = jax.nn.sigmoid(_dot(h, wgd_ref[...])).astype(BF16)


def _proj_call(x2, seq_len, p):
    n, d = x2.shape
    tm = PROJ_ROWS
    pos_blocks = seq_len // tm
    row = lambda i: (i, 0)
    const = lambda i: (0, 0)
    pos = lambda i: (i % pos_blocks, 0)
    weights = [p["gmix"], p["wql"], p["wkvl"], p["wkr"], p["wdq"], p["wdk"], p["wdvt"], p["wgm"], p["wgd"],
               p["gql"], p["wuq"], p["gkvl"], p["wuk"], p["wuvt"], p["gq"], p["gk"], p["gdq"], p["gdk"]]
    tables = [p["cm"], p["sm"], p["cd"], p["sd"]]
    in_specs = ([pl.BlockSpec((tm, d), row)]
                + [pl.BlockSpec(w.shape, const) for w in weights]
                + [pl.BlockSpec((tm, LANES), pos) for _ in tables])
    tk = ATTN_K_ROWS
    widths = [MLA_HEADS * LANES, MLA_HEADS * LANES, -MLA_HEADS * MLA_V, DIFF_QK_WIDTH, DIFF_QK_WIDTH, -DIFF_V_WIDTH, d, d]
    out_specs = [pl.BlockSpec((tm, w), row) if w > 0 else pl.BlockSpec((tm // tk, -w, tk), lambda i: (i, 0, 0))
                 for w in widths]
    out_shape = [jax.ShapeDtypeStruct((n, w) if w > 0 else (n // tk, -w, tk), BF16) for w in widths]
    return pl.pallas_call(
        _proj_kernel,
        grid=(n // tm,),
        in_specs=in_specs,
        out_specs=out_specs,
        out_shape=out_shape,
        compiler_params=pltpu.CompilerParams(dimension_semantics=("parallel",), vmem_limit_bytes=VMEM_LIMIT_BYTES),
        name="proj",
    )(x2, *weights, *tables)


def _chunk_mask_t(tk, tq):
    kc = lax.broadcasted_iota(jnp.int32, (tk, tq), 0) // CHUNK
    qc = lax.broadcasted_iota(jnp.int32, (tk, tq), 1) // CHUNK
    return kc <= qc


def _softmax_step_t(st, vt, m_ref, l_ref, acc_ref):
    m_prev = m_ref[...]
    m_new = jnp.maximum(m_prev, jnp.max(st, axis=0, keepdims=True))
    alpha = jnp.exp2(m_prev - m_new)
    pr = jnp.exp2(st - m_new)
    l_ref[...] = alpha * l_ref[...] + jnp.sum(pr, axis=0, keepdims=True)
    acc_ref[...] = alpha * acc_ref[...] + _dot(vt, pr.astype(BF16))
    m_ref[...] = m_new


def _attn_scratch(chains, dv, tq, tk):
    per_chain = [pltpu.VMEM((1, tq), F32), pltpu.VMEM((1, tq), F32), pltpu.VMEM((dv, tq), F32),
                 pltpu.VMEM((tk, tq), F32), pltpu.VMEM((tk, tq), F32)]
    return per_chain * chains


def _flash_pipeline(last_tile, scratch_refs, score_fn, value_fn, tk, tq):
    n_chains = len(scratch_refs) // 5
    chains = [scratch_refs[5 * c:5 * c + 5] for c in range(n_chains)]
    for m_ref, l_ref, acc_ref, _, _ in chains:
        m_ref[...] = jnp.full(m_ref.shape, -jnp.inf, F32)
        l_ref[...] = jnp.zeros(l_ref.shape, F32)
        acc_ref[...] = jnp.zeros(acc_ref.shape, F32)

    def scores(t, slot):
        for c, ch in enumerate(chains):
            ch[3 + slot][...] = score_fn(c, t)

    def update(t, slot, masked):
        for c, ch in enumerate(chains):
            st = ch[3 + slot][...]
            if masked:
                st = jnp.where(_chunk_mask_t(tk, tq), st, -jnp.inf)
            _softmax_step_t(st, value_fn(c, t), ch[0], ch[1], ch[2])

    scores(0, 0)

    def pair(p, carry):
        t = 2 * p
        scores(t + 1, 1)
        update(t, 0, False)
        scores(t + 2, 0)
        update(t + 1, 1, False)
        return carry

    lax.fori_loop(0, last_tile // 2, pair, 0)

    @pl.when(last_tile % 2 == 0)
    def _():
        update(last_tile, 0, True)

    @pl.when(last_tile % 2 == 1)
    def _():
        scores(last_tile, 1)
        update(last_tile - 1, 0, False)
        update(last_tile, 1, True)

    return [(ch[1], ch[2]) for ch in chains]


def _mla_kernel(q_ref, k_ref, vt_ref, o_ref, *scratch_refs):
    tq, tk = ATTN_Q_ROWS, ATTN_K_ROWS

    def score_fn(c, t):
        rows = pl.ds(pl.multiple_of(t * tk, tk), tk)
        sl = slice(c * LANES, (c + 1) * LANES)
        return _dot_nt(k_ref[rows, sl], q_ref[:, sl])

    def value_fn(c, t):
        return vt_ref[t, c * MLA_V:(c + 1) * MLA_V, :]

    out = _flash_pipeline(pl.program_id(2), scratch_refs, score_fn, value_fn, tk, tq)
    ot = jnp.concatenate([acc_ref[...] / l_ref[...] for l_ref, acc_ref in out], axis=0)
    o_ref[...] = ot.T.astype(BF16)


def _mla_call(qm, km, vtm, batch, seq_len):
    n = qm.shape[0]
    tq, tk, hps = ATTN_Q_ROWS, ATTN_K_ROWS, MLA_HEADS_PER_STEP
    qt = seq_len // tq
    return pl.pallas_call(
        _mla_kernel,
        grid=(batch, MLA_HEADS // hps, qt),
        in_specs=[pl.BlockSpec((tq, hps * LANES), lambda b, h, i: (b * qt + i, h)),
                  pl.BlockSpec((seq_len, hps * LANES), lambda b, h, i: (b, h)),
                  pl.BlockSpec((seq_len // tk, hps * MLA_V, tk), lambda b, h, i: (b, h, 0))],
        out_specs=pl.BlockSpec((tq, hps * MLA_V), lambda b, h, i: (b * qt + i, h)),
        out_shape=jax.ShapeDtypeStruct((n, MLA_HEADS * MLA_V), BF16),
        scratch_shapes=_attn_scratch(hps, MLA_V, tq, tk),
        compiler_params=pltpu.CompilerParams(dimension_semantics=("parallel", "parallel", "arbitrary"),
                                             vmem_limit_bytes=VMEM_LIMIT_BYTES),
        name="mla_attn",
    )(qm, km, vtm)


def _diff_kernel(lam_init, q_ref, k_ref, vt_ref, lq1_ref, lk1_ref, lq2_ref, lk2_ref, subln_ref, o_ref,
                 *scratch_refs):
    tq, tk = ATTN_Q_ROWS, ATTN_K_ROWS
    hps = DIFF_HEADS_PER_STEP

    q = q_ref[...]
    lane = lax.broadcasted_iota(jnp.int32, (tq, LANES), 1)
    qs = []
    for hd in range(hps):
        qh = q[:, hd * LANES:(hd + 1) * LANES]
        zero = jnp.zeros_like(qh)
        qs += [jnp.where(lane < DIFF_HEAD_DIM, qh, zero), jnp.where(lane < DIFF_HEAD_DIM, zero, qh)]

    def score_fn(c, t):
        rows = pl.ds(pl.multiple_of(t * tk, tk), tk)
        hd = c // 2
        return _dot_nt(k_ref[rows, hd * LANES:(hd + 1) * LANES], qs[c])

    def value_fn(c, t):
        hd = c // 2
        return vt_ref[t, hd * DIFF_V_DIM:(hd + 1) * DIFF_V_DIM, :]

    out = _flash_pipeline(pl.program_id(2), scratch_refs, score_fn, value_fn, tk, tq)

    lam = (jnp.exp(jnp.sum(lq1_ref[...] * lk1_ref[...], axis=-1, keepdims=True))
           - jnp.exp(jnp.sum(lq2_ref[...] * lk2_ref[...], axis=-1, keepdims=True)) + lam_init)
    heads = []
    for hd in range(hps):
        (l0, a0), (l1, a1) = out[2 * hd], out[2 * hd + 1]
        ot = a0[...] / l0[...] - lam * (a1[...] / l1[...])
        ot = ot * lax.rsqrt(jnp.sum(ot * ot, axis=0, keepdims=True) * (1.0 / DIFF_V_DIM) + EPS)
        heads.append(ot * subln_ref[...] * (1.0 - lam_init))
    o_ref[...] = jnp.concatenate(heads, axis=0).T.astype(BF16)


def _diff_call(qd, kd, vtd, lq1, lk1, lq2, lk2, subln_col, lam_init, batch, seq_len):
    n = qd.shape[0]
    tq, tk, hps = ATTN_Q_ROWS, ATTN_K_ROWS, DIFF_HEADS_PER_STEP
    qt = seq_len // tq
    small = lambda a: pl.BlockSpec(a.shape, lambda b, h, i: (0, 0))
    return pl.pallas_call(
        functools.partial(_diff_kernel, lam_init),
        grid=(batch, DIFF_HEADS // hps, qt),
        in_specs=[pl.BlockSpec((tq, hps * LANES), lambda b, h, i: (b * qt + i, h)),
                  pl.BlockSpec((seq_len, hps * LANES), lambda b, h, i: (b, h)),
                  pl.BlockSpec((seq_len // tk, hps * DIFF_V_DIM, tk), lambda b, h, i: (b, h, 0)),
                  small(lq1), small(lk1), small(lq2), small(lk2), small(subln_col)],
        out_specs=pl.BlockSpec((tq, hps * LANES), lambda b, h, i: (b * qt + i, h)),
        out_shape=jax.ShapeDtypeStruct((n, DIFF_V_WIDTH), BF16),
        scratch_shapes=_attn_scratch(2 * hps, DIFF_V_DIM, tq, tk),
        compiler_params=pltpu.CompilerParams(dimension_semantics=("parallel", "parallel", "arbitrary"),
                                             vmem_limit_bytes=VMEM_LIMIT_BYTES),
        name="diff_attn",
    )(qd, kd, vtd, lq1, lk1, lq2, lk2, subln_col)


def _merge_kernel(x_ref, om_ref, od_ref, sgm_ref, sgd_ref, wmu_ref, wdu_ref, wout_ref, gffn_ref, wr_ref, br_ref,
                  x1_ref, h2_ref, comb_ref):
    merged = (sgm_ref[...].astype(F32) * _dot(om_ref[...], wmu_ref[...])
              + sgd_ref[...].astype(F32) * _dot(od_ref[...], wdu_ref[...]))
    x1 = x_ref[...] + _dot(merged.astype(BF16), wout_ref[...])
    x1_ref[...] = x1
    h2 = _rms(x1, x1.shape[-1]) * gffn_ref[...]
    h2_ref[...] = h2.astype(BF16)

    logits = jnp.dot(h2, wr_ref[...], preferred_element_type=F32, precision=lax.Precision.HIGHEST) + br_ref[...]
    lane = lax.broadcasted_iota(jnp.int32, logits.shape, 1)
    neg = -jnp.inf
    big = jnp.int32(1 << 20)

    def top(vals):
        mx = jnp.max(vals, axis=-1, keepdims=True)
        idx = jnp.min(jnp.where(vals == mx, lane, big), axis=-1, keepdims=True)
        return mx, idx

    gl = jnp.where((lane >= N_EXPERTS) & (lane < N_EXPERTS + N_GROUPS), logits, neg)
    gmax, gidx = top(gl)
    pg_sel = 1.0 / jnp.sum(jnp.exp(gl - gmax), axis=-1, keepdims=True)
    el = jnp.where((lane < N_EXPERTS) & (lane // EXPERTS_PER_GROUP == gidx - N_EXPERTS), logits, neg)
    m1, i1 = top(el)
    m2, i2 = top(jnp.where(lane == i1, neg, el))
    e2 = jnp.exp(m2 - m1)
    w1 = pg_sel / (1.0 + e2)
    comb_ref[...] = jnp.where(lane == i1, w1, 0.0) + jnp.where(lane == i2, w1 * e2, 0.0)


def _merge_call(x2, om, od, sgm, sgd, p):
    n, d = x2.shape
    tm = MERGE_ROWS
    row = lambda i: (i, 0)
    const = lambda i: (0, 0)
    weights = [p["wmu"], p["wdu"], p["wout"], p["gffn"], p["wr"], p["br"]]
    return pl.pallas_call(
        _merge_kernel,
        grid=(n // tm,),
        in_specs=([pl.BlockSpec((tm, a.shape[1]), row) for a in (x2, om, od, sgm, sgd)]
                  + [pl.BlockSpec(w.shape, const) for w in weights]),
        out_specs=[pl.BlockSpec((tm, d), row), pl.BlockSpec((tm, d), row), pl.BlockSpec((tm, LANES), row)],
        out_shape=[jax.ShapeDtypeStruct((n, d), F32), jax.ShapeDtypeStruct((n, d), BF16),
                   jax.ShapeDtypeStruct((n, LANES), F32)],
        compiler_params=pltpu.CompilerParams(dimension_semantics=("parallel",), vmem_limit_bytes=VMEM_LIMIT_BYTES),
        name="merge_router",
    )(x2, om, od, sgm, sgd, *weights)


def _moe_kernel(x1_ref, h2_ref, comb_ref, wg_ref, wu_ref, wd_ref, o_ref):
    e = pl.program_id(1)

    @pl.when(e == 0)
    def _():
        o_ref[...] = x1_ref[...]

    h2 = h2_ref[...]
    comb = comb_ref[...]
    lane = lax.broadcasted_iota(jnp.int32, comb.shape, 1)
    c = jnp.sum(jnp.where(lane == e, comb, 0.0), axis=-1, keepdims=True)
    gate = _dot(h2, wg_ref[0])
    up = _dot(h2, wu_ref[0])
    hidden = (gate * jax.nn.sigmoid(gate) * up).astype(BF16)
    o_ref[...] += c * _dot(hidden, wd_ref[0])


def _moe_call(x1, h2, comb, wg, wu, wd):
    n, d = x1.shape
    tm = MOE_ROWS
    row = lambda i, e: (i, 0)
    wsel = lambda i, e: (e, 0, 0)
    return pl.pallas_call(
        _moe_kernel,
        grid=(n // tm, N_EXPERTS),
        in_specs=[pl.BlockSpec((tm, d), row), pl.BlockSpec((tm, d), row), pl.BlockSpec((tm, LANES), row),
                  pl.BlockSpec((1, d, EXPERT_FF), wsel), pl.BlockSpec((1, d, EXPERT_FF), wsel),
                  pl.BlockSpec((1, EXPERT_FF, d), wsel)],
        out_specs=pl.BlockSpec((tm, d), row),
        out_shape=jax.ShapeDtypeStruct((n, d), F32),
        compiler_params=pltpu.CompilerParams(dimension_semantics=("parallel", "arbitrary"),
                                             vmem_limit_bytes=VMEM_LIMIT_BYTES),
        name="moe",
    )(x1, h2, comb, wg, wu, wd)


def _rope_lane_tables(seq_len, rot_dim, period, first):
    half = rot_dim // 2
    pos = jnp.arange(seq_len, dtype=F32)
    inv = 1.0 / (ROPE_THETA ** (jnp.arange(0, rot_dim, 2, dtype=F32) / rot_dim))
    ang = pos[:, None] * inv[None, :]
    cos, sin = jnp.cos(ang), jnp.sin(ang)
    lane = jnp.arange(LANES)
    rel = (lane % period) - first
    active = (rel >= 0) & (rel < rot_dim)
    idx = jnp.clip(rel, 0, rot_dim - 1) % half
    sign = jnp.where(rel < half, -1.0, 1.0)
    c = jnp.where(active[None, :], cos[:, idx], 1.0)
    s = jnp.where(active[None, :], sin[:, idx] * sign[None, :], 0.0)
    return c.astype(F32), s.astype(F32)


def _head_pad(w, heads, width):
    r = w.shape[0]
    w = w.reshape(r, heads, width)
    return jnp.pad(w, ((0, 0), (0, 0), (0, LANES - width))).reshape(r, heads * LANES)


def _layer_params(l, seq_len, norm_mix, w_in, mla_q_latent_norm, w_mla_uq, mla_kv_latent_norm, w_mla_ukv,
                  mla_q_gain, mla_k_gain, diff_q_gain, diff_k_gain, w_mla_up, w_diff_up, w_out, norm_ffn,
                  w_router_group, b_router_group, w_router_expert, b_router_expert):
    d = w_in.shape[1]
    sizes = (MLA_Q_RANK, MLA_KV_RANK, MLA_ROPE, DIFF_QK_WIDTH, DIFF_QK_WIDTH, DIFF_V_WIDTH, d, d)
    offs = [0]
    for s in sizes:
        offs.append(offs[-1] + s)
    wi = w_in[l]
    seg = [wi[:, offs[k]:offs[k + 1]] for k in range(len(sizes))]
    row = lambda g: g.astype(F32)[None, :]
    p = {}
    p["gmix"] = row(norm_mix[l])
    p["wql"] = seg[0].astype(BF16)
    p["wkvl"] = seg[1].astype(BF16)
    p["wkr"] = jnp.pad(seg[2], ((0, 0), (MLA_NOPE, LANES - MLA_QK))).astype(BF16)
    p["wdq"], p["wdk"] = seg[3].astype(BF16), seg[4].astype(BF16)
    p["wdvt"] = seg[5].T.astype(BF16)
    p["wgm"], p["wgd"] = seg[6].astype(BF16), seg[7].astype(BF16)
    p["gql"] = row(mla_q_latent_norm[l])
    p["wuq"] = _head_pad(w_mla_uq[l], MLA_HEADS, MLA_QK).astype(BF16)
    p["gkvl"] = row(mla_kv_latent_norm[l])
    ukv = w_mla_ukv[l].reshape(MLA_KV_RANK, MLA_HEADS, MLA_NOPE + MLA_V)
    p["wuk"] = _head_pad(ukv[:, :, :MLA_NOPE].reshape(MLA_KV_RANK, -1), MLA_HEADS, MLA_NOPE).astype(BF16)
    p["wuvt"] = ukv[:, :, MLA_NOPE:].reshape(MLA_KV_RANK, -1).T.astype(BF16)
    p["gq"] = jnp.pad(mla_q_gain[l].astype(F32), (0, LANES - MLA_QK))[None, :]
    p["gk"] = jnp.pad(mla_k_gain[l].astype(F32), (0, LANES - MLA_QK))[None, :]
    p["gdq"] = jnp.tile(diff_q_gain[l].astype(F32), 2)[None, :]
    p["gdk"] = jnp.tile(diff_k_gain[l].astype(F32), 2)[None, :]
    p["cm"], p["sm"] = _rope_lane_tables(seq_len, MLA_ROPE, LANES, MLA_NOPE)
    p["cd"], p["sd"] = _rope_lane_tables(seq_len, DIFF_ROPE, DIFF_HEAD_DIM, 0)
    p["wmu"] = w_mla_up[l].astype(BF16)
    p["wdu"] = w_diff_up[l].astype(BF16)
    p["wout"] = w_out[l].astype(BF16)
    p["gffn"] = row(norm_ffn[l])
    wr = jnp.concatenate([w_router_expert[l], w_router_group[l]], axis=1).astype(F32)
    p["wr"] = jnp.pad(wr, ((0, 0), (0, LANES - wr.shape[1])))
    br = jnp.concatenate([b_router_expert[l], b_router_group[l]]).astype(F32)
    p["br"] = jnp.pad(br, (0, LANES - br.shape[0]))[None, :]
    return p


def kernel(x, norm_mix, w_in, mla_q_latent_norm, w_mla_uq, mla_kv_latent_norm, w_mla_ukv, mla_q_gain, mla_k_gain, diff_q_gain, diff_k_gain, lambda_q1, lambda_k1, lambda_q2, lambda_k2, diff_subln, w_mla_up, w_diff_up, w_out, norm_ffn, w_router_group, b_router_group, w_router_expert, b_router_expert, w_expert_gate, w_expert_up, w_expert_down):
    batch, seq_len, d = x.shape
    x2 = x.reshape(batch * seq_len, d)
    row = lambda g: g.astype(F32)[None, :]
    for l in range(norm_mix.shape[0]):
        lam_init = 0.8 - 0.6 * math.exp(-0.3 * l)
        p = _layer_params(l, seq_len, norm_mix, w_in, mla_q_latent_norm, w_mla_uq, mla_kv_latent_norm, w_mla_ukv,
                          mla_q_gain, mla_k_gain, diff_q_gain, diff_k_gain, w_mla_up, w_diff_up, w_out, norm_ffn,
                          w_router_group, b_router_group, w_router_expert, b_router_expert)
        qm, km, vtm, qd, kd, vtd, sgm, sgd = _proj_call(x2, seq_len, p)
        om = _mla_call(qm, km, vtm, batch, seq_len)
        od = _diff_call(qd, kd, vtd, row(lambda_q1[l]), row(lambda_k1[l]), row(lambda_q2[l]), row(lambda_k2[l]),
                        diff_subln[l].astype(F32)[:, None], lam_init, batch, seq_len)
        x1, h2, comb = _merge_call(x2, om, od, sgm, sgd, p)
        x2 = _moe_call(x1, h2, comb, w_expert_gate[l].astype(BF16), w_expert_up[l].astype(BF16),
                       w_expert_down[l].astype(BF16))
    return x2.reshape(batch, seq_len, d)
```

```python
import functools
import math

import jax
import jax.numpy as jnp
from jax import lax
from jax.experimental import pallas as pl
from jax.experimental.pallas import tpu as pltpu

CHUNK = 64
ROPE_THETA = 500000.0
EPS = 1e-6

MLA_HEADS = 8
MLA_NOPE = 64
MLA_ROPE = 32
MLA_V = 64
MLA_QK = MLA_NOPE + MLA_ROPE
MLA_Q_RANK = 256
MLA_KV_RANK = 128

DIFF_HEADS = 4
DIFF_HEAD_DIM = 64
DIFF_V_DIM = 2 * DIFF_HEAD_DIM
DIFF_ROPE = DIFF_HEAD_DIM // 4
DIFF_QK_WIDTH = DIFF_HEADS * 2 * DIFF_HEAD_DIM
DIFF_V_WIDTH = DIFF_HEADS * DIFF_V_DIM

N_GROUPS = 4
EXPERTS_PER_GROUP = 8
N_EXPERTS = N_GROUPS * EXPERTS_PER_GROUP
EXPERT_FF = 256

LANES = 128
VMEM_LIMIT_BYTES = 48 * 1024 * 1024

PROJ_ROWS = 512
ATTN_Q_ROWS = 256
ATTN_K_ROWS = 256
MERGE_ROWS = 512
ROUTE_ROWS = MERGE_ROWS
SEG_ALIGN = 16
SORT_ROWS = 2 * ROUTE_ROWS + N_EXPERTS * SEG_ALIGN
EXPERT_ROWS = 256
MLA_HEADS_PER_STEP = 4
DIFF_HEADS_PER_STEP = 2
LOG2E = 1.4426950408889634

BF16 = jnp.bfloat16
F32 = jnp.float32


def _dot(a, b):
    return jnp.dot(a, b, preferred_element_type=F32)


def _dot_nt(a, b):
    return lax.dot_general(a, b, (((1,), (1,)), ((), ())), preferred_element_type=F32)


def _rms(x, width):
    return x * lax.rsqrt(jnp.sum(x * x, axis=-1, keepdims=True) * (1.0 / width) + EPS)


def _rotate_pairs(y, half, cos, sin):
    lane = lax.broadcasted_iota(jnp.int32, y.shape, 1)
    up = pltpu.roll(y, LANES - half, 1)
    down = pltpu.roll(y, half, 1)
    partner = jnp.where((lane // half) % 2 == 0, up, down)
    return y * cos + partner * sin


def _store_k_tiles(o_ref, vt):
    tk = o_ref.shape[-1]
    for c in range(o_ref.shape[0]):
        o_ref[c] = vt[:, c * tk:(c + 1) * tk].astype(BF16)


def _proj_kernel(x_ref, gmix_ref, wql_ref, wkvl_ref, wkr_ref, wdq_ref, wdk_ref, wdvt_ref, wgm_ref, wgd_ref,
                 gql_ref, wuq_ref, gkvl_ref, wuk_ref, wuvt_ref, gq_ref, gk_ref, gdq_ref, gdk_ref,
                 cm_ref, sm_ref, cd_ref, sd_ref,
                 qm_ref, km_ref, vtm_ref, qd_ref, kd_ref, vtd_ref, sgm_ref, sgd_ref):
    x = x_ref[...]
    h = (_rms(x, x.shape[-1]) * gmix_ref[...]).astype(BF16)

    cm, sm = cm_ref[...], sm_ref[...]
    cd, sd = cd_ref[...], sd_ref[...]

    ql = _rms(_dot(h, wql_ref[...]), MLA_Q_RANK) * gql_ref[...]
    q = _dot(ql.astype(BF16), wuq_ref[...])
    gq = gq_ref[...]
    for hd in range(MLA_HEADS):
        sl = slice(hd * LANES, (hd + 1) * LANES)
        y = _rms(q[:, sl], MLA_QK) * gq
        y = _rotate_pairs(y, MLA_ROPE // 2, cm, sm) * (LOG2E * MLA_QK ** -0.5)
        qm_ref[:, sl] = y.astype(BF16)

    kvl = (_rms(_dot(h, wkvl_ref[...]), MLA_KV_RANK) * gkvl_ref[...]).astype(BF16)
    kr = _dot(h, wkr_ref[...])
    kn = _dot(kvl, wuk_ref[...])
    _store_k_tiles(vtm_ref, _dot_nt(wuvt_ref[...], kvl))
    gk = gk_ref[...]
    for hd in range(MLA_HEADS):
        sl = slice(hd * LANES, (hd + 1) * LANES)
        y = _rms(kn[:, sl] + kr, MLA_QK) * gk
        km_ref[:, sl] = _rotate_pairs(y, MLA_ROPE // 2, cm, sm).astype(BF16)

    def diff_qk(w_ref, g_ref, o_ref, scale):
        t = _dot(h, w_ref[...])
        g = g_ref[...]
        for hd in range(DIFF_HEADS):
            sl = slice(hd * LANES, (hd + 1) * LANES)
            th = t[:, sl]
            lane = lax.broadcasted_iota(jnp.int32, th.shape, 1)
            sq = th * th
            lo = jnp.sum(jnp.where(lane < DIFF_HEAD_DIM, sq, 0.0), axis=-1, keepdims=True)
            tot = jnp.sum(sq, axis=-1, keepdims=True)
            ss = jnp.where(lane < DIFF_HEAD_DIM, lo, tot - lo)
            y = th * lax.rsqrt(ss * (1.0 / DIFF_HEAD_DIM) + EPS) * g
            o_ref[:, sl] = (_rotate_pairs(y, DIFF_ROPE // 2, cd, sd) * scale).astype(BF16)

    diff_qk(wdq_ref, gdq_ref, qd_ref, LOG2E * DIFF_HEAD_DIM ** -0.5)
    diff_qk(wdk_ref, gdk_ref, kd_ref, 1.0)
    _store_k_tiles(vtd_ref, _dot_nt(wdvt_ref[...], h))

    sgm_ref[...] = jax.nn.sigmoid(_dot(h, wgm_ref[...])).astype(BF16)
    sgd_ref[...] = jax.nn.sigmoid(_dot(h, wgd_ref[...])).astype(BF16)


def _proj_call(x2, seq_len, p):
    n, d = x2.shape
    tm = PROJ_ROWS
    pos_blocks = seq_len // tm
    row = lambda i: (i, 0)
    const = lambda i: (0, 0)
    pos = lambda i: (i % pos_blocks, 0)
    weights = [p["gmix"], p["wql"], p["wkvl"], p["wkr"], p["wdq"], p["wdk"], p["wdvt"], p["wgm"], p["wgd"],
               p["gql"], p["wuq"], p["gkvl"], p["wuk"], p["wuvt"], p["gq"], p["gk"], p["gdq"], p["gdk"]]
    tables = [p["cm"], p["sm"], p["cd"], p["sd"]]
    in_specs = ([pl.BlockSpec((tm, d), row)]
                + [pl.BlockSpec(w.shape, const) for w in weights]
                + [pl.BlockSpec((tm, LANES), pos) for _ in tables])
    tk = ATTN_K_ROWS
    widths = [MLA_HEADS * LANES, MLA_HEADS * LANES, -MLA_HEADS * MLA_V, DIFF_QK_WIDTH, DIFF_QK_WIDTH, -DIFF_V_WIDTH, d, d]
    out_specs = [pl.BlockSpec((tm, w), row) if w > 0 else pl.BlockSpec((tm // tk, -w, tk), lambda i: (i, 0, 0))
                 for w in widths]
    out_shape = [jax.ShapeDtypeStruct((n, w) if w > 0 else (n // tk, -w, tk), BF16) for w in widths]
    return pl.pallas_call(
        _proj_kernel,
        grid=(n // tm,),
        in_specs=in_specs,
        out_specs=out_specs,
        out_shape=out_shape,
        compiler_params=pltpu.CompilerParams(dimension_semantics=("parallel",), vmem_limit_bytes=VMEM_LIMIT_BYTES),
        name="proj",
    )(x2, *weights, *tables)


def _chunk_mask_t(tk, tq):
    kc = lax.broadcasted_iota(jnp.int32, (tk, tq), 0) // CHUNK
    qc = lax.broadcasted_iota(jnp.int32, (tk, tq), 1) // CHUNK
    return kc <= qc


def _softmax_step_t(st, vt, m_ref, l_ref, acc_ref):
    m_prev = m_ref[...]
    m_new = jnp.maximum(m_prev, jnp.max(st, axis=0, keepdims=True))
    alpha = jnp.exp2(m_prev - m_new)
    pr = jnp.exp2(st - m_new)
    l_ref[...] = alpha * l_ref[...] + jnp.sum(pr, axis=0, keepdims=True)
    acc_ref[...] = alpha * acc_ref[...] + _dot(vt, pr.astype(BF16))
    m_ref[...] = m_new


def _attn_scratch(chains, dv, tq, tk):
    per_chain = [pltpu.VMEM((1, tq), F32), pltpu.VMEM((1, tq), F32), pltpu.VMEM((dv, tq), F32),
                 pltpu.VMEM((tk, tq), F32), pltpu.VMEM((tk, tq), F32)]
    return per_chain * chains


def _flash_pipeline(last_tile, scratch_refs, score_fn, value_fn, tk, tq):
    n_chains = len(scratch_refs) // 5
    chains = [scratch_refs[5 * c:5 * c + 5] for c in range(n_chains)]
    for m_ref, l_ref, acc_ref, _, _ in chains:
        m_ref[...] = jnp.full(m_ref.shape, -jnp.inf, F32)
        l_ref[...] = jnp.zeros(l_ref.shape, F32)
        acc_ref[...] = jnp.zeros(acc_ref.shape, F32)

    def scores(t, slot):
        for c, ch in enumerate(chains):
            ch[3 + slot][...] = score_fn(c, t)

    def update(t, slot, masked):
        for c, ch in enumerate(chains):
            st = ch[3 + slot][...]
            if masked:
                st = jnp.where(_chunk_mask_t(tk, tq), st, -jnp.inf)
            _softmax_step_t(st, value_fn(c, t), ch[0], ch[1], ch[2])

    scores(0, 0)

    def pair(p, carry):
        t = 2 * p
        scores(t + 1, 1)
        update(t, 0, False)
        scores(t + 2, 0)
        update(t + 1, 1, False)
        return carry

    lax.fori_loop(0, last_tile // 2, pair, 0)

    @pl.when(last_tile % 2 == 0)
    def _():
        update(last_tile, 0, True)

    @pl.when(last_tile % 2 == 1)
    def _():
        scores(last_tile, 1)
        update(last_tile - 1, 0, False)
        update(last_tile, 1, True)

    return [(ch[1], ch[2]) for ch in chains]


def _mla_kernel(q_ref, k_ref, vt_ref, o_ref, *scratch_refs):
    tq, tk = ATTN_Q_ROWS, ATTN_K_ROWS

    def score_fn(c, t):
        rows = pl.ds(pl.multiple_of(t * tk, tk), tk)
        sl = slice(c * LANES, (c + 1) * LANES)
        return _dot_nt(k_ref[rows, sl], q_ref[:, sl])

    def value_fn(c, t):
        return vt_ref[t, c * MLA_V:(c + 1) * MLA_V, :]

    out = _flash_pipeline(pl.program_id(2), scratch_refs, score_fn, value_fn, tk, tq)
    ot = jnp.concatenate([acc_ref[...] / l_ref[...] for l_ref, acc_ref in out], axis=0)
    o_ref[...] = ot.T.astype(BF16)


def _mla_call(qm, km, vtm, batch, seq_len):
    n = qm.shape[0]
    tq, tk, hps = ATTN_Q_ROWS, ATTN_K_ROWS, MLA_HEADS_PER_STEP
    qt = seq_len // tq
    return pl.pallas_call(
        _mla_kernel,
        grid=(batch, MLA_HEADS // hps, qt),
        in_specs=[pl.BlockSpec((tq, hps * LANES), lambda b, h, i: (b * qt + i, h)),
                  pl.BlockSpec((seq_len, hps * LANES), lambda b, h, i: (b, h)),
                  pl.BlockSpec((seq_len // tk, hps * MLA_V, tk), lambda b, h, i: (b, h, 0))],
        out_specs=pl.BlockSpec((tq, hps * MLA_V), lambda b, h, i: (b * qt + i, h)),
        out_shape=jax.ShapeDtypeStruct((n, MLA_HEADS * MLA_V), BF16),
        scratch_shapes=_attn_scratch(hps, MLA_V, tq, tk),
        compiler_params=pltpu.CompilerParams(dimension_semantics=("parallel", "parallel", "arbitrary"),
                                             vmem_limit_bytes=VMEM_LIMIT_BYTES),
        name="mla_attn",
    )(qm, km, vtm)


def _diff_kernel(lam_init, q_ref, k_ref, vt_ref, lq1_ref, lk1_ref, lq2_ref, lk2_ref, subln_ref, o_ref,
                 *scratch_refs):
    tq, tk = ATTN_Q_ROWS, ATTN_K_ROWS
    hps = DIFF_HEADS_PER_STEP

    q = q_ref[...]
    lane = lax.broadcasted_iota(jnp.int32, (tq, LANES), 1)
    qs = []
    for hd in range(hps):
        qh = q[:, hd * LANES:(hd + 1) * LANES]
        zero = jnp.zeros_like(qh)
        qs += [jnp.where(lane < DIFF_HEAD_DIM, qh, zero), jnp.where(lane < DIFF_HEAD_DIM, zero, qh)]

    def score_fn(c, t):
        rows = pl.ds(pl.multiple_of(t * tk, tk), tk)
        hd = c // 2
        return _dot_nt(k_ref[rows, hd * LANES:(hd + 1) * LANES], qs[c])

    def value_fn(c, t):
        hd = c // 2
        return vt_ref[t, hd * DIFF_V_DIM:(hd + 1) * DIFF_V_DIM, :]

    out = _flash_pipeline(pl.program_id(2), scratch_refs, score_fn, value_fn, tk, tq)

    lam = (jnp.exp(jnp.sum(lq1_ref[...] * lk1_ref[...], axis=-1, keepdims=True))
           - jnp.exp(jnp.sum(lq2_ref[...] * lk2_ref[...], axis=-1, keepdims=True)) + lam_init)
    heads = []
    for hd in range(hps):
        (l0, a0), (l1, a1) = out[2 * hd], out[2 * hd + 1]
        ot = a0[...] / l0[...] - lam * (a1[...] / l1[...])
        ot = ot * lax.rsqrt(jnp.sum(ot * ot, axis=0, keepdims=True) * (1.0 / DIFF_V_DIM) + EPS)
        heads.append(ot * subln_ref[...] * (1.0 - lam_init))
    o_ref[...] = jnp.concatenate(heads, axis=0).T.astype(BF16)


def _diff_call(qd, kd, vtd, lq1, lk1, lq2, lk2, subln_col, lam_init, batch, seq_len):
    n = qd.shape[0]
    tq, tk, hps = ATTN_Q_ROWS, ATTN_K_ROWS, DIFF_HEADS_PER_STEP
    qt = seq_len // tq
    small = lambda a: pl.BlockSpec(a.shape, lambda b, h, i: (0, 0))
    return pl.pallas_call(
        functools.partial(_diff_kernel, lam_init),
        grid=(batch, DIFF_HEADS // hps, qt),
        in_specs=[pl.BlockSpec((tq, hps * LANES), lambda b, h, i: (b * qt + i, h)),
                  pl.BlockSpec((seq_len, hps * LANES), lambda b, h, i: (b, h)),
                  pl.BlockSpec((seq_len // tk, hps * DIFF_V_DIM, tk), lambda b, h, i: (b, h, 0)),
                  small(lq1), small(lk1), small(lq2), small(lk2), small(subln_col)],
        out_specs=pl.BlockSpec((tq, hps * LANES), lambda b, h, i: (b * qt + i, h)),
        out_shape=jax.ShapeDtypeStruct((n, DIFF_V_WIDTH), BF16),
        scratch_shapes=_attn_scratch(2 * hps, DIFF_V_DIM, tq, tk),
        compiler_params=pltpu.CompilerParams(dimension_semantics=("parallel", "parallel", "arbitrary"),
                                             vmem_limit_bytes=VMEM_LIMIT_BYTES),
        name="diff_attn",
    )(qd, kd, vtd, lq1, lk1, lq2, lk2, subln_col)


def _merge_kernel(x_ref, om_ref, od_ref, sgm_ref, sgd_ref, wmu_ref, wdu_ref, wout_ref, gffn_ref, wr_ref, br_ref,
                  x1_ref, h2_ref, route_ref, route_t_ref, cnt_ref):
    merged = (sgm_ref[...].astype(F32) * _dot(om_ref[...], wmu_ref[...])
              + sgd_ref[...].astype(F32) * _dot(od_ref[...], wdu_ref[...]))
    x1 = x_ref[...] + _dot(merged.astype(BF16), wout_ref[...])
    x1_ref[...] = x1
    h2 = _rms(x1, x1.shape[-1]) * gffn_ref[...]
    h2_ref[...] = h2.astype(BF16)

    logits = jnp.dot(h2, wr_ref[...], preferred_element_type=F32, precision=lax.Precision.HIGHEST) + br_ref[...]
    lane = lax.broadcasted_iota(jnp.int32, logits.shape, 1)
    neg = -jnp.inf
    big = jnp.int32(1 << 20)

    def top(vals):
        mx = jnp.max(vals, axis=-1, keepdims=True)
        idx = jnp.min(jnp.where(vals == mx, lane, big), axis=-1, keepdims=True)
        return mx, idx

    gl = jnp.where((lane >= N_EXPERTS) & (lane < N_EXPERTS + N_GROUPS), logits, neg)
    gmax, gidx = top(gl)
    pg_sel = 1.0 / jnp.sum(jnp.exp(gl - gmax), axis=-1, keepdims=True)
    el = jnp.where((lane < N_EXPERTS) & (lane // EXPERTS_PER_GROUP == gidx - N_EXPERTS), logits, neg)
    m1, i1 = top(el)
    m2, i2 = top(jnp.where(lane == i1, neg, el))
    e2 = jnp.exp(m2 - m1)
    w1 = pg_sel / (1.0 + e2)
    w2 = w1 * e2

    tm = logits.shape[0]
    sel = (lane == i1) | (lane == i2)
    earlier = (lax.broadcasted_iota(jnp.int32, (tm, tm), 1) < lax.broadcasted_iota(jnp.int32, (tm, tm), 0))
    rank = _dot(jnp.where(earlier, 1.0, 0.0).astype(BF16), jnp.where(sel, 1.0, 0.0).astype(BF16))
    cnt = jnp.sum(jnp.where(sel, 1.0, 0.0), axis=0, keepdims=True)
    seg = jnp.floor((cnt + (SEG_ALIGN - 1)) * (1.0 / SEG_ALIGN))
    before = (lax.broadcasted_iota(jnp.int32, (LANES, LANES), 0) < lax.broadcasted_iota(jnp.int32, (LANES, LANES), 1))
    off = _dot(jnp.broadcast_to(seg, (8, LANES)).astype(BF16), jnp.where(before, 1.0, 0.0).astype(BF16))[0:1] * SEG_ALIGN
    dest = off + rank
    d1 = jnp.sum(jnp.where(lane == i1, dest, 0.0), axis=-1, keepdims=True)
    d2 = jnp.sum(jnp.where(lane == i2, dest, 0.0), axis=-1, keepdims=True)
    route = jnp.where(lane == 0, d1, jnp.where(lane == 1, d2, jnp.where(lane == 2, w1, jnp.where(lane == 3, w2, 0.0))))
    route_ref[...] = route
    route_t_ref[0] = route.T[0:8, :]
    cnt_ref[0] = seg * SEG_ALIGN


def _merge_call(x2, om, od, sgm, sgd, p):
    n, d = x2.shape
    tm = MERGE_ROWS
    row = lambda i: (i, 0)
    const = lambda i: (0, 0)
    weights = [p["wmu"], p["wdu"], p["wout"], p["gffn"], p["wr"], p["br"]]
    return pl.pallas_call(
        _merge_kernel,
        grid=(n // tm,),
        in_specs=([pl.BlockSpec((tm, a.shape[1]), row) for a in (x2, om, od, sgm, sgd)]
                  + [pl.BlockSpec(w.shape, const) for w in weights]),
        out_specs=[pl.BlockSpec((tm, d), row), pl.BlockSpec((tm, d), row), pl.BlockSpec((tm, LANES), row),
                   pl.BlockSpec((1, 8, tm), lambda i: (i, 0, 0)), pl.BlockSpec((1, 1, LANES), lambda i: (i, 0, 0))],
        out_shape=[jax.ShapeDtypeStruct((n, d), F32), jax.ShapeDtypeStruct((n, d), BF16),
                   jax.ShapeDtypeStruct((n, LANES), F32), jax.ShapeDtypeStruct((n // tm, 8, tm), F32),
                   jax.ShapeDtypeStruct((n // tm, 1, LANES), F32)],
        compiler_params=pltpu.CompilerParams(dimension_semantics=("parallel",), vmem_limit_bytes=VMEM_LIMIT_BYTES),
        name="merge_router",
    )(x2, om, od, sgm, sgd, *weights)


def _segment_copies(i, seg_dst_ref, seg_rows_ref, tile_off_ref, global_ref, tile_ref, sem, to_global):
    def body(e, carry):
        k = i * N_EXPERTS + e
        rows = pl.multiple_of(seg_rows_ref[k], SEG_ALIGN)

        @pl.when(rows > 0)
        def _():
            g = global_ref.at[pl.ds(pl.multiple_of(seg_dst_ref[k], SEG_ALIGN), rows)]
            t = tile_ref.at[pl.ds(pl.multiple_of(tile_off_ref[k], SEG_ALIGN), rows)]
            src, dst = (t, g) if to_global else (g, t)
            pltpu.make_async_copy(src, dst, sem).start()

        return carry

    lax.fori_loop(0, N_EXPERTS, body, 0)


def _wait_rows(tile_ref, rows, sem):
    @pl.when(rows > 0)
    def _():
        view = tile_ref.at[pl.ds(0, pl.multiple_of(rows, SEG_ALIGN))]
        pltpu.make_async_copy(view, view, sem).wait()


def _zero_unused_rows(tail_dst_ref, tail_rows_ref, n_used_ref, xs_ref, zero_ref, sem, start):
    n_tiles = xs_ref.shape[0] // EXPERT_ROWS
    if start:
        zero_ref[...] = jnp.zeros(zero_ref.shape, BF16)

    def tail(e, total):
        rows = pl.multiple_of(tail_rows_ref[e], SEG_ALIGN)
        if start:
            @pl.when(rows > 0)
            def _():
                dst = xs_ref.at[pl.ds(pl.multiple_of(tail_dst_ref[e], SEG_ALIGN), rows)]
                pltpu.make_async_copy(zero_ref.at[pl.ds(0, rows)], dst, sem).start()

        return total + rows

    total = lax.fori_loop(0, N_EXPERTS, tail, 0)
    if not start:
        _wait_rows(xs_ref, total + (n_tiles - n_used_ref[0]) * EXPERT_ROWS, sem)
        return

    def unused(t, carry):
        dst = xs_ref.at[pl.ds(pl.multiple_of(t * EXPERT_ROWS, EXPERT_ROWS), EXPERT_ROWS)]
        pltpu.make_async_copy(zero_ref, dst, sem).start()
        return carry

    lax.fori_loop(n_used_ref[0], n_tiles, unused, 0)


def _sort_kernel(seg_dst_ref, seg_rows_ref, tile_off_ref, tile_rows_ref, tail_dst_ref, tail_rows_ref, n_used_ref,
                 h2_ref, route_t_ref, xs_ref, sorted_ref, zero_ref, sem, zero_sem):
    i = pl.program_id(0)
    tm = h2_ref.shape[0]

    @pl.when(i == 0)
    def _():
        _zero_unused_rows(tail_dst_ref, tail_rows_ref, n_used_ref, xs_ref, zero_ref, zero_sem, True)

    d1 = route_t_ref[0, 0:1, :].astype(jnp.int32)
    d2 = route_t_ref[0, 1:2, :].astype(jnp.int32)
    r = lax.broadcasted_iota(jnp.int32, (SORT_ROWS, tm), 0)
    perm = jnp.where((r == d1) | (r == d2), 1.0, 0.0).astype(BF16)
    sorted_ref[...] = _dot(perm, h2_ref[...]).astype(BF16)
    _segment_copies(i, seg_dst_ref, seg_rows_ref, tile_off_ref, xs_ref, sorted_ref, sem, True)
    _wait_rows(sorted_ref, tile_rows_ref[i], sem)

    @pl.when(i == pl.num_programs(0) - 1)
    def _():
        _zero_unused_rows(tail_dst_ref, tail_rows_ref, n_used_ref, xs_ref, zero_ref, zero_sem, False)


def _sort_call(h2, route_t, sched, max_rows):
    n, d = h2.shape
    tm = ROUTE_ROWS
    return pl.pallas_call(
        _sort_kernel,
        grid_spec=pltpu.PrefetchScalarGridSpec(
            num_scalar_prefetch=7,
            grid=(n // tm,),
            in_specs=[pl.BlockSpec((tm, d), lambda i, *_: (i, 0)),
                      pl.BlockSpec((1, 8, tm), lambda i, *_: (i, 0, 0))],
            out_specs=pl.BlockSpec(memory_space=pl.ANY),
            scratch_shapes=[pltpu.VMEM((SORT_ROWS, d), BF16), pltpu.VMEM((EXPERT_ROWS, d), BF16),
                            pltpu.SemaphoreType.DMA(()), pltpu.SemaphoreType.DMA(())],
        ),
        out_shape=jax.ShapeDtypeStruct((max_rows, d), BF16),
        compiler_params=pltpu.CompilerParams(dimension_semantics=("arbitrary",), vmem_limit_bytes=VMEM_LIMIT_BYTES),
        name="moe_sort",
    )(sched["seg_dst"], sched["seg_rows"], sched["tile_off"], sched["tile_rows"], sched["tail_dst"],
      sched["tail_rows"], sched["n_used"], h2, route_t)


def _expert_kernel(tile_expert_ref, n_used_ref, xs_ref, wg_ref, wu_ref, wd_ref, ys_ref):
    used = pl.program_id(0) < n_used_ref[0]

    @pl.when(used)
    def _():
        xs = xs_ref[...]
        gate = _dot(xs, wg_ref[0])
        up = _dot(xs, wu_ref[0])
        hidden = (gate * jax.nn.sigmoid(gate) * up).astype(BF16)
        ys_ref[...] = _dot(hidden, wd_ref[0]).astype(BF16)

    @pl.when(jnp.logical_not(used))
    def _():
        ys_ref[...] = jnp.zeros(ys_ref.shape, BF16)


def _expert_call(xs, wg, wu, wd, sched):
    rows, d = xs.shape
    tr = EXPERT_ROWS
    blk = lambda t, te, nu: (jnp.minimum(t, nu[0] - 1), 0)
    wsel = lambda t, te, nu: (te[jnp.minimum(t, nu[0] - 1)], 0, 0)
    return pl.pallas_call(
        _expert_kernel,
        grid_spec=pltpu.PrefetchScalarGridSpec(
            num_scalar_prefetch=2,
            grid=(rows // tr,),
            in_specs=[pl.BlockSpec((tr, d), blk),
                      pl.BlockSpec((1, d, EXPERT_FF), wsel), pl.BlockSpec((1, d, EXPERT_FF), wsel),
                      pl.BlockSpec((1, EXPERT_FF, d), wsel)],
            out_specs=pl.BlockSpec((tr, d), lambda t, te, nu: (t, 0)),
        ),
        out_shape=jax.ShapeDtypeStruct((rows, d), BF16),
        compiler_params=pltpu.CompilerParams(dimension_semantics=("arbitrary",), vmem_limit_bytes=VMEM_LIMIT_BYTES),
        name="moe_experts",
    )(sched["tile_expert"], sched["n_used"], xs, wg, wu, wd)


def _combine_kernel(seg_dst_ref, seg_rows_ref, tile_off_ref, tile_rows_ref, ys_ref, route_ref, x1_ref, o_ref,
                    buf_ref, sem):
    i = pl.program_id(0)
    tm = x1_ref.shape[0]
    buf_ref[...] = jnp.zeros(buf_ref.shape, BF16)
    _segment_copies(i, seg_dst_ref, seg_rows_ref, tile_off_ref, ys_ref, buf_ref, sem, False)
    route = route_ref[...]
    d1 = route[:, 0:1].astype(jnp.int32)
    d2 = route[:, 1:2].astype(jnp.int32)
    w1 = route[:, 2:3]
    w2 = route[:, 3:4]
    r = lax.broadcasted_iota(jnp.int32, (tm, SORT_ROWS), 1)
    weights = (jnp.where(r == d1, w1, 0.0) + jnp.where(r == d2, w2, 0.0)).astype(BF16)
    _wait_rows(buf_ref, tile_rows_ref[i], sem)
    o_ref[...] = x1_ref[...] + _dot(weights, buf_ref[...])


def _combine_call(ys, route, x1, sched):
    n, d = x1.shape
    tm = ROUTE_ROWS
    return pl.pallas_call(
        _combine_kernel,
        grid_spec=pltpu.PrefetchScalarGridSpec(
            num_scalar_prefetch=4,
            grid=(n // tm,),
            in_specs=[pl.BlockSpec(memory_space=pl.ANY),
                      pl.BlockSpec((tm, LANES), lambda i, *_: (i, 0)),
                      pl.BlockSpec((tm, d), lambda i, *_: (i, 0))],
            out_specs=pl.BlockSpec((tm, d), lambda i, *_: (i, 0)),
            scratch_shapes=[pltpu.VMEM((SORT_ROWS, d), BF16), pltpu.SemaphoreType.DMA(())],
        ),
        out_shape=jax.ShapeDtypeStruct((n, d), F32),
        compiler_params=pltpu.CompilerParams(dimension_semantics=("arbitrary",), vmem_limit_bytes=VMEM_LIMIT_BYTES),
        name="moe_combine",
    )(sched["seg_dst"], sched["seg_rows"], sched["tile_off"], sched["tile_rows"], ys, route, x1)


def _moe_schedule(cnt, n_tokens):
    n_tiles = cnt.shape[0]
    seg_rows = cnt.reshape(n_tiles, LANES)[:, :N_EXPERTS].astype(jnp.int32)
    tile_off = jnp.cumsum(seg_rows, axis=1) - seg_rows
    expert_rows = jnp.sum(seg_rows, axis=0)
    region = (expert_rows + (EXPERT_ROWS - 1)) // EXPERT_ROWS * EXPERT_ROWS
    region_end = jnp.cumsum(region)
    seg_dst = (region_end - region)[None, :] + jnp.cumsum(seg_rows, axis=0) - seg_rows
    max_rows = 2 * n_tokens + n_tiles * N_EXPERTS * (SEG_ALIGN - 1) + N_EXPERTS * (EXPERT_ROWS - 1)
    max_tiles = -(-max_rows // EXPERT_ROWS)
    tile_start = jnp.arange(max_tiles, dtype=jnp.int32) * EXPERT_ROWS
    tile_expert = jnp.minimum(jnp.searchsorted(region_end, tile_start, side="right"), N_EXPERTS - 1)
    sched = {
        "seg_dst": seg_dst.reshape(-1).astype(jnp.int32),
        "seg_rows": seg_rows.reshape(-1),
        "tile_off": tile_off.reshape(-1).astype(jnp.int32),
        "tile_rows": jnp.sum(seg_rows, axis=1).astype(jnp.int32),
        "tail_dst": (region_end - region + expert_rows).astype(jnp.int32),
        "tail_rows": (region - expert_rows).astype(jnp.int32),
        "tile_expert": tile_expert.astype(jnp.int32),
        "n_used": (region_end[-1:] // EXPERT_ROWS).astype(jnp.int32),
    }
    return sched, max_tiles * EXPERT_ROWS


def _rope_lane_tables(seq_len, rot_dim, period, first):
    half = rot_dim // 2
    pos = jnp.arange(seq_len, dtype=F32)
    inv = 1.0 / (ROPE_THETA ** (jnp.arange(0, rot_dim, 2, dtype=F32) / rot_dim))
    ang = pos[:, None] * inv[None, :]
    cos, sin = jnp.cos(ang), jnp.sin(ang)
    lane = jnp.arange(LANES)
    rel = (lane % period) - first
    active = (rel >= 0) & (rel < rot_dim)
    idx = jnp.clip(rel, 0, rot_dim - 1) % half
    sign = jnp.where(rel < half, -1.0, 1.0)
    c = jnp.where(active[None, :], cos[:, idx], 1.0)
    s = jnp.where(active[None, :], sin[:, idx] * sign[None, :], 0.0)
    return c.astype(F32), s.astype(F32)


def _head_pad(w, heads, width):
    r = w.shape[0]
    w = w.reshape(r, heads, width)
    return jnp.pad(w, ((0, 0), (0, 0), (0, LANES - width))).reshape(r, heads * LANES)


def _layer_params(l, seq_len, norm_mix, w_in, mla_q_latent_norm, w_mla_uq, mla_kv_latent_norm, w_mla_ukv,
                  mla_q_gain, mla_k_gain, diff_q_gain, diff_k_gain, w_mla_up, w_diff_up, w_out, norm_ffn,
                  w_router_group, b_router_group, w_router_expert, b_router_expert):
    d = w_in.shape[1]
    sizes = (MLA_Q_RANK, MLA_KV_RANK, MLA_ROPE, DIFF_QK_WIDTH, DIFF_QK_WIDTH, DIFF_V_WIDTH, d, d)
    offs = [0]
    for s in sizes:
        offs.append(offs[-1] + s)
    wi = w_in[l]
    seg = [wi[:, offs[k]:offs[k + 1]] for k in range(len(sizes))]
    row = lambda g: g.astype(F32)[None, :]
    p = {}
    p["gmix"] = row(norm_mix[l])
    p["wql"] = seg[0].astype(BF16)
    p["wkvl"] = seg[1].astype(BF16)
    p["wkr"] = jnp.pad(seg[2], ((0, 0), (MLA_NOPE, LANES - MLA_QK))).astype(BF16)
    p["wdq"], p["wdk"] = seg[3].astype(BF16), seg[4].astype(BF16)
    p["wdvt"] = seg[5].T.astype(BF16)
    p["wgm"], p["wgd"] = seg[6].astype(BF16), seg[7].astype(BF16)
    p["gql"] = row(mla_q_latent_norm[l])
    p["wuq"] = _head_pad(w_mla_uq[l], MLA_HEADS, MLA_QK).astype(BF16)
    p["gkvl"] = row(mla_kv_latent_norm[l])
    ukv = w_mla_ukv[l].reshape(MLA_KV_RANK, MLA_HEADS, MLA_NOPE + MLA_V)
    p["wuk"] = _head_pad(ukv[:, :, :MLA_NOPE].reshape(MLA_KV_RANK, -1), MLA_HEADS, MLA_NOPE).astype(BF16)
    p["wuvt"] = ukv[:, :, MLA_NOPE:].reshape(MLA_KV_RANK, -1).T.astype(BF16)
    p["gq"] = jnp.pad(mla_q_gain[l].astype(F32), (0, LANES - MLA_QK))[None, :]
    p["gk"] = jnp.pad(mla_k_gain[l].astype(F32), (0, LANES - MLA_QK))[None, :]
    p["gdq"] = jnp.tile(diff_q_gain[l].astype(F32), 2)[None, :]
    p["gdk"] = jnp.tile(diff_k_gain[l].astype(F32), 2)[None, :]
    p["cm"], p["sm"] = _rope_lane_tables(seq_len, MLA_ROPE, LANES, MLA_NOPE)
    p["cd"], p["sd"] = _rope_lane_tables(seq_len, DIFF_ROPE, DIFF_HEAD_DIM, 0)
    p["wmu"] = w_mla_up[l].astype(BF16)
    p["wdu"] = w_diff_up[l].astype(BF16)
    p["wout"] = w_out[l].astype(BF16)
    p["gffn"] = row(norm_ffn[l])
    wr = jnp.concatenate([w_router_expert[l], w_router_group[l]], axis=1).astype(F32)
    p["wr"] = jnp.pad(wr, ((0, 0), (0, LANES - wr.shape[1])))
    br = jnp.concatenate([b_router_expert[l], b_router_group[l]]).astype(F32)
    p["br"] = jnp.pad(br, (0, LANES - br.shape[0]))[None, :]
    return p


def kernel(x, norm_mix, w_in, mla_q_latent_norm, w_mla_uq, mla_kv_latent_norm, w_mla_ukv, mla_q_gain, mla_k_gain, diff_q_gain, diff_k_gain, lambda_q1, lambda_k1, lambda_q2, lambda_k2, diff_subln, w_mla_up, w_diff_up, w_out, norm_ffn, w_router_group, b_router_group, w_router_expert, b_router_expert, w_expert_gate, w_expert_up, w_expert_down):
    batch, seq_len, d = x.shape
    x2 = x.reshape(batch * seq_len, d)
    row = lambda g: g.astype(F32)[None, :]
    for l in range(norm_mix.shape[0]):
        lam_init = 0.8 - 0.6 * math.exp(-0.3 * l)
        p = _layer_params(l, seq_len, norm_mix, w_in, mla_q_latent_norm, w_mla_uq, mla_kv_latent_norm, w_mla_ukv,
                          mla_q_gain, mla_k_gain, diff_q_gain, diff_k_gain, w_mla_up, w_diff_up, w_out, norm_ffn,
                          w_router_group, b_router_group, w_router_expert, b_router_expert)
        qm, km, vtm, qd, kd, vtd, sgm, sgd = _proj_call(x2, seq_len, p)
        om = _mla_call(qm, km, vtm, batch, seq_len)
        od = _diff_call(qd, kd, vtd, row(lambda_q1[l]), row(lambda_k1[l]), row(lambda_q2[l]), row(lambda_k2[l]),
                        diff_subln[l].astype(F32)[:, None], lam_init, batch, seq_len)
        x1, h2, route, route_t, cnt = _merge_call(x2, om, od, sgm, sgd, p)
        sched, max_rows = _moe_schedule(cnt, x2.shape[0])
        xs = _sort_call(h2, route_t, sched, max_rows)
        ys = _expert_call(xs, w_expert_gate[l].astype(BF16), w_expert_up[l].astype(BF16),
                          w_expert_down[l].astype(BF16), sched)
        x2 = _combine_call(ys, route, x1, sched)
    return x2.reshape(batch, seq_len, d)
```

```python
import functools
import math

import jax
import jax.numpy as jnp
from jax import lax
from jax.experimental import pallas as pl
from jax.experimental.pallas import tpu as pltpu

CHUNK = 64
ROPE_THETA = 500000.0
EPS = 1e-6

MLA_HEADS = 8
MLA_NOPE = 64
MLA_ROPE = 32
MLA_V = 64
MLA_QK = MLA_NOPE + MLA_ROPE
MLA_Q_RANK = 256
MLA_KV_RANK = 128

DIFF_HEADS = 4
DIFF_HEAD_DIM = 64
DIFF_V_DIM = 2 * DIFF_HEAD_DIM
DIFF_ROPE = DIFF_HEAD_DIM // 4
DIFF_QK_WIDTH = DIFF_HEADS * 2 * DIFF_HEAD_DIM
DIFF_V_WIDTH = DIFF_HEADS * DIFF_V_DIM

N_GROUPS = 4
EXPERTS_PER_GROUP = 8
N_EXPERTS = N_GROUPS * EXPERTS_PER_GROUP
EXPERT_FF = 256

LANES = 128
VMEM_LIMIT_BYTES = 48 * 1024 * 1024

PROJ_ROWS = 512
ATTN_Q_ROWS = 256
ATTN_K_ROWS = 256
MERGE_ROWS = 512
ROUTE_ROWS = MERGE_ROWS
SEG_ALIGN = 16
SORT_ROWS = 2 * ROUTE_ROWS + N_EXPERTS * SEG_ALIGN
EXPERT_ROWS = 256
MLA_HEADS_PER_STEP = 4
DIFF_HEADS_PER_STEP = 2
LOG2E = 1.4426950408889634

BF16 = jnp.bfloat16
F32 = jnp.float32


def _dot(a, b):
    return jnp.dot(a, b, preferred_element_type=F32)


def _dot_nt(a, b):
    return lax.dot_general(a, b, (((1,), (1,)), ((), ())), preferred_element_type=F32)


def _rms(x, width):
    return x * lax.rsqrt(jnp.sum(x * x, axis=-1, keepdims=True) * (1.0 / width) + EPS)


def _rotate_pairs(y, half, cos, sin):
    lane = lax.broadcasted_iota(jnp.int32, y.shape, 1)
    up = pltpu.roll(y, LANES - half, 1)
    down = pltpu.roll(y, half, 1)
    partner = jnp.where((lane // half) % 2 == 0, up, down)
    return y * cos + partner * sin


def _store_k_tiles(o_ref, vt):
    tk = o_ref.shape[-1]
    for c in range(o_ref.shape[0]):
        o_ref[c] = vt[:, c * tk:(c + 1) * tk].astype(BF16)


def _proj_kernel(x_ref, gmix_ref, wql_ref, wkvl_ref, wkr_ref, wdq_ref, wdk_ref, wdvt_ref, wgm_ref, wgd_ref,
                 gql_ref, wuq_ref, gkvl_ref, wuk_ref, wuvt_ref, gq_ref, gk_ref, gdq_ref, gdk_ref,
                 cm_ref, sm_ref, cd_ref, sd_ref,
                 qm_ref, km_ref, vtm_ref, qd_ref, kd_ref, vtd_ref, sgm_ref, sgd_ref):
    x = x_ref[...]
    h = (_rms(x, x.shape[-1]) * gmix_ref[...]).astype(BF16)

    cm, sm = cm_ref[...], sm_ref[...]
    cd, sd = cd_ref[...], sd_ref[...]

    ql = _rms(_dot(h, wql_ref[...]), MLA_Q_RANK) * gql_ref[...]
    q = _dot(ql.astype(BF16), wuq_ref[...])
    gq = gq_ref[...]
    for hd in range(MLA_HEADS):
        sl = slice(hd * LANES, (hd + 1) * LANES)
        y = _rms(q[:, sl], MLA_QK) * gq
        y = _rotate_pairs(y, MLA_ROPE // 2, cm, sm) * (LOG2E * MLA_QK ** -0.5)
        qm_ref[:, sl] = y.astype(BF16)

    kvl = (_rms(_dot(h, wkvl_ref[...]), MLA_KV_RANK) * gkvl_ref[...]).astype(BF16)
    kr = _dot(h, wkr_ref[...])
    kn = _dot(kvl, wuk_ref[...])
    _store_k_tiles(vtm_ref, _dot_nt(wuvt_ref[...], kvl))
    gk = gk_ref[...]
    for hd in range(MLA_HEADS):
        sl = slice(hd * LANES, (hd + 1) * LANES)
        y = _rms(kn[:, sl] + kr, MLA_QK) * gk
        km_ref[:, sl] = _rotate_pairs(y, MLA_ROPE // 2, cm, sm).astype(BF16)

    def diff_qk(w_ref, g_ref, o_ref, scale):
        t = _dot(h, w_ref[...])
        g = g_ref[...]
        for hd in range(DIFF_HEADS):
            sl = slice(hd * LANES, (hd + 1) * LANES)
            th = t[:, sl]
            lane = lax.broadcasted_iota(jnp.int32, th.shape, 1)
            sq = th * th
            lo = jnp.sum(jnp.where(lane < DIFF_HEAD_DIM, sq, 0.0), axis=-1, keepdims=True)
            tot = jnp.sum(sq, axis=-1, keepdims=True)
            ss = jnp.where(lane < DIFF_HEAD_DIM, lo, tot - lo)
            y = th * lax.rsqrt(ss * (1.0 / DIFF_HEAD_DIM) + EPS) * g
            o_ref[:, sl] = (_rotate_pairs(y, DIFF_ROPE // 2, cd, sd) * scale).astype(BF16)

    diff_qk(wdq_ref, gdq_ref, qd_ref, LOG2E * DIFF_HEAD_DIM ** -0.5)
    diff_qk(wdk_ref, gdk_ref, kd_ref, 1.0)
    _store_k_tiles(vtd_ref, _dot_nt(wdvt_ref[...], h))

    sgm_ref[...] = jax.nn.sigmoid(_dot(h, wgm_ref[...])).astype(BF16)
    sgd_ref[...] = jax.nn.sigmoid(_dot(h, wgd_ref[...])).astype(BF16)


def _proj_call(x2, seq_len, p):
    n, d = x2.shape
    tm = PROJ_ROWS
    pos_blocks = seq_len // tm
    row = lambda i: (i, 0)
    const = lambda i: (0, 0)
    pos = lambda i: (i % pos_blocks, 0)
    weights = [p["gmix"], p["wql"], p["wkvl"], p["wkr"], p["wdq"], p["wdk"], p["wdvt"], p["wgm"], p["wgd"],
               p["gql"], p["wuq"], p["gkvl"], p["wuk"], p["wuvt"], p["gq"], p["gk"], p["gdq"], p["gdk"]]
    tables = [p["cm"], p["sm"], p["cd"], p["sd"]]
    in_specs = ([pl.BlockSpec((tm, d), row)]
                + [pl.BlockSpec(w.shape, const) for w in weights]
                + [pl.BlockSpec((tm, LANES), pos) for _ in tables])
    tk = ATTN_K_ROWS
    widths = [MLA_HEADS * LANES, MLA_HEADS * LANES, -MLA_HEADS * MLA_V, DIFF_QK_WIDTH, DIFF_QK_WIDTH, -DIFF_V_WIDTH, d, d]
    out_specs = [pl.BlockSpec((tm, w), row) if w > 0 else pl.BlockSpec((tm // tk, -w, tk), lambda i: (i, 0, 0))
                 for w in widths]
    out_shape = [jax.ShapeDtypeStruct((n, w) if w > 0 else (n // tk, -w, tk), BF16) for w in widths]
    return pl.pallas_call(
        _proj_kernel,
        grid=(n // tm,),
        in_specs=in_specs,
        out_specs=out_specs,
        out_shape=out_shape,
        compiler_params=pltpu.CompilerParams(dimension_semantics=("parallel",), vmem_limit_bytes=VMEM_LIMIT_BYTES),
        name="proj",
    )(x2, *weights, *tables)


def _chunk_mask_t(tk, tq):
    kc = lax.broadcasted_iota(jnp.int32, (tk, tq), 0) // CHUNK
    qc = lax.broadcasted_iota(jnp.int32, (tk, tq), 1) // CHUNK
    return kc <= qc


def _softmax_step_t(st, vt, m_ref, l_ref, acc_ref):
    m_prev = m_ref[...]
    m_new = jnp.maximum(m_prev, jnp.max(st, axis=0, keepdims=True))
    alpha = jnp.exp2(m_prev - m_new)
    pr = jnp.exp2(st - m_new)
    l_ref[...] = alpha * l_ref[...] + jnp.sum(pr, axis=0, keepdims=True)
    acc_ref[...] = alpha * acc_ref[...] + _dot(vt, pr.astype(BF16))
    m_ref[...] = m_new


def _attn_scratch(chains, dv, tq, tk):
    per_chain = [pltpu.VMEM((1, tq), F32), pltpu.VMEM((1, tq), F32), pltpu.VMEM((dv, tq), F32),
                 pltpu.VMEM((tk, tq), F32), pltpu.VMEM((tk, tq), F32)]
    return per_chain * chains


def _flash_pipeline(last_tile, scratch_refs, score_fn, value_fn, tk, tq):
    n_chains = len(scratch_refs) // 5
    chains = [scratch_refs[5 * c:5 * c + 5] for c in range(n_chains)]
    for m_ref, l_ref, acc_ref, _, _ in chains:
        m_ref[...] = jnp.full(m_ref.shape, -jnp.inf, F32)
        l_ref[...] = jnp.zeros(l_ref.shape, F32)
        acc_ref[...] = jnp.zeros(acc_ref.shape, F32)

    def scores(t, slot):
        for c, ch in enumerate(chains):
            ch[3 + slot][...] = score_fn(c, t)

    def update(t, slot, masked):
        for c, ch in enumerate(chains):
            st = ch[3 + slot][...]
            if masked:
                st = jnp.where(_chunk_mask_t(tk, tq), st, -jnp.inf)
            _softmax_step_t(st, value_fn(c, t), ch[0], ch[1], ch[2])

    scores(0, 0)

    def pair(p, carry):
        t = 2 * p
        scores(t + 1, 1)
        update(t, 0, False)
        scores(t + 2, 0)
        update(t + 1, 1, False)
        return carry

    lax.fori_loop(0, last_tile // 2, pair, 0)

    @pl.when(last_tile % 2 == 0)
    def _():
        update(last_tile, 0, True)

    @pl.when(last_tile % 2 == 1)
    def _():
        scores(last_tile, 1)
        update(last_tile - 1, 0, False)
        update(last_tile, 1, True)

    return [(ch[1], ch[2]) for ch in chains]


def _mla_kernel(q_ref, k_ref, vt_ref, o_ref, *scratch_refs):
    tq, tk = ATTN_Q_ROWS, ATTN_K_ROWS

    def score_fn(c, t):
        rows = pl.ds(pl.multiple_of(t * tk, tk), tk)
        sl = slice(c * LANES, (c + 1) * LANES)
        return _dot_nt(k_ref[rows, sl], q_ref[:, sl])

    def value_fn(c, t):
        return vt_ref[t, c * MLA_V:(c + 1) * MLA_V, :]

    out = _flash_pipeline(pl.program_id(2), scratch_refs, score_fn, value_fn, tk, tq)
    ot = jnp.concatenate([acc_ref[...] / l_ref[...] for l_ref, acc_ref in out], axis=0)
    o_ref[...] = ot.T.astype(BF16)


def _mla_call(qm, km, vtm, batch, seq_len):
    n = qm.shape[0]
    tq, tk, hps = ATTN_Q_ROWS, ATTN_K_ROWS, MLA_HEADS_PER_STEP
    qt = seq_len // tq
    return pl.pallas_call(
        _mla_kernel,
        grid=(batch, MLA_HEADS // hps, qt),
        in_specs=[pl.BlockSpec((tq, hps * LANES), lambda b, h, i: (b * qt + i, h)),
                  pl.BlockSpec((seq_len, hps * LANES), lambda b, h, i: (b, h)),
                  pl.BlockSpec((seq_len // tk, hps * MLA_V, tk), lambda b, h, i: (b, h, 0))],
        out_specs=pl.BlockSpec((tq, hps * MLA_V), lambda b, h, i: (b * qt + i, h)),
        out_shape=jax.ShapeDtypeStruct((n, MLA_HEADS * MLA_V), BF16),
        scratch_shapes=_attn_scratch(hps, MLA_V, tq, tk),
        compiler_params=pltpu.CompilerParams(dimension_semantics=("parallel", "parallel", "arbitrary"),
                                             vmem_limit_bytes=VMEM_LIMIT_BYTES),
        name="mla_attn",
    )(qm, km, vtm)


def _diff_kernel(lam_init, q_ref, k_ref, vt_ref, lq1_ref, lk1_ref, lq2_ref, lk2_ref, subln_ref, o_ref,
                 *scratch_refs):
    tq, tk = ATTN_Q_ROWS, ATTN_K_ROWS
    hps = DIFF_HEADS_PER_STEP

    q = q_ref[...]
    lane = lax.broadcasted_iota(jnp.int32, (tq, LANES), 1)
    qs = []
    for hd in range(hps):
        qh = q[:, hd * LANES:(hd + 1) * LANES]
        zero = jnp.zeros_like(qh)
        qs += [jnp.where(lane < DIFF_HEAD_DIM, qh, zero), jnp.where(lane < DIFF_HEAD_DIM, zero, qh)]

    def score_fn(c, t):
        rows = pl.ds(pl.multiple_of(t * tk, tk), tk)
        hd = c // 2
        return _dot_nt(k_ref[rows, hd * LANES:(hd + 1) * LANES], qs[c])

    def value_fn(c, t):
        hd = c // 2
        return vt_ref[t, hd * DIFF_V_DIM:(hd + 1) * DIFF_V_DIM, :]

    out = _flash_pipeline(pl.program_id(2), scratch_refs, score_fn, value_fn, tk, tq)

    lam = (jnp.exp(jnp.sum(lq1_ref[...] * lk1_ref[...], axis=-1, keepdims=True))
           - jnp.exp(jnp.sum(lq2_ref[...] * lk2_ref[...], axis=-1, keepdims=True)) + lam_init)
    heads = []
    for hd in range(hps):
        (l0, a0), (l1, a1) = out[2 * hd], out[2 * hd + 1]
        ot = a0[...] / l0[...] - lam * (a1[...] / l1[...])
        ot = ot * lax.rsqrt(jnp.sum(ot * ot, axis=0, keepdims=True) * (1.0 / DIFF_V_DIM) + EPS)
        heads.append(ot * subln_ref[...] * (1.0 - lam_init))
    o_ref[...] = jnp.concatenate(heads, axis=0).T.astype(BF16)


def _diff_call(qd, kd, vtd, lq1, lk1, lq2, lk2, subln_col, lam_init, batch, seq_len):
    n = qd.shape[0]
    tq, tk, hps = ATTN_Q_ROWS, ATTN_K_ROWS, DIFF_HEADS_PER_STEP
    qt = seq_len // tq
    small = lambda a: pl.BlockSpec(a.shape, lambda b, h, i: (0, 0))
    return pl.pallas_call(
        functools.partial(_diff_kernel, lam_init),
        grid=(batch, DIFF_HEADS // hps, qt),
        in_specs=[pl.BlockSpec((tq, hps * LANES), lambda b, h, i: (b * qt + i, h)),
                  pl.BlockSpec((seq_len, hps * LANES), lambda b, h, i: (b, h)),
                  pl.BlockSpec((seq_len // tk, hps * DIFF_V_DIM, tk), lambda b, h, i: (b, h, 0)),
                  small(lq1), small(lk1), small(lq2), small(lk2), small(subln_col)],
        out_specs=pl.BlockSpec((tq, hps * LANES), lambda b, h, i: (b * qt + i, h)),
        out_shape=jax.ShapeDtypeStruct((n, DIFF_V_WIDTH), BF16),
        scratch_shapes=_attn_scratch(2 * hps, DIFF_V_DIM, tq, tk),
        compiler_params=pltpu.CompilerParams(dimension_semantics=("parallel", "parallel", "arbitrary"),
                                             vmem_limit_bytes=VMEM_LIMIT_BYTES),
        name="diff_attn",
    )(qd, kd, vtd, lq1, lk1, lq2, lk2, subln_col)


def _merge_kernel(x_ref, om_ref, od_ref, sgm_ref, sgd_ref, wmu_ref, wdu_ref, wout_ref, gffn_ref, wr_ref, br_ref,
                  x1_ref, h2_ref, route_ref, route_t_ref, cnt_ref):
    merged = (sgm_ref[...].astype(F32) * _dot(om_ref[...], wmu_ref[...])
              + sgd_ref[...].astype(F32) * _dot(od_ref[...], wdu_ref[...]))
    x1 = x_ref[...] + _dot(merged.astype(BF16), wout_ref[...])
    x1_ref[...] = x1
    h2 = _rms(x1, x1.shape[-1]) * gffn_ref[...]
    h2_ref[...] = h2.astype(BF16)

    logits = jnp.dot(h2, wr_ref[...], preferred_element_type=F32, precision=lax.Precision.HIGHEST) + br_ref[...]
    lane = lax.broadcasted_iota(jnp.int32, logits.shape, 1)
    neg = -jnp.inf
    big = jnp.int32(1 << 20)

    def top(vals):
        mx = jnp.max(vals, axis=-1, keepdims=True)
        idx = jnp.min(jnp.where(vals == mx, lane, big), axis=-1, keepdims=True)
        return mx, idx

    gl = jnp.where((lane >= N_EXPERTS) & (lane < N_EXPERTS + N_GROUPS), logits, neg)
    gmax, gidx = top(gl)
    pg_sel = 1.0 / jnp.sum(jnp.exp(gl - gmax), axis=-1, keepdims=True)
    el = jnp.where((lane < N_EXPERTS) & (lane // EXPERTS_PER_GROUP == gidx - N_EXPERTS), logits, neg)
    m1, i1 = top(el)
    m2, i2 = top(jnp.where(lane == i1, neg, el))
    e2 = jnp.exp(m2 - m1)
    w1 = pg_sel / (1.0 + e2)
    w2 = w1 * e2

    tm = logits.shape[0]
    sel = (lane == i1) | (lane == i2)
    earlier = (lax.broadcasted_iota(jnp.int32, (tm, tm), 1) < lax.broadcasted_iota(jnp.int32, (tm, tm), 0))
    rank = _dot(jnp.where(earlier, 1.0, 0.0).astype(BF16), jnp.where(sel, 1.0, 0.0).astype(BF16))
    cnt = jnp.sum(jnp.where(sel, 1.0, 0.0), axis=0, keepdims=True)
    seg = jnp.floor((cnt + (SEG_ALIGN - 1)) * (1.0 / SEG_ALIGN))
    before = (lax.broadcasted_iota(jnp.int32, (LANES, LANES), 0) < lax.broadcasted_iota(jnp.int32, (LANES, LANES), 1))
    off = _dot(jnp.broadcast_to(seg, (8, LANES)).astype(BF16), jnp.where(before, 1.0, 0.0).astype(BF16))[0:1] * SEG_ALIGN
    dest = off + rank
    d1 = jnp.sum(jnp.where(lane == i1, dest, 0.0), axis=-1, keepdims=True)
    d2 = jnp.sum(jnp.where(lane == i2, dest, 0.0), axis=-1, keepdims=True)
    route = jnp.where(lane == 0, d1, jnp.where(lane == 1, d2, jnp.where(lane == 2, w1, jnp.where(lane == 3, w2, 0.0))))
    route_ref[...] = route
    route_t_ref[0] = route.T[0:8, :]
    cnt_ref[0] = seg * SEG_ALIGN


def _merge_call(x2, om, od, sgm, sgd, p):
    n, d = x2.shape
    tm = MERGE_ROWS
    row = lambda i: (i, 0)
    const = lambda i: (0, 0)
    weights = [p["wmu"], p["wdu"], p["wout"], p["gffn"], p["wr"], p["br"]]
    return pl.pallas_call(
        _merge_kernel,
        grid=(n // tm,),
        in_specs=([pl.BlockSpec((tm, a.shape[1]), row) for a in (x2, om, od, sgm, sgd)]
                  + [pl.BlockSpec(w.shape, const) for w in weights]),
        out_specs=[pl.BlockSpec((tm, d), row), pl.BlockSpec((tm, d), row), pl.BlockSpec((tm, LANES), row),
                   pl.BlockSpec((1, 8, tm), lambda i: (i, 0, 0)), pl.BlockSpec((1, 1, LANES), lambda i: (i, 0, 0))],
        out_shape=[jax.ShapeDtypeStruct((n, d), F32), jax.ShapeDtypeStruct((n, d), BF16),
                   jax.ShapeDtypeStruct((n, LANES), F32), jax.ShapeDtypeStruct((n // tm, 8, tm), F32),
                   jax.ShapeDtypeStruct((n // tm, 1, LANES), F32)],
        compiler_params=pltpu.CompilerParams(dimension_semantics=("parallel",), vmem_limit_bytes=VMEM_LIMIT_BYTES),
        name="merge_router",
    )(x2, om, od, sgm, sgd, *weights)


def _segment_copies(i, seg_dst_ref, seg_rows_ref, tile_off_ref, global_ref, tile_ref, sem, to_global):
    def body(e, carry):
        k = i * N_EXPERTS + e
        rows = pl.multiple_of(seg_rows_ref[k], SEG_ALIGN)

        @pl.when(rows > 0)
        def _():
            g = global_ref.at[pl.ds(pl.multiple_of(seg_dst_ref[k], SEG_ALIGN), rows)]
            t = tile_ref.at[pl.ds(pl.multiple_of(tile_off_ref[k], SEG_ALIGN), rows)]
            src, dst = (t, g) if to_global else (g, t)
            pltpu.make_async_copy(src, dst, sem).start()

        return carry

    lax.fori_loop(0, N_EXPERTS, body, 0)


def _wait_rows(tile_ref, rows, sem):
    @pl.when(rows > 0)
    def _():
        view = tile_ref.at[pl.ds(0, pl.multiple_of(rows, SEG_ALIGN))]
        pltpu.make_async_copy(view, view, sem).wait()


def _zero_unused_rows(tail_dst_ref, tail_rows_ref, n_used_ref, xs_ref, zero_ref, sem, start):
    n_tiles = xs_ref.shape[0] // EXPERT_ROWS
    if start:
        zero_ref[...] = jnp.zeros(zero_ref.shape, BF16)

    def tail(e, total):
        rows = pl.multiple_of(tail_rows_ref[e], SEG_ALIGN)
        if start:
            @pl.when(rows > 0)
            def _():
                dst = xs_ref.at[pl.ds(pl.multiple_of(tail_dst_ref[e], SEG_ALIGN), rows)]
                pltpu.make_async_copy(zero_ref.at[pl.ds(0, rows)], dst, sem).start()

        return total + rows

    total = lax.fori_loop(0, N_EXPERTS, tail, 0)
    if not start:
        _wait_rows(xs_ref, total + (n_tiles - n_used_ref[0]) * EXPERT_ROWS, sem)
        return

    def unused(t, carry):
        dst = xs_ref.at[pl.ds(pl.multiple_of(t * EXPERT_ROWS, EXPERT_ROWS), EXPERT_ROWS)]
        pltpu.make_async_copy(zero_ref, dst, sem).start()
        return carry

    lax.fori_loop(n_used_ref[0], n_tiles, unused, 0)


def _sort_kernel(seg_dst_ref, seg_rows_ref, tile_off_ref, tile_rows_ref, tail_dst_ref, tail_rows_ref, n_used_ref,
                 h2_ref, route_t_ref, xs_ref, sorted_ref, zero_ref, sem, zero_sem):
    i = pl.program_id(0)
    tm = h2_ref.shape[0]

    @pl.when(i == 0)
    def _():
        _zero_unused_rows(tail_dst_ref, tail_rows_ref, n_used_ref, xs_ref, zero_ref, zero_sem, True)

    d1 = route_t_ref[0, 0:1, :].astype(jnp.int32)
    d2 = route_t_ref[0, 1:2, :].astype(jnp.int32)
    r = lax.broadcasted_iota(jnp.int32, (SORT_ROWS, tm), 0)
    perm = jnp.where((r == d1) | (r == d2), 1.0, 0.0).astype(BF16)
    sorted_ref[...] = _dot(perm, h2_ref[...]).astype(BF16)
    _segment_copies(i, seg_dst_ref, seg_rows_ref, tile_off_ref, xs_ref, sorted_ref, sem, True)
    _wait_rows(sorted_ref, tile_rows_ref[i], sem)

    @pl.when(i == pl.num_programs(0) - 1)
    def _():
        _zero_unused_rows(tail_dst_ref, tail_rows_ref, n_used_ref, xs_ref, zero_ref, zero_sem, False)


def _sort_call(h2, route_t, sched, max_rows):
    n, d = h2.shape
    tm = ROUTE_ROWS
    return pl.pallas_call(
        _sort_kernel,
        grid_spec=pltpu.PrefetchScalarGridSpec(
            num_scalar_prefetch=7,
            grid=(n // tm,),
            in_specs=[pl.BlockSpec((tm, d), lambda i, *_: (i, 0)),
                      pl.BlockSpec((1, 8, tm), lambda i, *_: (i, 0, 0))],
            out_specs=pl.BlockSpec(memory_space=pl.ANY),
            scratch_shapes=[pltpu.VMEM((SORT_ROWS, d), BF16), pltpu.VMEM((EXPERT_ROWS, d), BF16),
                            pltpu.SemaphoreType.DMA(()), pltpu.SemaphoreType.DMA(())],
        ),
        out_shape=jax.ShapeDtypeStruct((max_rows, d), BF16),
        compiler_params=pltpu.CompilerParams(dimension_semantics=("arbitrary",), vmem_limit_bytes=VMEM_LIMIT_BYTES),
        name="moe_sort",
    )(sched["seg_dst"], sched["seg_rows"], sched["tile_off"], sched["tile_rows"], sched["tail_dst"],
      sched["tail_rows"], sched["n_used"], h2, route_t)


def _expert_kernel(tile_expert_ref, n_used_ref, xs_ref, wg_ref, wu_ref, wd_ref, ys_ref, wg_bf, wu_bf, wd_bf):
    t = pl.program_id(0)
    used = t < n_used_ref[0]

    @pl.when(used & ((t == 0) | (tile_expert_ref[t] != tile_expert_ref[jnp.maximum(t - 1, 0)])))
    def _():
        wg_bf[...] = wg_ref[0].astype(BF16)
        wu_bf[...] = wu_ref[0].astype(BF16)
        wd_bf[...] = wd_ref[0].astype(BF16)

    @pl.when(used)
    def _():
        xs = xs_ref[...]
        gate = _dot(xs, wg_bf[...])
        up = _dot(xs, wu_bf[...])
        hidden = (gate * jax.nn.sigmoid(gate) * up).astype(BF16)
        ys_ref[...] = _dot(hidden, wd_bf[...]).astype(BF16)

    @pl.when(jnp.logical_not(used))
    def _():
        ys_ref[...] = jnp.zeros(ys_ref.shape, BF16)


def _expert_call(xs, wg, wu, wd, sched):
    rows, d = xs.shape
    tr = EXPERT_ROWS
    blk = lambda t, te, nu: (jnp.minimum(t, nu[0] - 1), 0)
    wsel = lambda t, te, nu: (te[jnp.minimum(t, nu[0] - 1)], 0, 0)
    return pl.pallas_call(
        _expert_kernel,
        grid_spec=pltpu.PrefetchScalarGridSpec(
            num_scalar_prefetch=2,
            grid=(rows // tr,),
            in_specs=[pl.BlockSpec((tr, d), blk),
                      pl.BlockSpec((1, d, EXPERT_FF), wsel), pl.BlockSpec((1, d, EXPERT_FF), wsel),
                      pl.BlockSpec((1, EXPERT_FF, d), wsel)],
            out_specs=pl.BlockSpec((tr, d), lambda t, te, nu: (t, 0)),
            scratch_shapes=[pltpu.VMEM((d, EXPERT_FF), BF16), pltpu.VMEM((d, EXPERT_FF), BF16),
                            pltpu.VMEM((EXPERT_FF, d), BF16)],
        ),
        out_shape=jax.ShapeDtypeStruct((rows, d), BF16),
        compiler_params=pltpu.CompilerParams(dimension_semantics=("arbitrary",), vmem_limit_bytes=VMEM_LIMIT_BYTES),
        name="moe_experts",
    )(sched["tile_expert"], sched["n_used"], xs, wg, wu, wd)


def _combine_kernel(seg_dst_ref, seg_rows_ref, tile_off_ref, tile_rows_ref, ys_ref, route_ref, x1_ref, o_ref,
                    buf_ref, sem):
    i = pl.program_id(0)
    tm = x1_ref.shape[0]
    buf_ref[...] = jnp.zeros(buf_ref.shape, BF16)
    _segment_copies(i, seg_dst_ref, seg_rows_ref, tile_off_ref, ys_ref, buf_ref, sem, False)
    route = route_ref[...]
    d1 = route[:, 0:1].astype(jnp.int32)
    d2 = route[:, 1:2].astype(jnp.int32)
    w1 = route[:, 2:3]
    w2 = route[:, 3:4]
    r = lax.broadcasted_iota(jnp.int32, (tm, SORT_ROWS), 1)
    weights = (jnp.where(r == d1, w1, 0.0) + jnp.where(r == d2, w2, 0.0)).astype(BF16)
    _wait_rows(buf_ref, tile_rows_ref[i], sem)
    o_ref[...] = x1_ref[...] + _dot(weights, buf_ref[...])


def _combine_call(ys, route, x1, sched):
    n, d = x1.shape
    tm = ROUTE_ROWS
    return pl.pallas_call(
        _combine_kernel,
        grid_spec=pltpu.PrefetchScalarGridSpec(
            num_scalar_prefetch=4,
            grid=(n // tm,),
            in_specs=[pl.BlockSpec(memory_space=pl.ANY),
                      pl.BlockSpec((tm, LANES), lambda i, *_: (i, 0)),
                      pl.BlockSpec((tm, d), lambda i, *_: (i, 0))],
            out_specs=pl.BlockSpec((tm, d), lambda i, *_: (i, 0)),
            scratch_shapes=[pltpu.VMEM((SORT_ROWS, d), BF16), pltpu.SemaphoreType.DMA(())],
        ),
        out_shape=jax.ShapeDtypeStruct((n, d), F32),
        compiler_params=pltpu.CompilerParams(dimension_semantics=("arbitrary",), vmem_limit_bytes=VMEM_LIMIT_BYTES),
        name="moe_combine",
    )(sched["seg_dst"], sched["seg_rows"], sched["tile_off"], sched["tile_rows"], ys, route, x1)


def _schedule_kernel(cnt_ref, seg_dst_ref, tile_off_ref, tile_rows_ref, misc_ref):
    hp = functools.partial(jnp.dot, preferred_element_type=F32, precision=lax.Precision.HIGHEST)
    cnt = cnt_ref[...]
    n_tiles = cnt.shape[0]
    tile_before = jnp.where(lax.broadcasted_iota(jnp.int32, (n_tiles, n_tiles), 1)
                            < lax.broadcasted_iota(jnp.int32, (n_tiles, n_tiles), 0), 1.0, 0.0)
    expert_before = jnp.where(lax.broadcasted_iota(jnp.int32, (LANES, LANES), 0)
                              < lax.broadcasted_iota(jnp.int32, (LANES, LANES), 1), 1.0, 0.0)
    expert_rows = jnp.sum(cnt, axis=0, keepdims=True)
    region = jnp.floor((expert_rows + (EXPERT_ROWS - 1)) * (1.0 / EXPERT_ROWS)) * EXPERT_ROWS
    region_start = hp(jnp.broadcast_to(region, (8, LANES)), expert_before)[0:1]
    seg_dst_ref[...] = (region_start + hp(tile_before, cnt)).astype(jnp.int32)
    tile_off_ref[...] = hp(cnt, expert_before).astype(jnp.int32)
    tile_rows_ref[...] = jnp.broadcast_to(jnp.sum(cnt, axis=-1, keepdims=True), cnt.shape).astype(jnp.int32)
    n_used = jnp.sum(region, axis=-1, keepdims=True) * (1.0 / EXPERT_ROWS)
    row = lax.broadcasted_iota(jnp.int32, (8, LANES), 0)
    misc = jnp.where(row == 0, region_start + expert_rows,
                     jnp.where(row == 1, region - expert_rows,
                               jnp.where(row == 2, region_start + region, n_used)))
    misc_ref[...] = misc.astype(jnp.int32)


def _moe_schedule(cnt, n_tokens):
    n_tiles = cnt.shape[0]
    table = jax.ShapeDtypeStruct((n_tiles, LANES), jnp.int32)
    seg_dst, tile_off, tile_rows, misc = pl.pallas_call(
        _schedule_kernel,
        out_shape=[table, table, table, jax.ShapeDtypeStruct((8, LANES), jnp.int32)],
        name="moe_schedule",
    )(cnt.reshape(n_tiles, LANES))
    max_rows = 2 * n_tokens + n_tiles * N_EXPERTS * (SEG_ALIGN - 1) + N_EXPERTS * (EXPERT_ROWS - 1)
    max_tiles = -(-max_rows // EXPERT_ROWS)
    tile_start = jnp.arange(max_tiles, dtype=jnp.int32) * EXPERT_ROWS
    region_end = misc[2, :N_EXPERTS]
    tile_expert = jnp.minimum(jnp.sum((region_end[None, :] <= tile_start[:, None]).astype(jnp.int32), axis=1),
                              N_EXPERTS - 1)
    flat = lambda a: a[:, :N_EXPERTS].reshape(-1)
    sched = {
        "seg_dst": flat(seg_dst),
        "seg_rows": flat(cnt.reshape(n_tiles, LANES).astype(jnp.int32)),
        "tile_off": flat(tile_off),
        "tile_rows": tile_rows[:, 0],
        "tail_dst": misc[0, :N_EXPERTS],
        "tail_rows": misc[1, :N_EXPERTS],
        "tile_expert": tile_expert,
        "n_used": misc[3, :1],
    }
    return sched, max_tiles * EXPERT_ROWS


def _rope_lane_tables(seq_len, rot_dim, period, first):
    half = rot_dim // 2
    pos = jnp.arange(seq_len, dtype=F32)
    inv = 1.0 / (ROPE_THETA ** (jnp.arange(0, rot_dim, 2, dtype=F32) / rot_dim))
    ang = pos[:, None] * inv[None, :]
    cos, sin = jnp.cos(ang), jnp.sin(ang)
    lane = jnp.arange(LANES)
    rel = (lane % period) - first
    active = (rel >= 0) & (rel < rot_dim)
    idx = jnp.clip(rel, 0, rot_dim - 1) % half
    sign = jnp.where(rel < half, -1.0, 1.0)
    c = jnp.where(active[None, :], cos[:, idx], 1.0)
    s = jnp.where(active[None, :], sin[:, idx] * sign[None, :], 0.0)
    return c.astype(F32), s.astype(F32)


def _head_pad(w, heads, width):
    r = w.shape[0]
    w = w.reshape(r, heads, width)
    return jnp.pad(w, ((0, 0), (0, 0), (0, LANES - width))).reshape(r, heads * LANES)


def _layer_params(l, seq_len, norm_mix, w_in, mla_q_latent_norm, w_mla_uq, mla_kv_latent_norm, w_mla_ukv,
                  mla_q_gain, mla_k_gain, diff_q_gain, diff_k_gain, w_mla_up, w_diff_up, w_out, norm_ffn,
                  w_router_group, b_router_group, w_router_expert, b_router_expert):
    d = w_in.shape[1]
    sizes = (MLA_Q_RANK, MLA_KV_RANK, MLA_ROPE, DIFF_QK_WIDTH, DIFF_QK_WIDTH, DIFF_V_WIDTH, d, d)
    offs = [0]
    for s in sizes:
        offs.append(offs[-1] + s)
    wi = w_in[l]
    seg = [wi[:, offs[k]:offs[k + 1]] for k in range(len(sizes))]
    row = lambda g: g.astype(F32)[None, :]
    p = {}
    p["gmix"] = row(norm_mix[l])
    p["wql"] = seg[0].astype(BF16)
    p["wkvl"] = seg[1].astype(BF16)
    p["wkr"] = jnp.pad(seg[2], ((0, 0), (MLA_NOPE, LANES - MLA_QK))).astype(BF16)
    p["wdq"], p["wdk"] = seg[3].astype(BF16), seg[4].astype(BF16)
    p["wdvt"] = seg[5].T.astype(BF16)
    p["wgm"], p["wgd"] = seg[6].astype(BF16), seg[7].astype(BF16)
    p["gql"] = row(mla_q_latent_norm[l])
    p["wuq"] = _head_pad(w_mla_uq[l], MLA_HEADS, MLA_QK).astype(BF16)
    p["gkvl"] = row(mla_kv_latent_norm[l])
    ukv = w_mla_ukv[l].reshape(MLA_KV_RANK, MLA_HEADS, MLA_NOPE + MLA_V)
    p["wuk"] = _head_pad(ukv[:, :, :MLA_NOPE].reshape(MLA_KV_RANK, -1), MLA_HEADS, MLA_NOPE).astype(BF16)
    p["wuvt"] = ukv[:, :, MLA_NOPE:].reshape(MLA_KV_RANK, -1).T.astype(BF16)
    p["gq"] = jnp.pad(mla_q_gain[l].astype(F32), (0, LANES - MLA_QK))[None, :]
    p["gk"] = jnp.pad(mla_k_gain[l].astype(F32), (0, LANES - MLA_QK))[None, :]
    p["gdq"] = jnp.tile(diff_q_gain[l].astype(F32), 2)[None, :]
    p["gdk"] = jnp.tile(diff_k_gain[l].astype(F32), 2)[None, :]
    p["cm"], p["sm"] = _rope_lane_tables(seq_len, MLA_ROPE, LANES, MLA_NOPE)
    p["cd"], p["sd"] = _rope_lane_tables(seq_len, DIFF_ROPE, DIFF_HEAD_DIM, 0)
    p["wmu"] = w_mla_up[l].astype(BF16)
    p["wdu"] = w_diff_up[l].astype(BF16)
    p["wout"] = w_out[l].astype(BF16)
    p["gffn"] = row(norm_ffn[l])
    wr = jnp.concatenate([w_router_expert[l], w_router_group[l]], axis=1).astype(F32)
    p["wr"] = jnp.pad(wr, ((0, 0), (0, LANES - wr.shape[1])))
    br = jnp.concatenate([b_router_expert[l], b_router_group[l]]).astype(F32)
    p["br"] = jnp.pad(br, (0, LANES - br.shape[0]))[None, :]
    return p


def kernel(x, norm_mix, w_in, mla_q_latent_norm, w_mla_uq, mla_kv_latent_norm, w_mla_ukv, mla_q_gain, mla_k_gain, diff_q_gain, diff_k_gain, lambda_q1, lambda_k1, lambda_q2, lambda_k2, diff_subln, w_mla_up, w_diff_up, w_out, norm_ffn, w_router_group, b_router_group, w_router_expert, b_router_expert, w_expert_gate, w_expert_up, w_expert_down):
    batch, seq_len, d = x.shape
    x2 = x.reshape(batch * seq_len, d)
    row = lambda g: g.astype(F32)[None, :]
    for l in range(norm_mix.shape[0]):
        lam_init = 0.8 - 0.6 * math.exp(-0.3 * l)
        p = _layer_params(l, seq_len, norm_mix, w_in, mla_q_latent_norm, w_mla_uq, mla_kv_latent_norm, w_mla_ukv,
                          mla_q_gain, mla_k_gain, diff_q_gain, diff_k_gain, w_mla_up, w_diff_up, w_out, norm_ffn,
                          w_router_group, b_router_group, w_router_expert, b_router_expert)
        qm, km, vtm, qd, kd, vtd, sgm, sgd = _proj_call(x2, seq_len, p)
        om = _mla_call(qm, km, vtm, batch, seq_len)
        od = _diff_call(qd, kd, vtd, row(lambda_q1[l]), row(lambda_k1[l]), row(lambda_q2[l]), row(lambda_k2[l]),
                        diff_subln[l].astype(F32)[:, None], lam_init, batch, seq_len)
        x1, h2, route, route_t, cnt = _merge_call(x2, om, od, sgm, sgd, p)
        sched, max_rows = _moe_schedule(cnt, x2.shape[0])
        xs = _sort_call(h2, route_t, sched, max_rows)
        ys = _expert_call(xs, w_expert_gate[l], w_expert_up[l], w_expert_down[l], sched)
        x2 = _combine_call(ys, route, x1, sched)
    return x2.reshape(batch, seq_len, d)
```

```python
import functools
import math

import jax
import jax.numpy as jnp
from jax import lax
from jax.experimental import pallas as pl
from jax.experimental.pallas import tpu as pltpu

CHUNK = 64
ROPE_THETA = 500000.0
EPS = 1e-6

MLA_HEADS = 8
MLA_NOPE = 64
MLA_ROPE = 32
MLA_V = 64
MLA_QK = MLA_NOPE + MLA_ROPE
MLA_Q_RANK = 256
MLA_KV_RANK = 128

DIFF_HEADS = 4
DIFF_HEAD_DIM = 64
DIFF_V_DIM = 2 * DIFF_HEAD_DIM
DIFF_ROPE = DIFF_HEAD_DIM // 4
DIFF_QK_WIDTH = DIFF_HEADS * 2 * DIFF_HEAD_DIM
DIFF_V_WIDTH = DIFF_HEADS * DIFF_V_DIM

N_GROUPS = 4
EXPERTS_PER_GROUP = 8
N_EXPERTS = N_GROUPS * EXPERTS_PER_GROUP
EXPERT_FF = 256

LANES = 128
VMEM_LIMIT_BYTES = 48 * 1024 * 1024

PROJ_ROWS = 512
ATTN_Q_ROWS = 256
ATTN_K_ROWS = 256
MERGE_ROWS = 512
ROUTE_ROWS = MERGE_ROWS
SEG_ALIGN = 16
SORT_ROWS = 2 * ROUTE_ROWS + N_EXPERTS * SEG_ALIGN
EXPERT_ROWS = 256
MLA_HEADS_PER_STEP = 4
DIFF_HEADS_PER_STEP = 2
LOG2E = 1.4426950408889634

BF16 = jnp.bfloat16
F32 = jnp.float32


def _dot(a, b):
    return jnp.dot(a, b, preferred_element_type=F32)


def _dot_nt(a, b):
    return lax.dot_general(a, b, (((1,), (1,)), ((), ())), preferred_element_type=F32)


def _rms(x, width):
    return x * lax.rsqrt(jnp.sum(x * x, axis=-1, keepdims=True) * (1.0 / width) + EPS)


def _rotate_pairs(y, half, cos, sin):
    lane = lax.broadcasted_iota(jnp.int32, y.shape, 1)
    up = pltpu.roll(y, LANES - half, 1)
    down = pltpu.roll(y, half, 1)
    partner = jnp.where((lane // half) % 2 == 0, up, down)
    return y * cos + partner * sin


def _store_k_tiles(o_ref, vt):
    tk = o_ref.shape[-1]
    for c in range(o_ref.shape[0]):
        o_ref[c] = vt[:, c * tk:(c + 1) * tk].astype(BF16)


def _proj_kernel(x_ref, gmix_ref, wql_ref, wkvl_ref, wkr_ref, wdq_ref, wdk_ref, wdvt_ref, wgm_ref, wgd_ref,
                 gql_ref, wuq_ref, gkvl_ref, wuk_ref, wuvt_ref, gq_ref, gk_ref, gdq_ref, gdk_ref,
                 cm_ref, sm_ref, cd_ref, sd_ref,
                 qm_ref, km_ref, vtm_ref, qd_ref, kd_ref, vtd_ref, sgm_ref, sgd_ref):
    x = x_ref[...]
    h = (_rms(x, x.shape[-1]) * gmix_ref[...]).astype(BF16)

    cm, sm = cm_ref[...], sm_ref[...]
    cd, sd = cd_ref[...], sd_ref[...]

    ql = _rms(_dot(h, wql_ref[...]), MLA_Q_RANK) * gql_ref[...]
    q = _dot(ql.astype(BF16), wuq_ref[...])
    gq = gq_ref[...]
    for hd in range(MLA_HEADS):
        sl = slice(hd * LANES, (hd + 1) * LANES)
        y = _rms(q[:, sl], MLA_QK) * gq
        y = _rotate_pairs(y, MLA_ROPE // 2, cm, sm) * (LOG2E * MLA_QK ** -0.5)
        qm_ref[:, sl] = y.astype(BF16)

    kvl = (_rms(_dot(h, wkvl_ref[...]), MLA_KV_RANK) * gkvl_ref[...]).astype(BF16)
    kr = _dot(h, wkr_ref[...])
    kn = _dot(kvl, wuk_ref[...])
    _store_k_tiles(vtm_ref, _dot_nt(wuvt_ref[...], kvl))
    gk = gk_ref[...]
    for hd in range(MLA_HEADS):
        sl = slice(hd * LANES, (hd + 1) * LANES)
        y = _rms(kn[:, sl] + kr, MLA_QK) * gk
        km_ref[:, sl] = _rotate_pairs(y, MLA_ROPE // 2, cm, sm).astype(BF16)

    def diff_qk(w_ref, g_ref, o_ref, scale):
        t = _dot(h, w_ref[...])
        g = g_ref[...]
        for hd in range(DIFF_HEADS):
            sl = slice(hd * LANES, (hd + 1) * LANES)
            th = t[:, sl]
            lane = lax.broadcasted_iota(jnp.int32, th.shape, 1)
            sq = th * th
            lo = jnp.sum(jnp.where(lane < DIFF_HEAD_DIM, sq, 0.0), axis=-1, keepdims=True)
            tot = jnp.sum(sq, axis=-1, keepdims=True)
            ss = jnp.where(lane < DIFF_HEAD_DIM, lo, tot - lo)
            y = th * lax.rsqrt(ss * (1.0 / DIFF_HEAD_DIM) + EPS) * g
            o_ref[:, sl] = (_rotate_pairs(y, DIFF_ROPE // 2, cd, sd) * scale).astype(BF16)

    diff_qk(wdq_ref, gdq_ref, qd_ref, LOG2E * DIFF_HEAD_DIM ** -0.5)
    diff_qk(wdk_ref, gdk_ref, kd_ref, 1.0)
    _store_k_tiles(vtd_ref, _dot_nt(wdvt_ref[...], h))

    sgm_ref[...] = jax.nn.sigmoid(_dot(h, wgm_ref[...])).astype(BF16)
    sgd_ref[...] = jax.nn.sigmoid(_dot(h, wgd_ref[...])).astype(BF16)


def _proj_call(x2, seq_len, p):
    n, d = x2.shape
    tm = PROJ_ROWS
    pos_blocks = seq_len // tm
    row = lambda i: (i, 0)
    const = lambda i: (0, 0)
    pos = lambda i: (i % pos_blocks, 0)
    weights = [p["gmix"], p["wql"], p["wkvl"], p["wkr"], p["wdq"], p["wdk"], p["wdvt"], p["wgm"], p["wgd"],
               p["gql"], p["wuq"], p["gkvl"], p["wuk"], p["wuvt"], p["gq"], p["gk"], p["gdq"], p["gdk"]]
    tables = [p["cm"], p["sm"], p["cd"], p["sd"]]
    in_specs = ([pl.BlockSpec((tm, d), row)]
                + [pl.BlockSpec(w.shape, const) for w in weights]
                + [pl.BlockSpec((tm, LANES), pos) for _ in tables])
    tk = ATTN_K_ROWS
    widths = [MLA_HEADS * LANES, MLA_HEADS * LANES, -MLA_HEADS * MLA_V, DIFF_QK_WIDTH, DIFF_QK_WIDTH, -DIFF_V_WIDTH, d, d]
    out_specs = [pl.BlockSpec((tm, w), row) if w > 0 else pl.BlockSpec((tm // tk, -w, tk), lambda i: (i, 0, 0))
                 for w in widths]
    out_shape = [jax.ShapeDtypeStruct((n, w) if w > 0 else (n // tk, -w, tk), BF16) for w in widths]
    return pl.pallas_call(
        _proj_kernel,
        grid=(n // tm,),
        in_specs=in_specs,
        out_specs=out_specs,
        out_shape=out_shape,
        compiler_params=pltpu.CompilerParams(dimension_semantics=("parallel",), vmem_limit_bytes=VMEM_LIMIT_BYTES),
        name="proj",
    )(x2, *weights, *tables)


def _chunk_mask_t(tk, tq):
    kc = lax.broadcasted_iota(jnp.int32, (tk, tq), 0) // CHUNK
    qc = lax.broadcasted_iota(jnp.int32, (tk, tq), 1) // CHUNK
    return kc <= qc


def _softmax_step_t(st, vt, m_ref, l_ref, acc_ref):
    m_prev = m_ref[...]
    m_new = jnp.maximum(m_prev, jnp.max(st, axis=0, keepdims=True))
    alpha = jnp.exp2(m_prev - m_new)
    pr = jnp.exp2(st - m_new)
    l_ref[...] = alpha * l_ref[...] + jnp.sum(pr, axis=0, keepdims=True)
    acc_ref[...] = alpha * acc_ref[...] + _dot(vt, pr.astype(BF16))
    m_ref[...] = m_new


def _attn_scratch(chains, dv, tq, tk):
    per_chain = [pltpu.VMEM((1, tq), F32), pltpu.VMEM((1, tq), F32), pltpu.VMEM((dv, tq), F32),
                 pltpu.VMEM((tk, tq), F32), pltpu.VMEM((tk, tq), F32)]
    return per_chain * chains


def _flash_pipeline(last_tile, scratch_refs, score_fn, value_fn, tk, tq):
    n_chains = len(scratch_refs) // 5
    chains = [scratch_refs[5 * c:5 * c + 5] for c in range(n_chains)]
    for m_ref, l_ref, acc_ref, _, _ in chains:
        m_ref[...] = jnp.full(m_ref.shape, -jnp.inf, F32)
        l_ref[...] = jnp.zeros(l_ref.shape, F32)
        acc_ref[...] = jnp.zeros(acc_ref.shape, F32)

    def scores(t, slot):
        for c, ch in enumerate(chains):
            ch[3 + slot][...] = score_fn(c, t)

    def update(t, slot, masked):
        for c, ch in enumerate(chains):
            st = ch[3 + slot][...]
            if masked:
                st = jnp.where(_chunk_mask_t(tk, tq), st, -jnp.inf)
            _softmax_step_t(st, value_fn(c, t), ch[0], ch[1], ch[2])

    scores(0, 0)

    def pair(p, carry):
        t = 2 * p
        scores(t + 1, 1)
        update(t, 0, False)
        scores(t + 2, 0)
        update(t + 1, 1, False)
        return carry

    lax.fori_loop(0, last_tile // 2, pair, 0)

    @pl.when(last_tile % 2 == 0)
    def _():
        update(last_tile, 0, True)

    @pl.when(last_tile % 2 == 1)
    def _():
        scores(last_tile, 1)
        update(last_tile - 1, 0, False)
        update(last_tile, 1, True)

    return [(ch[1], ch[2]) for ch in chains]


def _mla_kernel(q_ref, k_ref, vt_ref, o_ref, *scratch_refs):
    tq, tk = ATTN_Q_ROWS, ATTN_K_ROWS

    def score_fn(c, t):
        rows = pl.ds(pl.multiple_of(t * tk, tk), tk)
        sl = slice(c * LANES, (c + 1) * LANES)
        return _dot_nt(k_ref[rows, sl], q_ref[:, sl])

    def value_fn(c, t):
        return vt_ref[t, c * MLA_V:(c + 1) * MLA_V, :]

    out = _flash_pipeline(pl.program_id(2), scratch_refs, score_fn, value_fn, tk, tq)
    ot = jnp.concatenate([acc_ref[...] / l_ref[...] for l_ref, acc_ref in out], axis=0)
    o_ref[...] = ot.T.astype(BF16)


def _mla_call(qm, km, vtm, batch, seq_len):
    n = qm.shape[0]
    tq, tk, hps = ATTN_Q_ROWS, ATTN_K_ROWS, MLA_HEADS_PER_STEP
    qt = seq_len // tq
    return pl.pallas_call(
        _mla_kernel,
        grid=(batch, MLA_HEADS // hps, qt),
        in_specs=[pl.BlockSpec((tq, hps * LANES), lambda b, h, i: (b * qt + i, h)),
                  pl.BlockSpec((seq_len, hps * LANES), lambda b, h, i: (b, h)),
                  pl.BlockSpec((seq_len // tk, hps * MLA_V, tk), lambda b, h, i: (b, h, 0))],
        out_specs=pl.BlockSpec((tq, hps * MLA_V), lambda b, h, i: (b * qt + i, h)),
        out_shape=jax.ShapeDtypeStruct((n, MLA_HEADS * MLA_V), BF16),
        scratch_shapes=_attn_scratch(hps, MLA_V, tq, tk),
        compiler_params=pltpu.CompilerParams(dimension_semantics=("parallel", "parallel", "arbitrary"),
                                             vmem_limit_bytes=VMEM_LIMIT_BYTES),
        name="mla_attn",
    )(qm, km, vtm)


def _diff_kernel(lam_init, q_ref, k_ref, vt_ref, lq1_ref, lk1_ref, lq2_ref, lk2_ref, subln_ref, o_ref,
                 *scratch_refs):
    tq, tk = ATTN_Q_ROWS, ATTN_K_ROWS
    hps = DIFF_HEADS_PER_STEP

    q = q_ref[...]
    lane = lax.broadcasted_iota(jnp.int32, (tq, LANES), 1)
    qs = []
    for hd in range(hps):
        qh = q[:, hd * LANES:(hd + 1) * LANES]
        zero = jnp.zeros_like(qh)
        qs += [jnp.where(lane < DIFF_HEAD_DIM, qh, zero), jnp.where(lane < DIFF_HEAD_DIM, zero, qh)]

    def score_fn(c, t):
        rows = pl.ds(pl.multiple_of(t * tk, tk), tk)
        hd = c // 2
        return _dot_nt(k_ref[rows, hd * LANES:(hd + 1) * LANES], qs[c])

    def value_fn(c, t):
        hd = c // 2
        return vt_ref[t, hd * DIFF_V_DIM:(hd + 1) * DIFF_V_DIM, :]

    out = _flash_pipeline(pl.program_id(2), scratch_refs, score_fn, value_fn, tk, tq)

    lam = (jnp.exp(jnp.sum(lq1_ref[...] * lk1_ref[...], axis=-1, keepdims=True))
           - jnp.exp(jnp.sum(lq2_ref[...] * lk2_ref[...], axis=-1, keepdims=True)) + lam_init)
    heads = []
    for hd in range(hps):
        (l0, a0), (l1, a1) = out[2 * hd], out[2 * hd + 1]
        ot = a0[...] / l0[...] - lam * (a1[...] / l1[...])
        ot = ot * lax.rsqrt(jnp.sum(ot * ot, axis=0, keepdims=True) * (1.0 / DIFF_V_DIM) + EPS)
        heads.append(ot * subln_ref[...] * (1.0 - lam_init))
    o_ref[...] = jnp.concatenate(heads, axis=0).T.astype(BF16)


def _diff_call(qd, kd, vtd, lq1, lk1, lq2, lk2, subln_col, lam_init, batch, seq_len):
    n = qd.shape[0]
    tq, tk, hps = ATTN_Q_ROWS, ATTN_K_ROWS, DIFF_HEADS_PER_STEP
    qt = seq_len // tq
    small = lambda a: pl.BlockSpec(a.shape, lambda b, h, i: (0, 0))
    return pl.pallas_call(
        functools.partial(_diff_kernel, lam_init),
        grid=(batch, DIFF_HEADS // hps, qt),
        in_specs=[pl.BlockSpec((tq, hps * LANES), lambda b, h, i: (b * qt + i, h)),
                  pl.BlockSpec((seq_len, hps * LANES), lambda b, h, i: (b, h)),
                  pl.BlockSpec((seq_len // tk, hps * DIFF_V_DIM, tk), lambda b, h, i: (b, h, 0)),
                  small(lq1), small(lk1), small(lq2), small(lk2), small(subln_col)],
        out_specs=pl.BlockSpec((tq, hps * LANES), lambda b, h, i: (b * qt + i, h)),
        out_shape=jax.ShapeDtypeStruct((n, DIFF_V_WIDTH), BF16),
        scratch_shapes=_attn_scratch(2 * hps, DIFF_V_DIM, tq, tk),
        compiler_params=pltpu.CompilerParams(dimension_semantics=("parallel", "parallel", "arbitrary"),
                                             vmem_limit_bytes=VMEM_LIMIT_BYTES),
        name="diff_attn",
    )(qd, kd, vtd, lq1, lk1, lq2, lk2, subln_col)


def _merge_kernel(x_ref, om_ref, od_ref, sgm_ref, sgd_ref, wmu_ref, wdu_ref, wout_ref, gffn_ref, wr_hi_ref,
                  wr_lo_ref, br_ref, x1_ref, h2_ref, route_ref, route_t_ref, cnt_ref):
    merged = (sgm_ref[...].astype(F32) * _dot(om_ref[...], wmu_ref[...])
              + sgd_ref[...].astype(F32) * _dot(od_ref[...], wdu_ref[...]))
    x1 = x_ref[...] + _dot(merged.astype(BF16), wout_ref[...])
    x1_ref[...] = x1
    h2 = _rms(x1, x1.shape[-1]) * gffn_ref[...]
    h2_hi = h2.astype(BF16)
    h2_ref[...] = h2_hi

    h2_lo = (h2 - h2_hi.astype(F32)).astype(BF16)
    logits = (_dot(h2_hi, wr_hi_ref[...]) + _dot(h2_lo, wr_hi_ref[...]) + _dot(h2_hi, wr_lo_ref[...])
              + br_ref[...])
    lane = lax.broadcasted_iota(jnp.int32, logits.shape, 1)
    neg = -jnp.inf
    big = jnp.int32(1 << 20)

    def top(vals):
        mx = jnp.max(vals, axis=-1, keepdims=True)
        idx = jnp.min(jnp.where(vals == mx, lane, big), axis=-1, keepdims=True)
        return mx, idx

    gl = jnp.where((lane >= N_EXPERTS) & (lane < N_EXPERTS + N_GROUPS), logits, neg)
    gmax, gidx = top(gl)
    pg_sel = 1.0 / jnp.sum(jnp.exp(gl - gmax), axis=-1, keepdims=True)
    el = jnp.where((lane < N_EXPERTS) & (lane // EXPERTS_PER_GROUP == gidx - N_EXPERTS), logits, neg)
    m1, i1 = top(el)
    m2, i2 = top(jnp.where(lane == i1, neg, el))
    e2 = jnp.exp(m2 - m1)
    w1 = pg_sel / (1.0 + e2)
    w2 = w1 * e2

    tm = logits.shape[0]
    sel = (lane == i1) | (lane == i2)
    earlier = (lax.broadcasted_iota(jnp.int32, (tm, tm), 1) < lax.broadcasted_iota(jnp.int32, (tm, tm), 0))
    rank = _dot(jnp.where(earlier, 1.0, 0.0).astype(BF16), jnp.where(sel, 1.0, 0.0).astype(BF16))
    cnt = jnp.sum(jnp.where(sel, 1.0, 0.0), axis=0, keepdims=True)
    seg = jnp.floor((cnt + (SEG_ALIGN - 1)) * (1.0 / SEG_ALIGN))
    before = (lax.broadcasted_iota(jnp.int32, (LANES, LANES), 0) < lax.broadcasted_iota(jnp.int32, (LANES, LANES), 1))
    off = _dot(jnp.broadcast_to(seg, (8, LANES)).astype(BF16), jnp.where(before, 1.0, 0.0).astype(BF16))[0:1] * SEG_ALIGN
    dest = off + rank
    d1 = jnp.sum(jnp.where(lane == i1, dest, 0.0), axis=-1, keepdims=True)
    d2 = jnp.sum(jnp.where(lane == i2, dest, 0.0), axis=-1, keepdims=True)
    route = jnp.where(lane == 0, d1, jnp.where(lane == 1, d2, jnp.where(lane == 2, w1, jnp.where(lane == 3, w2, 0.0))))
    route_ref[...] = route
    route_t_ref[0] = route.T[0:8, :]
    cnt_ref[0] = seg * SEG_ALIGN


def _merge_call(x2, om, od, sgm, sgd, p):
    n, d = x2.shape
    tm = MERGE_ROWS
    row = lambda i: (i, 0)
    const = lambda i: (0, 0)
    weights = [p["wmu"], p["wdu"], p["wout"], p["gffn"], p["wr_hi"], p["wr_lo"], p["br"]]
    return pl.pallas_call(
        _merge_kernel,
        grid=(n // tm,),
        in_specs=([pl.BlockSpec((tm, a.shape[1]), row) for a in (x2, om, od, sgm, sgd)]
                  + [pl.BlockSpec(w.shape, const) for w in weights]),
        out_specs=[pl.BlockSpec((tm, d), row), pl.BlockSpec((tm, d), row), pl.BlockSpec((tm, LANES), row),
                   pl.BlockSpec((1, 8, tm), lambda i: (i, 0, 0)), pl.BlockSpec((1, 1, LANES), lambda i: (i, 0, 0))],
        out_shape=[jax.ShapeDtypeStruct((n, d), F32), jax.ShapeDtypeStruct((n, d), BF16),
                   jax.ShapeDtypeStruct((n, LANES), F32), jax.ShapeDtypeStruct((n // tm, 8, tm), F32),
                   jax.ShapeDtypeStruct((n // tm, 1, LANES), F32)],
        compiler_params=pltpu.CompilerParams(dimension_semantics=("parallel",), vmem_limit_bytes=VMEM_LIMIT_BYTES),
        name="merge_router",
    )(x2, om, od, sgm, sgd, *weights)


def _segment_copies(i, seg_dst_ref, seg_rows_ref, tile_off_ref, global_ref, tile_ref, sem, to_global):
    def body(e, carry):
        k = i * N_EXPERTS + e
        rows = pl.multiple_of(seg_rows_ref[k], SEG_ALIGN)

        @pl.when(rows > 0)
        def _():
            g = global_ref.at[pl.ds(pl.multiple_of(seg_dst_ref[k], SEG_ALIGN), rows)]
            t = tile_ref.at[pl.ds(pl.multiple_of(tile_off_ref[k], SEG_ALIGN), rows)]
            src, dst = (t, g) if to_global else (g, t)
            pltpu.make_async_copy(src, dst, sem).start()

        return carry

    lax.fori_loop(0, N_EXPERTS, body, 0)


def _wait_rows(tile_ref, rows, sem):
    @pl.when(rows > 0)
    def _():
        view = tile_ref.at[pl.ds(0, pl.multiple_of(rows, SEG_ALIGN))]
        pltpu.make_async_copy(view, view, sem).wait()


def _zero_unused_rows(tail_dst_ref, tail_rows_ref, n_used_ref, xs_ref, zero_ref, sem, start):
    n_tiles = xs_ref.shape[0] // EXPERT_ROWS
    if start:
        zero_ref[...] = jnp.zeros(zero_ref.shape, BF16)

    def tail(e, total):
        rows = pl.multiple_of(tail_rows_ref[e], SEG_ALIGN)
        if start:
            @pl.when(rows > 0)
            def _():
                dst = xs_ref.at[pl.ds(pl.multiple_of(tail_dst_ref[e], SEG_ALIGN), rows)]
                pltpu.make_async_copy(zero_ref.at[pl.ds(0, rows)], dst, sem).start()

        return total + rows

    total = lax.fori_loop(0, N_EXPERTS, tail, 0)
    if not start:
        _wait_rows(xs_ref, total + (n_tiles - n_used_ref[0]) * EXPERT_ROWS, sem)
        return

    def unused(t, carry):
        dst = xs_ref.at[pl.ds(pl.multiple_of(t * EXPERT_ROWS, EXPERT_ROWS), EXPERT_ROWS)]
        pltpu.make_async_copy(zero_ref, dst, sem).start()
        return carry

    lax.fori_loop(n_used_ref[0], n_tiles, unused, 0)


def _sort_kernel(seg_dst_ref, seg_rows_ref, tile_off_ref, tile_rows_ref, tail_dst_ref, tail_rows_ref, n_used_ref,
                 h2_ref, route_t_ref, xs_ref, sorted_ref, zero_ref, sem, zero_sem):
    i = pl.program_id(0)
    tm = h2_ref.shape[0]

    @pl.when(i == 0)
    def _():
        _zero_unused_rows(tail_dst_ref, tail_rows_ref, n_used_ref, xs_ref, zero_ref, zero_sem, True)

    d1 = route_t_ref[0, 0:1, :].astype(jnp.int32)
    d2 = route_t_ref[0, 1:2, :].astype(jnp.int32)
    r = lax.broadcasted_iota(jnp.int32, (SORT_ROWS, tm), 0)
    perm = jnp.where((r == d1) | (r == d2), 1.0, 0.0).astype(BF16)
    slot = i % 2
    sorted_ref[slot] = _dot(perm, h2_ref[...]).astype(BF16)
    _segment_copies(i, seg_dst_ref, seg_rows_ref, tile_off_ref, xs_ref, sorted_ref.at[slot], sem.at[slot], True)

    @pl.when(i > 0)
    def _():
        _wait_rows(sorted_ref.at[1 - slot], tile_rows_ref[jnp.maximum(i - 1, 0)], sem.at[1 - slot])

    @pl.when(i == pl.num_programs(0) - 1)
    def _():
        _wait_rows(sorted_ref.at[slot], tile_rows_ref[i], sem.at[slot])
        _zero_unused_rows(tail_dst_ref, tail_rows_ref, n_used_ref, xs_ref, zero_ref, zero_sem, False)


def _sort_call(h2, route_t, sched, max_rows):
    n, d = h2.shape
    tm = ROUTE_ROWS
    return pl.pallas_call(
        _sort_kernel,
        grid_spec=pltpu.PrefetchScalarGridSpec(
            num_scalar_prefetch=7,
            grid=(n // tm,),
            in_specs=[pl.BlockSpec((tm, d), lambda i, *_: (i, 0)),
                      pl.BlockSpec((1, 8, tm), lambda i, *_: (i, 0, 0))],
            out_specs=pl.BlockSpec(memory_space=pl.ANY),
            scratch_shapes=[pltpu.VMEM((2, SORT_ROWS, d), BF16), pltpu.VMEM((EXPERT_ROWS, d), BF16),
                            pltpu.SemaphoreType.DMA((2,)), pltpu.SemaphoreType.DMA(())],
        ),
        out_shape=jax.ShapeDtypeStruct((max_rows, d), BF16),
        compiler_params=pltpu.CompilerParams(dimension_semantics=("arbitrary",), vmem_limit_bytes=VMEM_LIMIT_BYTES),
        name="moe_sort",
    )(sched["seg_dst"], sched["seg_rows"], sched["tile_off"], sched["tile_rows"], sched["tail_dst"],
      sched["tail_rows"], sched["n_used"], h2, route_t)


def _expert_kernel(tile_expert_ref, n_used_ref, xs_ref, wg_ref, wu_ref, wd_ref, ys_ref, wg_bf, wu_bf, wd_bf):
    t = pl.program_id(0)
    used = t < n_used_ref[0]

    @pl.when(used & ((t == 0) | (tile_expert_ref[t] != tile_expert_ref[jnp.maximum(t - 1, 0)])))
    def _():
        wg_bf[...] = wg_ref[0].astype(BF16)
        wu_bf[...] = wu_ref[0].astype(BF16)
        wd_bf[...] = wd_ref[0].astype(BF16)

    @pl.when(used)
    def _():
        xs = xs_ref[...]
        gate = _dot(xs, wg_bf[...])
        up = _dot(xs, wu_bf[...])
        hidden = (gate * jax.nn.sigmoid(gate) * up).astype(BF16)
        ys_ref[...] = _dot(hidden, wd_bf[...]).astype(BF16)

    @pl.when(jnp.logical_not(used))
    def _():
        ys_ref[...] = jnp.zeros(ys_ref.shape, BF16)


def _expert_call(xs, wg, wu, wd, sched):
    rows, d = xs.shape
    tr = EXPERT_ROWS
    blk = lambda t, te, nu: (jnp.minimum(t, nu[0] - 1), 0)
    wsel = lambda t, te, nu: (te[jnp.minimum(t, nu[0] - 1)], 0, 0)
    return pl.pallas_call(
        _expert_kernel,
        grid_spec=pltpu.PrefetchScalarGridSpec(
            num_scalar_prefetch=2,
            grid=(rows // tr,),
            in_specs=[pl.BlockSpec((tr, d), blk),
                      pl.BlockSpec((1, d, EXPERT_FF), wsel), pl.BlockSpec((1, d, EXPERT_FF), wsel),
                      pl.BlockSpec((1, EXPERT_FF, d), wsel)],
            out_specs=pl.BlockSpec((tr, d), lambda t, te, nu: (t, 0)),
            scratch_shapes=[pltpu.VMEM((d, EXPERT_FF), BF16), pltpu.VMEM((d, EXPERT_FF), BF16),
                            pltpu.VMEM((EXPERT_FF, d), BF16)],
        ),
        out_shape=jax.ShapeDtypeStruct((rows, d), BF16),
        compiler_params=pltpu.CompilerParams(dimension_semantics=("arbitrary",), vmem_limit_bytes=VMEM_LIMIT_BYTES),
        name="moe_experts",
    )(sched["tile_expert"], sched["n_used"], xs, wg, wu, wd)


def _combine_kernel(seg_dst_ref, seg_rows_ref, tile_off_ref, tile_rows_ref, ys_ref, route_ref, x1_ref, o_ref,
                    buf_ref, sem):
    i = pl.program_id(0)
    tm = x1_ref.shape[0]
    slot = i % 2

    def fetch(tile, into):
        buf_ref[into] = jnp.zeros(buf_ref.shape[1:], BF16)
        _segment_copies(tile, seg_dst_ref, seg_rows_ref, tile_off_ref, ys_ref, buf_ref.at[into], sem.at[into], False)

    @pl.when(i == 0)
    def _():
        fetch(i, slot)

    @pl.when(i + 1 < pl.num_programs(0))
    def _():
        fetch(i + 1, 1 - slot)

    route = route_ref[...]
    d1 = route[:, 0:1].astype(jnp.int32)
    d2 = route[:, 1:2].astype(jnp.int32)
    w1 = route[:, 2:3]
    w2 = route[:, 3:4]
    r = lax.broadcasted_iota(jnp.int32, (tm, SORT_ROWS), 1)
    weights = (jnp.where(r == d1, w1, 0.0) + jnp.where(r == d2, w2, 0.0)).astype(BF16)
    _wait_rows(buf_ref.at[slot], tile_rows_ref[i], sem.at[slot])
    o_ref[...] = x1_ref[...] + _dot(weights, buf_ref[slot])


def _combine_call(ys, route, x1, sched):
    n, d = x1.shape
    tm = ROUTE_ROWS
    return pl.pallas_call(
        _combine_kernel,
        grid_spec=pltpu.PrefetchScalarGridSpec(
            num_scalar_prefetch=4,
            grid=(n // tm,),
            in_specs=[pl.BlockSpec(memory_space=pl.ANY),
                      pl.BlockSpec((tm, LANES), lambda i, *_: (i, 0)),
                      pl.BlockSpec((tm, d), lambda i, *_: (i, 0))],
            out_specs=pl.BlockSpec((tm, d), lambda i, *_: (i, 0)),
            scratch_shapes=[pltpu.VMEM((2, SORT_ROWS, d), BF16), pltpu.SemaphoreType.DMA((2,))],
        ),
        out_shape=jax.ShapeDtypeStruct((n, d), F32),
        compiler_params=pltpu.CompilerParams(dimension_semantics=("arbitrary",), vmem_limit_bytes=VMEM_LIMIT_BYTES),
        name="moe_combine",
    )(sched["seg_dst"], sched["seg_rows"], sched["tile_off"], sched["tile_rows"], ys, route, x1)


def _schedule_kernel(cnt_ref, seg_dst_ref, tile_off_ref, tile_rows_ref, misc_ref):
    hp = functools.partial(jnp.dot, preferred_element_type=F32, precision=lax.Precision.HIGHEST)
    cnt = cnt_ref[...]
    n_tiles = cnt.shape[0]
    tile_before = jnp.where(lax.broadcasted_iota(jnp.int32, (n_tiles, n_tiles), 1)
                            < lax.broadcasted_iota(jnp.int32, (n_tiles, n_tiles), 0), 1.0, 0.0)
    expert_before = jnp.where(lax.broadcasted_iota(jnp.int32, (LANES, LANES), 0)
                              < lax.broadcasted_iota(jnp.int32, (LANES, LANES), 1), 1.0, 0.0)
    expert_rows = jnp.sum(cnt, axis=0, keepdims=True)
    region = jnp.floor((expert_rows + (EXPERT_ROWS - 1)) * (1.0 / EXPERT_ROWS)) * EXPERT_ROWS
    region_start = hp(jnp.broadcast_to(region, (8, LANES)), expert_before)[0:1]
    seg_dst_ref[...] = (region_start + hp(tile_before, cnt)).astype(jnp.int32)
    tile_off_ref[...] = hp(cnt, expert_before).astype(jnp.int32)
    tile_rows_ref[...] = jnp.broadcast_to(jnp.sum(cnt, axis=-1, keepdims=True), cnt.shape).astype(jnp.int32)
    n_used = jnp.sum(region, axis=-1, keepdims=True) * (1.0 / EXPERT_ROWS)
    row = lax.broadcasted_iota(jnp.int32, (8, LANES), 0)
    misc = jnp.where(row == 0, region_start + expert_rows,
                     jnp.where(row == 1, region - expert_rows,
                               jnp.where(row == 2, region_start + region, n_used)))
    misc_ref[...] = misc.astype(jnp.int32)


def _moe_schedule(cnt, n_tokens):
    n_tiles = cnt.shape[0]
    table = jax.ShapeDtypeStruct((n_tiles, LANES), jnp.int32)
    seg_dst, tile_off, tile_rows, misc = pl.pallas_call(
        _schedule_kernel,
        out_shape=[table, table, table, jax.ShapeDtypeStruct((8, LANES), jnp.int32)],
        name="moe_schedule",
    )(cnt.reshape(n_tiles, LANES))
    max_rows = 2 * n_tokens + n_tiles * N_EXPERTS * (SEG_ALIGN - 1) + N_EXPERTS * (EXPERT_ROWS - 1)
    max_tiles = -(-max_rows // EXPERT_ROWS)
    tile_start = jnp.arange(max_tiles, dtype=jnp.int32) * EXPERT_ROWS
    region_end = misc[2, :N_EXPERTS]
    tile_expert = jnp.minimum(jnp.sum((region_end[None, :] <= tile_start[:, None]).astype(jnp.int32), axis=1),
                              N_EXPERTS - 1)
    flat = lambda a: a[:, :N_EXPERTS].reshape(-1)
    sched = {
        "seg_dst": flat(seg_dst),
        "seg_rows": flat(cnt.reshape(n_tiles, LANES).astype(jnp.int32)),
        "tile_off": flat(tile_off),
        "tile_rows": tile_rows[:, 0],
        "tail_dst": misc[0, :N_EXPERTS],
        "tail_rows": misc[1, :N_EXPERTS],
        "tile_expert": tile_expert,
        "n_used": misc[3, :1],
    }
    return sched, max_tiles * EXPERT_ROWS


def _rope_lane_tables(seq_len, rot_dim, period, first):
    half = rot_dim // 2
    pos = jnp.arange(seq_len, dtype=F32)
    inv = 1.0 / (ROPE_THETA ** (jnp.arange(0, rot_dim, 2, dtype=F32) / rot_dim))
    ang = pos[:, None] * inv[None, :]
    cos, sin = jnp.cos(ang), jnp.sin(ang)
    lane = jnp.arange(LANES)
    rel = (lane % period) - first
    active = (rel >= 0) & (rel < rot_dim)
    idx = jnp.clip(rel, 0, rot_dim - 1) % half
    sign = jnp.where(rel < half, -1.0, 1.0)
    c = jnp.where(active[None, :], cos[:, idx], 1.0)
    s = jnp.where(active[None, :], sin[:, idx] * sign[None, :], 0.0)
    return c.astype(F32), s.astype(F32)


def _head_pad(w, heads, width):
    r = w.shape[0]
    w = w.reshape(r, heads, width)
    return jnp.pad(w, ((0, 0), (0, 0), (0, LANES - width))).reshape(r, heads * LANES)


def _layer_params(l, seq_len, norm_mix, w_in, mla_q_latent_norm, w_mla_uq, mla_kv_latent_norm, w_mla_ukv,
                  mla_q_gain, mla_k_gain, diff_q_gain, diff_k_gain, w_mla_up, w_diff_up, w_out, norm_ffn,
                  w_router_group, b_router_group, w_router_expert, b_router_expert):
    d = w_in.shape[1]
    sizes = (MLA_Q_RANK, MLA_KV_RANK, MLA_ROPE, DIFF_QK_WIDTH, DIFF_QK_WIDTH, DIFF_V_WIDTH, d, d)
    offs = [0]
    for s in sizes:
        offs.append(offs[-1] + s)
    wi = w_in[l]
    seg = [wi[:, offs[k]:offs[k + 1]] for k in range(len(sizes))]
    row = lambda g: g.astype(F32)[None, :]
    p = {}
    p["gmix"] = row(norm_mix[l])
    p["wql"] = seg[0].astype(BF16)
    p["wkvl"] = seg[1].astype(BF16)
    p["wkr"] = jnp.pad(seg[2], ((0, 0), (MLA_NOPE, LANES - MLA_QK))).astype(BF16)
    p["wdq"], p["wdk"] = seg[3].astype(BF16), seg[4].astype(BF16)
    p["wdvt"] = seg[5].T.astype(BF16)
    p["wgm"], p["wgd"] = seg[6].astype(BF16), seg[7].astype(BF16)
    p["gql"] = row(mla_q_latent_norm[l])
    p["wuq"] = _head_pad(w_mla_uq[l], MLA_HEADS, MLA_QK).astype(BF16)
    p["gkvl"] = row(mla_kv_latent_norm[l])
    ukv = w_mla_ukv[l].reshape(MLA_KV_RANK, MLA_HEADS, MLA_NOPE + MLA_V)
    p["wuk"] = _head_pad(ukv[:, :, :MLA_NOPE].reshape(MLA_KV_RANK, -1), MLA_HEADS, MLA_NOPE).astype(BF16)
    p["wuvt"] = ukv[:, :, MLA_NOPE:].reshape(MLA_KV_RANK, -1).T.astype(BF16)
    p["gq"] = jnp.pad(mla_q_gain[l].astype(F32), (0, LANES - MLA_QK))[None, :]
    p["gk"] = jnp.pad(mla_k_gain[l].astype(F32), (0, LANES - MLA_QK))[None, :]
    p["gdq"] = jnp.tile(diff_q_gain[l].astype(F32), 2)[None, :]
    p["gdk"] = jnp.tile(diff_k_gain[l].astype(F32), 2)[None, :]
    p["cm"], p["sm"] = _rope_lane_tables(seq_len, MLA_ROPE, LANES, MLA_NOPE)
    p["cd"], p["sd"] = _rope_lane_tables(seq_len, DIFF_ROPE, DIFF_HEAD_DIM, 0)
    p["wmu"] = w_mla_up[l].astype(BF16)
    p["wdu"] = w_diff_up[l].astype(BF16)
    p["wout"] = w_out[l].astype(BF16)
    p["gffn"] = row(norm_ffn[l])
    wr = jnp.concatenate([w_router_expert[l], w_router_group[l]], axis=1).astype(F32)
    wr = jnp.pad(wr, ((0, 0), (0, LANES - wr.shape[1])))
    p["wr_hi"] = wr.astype(BF16)
    p["wr_lo"] = (wr - p["wr_hi"].astype(F32)).astype(BF16)
    br = jnp.concatenate([b_router_expert[l], b_router_group[l]]).astype(F32)
    p["br"] = jnp.pad(br, (0, LANES - br.shape[0]))[None, :]
    return p


def kernel(x, norm_mix, w_in, mla_q_latent_norm, w_mla_uq, mla_kv_latent_norm, w_mla_ukv, mla_q_gain, mla_k_gain, diff_q_gain, diff_k_gain, lambda_q1, lambda_k1, lambda_q2, lambda_k2, diff_subln, w_mla_up, w_diff_up, w_out, norm_ffn, w_router_group, b_router_group, w_router_expert, b_router_expert, w_expert_gate, w_expert_up, w_expert_down):
    batch, seq_len, d = x.shape
    x2 = x.reshape(batch * seq_len, d)
    row = lambda g: g.astype(F32)[None, :]
    for l in range(norm_mix.shape[0]):
        lam_init = 0.8 - 0.6 * math.exp(-0.3 * l)
        p = _layer_params(l, seq_len, norm_mix, w_in, mla_q_latent_norm, w_mla_uq, mla_kv_latent_norm, w_mla_ukv,
                          mla_q_gain, mla_k_gain, diff_q_gain, diff_k_gain, w_mla_up, w_diff_up, w_out, norm_ffn,
                          w_router_group, b_router_group, w_router_expert, b_router_expert)
        qm, km, vtm, qd, kd, vtd, sgm, sgd = _proj_call(x2, seq_len, p)
        om = _mla_call(qm, km, vtm, batch, seq_len)
        od = _diff_call(qd, kd, vtd, row(lambda_q1[l]), row(lambda_k1[l]), row(lambda_q2[l]), row(lambda_k2[l]),
                        diff_subln[l].astype(F32)[:, None], lam_init, batch, seq_len)
        x1, h2, route, route_t, cnt = _merge_call(x2, om, od, sgm, sgd, p)
        sched, max_rows = _moe_schedule(cnt, x2.shape[0])
        xs = _sort_call(h2, route_t, sched, max_rows)
        ys = _expert_call(xs, w_expert_gate[l], w_expert_up[l], w_expert_down[l], sched)
        x2 = _combine_call(ys, route, x1, sched)
    return x2.reshape(batch, seq_len, d)
```

```python
import functools
import math

import jax
import jax.numpy as jnp
from jax import lax
from jax.experimental import pallas as pl
from jax.experimental.pallas import tpu as pltpu

CHUNK = 64
ROPE_THETA = 500000.0
EPS = 1e-6

MLA_HEADS = 8
MLA_NOPE = 64
MLA_ROPE = 32
MLA_V = 64
MLA_QK = MLA_NOPE + MLA_ROPE
MLA_Q_RANK = 256
MLA_KV_RANK = 128

DIFF_HEADS = 4
DIFF_HEAD_DIM = 64
DIFF_V_DIM = 2 * DIFF_HEAD_DIM
DIFF_ROPE = DIFF_HEAD_DIM // 4
DIFF_QK_WIDTH = DIFF_HEADS * 2 * DIFF_HEAD_DIM
DIFF_V_WIDTH = DIFF_HEADS * DIFF_V_DIM

N_GROUPS = 4
EXPERTS_PER_GROUP = 8
N_EXPERTS = N_GROUPS * EXPERTS_PER_GROUP
EXPERT_FF = 256

LANES = 128
VMEM_LIMIT_BYTES = 48 * 1024 * 1024

PROJ_ROWS = 512
ATTN_Q_ROWS = 256
ATTN_K_ROWS = 256
MERGE_ROWS = 512
ROUTE_ROWS = MERGE_ROWS
SEG_ALIGN = 16
SORT_ROWS = 2 * ROUTE_ROWS + N_EXPERTS * SEG_ALIGN
EXPERT_ROWS = 256
MLA_HEADS_PER_STEP = 4
DIFF_HEADS_PER_STEP = 2
LOG2E = 1.4426950408889634

BF16 = jnp.bfloat16
F32 = jnp.float32


def _dot(a, b):
    return jnp.dot(a, b, preferred_element_type=F32)


def _dot_nt(a, b):
    return lax.dot_general(a, b, (((1,), (1,)), ((), ())), preferred_element_type=F32)


def _rms(x, width):
    return x * lax.rsqrt(jnp.sum(x * x, axis=-1, keepdims=True) * (1.0 / width) + EPS)


def _rotary_partner(y, half):
    lane = lax.broadcasted_iota(jnp.int32, y.shape, 1)
    up = pltpu.roll(y, LANES - half, 1)
    down = pltpu.roll(y, half, 1)
    return jnp.where((lane // half) % 2 == 0, up, down)


def _swap_row_blocks(y, first, half, period):
    parts = []
    for base in range(0, y.shape[0], period):
        a = base + first
        parts += [y[base:a], y[a + half:a + 2 * half], y[a:a + half], y[a + 2 * half:base + period]]
    return jnp.concatenate([p for p in parts if p.shape[0]], axis=0)


def _store_k_tiles(o_ref, vt):
    tk = o_ref.shape[-1]
    for c in range(o_ref.shape[0]):
        o_ref[c] = vt[:, c * tk:(c + 1) * tk].astype(BF16)


def _proj_kernel(x_ref, gmix_ref, wql_ref, wkvl_ref, wkr_ref, wdk_ref, wdqvt_ref, wgm_ref, wgd_ref,
                 gql_ref, wuqt_ref, gkvl_ref, wuk_ref, wuvt_ref, gkn_ref,
                 aq_ref, bq_ref, adq_ref, bdq_ref, ak_ref, bk_ref, adk_ref, bdk_ref,
                 qmt_ref, km_ref, vtm_ref, qdt_ref, kd_ref, vtd_ref, sgm_ref, sgd_ref):
    x = x_ref[...]
    h = (_rms(x, x.shape[-1]) * gmix_ref[...]).astype(BF16)

    ql = (_rms(_dot(h, wql_ref[...]), MLA_Q_RANK) * gql_ref[...]).astype(BF16)
    qt = _dot_nt(wuqt_ref[...], ql)
    aq, bq = aq_ref[...], bq_ref[...]
    for hd in range(MLA_HEADS):
        rows = slice(hd * LANES, (hd + 1) * LANES)
        qh = qt[rows]
        r = lax.rsqrt(jnp.sum(qh * qh, axis=0, keepdims=True) * (1.0 / MLA_QK) + EPS)
        y = (qh * aq + _swap_row_blocks(qh, MLA_NOPE, MLA_ROPE // 2, LANES) * bq) * r
        qmt_ref[rows, :] = y.astype(BF16)

    kvl = (_rms(_dot(h, wkvl_ref[...]), MLA_KV_RANK) * gkvl_ref[...]).astype(BF16)
    kr = _dot(h, wkr_ref[...])
    kr_rot = kr * ak_ref[...] + _rotary_partner(kr, MLA_ROPE // 2) * bk_ref[...]
    kr_ss = jnp.sum(kr * kr, axis=-1, keepdims=True)
    kn = _dot(kvl, wuk_ref[...])
    _store_k_tiles(vtm_ref, _dot_nt(wuvt_ref[...], kvl))
    gkn = gkn_ref[...]
    for hd in range(MLA_HEADS):
        sl = slice(hd * LANES, (hd + 1) * LANES)
        knh = kn[:, sl]
        r = lax.rsqrt((jnp.sum(knh * knh, axis=-1, keepdims=True) + kr_ss) * (1.0 / MLA_QK) + EPS)
        km_ref[:, sl] = ((knh * gkn + kr_rot) * r).astype(BF16)

    qvt = _dot_nt(wdqvt_ref[...], h)
    _store_k_tiles(vtd_ref, qvt[DIFF_QK_WIDTH:])
    adq, bdq = adq_ref[...], bdq_ref[...]
    for hd in range(DIFF_HEADS):
        rows = slice(hd * LANES, (hd + 1) * LANES)
        qh = qvt[rows]
        t = qh * adq + _swap_row_blocks(qh, 0, DIFF_ROPE // 2, DIFF_HEAD_DIM) * bdq
        halves = []
        for f in range(2):
            part = qh[f * DIFF_HEAD_DIM:(f + 1) * DIFF_HEAD_DIM]
            r = lax.rsqrt(jnp.sum(part * part, axis=0, keepdims=True) * (1.0 / DIFF_HEAD_DIM) + EPS)
            halves.append(t[f * DIFF_HEAD_DIM:(f + 1) * DIFF_HEAD_DIM] * r)
        qdt_ref[rows, :] = jnp.concatenate(halves, axis=0).astype(BF16)

    kd = _dot(h, wdk_ref[...])
    adk, bdk = adk_ref[...], bdk_ref[...]
    for hd in range(DIFF_HEADS):
        sl = slice(hd * LANES, (hd + 1) * LANES)
        th = kd[:, sl]
        lane = lax.broadcasted_iota(jnp.int32, th.shape, 1)
        sq = th * th
        lo = jnp.sum(jnp.where(lane < DIFF_HEAD_DIM, sq, 0.0), axis=-1, keepdims=True)
        tot = jnp.sum(sq, axis=-1, keepdims=True)
        r = lax.rsqrt(jnp.where(lane < DIFF_HEAD_DIM, lo, tot - lo) * (1.0 / DIFF_HEAD_DIM) + EPS)
        kd_ref[:, sl] = ((th * adk + _rotary_partner(th, DIFF_ROPE // 2) * bdk) * r).astype(BF16)

    sgm_ref[...] = jax.nn.sigmoid(_dot(h, wgm_ref[...])).astype(BF16)
    sgd_ref[...] = jax.nn.sigmoid(_dot(h, wgd_ref[...])).astype(BF16)


def _proj_call(x2, seq_len, p):
    n, d = x2.shape
    tm = PROJ_ROWS
    pos_blocks = seq_len // tm
    row = lambda i: (i, 0)
    col = lambda i: (0, i)
    const = lambda i: (0, 0)
    weights = [p["gmix"], p["wql"], p["wkvl"], p["wkr"], p["wdk"], p["wdqvt"], p["wgm"], p["wgd"],
               p["gql"], p["wuqt"], p["gkvl"], p["wuk"], p["wuvt"], p["gkn"]]
    feature_major_tables = [p["aq"], p["bq"], p["adq"], p["bdq"]]
    token_major_tables = [p["ak"], p["bk"], p["adk"], p["bdk"]]
    in_specs = ([pl.BlockSpec((tm, d), row)]
                + [pl.BlockSpec(w.shape, const) for w in weights]
                + [pl.BlockSpec((LANES, tm), lambda i: (0, i % pos_blocks)) for _ in feature_major_tables]
                + [pl.BlockSpec((tm, LANES), lambda i: (i % pos_blocks, 0)) for _ in token_major_tables])
    tk = ATTN_K_ROWS
    k_tiles = lambda width: (pl.BlockSpec((tm // tk, width, tk), lambda i: (i, 0, 0)),
                             jax.ShapeDtypeStruct((n // tk, width, tk), BF16))
    token_major = lambda width: (pl.BlockSpec((tm, width), row), jax.ShapeDtypeStruct((n, width), BF16))
    feature_major = lambda width: (pl.BlockSpec((width, tm), col), jax.ShapeDtypeStruct((width, n), BF16))
    outs = [feature_major(MLA_HEADS * LANES), token_major(MLA_HEADS * LANES), k_tiles(MLA_HEADS * MLA_V),
            feature_major(DIFF_QK_WIDTH), token_major(DIFF_QK_WIDTH), k_tiles(DIFF_V_WIDTH),
            token_major(d), token_major(d)]
    return pl.pallas_call(
        _proj_kernel,
        grid=(n // tm,),
        in_specs=in_specs,
        out_specs=[o[0] for o in outs],
        out_shape=[o[1] for o in outs],
        compiler_params=pltpu.CompilerParams(dimension_semantics=("parallel",), vmem_limit_bytes=VMEM_LIMIT_BYTES),
        name="proj",
    )(x2, *weights, *feature_major_tables, *token_major_tables)


def _chunk_mask_t(tk, tq):
    kc = lax.broadcasted_iota(jnp.int32, (tk, tq), 0) // CHUNK
    qc = lax.broadcasted_iota(jnp.int32, (tk, tq), 1) // CHUNK
    return kc <= qc


def _softmax_step_t(st, vt, m_ref, l_ref, acc_ref):
    m_prev = m_ref[...]
    m_new = jnp.maximum(m_prev, jnp.max(st, axis=0, keepdims=True))
    alpha = jnp.exp2(m_prev - m_new)
    pr = jnp.exp2(st - m_new)
    l_ref[...] = alpha * l_ref[...] + jnp.sum(pr, axis=0, keepdims=True)
    acc_ref[...] = alpha * acc_ref[...] + _dot(vt, pr.astype(BF16))
    m_ref[...] = m_new


def _attn_scratch(chains, dv, tq, tk):
    per_chain = [pltpu.VMEM((1, tq), F32), pltpu.VMEM((1, tq), F32), pltpu.VMEM((dv, tq), F32),
                 pltpu.VMEM((tk, tq), F32), pltpu.VMEM((tk, tq), F32)]
    return per_chain * chains


def _flash_pipeline(last_tile, scratch_refs, score_fn, value_fn, tk, tq):
    n_chains = len(scratch_refs) // 5
    chains = [scratch_refs[5 * c:5 * c + 5] for c in range(n_chains)]
    for m_ref, l_ref, acc_ref, _, _ in chains:
        m_ref[...] = jnp.full(m_ref.shape, -jnp.inf, F32)
        l_ref[...] = jnp.zeros(l_ref.shape, F32)
        acc_ref[...] = jnp.zeros(acc_ref.shape, F32)

    def scores(t, slot):
        for c, ch in enumerate(chains):
            ch[3 + slot][...] = score_fn(c, t)

    def update(t, slot, masked):
        for c, ch in enumerate(chains):
            st = ch[3 + slot][...]
            if masked:
                st = jnp.where(_chunk_mask_t(tk, tq), st, -jnp.inf)
            _softmax_step_t(st, value_fn(c, t), ch[0], ch[1], ch[2])

    scores(0, 0)

    def pair(p, carry):
        t = 2 * p
        scores(t + 1, 1)
        update(t, 0, False)
        scores(t + 2, 0)
        update(t + 1, 1, False)
        return carry

    lax.fori_loop(0, last_tile // 2, pair, 0)

    @pl.when(last_tile % 2 == 0)
    def _():
        update(last_tile, 0, True)

    @pl.when(last_tile % 2 == 1)
    def _():
        scores(last_tile, 1)
        update(last_tile - 1, 0, False)
        update(last_tile, 1, True)

    return [(ch[1], ch[2]) for ch in chains]


def _mla_kernel(qt_ref, k_ref, vt_ref, o_ref, *scratch_refs):
    tq, tk = ATTN_Q_ROWS, ATTN_K_ROWS

    def score_fn(c, t):
        rows = pl.ds(pl.multiple_of(t * tk, tk), tk)
        sl = slice(c * LANES, (c + 1) * LANES)
        return _dot(k_ref[rows, sl], qt_ref[sl, :])

    def value_fn(c, t):
        return vt_ref[t, c * MLA_V:(c + 1) * MLA_V, :]

    out = _flash_pipeline(pl.program_id(2), scratch_refs, score_fn, value_fn, tk, tq)
    ot = jnp.concatenate([acc_ref[...] / l_ref[...] for l_ref, acc_ref in out], axis=0)
    o_ref[...] = ot.T.astype(BF16)


def _mla_call(qmt, km, vtm, batch, seq_len):
    n = km.shape[0]
    tq, tk, hps = ATTN_Q_ROWS, ATTN_K_ROWS, MLA_HEADS_PER_STEP
    qt = seq_len // tq
    return pl.pallas_call(
        _mla_kernel,
        grid=(batch, MLA_HEADS // hps, qt),
        in_specs=[pl.BlockSpec((hps * LANES, tq), lambda b, h, i: (h, b * qt + i)),
                  pl.BlockSpec((seq_len, hps * LANES), lambda b, h, i: (b, h)),
                  pl.BlockSpec((seq_len // tk, hps * MLA_V, tk), lambda b, h, i: (b, h, 0))],
        out_specs=pl.BlockSpec((tq, hps * MLA_V), lambda b, h, i: (b * qt + i, h)),
        out_shape=jax.ShapeDtypeStruct((n, MLA_HEADS * MLA_V), BF16),
        scratch_shapes=_attn_scratch(hps, MLA_V, tq, tk),
        compiler_params=pltpu.CompilerParams(dimension_semantics=("parallel", "parallel", "arbitrary"),
                                             vmem_limit_bytes=VMEM_LIMIT_BYTES),
        name="mla_attn",
    )(qmt, km, vtm)


def _diff_kernel(lam_init, qt_ref, k_ref, vt_ref, lq1_ref, lk1_ref, lq2_ref, lk2_ref, subln_ref, o_ref,
                 *scratch_refs):
    tq, tk = ATTN_Q_ROWS, ATTN_K_ROWS
    hps = DIFF_HEADS_PER_STEP

    qs = []
    for hd in range(hps):
        qh = qt_ref[hd * LANES:(hd + 1) * LANES, :]
        zero = jnp.zeros((DIFF_HEAD_DIM, tq), BF16)
        qs += [jnp.concatenate([qh[:DIFF_HEAD_DIM], zero], axis=0), jnp.concatenate([zero, qh[DIFF_HEAD_DIM:]], axis=0)]

    def score_fn(c, t):
        rows = pl.ds(pl.multiple_of(t * tk, tk), tk)
        hd = c // 2
        return _dot(k_ref[rows, hd * LANES:(hd + 1) * LANES], qs[c])

    def value_fn(c, t):
        hd = c // 2
        return vt_ref[t, hd * DIFF_V_DIM:(hd + 1) * DIFF_V_DIM, :]

    out = _flash_pipeline(pl.program_id(2), scratch_refs, score_fn, value_fn, tk, tq)

    lam = (jnp.exp(jnp.sum(lq1_ref[...] * lk1_ref[...], axis=-1, keepdims=True))
           - jnp.exp(jnp.sum(lq2_ref[...] * lk2_ref[...], axis=-1, keepdims=True)) + lam_init)
    heads = []
    for hd in range(hps):
        (l0, a0), (l1, a1) = out[2 * hd], out[2 * hd + 1]
        ot = a0[...] / l0[...] - lam * (a1[...] / l1[...])
        ot = ot * lax.rsqrt(jnp.sum(ot * ot, axis=0, keepdims=True) * (1.0 / DIFF_V_DIM) + EPS)
        heads.append(ot * subln_ref[...] * (1.0 - lam_init))
    o_ref[...] = jnp.concatenate(heads, axis=0).T.astype(BF16)


def _diff_call(qdt, kd, vtd, lq1, lk1, lq2, lk2, subln_col, lam_init, batch, seq_len):
    n = kd.shape[0]
    tq, tk, hps = ATTN_Q_ROWS, ATTN_K_ROWS, DIFF_HEADS_PER_STEP
    qt = seq_len // tq
    small = lambda a: pl.BlockSpec(a.shape, lambda b, h, i: (0, 0))
    return pl.pallas_call(
        functools.partial(_diff_kernel, lam_init),
        grid=(batch, DIFF_HEADS // hps, qt),
        in_specs=[pl.BlockSpec((hps * LANES, tq), lambda b, h, i: (h, b * qt + i)),
                  pl.BlockSpec((seq_len, hps * LANES), lambda b, h, i: (b, h)),
                  pl.BlockSpec((seq_len // tk, hps * DIFF_V_DIM, tk), lambda b, h, i: (b, h, 0)),
                  small(lq1), small(lk1), small(lq2), small(lk2), small(subln_col)],
        out_specs=pl.BlockSpec((tq, hps * LANES), lambda b, h, i: (b * qt + i, h)),
        out_shape=jax.ShapeDtypeStruct((n, DIFF_V_WIDTH), BF16),
        scratch_shapes=_attn_scratch(2 * hps, DIFF_V_DIM, tq, tk),
        compiler_params=pltpu.CompilerParams(dimension_semantics=("parallel", "parallel", "arbitrary"),
                                             vmem_limit_bytes=VMEM_LIMIT_BYTES),
        name="diff_attn",
    )(qdt, kd, vtd, lq1, lk1, lq2, lk2, subln_col)


def _merge_kernel(x_ref, om_ref, od_ref, sgm_ref, sgd_ref, wmu_ref, wdu_ref, wout_ref, gffn_ref, wr_hi_ref,
                  wr_lo_ref, br_ref, x1_ref, h2_ref, route_ref, route_t_ref, cnt_ref):
    merged = (sgm_ref[...].astype(F32) * _dot(om_ref[...], wmu_ref[...])
              + sgd_ref[...].astype(F32) * _dot(od_ref[...], wdu_ref[...]))
    x1 = x_ref[...] + _dot(merged.astype(BF16), wout_ref[...])
    x1_ref[...] = x1
    h2 = _rms(x1, x1.shape[-1]) * gffn_ref[...]
    h2_hi = h2.astype(BF16)
    h2_ref[...] = h2_hi

    h2_lo = (h2 - h2_hi.astype(F32)).astype(BF16)
    logits = (_dot(h2_hi, wr_hi_ref[...]) + _dot(h2_lo, wr_hi_ref[...]) + _dot(h2_hi, wr_lo_ref[...])
              + br_ref[...])
    lane = lax.broadcasted_iota(jnp.int32, logits.shape, 1)
    neg = -jnp.inf
    big = jnp.int32(1 << 20)

    def top(vals):
        mx = jnp.max(vals, axis=-1, keepdims=True)
        idx = jnp.min(jnp.where(vals == mx, lane, big), axis=-1, keepdims=True)
        return mx, idx

    gl = jnp.where((lane >= N_EXPERTS) & (lane < N_EXPERTS + N_GROUPS), logits, neg)
    gmax, gidx = top(gl)
    pg_sel = 1.0 / jnp.sum(jnp.exp(gl - gmax), axis=-1, keepdims=True)
    el = jnp.where((lane < N_EXPERTS) & (lane // EXPERTS_PER_GROUP == gidx - N_EXPERTS), logits, neg)
    m1, i1 = top(el)
    m2, i2 = top(jnp.where(lane == i1, neg, el))
    e2 = jnp.exp(m2 - m1)
    w1 = pg_sel / (1.0 + e2)
    w2 = w1 * e2

    tm = logits.shape[0]
    sel = (lane == i1) | (lane == i2)
    earlier = (lax.broadcasted_iota(jnp.int32, (tm, tm), 1) < lax.broadcasted_iota(jnp.int32, (tm, tm), 0))
    rank = _dot(jnp.where(earlier, 1.0, 0.0).astype(BF16), jnp.where(sel, 1.0, 0.0).astype(BF16))
    cnt = jnp.sum(jnp.where(sel, 1.0, 0.0), axis=0, keepdims=True)
    seg = jnp.floor((cnt + (SEG_ALIGN - 1)) * (1.0 / SEG_ALIGN))
    before = (lax.broadcasted_iota(jnp.int32, (LANES, LANES), 0) < lax.broadcasted_iota(jnp.int32, (LANES, LANES), 1))
    off = _dot(jnp.broadcast_to(seg, (8, LANES)).astype(BF16), jnp.where(before, 1.0, 0.0).astype(BF16))[0:1] * SEG_ALIGN
    dest = off + rank
    d1 = jnp.sum(jnp.where(lane == i1, dest, 0.0), axis=-1, keepdims=True)
    d2 = jnp.sum(jnp.where(lane == i2, dest, 0.0), axis=-1, keepdims=True)
    route = jnp.where(lane == 0, d1, jnp.where(lane == 1, d2, jnp.where(lane == 2, w1, jnp.where(lane == 3, w2, 0.0))))
    route_ref[...] = route
    route_t_ref[0] = route.T[0:8, :]
    cnt_ref[0] = seg * SEG_ALIGN


def _merge_call(x2, om, od, sgm, sgd, p):
    n, d = x2.shape
    tm = MERGE_ROWS
    row = lambda i: (i, 0)
    const = lambda i: (0, 0)
    weights = [p["wmu"], p["wdu"], p["wout"], p["gffn"], p["wr_hi"], p["wr_lo"], p["br"]]
    return pl.pallas_call(
        _merge_kernel,
        grid=(n // tm,),
        in_specs=([pl.BlockSpec((tm, a.shape[1]), row) for a in (x2, om, od, sgm, sgd)]
                  + [pl.BlockSpec(w.shape, const) for w in weights]),
        out_specs=[pl.BlockSpec((tm, d), row), pl.BlockSpec((tm, d), row), pl.BlockSpec((tm, LANES), row),
                   pl.BlockSpec((1, 8, tm), lambda i: (i, 0, 0)), pl.BlockSpec((1, 1, LANES), lambda i: (i, 0, 0))],
        out_shape=[jax.ShapeDtypeStruct((n, d), F32), jax.ShapeDtypeStruct((n, d), BF16),
                   jax.ShapeDtypeStruct((n, LANES), F32), jax.ShapeDtypeStruct((n // tm, 8, tm), F32),
                   jax.ShapeDtypeStruct((n // tm, 1, LANES), F32)],
        compiler_params=pltpu.CompilerParams(dimension_semantics=("parallel",), vmem_limit_bytes=VMEM_LIMIT_BYTES),
        name="merge_router",
    )(x2, om, od, sgm, sgd, *weights)


def _segment_copies(i, seg_dst_ref, seg_rows_ref, tile_off_ref, global_ref, tile_ref, sem, to_global):
    def body(e, carry):
        k = i * N_EXPERTS + e
        rows = pl.multiple_of(seg_rows_ref[k], SEG_ALIGN)

        @pl.when(rows > 0)
        def _():
            g = global_ref.at[pl.ds(pl.multiple_of(seg_dst_ref[k], SEG_ALIGN), rows)]
            t = tile_ref.at[pl.ds(pl.multiple_of(tile_off_ref[k], SEG_ALIGN), rows)]
            src, dst = (t, g) if to_global else (g, t)
            pltpu.make_async_copy(src, dst, sem).start()

        return carry

    lax.fori_loop(0, N_EXPERTS, body, 0)


def _wait_rows(tile_ref, rows, sem):
    @pl.when(rows > 0)
    def _():
        view = tile_ref.at[pl.ds(0, pl.multiple_of(rows, SEG_ALIGN))]
        pltpu.make_async_copy(view, view, sem).wait()


def _zero_unused_rows(tail_dst_ref, tail_rows_ref, n_used_ref, xs_ref, zero_ref, sem, start):
    n_tiles = xs_ref.shape[0] // EXPERT_ROWS
    if start:
        zero_ref[...] = jnp.zeros(zero_ref.shape, BF16)

    def tail(e, total):
        rows = pl.multiple_of(tail_rows_ref[e], SEG_ALIGN)
        if start:
            @pl.when(rows > 0)
            def _():
                dst = xs_ref.at[pl.ds(pl.multiple_of(tail_dst_ref[e], SEG_ALIGN), rows)]
                pltpu.make_async_copy(zero_ref.at[pl.ds(0, rows)], dst, sem).start()

        return total + rows

    total = lax.fori_loop(0, N_EXPERTS, tail, 0)
    if not start:
        _wait_rows(xs_ref, total + (n_tiles - n_used_ref[0]) * EXPERT_ROWS, sem)
        return

    def unused(t, carry):
        dst = xs_ref.at[pl.ds(pl.multiple_of(t * EXPERT_ROWS, EXPERT_ROWS), EXPERT_ROWS)]
        pltpu.make_async_copy(zero_ref, dst, sem).start()
        return carry

    lax.fori_loop(n_used_ref[0], n_tiles, unused, 0)


def _sort_kernel(seg_dst_ref, seg_rows_ref, tile_off_ref, tile_rows_ref, tail_dst_ref, tail_rows_ref, n_used_ref,
                 h2_ref, route_t_ref, xs_ref, sorted_ref, zero_ref, sem, zero_sem):
    i = pl.program_id(0)
    tm = h2_ref.shape[0]

    @pl.when(i == 0)
    def _():
        _zero_unused_rows(tail_dst_ref, tail_rows_ref, n_used_ref, xs_ref, zero_ref, zero_sem, True)

    d1 = route_t_ref[0, 0:1, :].astype(jnp.int32)
    d2 = route_t_ref[0, 1:2, :].astype(jnp.int32)
    r = lax.broadcasted_iota(jnp.int32, (SORT_ROWS, tm), 0)
    perm = jnp.where((r == d1) | (r == d2), 1.0, 0.0).astype(BF16)
    slot = i % 2
    sorted_ref[slot] = _dot(perm, h2_ref[...]).astype(BF16)
    _segment_copies(i, seg_dst_ref, seg_rows_ref, tile_off_ref, xs_ref, sorted_ref.at[slot], sem.at[slot], True)

    @pl.when(i > 0)
    def _():
        _wait_rows(sorted_ref.at[1 - slot], tile_rows_ref[jnp.maximum(i - 1, 0)], sem.at[1 - slot])

    @pl.when(i == pl.num_programs(0) - 1)
    def _():
        _wait_rows(sorted_ref.at[slot], tile_rows_ref[i], sem.at[slot])
        _zero_unused_rows(tail_dst_ref, tail_rows_ref, n_used_ref, xs_ref, zero_ref, zero_sem, False)


def _sort_call(h2, route_t, sched, max_rows):
    n, d = h2.shape
    tm = ROUTE_ROWS
    return pl.pallas_call(
        _sort_kernel,
        grid_spec=pltpu.PrefetchScalarGridSpec(
            num_scalar_prefetch=7,
            grid=(n // tm,),
            in_specs=[pl.BlockSpec((tm, d), lambda i, *_: (i, 0)),
                      pl.BlockSpec((1, 8, tm), lambda i, *_: (i, 0, 0))],
            out_specs=pl.BlockSpec(memory_space=pl.ANY),
            scratch_shapes=[pltpu.VMEM((2, SORT_ROWS, d), BF16), pltpu.VMEM((EXPERT_ROWS, d), BF16),
                            pltpu.SemaphoreType.DMA((2,)), pltpu.SemaphoreType.DMA(())],
        ),
        out_shape=jax.ShapeDtypeStruct((max_rows, d), BF16),
        compiler_params=pltpu.CompilerParams(dimension_semantics=("arbitrary",), vmem_limit_bytes=VMEM_LIMIT_BYTES),
        name="moe_sort",
    )(sched["seg_dst"], sched["seg_rows"], sched["tile_off"], sched["tile_rows"], sched["tail_dst"],
      sched["tail_rows"], sched["n_used"], h2, route_t)


def _expert_kernel(tile_expert_ref, n_used_ref, xs_ref, wg_ref, wu_ref, wd_ref, ys_ref, wg_bf, wu_bf, wd_bf):
    t = pl.program_id(0)
    used = t < n_used_ref[0]

    @pl.when(used & ((t == 0) | (tile_expert_ref[t] != tile_expert_ref[jnp.maximum(t - 1, 0)])))
    def _():
        wg_bf[...] = wg_ref[0].astype(BF16)
        wu_bf[...] = wu_ref[0].astype(BF16)
        wd_bf[...] = wd_ref[0].astype(BF16)

    @pl.when(used)
    def _():
        xs = xs_ref[...]
        gate = _dot(xs, wg_bf[...])
        up = _dot(xs, wu_bf[...])
        hidden = (gate * jax.nn.sigmoid(gate) * up).astype(BF16)
        ys_ref[...] = _dot(hidden, wd_bf[...]).astype(BF16)

    @pl.when(jnp.logical_not(used))
    def _():
        ys_ref[...] = jnp.zeros(ys_ref.shape, BF16)


def _expert_call(xs, wg, wu, wd, sched):
    rows, d = xs.shape
    tr = EXPERT_ROWS
    blk = lambda t, te, nu: (jnp.minimum(t, nu[0] - 1), 0)
    wsel = lambda t, te, nu: (te[jnp.minimum(t, nu[0] - 1)], 0, 0)
    return pl.pallas_call(
        _expert_kernel,
        grid_spec=pltpu.PrefetchScalarGridSpec(
            num_scalar_prefetch=2,
            grid=(rows // tr,),
            in_specs=[pl.BlockSpec((tr, d), blk),
                      pl.BlockSpec((1, d, EXPERT_FF), wsel), pl.BlockSpec((1, d, EXPERT_FF), wsel),
                      pl.BlockSpec((1, EXPERT_FF, d), wsel)],
            out_specs=pl.BlockSpec((tr, d), lambda t, te, nu: (t, 0)),
            scratch_shapes=[pltpu.VMEM((d, EXPERT_FF), BF16), pltpu.VMEM((d, EXPERT_FF), BF16),
                            pltpu.VMEM((EXPERT_FF, d), BF16)],
        ),
        out_shape=jax.ShapeDtypeStruct((rows, d), BF16),
        compiler_params=pltpu.CompilerParams(dimension_semantics=("arbitrary",), vmem_limit_bytes=VMEM_LIMIT_BYTES),
        name="moe_experts",
    )(sched["tile_expert"], sched["n_used"], xs, wg, wu, wd)


def _combine_kernel(seg_dst_ref, seg_rows_ref, tile_off_ref, tile_rows_ref, ys_ref, route_ref, x1_ref, o_ref,
                    buf_ref, sem):
    i = pl.program_id(0)
    tm = x1_ref.shape[0]
    slot = i % 2

    def fetch(tile, into):
        buf_ref[into] = jnp.zeros(buf_ref.shape[1:], BF16)
        _segment_copies(tile, seg_dst_ref, seg_rows_ref, tile_off_ref, ys_ref, buf_ref.at[into], sem.at[into], False)

    @pl.when(i == 0)
    def _():
        fetch(i, slot)

    @pl.when(i + 1 < pl.num_programs(0))
    def _():
        fetch(i + 1, 1 - slot)

    route = route_ref[...]
    d1 = route[:, 0:1].astype(jnp.int32)
    d2 = route[:, 1:2].astype(jnp.int32)
    w1 = route[:, 2:3]
    w2 = route[:, 3:4]
    r = lax.broadcasted_iota(jnp.int32, (tm, SORT_ROWS), 1)
    weights = (jnp.where(r == d1, w1, 0.0) + jnp.where(r == d2, w2, 0.0)).astype(BF16)
    _wait_rows(buf_ref.at[slot], tile_rows_ref[i], sem.at[slot])
    o_ref[...] = x1_ref[...] + _dot(weights, buf_ref[slot])


def _combine_call(ys, route, x1, sched):
    n, d = x1.shape
    tm = ROUTE_ROWS
    return pl.pallas_call(
        _combine_kernel,
        grid_spec=pltpu.PrefetchScalarGridSpec(
            num_scalar_prefetch=4,
            grid=(n // tm,),
            in_specs=[pl.BlockSpec(memory_space=pl.ANY),
                      pl.BlockSpec((tm, LANES), lambda i, *_: (i, 0)),
                      pl.BlockSpec((tm, d), lambda i, *_: (i, 0))],
            out_specs=pl.BlockSpec((tm, d), lambda i, *_: (i, 0)),
            scratch_shapes=[pltpu.VMEM((2, SORT_ROWS, d), BF16), pltpu.SemaphoreType.DMA((2,))],
        ),
        out_shape=jax.ShapeDtypeStruct((n, d), F32),
        compiler_params=pltpu.CompilerParams(dimension_semantics=("arbitrary",), vmem_limit_bytes=VMEM_LIMIT_BYTES),
        name="moe_combine",
    )(sched["seg_dst"], sched["seg_rows"], sched["tile_off"], sched["tile_rows"], ys, route, x1)


def _schedule_kernel(cnt_ref, seg_dst_ref, tile_off_ref, tile_rows_ref, misc_ref):
    hp = functools.partial(jnp.dot, preferred_element_type=F32, precision=lax.Precision.HIGHEST)
    cnt = cnt_ref[...]
    n_tiles = cnt.shape[0]
    tile_before = jnp.where(lax.broadcasted_iota(jnp.int32, (n_tiles, n_tiles), 1)
                            < lax.broadcasted_iota(jnp.int32, (n_tiles, n_tiles), 0), 1.0, 0.0)
    expert_before = jnp.where(lax.broadcasted_iota(jnp.int32, (LANES, LANES), 0)
                              < lax.broadcasted_iota(jnp.int32, (LANES, LANES), 1), 1.0, 0.0)
    expert_rows = jnp.sum(cnt, axis=0, keepdims=True)
    region = jnp.floor((expert_rows + (EXPERT_ROWS - 1)) * (1.0 / EXPERT_ROWS)) * EXPERT_ROWS
    region_start = hp(jnp.broadcast_to(region, (8, LANES)), expert_before)[0:1]
    seg_dst_ref[...] = (region_start + hp(tile_before, cnt)).astype(jnp.int32)
    tile_off_ref[...] = hp(cnt, expert_before).astype(jnp.int32)
    tile_rows_ref[...] = jnp.broadcast_to(jnp.sum(cnt, axis=-1, keepdims=True), cnt.shape).astype(jnp.int32)
    n_used = jnp.sum(region, axis=-1, keepdims=True) * (1.0 / EXPERT_ROWS)
    row = lax.broadcasted_iota(jnp.int32, (8, LANES), 0)
    misc = jnp.where(row == 0, region_start + expert_rows,
                     jnp.where(row == 1, region - expert_rows,
                               jnp.where(row == 2, region_start + region, n_used)))
    misc_ref[...] = misc.astype(jnp.int32)


def _moe_schedule(cnt, n_tokens):
    n_tiles = cnt.shape[0]
    table = jax.ShapeDtypeStruct((n_tiles, LANES), jnp.int32)
    seg_dst, tile_off, tile_rows, misc = pl.pallas_call(
        _schedule_kernel,
        out_shape=[table, table, table, jax.ShapeDtypeStruct((8, LANES), jnp.int32)],
        name="moe_schedule",
    )(cnt.reshape(n_tiles, LANES))
    max_rows = 2 * n_tokens + n_tiles * N_EXPERTS * (SEG_ALIGN - 1) + N_EXPERTS * (EXPERT_ROWS - 1)
    max_tiles = -(-max_rows // EXPERT_ROWS)
    tile_start = jnp.arange(max_tiles, dtype=jnp.int32) * EXPERT_ROWS
    region_end = misc[2, :N_EXPERTS]
    tile_expert = jnp.minimum(jnp.sum((region_end[None, :] <= tile_start[:, None]).astype(jnp.int32), axis=1),
                              N_EXPERTS - 1)
    flat = lambda a: a[:, :N_EXPERTS].reshape(-1)
    sched = {
        "seg_dst": flat(seg_dst),
        "seg_rows": flat(cnt.reshape(n_tiles, LANES).astype(jnp.int32)),
        "tile_off": flat(tile_off),
        "tile_rows": tile_rows[:, 0],
        "tail_dst": misc[0, :N_EXPERTS],
        "tail_rows": misc[1, :N_EXPERTS],
        "tile_expert": tile_expert,
        "n_used": misc[3, :1],
    }
    return sched, max_tiles * EXPERT_ROWS


def _rotary_tables(seq_len, rot_dim, period, first, gain, scale):
    half = rot_dim // 2
    pos = jnp.arange(seq_len, dtype=F32)
    inv = 1.0 / (ROPE_THETA ** (jnp.arange(0, rot_dim, 2, dtype=F32) / rot_dim))
    ang = pos[:, None] * inv[None, :]
    cos, sin = jnp.cos(ang), jnp.sin(ang)
    lane = jnp.arange(LANES)
    rel = (lane % period) - first
    active = (rel >= 0) & (rel < rot_dim)
    idx = jnp.clip(rel, 0, rot_dim - 1) % half
    sign = jnp.where(rel < half, -1.0, 1.0)
    partner = jnp.where(active, jnp.where(rel < half, lane + half, lane - half), lane)
    c = jnp.where(active[None, :], cos[:, idx], 1.0)
    s = jnp.where(active[None, :], sin[:, idx] * sign[None, :], 0.0)
    gain = gain.astype(F32)
    return (c * gain[None, :] * scale).astype(F32), (s * gain[partner][None, :] * scale).astype(F32)


def _head_pad(w, heads, width):
    r = w.shape[0]
    w = w.reshape(r, heads, width)
    return jnp.pad(w, ((0, 0), (0, 0), (0, LANES - width))).reshape(r, heads * LANES)


def _layer_params(l, seq_len, norm_mix, w_in, mla_q_latent_norm, w_mla_uq, mla_kv_latent_norm, w_mla_ukv,
                  mla_q_gain, mla_k_gain, diff_q_gain, diff_k_gain, w_mla_up, w_diff_up, w_out, norm_ffn,
                  w_router_group, b_router_group, w_router_expert, b_router_expert):
    d = w_in.shape[1]
    sizes = (MLA_Q_RANK, MLA_KV_RANK, MLA_ROPE, DIFF_QK_WIDTH, DIFF_QK_WIDTH, DIFF_V_WIDTH, d, d)
    offs = [0]
    for s in sizes:
        offs.append(offs[-1] + s)
    wi = w_in[l]
    seg = [wi[:, offs[k]:offs[k + 1]] for k in range(len(sizes))]
    row = lambda g: g.astype(F32)[None, :]
    p = {}
    p["gmix"] = row(norm_mix[l])
    p["wql"] = seg[0].astype(BF16)
    p["wkvl"] = seg[1].astype(BF16)
    p["wkr"] = jnp.pad(seg[2], ((0, 0), (MLA_NOPE, LANES - MLA_QK))).astype(BF16)
    p["wdk"] = seg[4].astype(BF16)
    p["wdqvt"] = jnp.concatenate([seg[3].T, seg[5].T], axis=0).astype(BF16)
    p["wgm"], p["wgd"] = seg[6].astype(BF16), seg[7].astype(BF16)
    p["gql"] = row(mla_q_latent_norm[l])
    p["wuqt"] = _head_pad(w_mla_uq[l], MLA_HEADS, MLA_QK).T.astype(BF16)
    p["gkvl"] = row(mla_kv_latent_norm[l])
    ukv = w_mla_ukv[l].reshape(MLA_KV_RANK, MLA_HEADS, MLA_NOPE + MLA_V)
    p["wuk"] = _head_pad(ukv[:, :, :MLA_NOPE].reshape(MLA_KV_RANK, -1), MLA_HEADS, MLA_NOPE).astype(BF16)
    p["wuvt"] = ukv[:, :, MLA_NOPE:].reshape(MLA_KV_RANK, -1).T.astype(BF16)
    gq = jnp.pad(mla_q_gain[l], (0, LANES - MLA_QK))
    gk = jnp.pad(mla_k_gain[l], (0, LANES - MLA_QK))
    nope = jnp.arange(LANES) < MLA_NOPE
    p["gkn"] = jnp.where(nope, gk, 0.0).astype(F32)[None, :]
    aq, bq = _rotary_tables(seq_len, MLA_ROPE, LANES, MLA_NOPE, gq, LOG2E * MLA_QK ** -0.5)
    p["aq"], p["bq"] = aq.T, bq.T
    ak, bk = _rotary_tables(seq_len, MLA_ROPE, LANES, MLA_NOPE, jnp.where(nope, 0.0, gk), 1.0)
    p["ak"], p["bk"] = ak, bk
    adq, bdq = _rotary_tables(seq_len, DIFF_ROPE, DIFF_HEAD_DIM, 0, jnp.tile(diff_q_gain[l], 2),
                              LOG2E * DIFF_HEAD_DIM ** -0.5)
    p["adq"], p["bdq"] = adq.T, bdq.T
    p["adk"], p["bdk"] = _rotary_tables(seq_len, DIFF_ROPE, DIFF_HEAD_DIM, 0, jnp.tile(diff_k_gain[l], 2), 1.0)
    p["wmu"] = w_mla_up[l].astype(BF16)
    p["wdu"] = w_diff_up[l].astype(BF16)
    p["wout"] = w_out[l].astype(BF16)
    p["gffn"] = row(norm_ffn[l])
    wr = jnp.concatenate([w_router_expert[l], w_router_group[l]], axis=1).astype(F32)
    wr = jnp.pad(wr, ((0, 0), (0, LANES - wr.shape[1])))
    p["wr_hi"] = wr.astype(BF16)
    p["wr_lo"] = (wr - p["wr_hi"].astype(F32)).astype(BF16)
    br = jnp.concatenate([b_router_expert[l], b_router_group[l]]).astype(F32)
    p["br"] = jnp.pad(br, (0, LANES - br.shape[0]))[None, :]
    return p


def kernel(x, norm_mix, w_in, mla_q_latent_norm, w_mla_uq, mla_kv_latent_norm, w_mla_ukv, mla_q_gain, mla_k_gain, diff_q_gain, diff_k_gain, lambda_q1, lambda_k1, lambda_q2, lambda_k2, diff_subln, w_mla_up, w_diff_up, w_out, norm_ffn, w_router_group, b_router_group, w_router_expert, b_router_expert, w_expert_gate, w_expert_up, w_expert_down):
    batch, seq_len, d = x.shape
    x2 = x.reshape(batch * seq_len, d)
    row = lambda g: g.astype(F32)[None, :]
    for l in range(norm_mix.shape[0]):
        lam_init = 0.8 - 0.6 * math.exp(-0.3 * l)
        p = _layer_params(l, seq_len, norm_mix, w_in, mla_q_latent_norm, w_mla_uq, mla_kv_latent_norm, w_mla_ukv,
                          mla_q_gain, mla_k_gain, diff_q_gain, diff_k_gain, w_mla_up, w_diff_up, w_out, norm_ffn,
                          w_router_group, b_router_group, w_router_expert, b_router_expert)
        qmt, km, vtm, qdt, kd, vtd, sgm, sgd = _proj_call(x2, seq_len, p)
        om = _mla_call(qmt, km, vtm, batch, seq_len)
        od = _diff_call(qdt, kd, vtd, row(lambda_q1[l]), row(lambda_k1[l]), row(lambda_q2[l]), row(lambda_k2[l]),
                        diff_subln[l].astype(F32)[:, None], lam_init, batch, seq_len)
        x1, h2, route, route_t, cnt = _merge_call(x2, om, od, sgm, sgd, p)
        sched, max_rows = _moe_schedule(cnt, x2.shape[0])
        xs = _sort_call(h2, route_t, sched, max_rows)
        ys = _expert_call(xs, w_expert_gate[l], w_expert_up[l], w_expert_down[l], sched)
        x2 = _combine_call(ys, route, x1, sched)
    return x2.reshape(batch, seq_len, d)
```

```python
import functools
import math

import jax
import jax.numpy as jnp
from jax import lax
from jax.experimental import pallas as pl
from jax.experimental.pallas import tpu as pltpu

CHUNK = 64
ROPE_THETA = 500000.0
EPS = 1e-6

MLA_HEADS = 8
MLA_NOPE = 64
MLA_ROPE = 32
MLA_V = 64
MLA_QK = MLA_NOPE + MLA_ROPE
MLA_Q_RANK = 256
MLA_KV_RANK = 128

DIFF_HEADS = 4
DIFF_HEAD_DIM = 64
DIFF_V_DIM = 2 * DIFF_HEAD_DIM
DIFF_ROPE = DIFF_HEAD_DIM // 4
DIFF_QK_WIDTH = DIFF_HEADS * 2 * DIFF_HEAD_DIM
DIFF_V_WIDTH = DIFF_HEADS * DIFF_V_DIM

N_GROUPS = 4
EXPERTS_PER_GROUP = 8
N_EXPERTS = N_GROUPS * EXPERTS_PER_GROUP
EXPERT_FF = 256

LANES = 128
VMEM_LIMIT_BYTES = 48 * 1024 * 1024

PROJ_ROWS = 512
ATTN_Q_ROWS = 512
ATTN_K_ROWS = 256
MERGE_ROWS = 512
ROUTE_ROWS = MERGE_ROWS
SEG_ALIGN = 16
SORT_ROWS = 2 * ROUTE_ROWS + N_EXPERTS * SEG_ALIGN
EXPERT_ROWS = 512
MLA_HEADS_PER_STEP = 4
DIFF_HEADS_PER_STEP = 2
LOG2E = 1.4426950408889634

BF16 = jnp.bfloat16
F32 = jnp.float32


def _dot(a, b):
    return jnp.dot(a, b, preferred_element_type=F32)


def _dot_nt(a, b):
    return lax.dot_general(a, b, (((1,), (1,)), ((), ())), preferred_element_type=F32)


def _rms(x, width):
    return x * lax.rsqrt(jnp.sum(x * x, axis=-1, keepdims=True) * (1.0 / width) + EPS)


def _rotary_partner(y, half):
    lane = lax.broadcasted_iota(jnp.int32, y.shape, 1)
    up = pltpu.roll(y, LANES - half, 1)
    down = pltpu.roll(y, half, 1)
    return jnp.where((lane // half) % 2 == 0, up, down)


def _swap_row_blocks(y, first, half, period):
    parts = []
    for base in range(0, y.shape[0], period):
        a = base + first
        parts += [y[base:a], y[a + half:a + 2 * half], y[a:a + half], y[a + 2 * half:base + period]]
    return jnp.concatenate([p for p in parts if p.shape[0]], axis=0)


def _store_k_tiles(o_ref, vt):
    tk = o_ref.shape[-1]
    for c in range(o_ref.shape[0]):
        o_ref[c] = vt[:, c * tk:(c + 1) * tk].astype(BF16)


def _proj_kernel(x_ref, gmix_ref, wql_ref, wkvl_ref, wkr_ref, wdk_ref, wdqvt_ref, wgm_ref, wgd_ref,
                 gql_ref, wuqt_ref, gkvl_ref, wuk_ref, wuvt_ref, gkn_ref,
                 aq_ref, bq_ref, adq_ref, bdq_ref, ak_ref, bk_ref, adk_ref, bdk_ref,
                 qmt_ref, km_ref, vtm_ref, qdt_ref, kd_ref, vtd_ref, sgm_ref, sgd_ref):
    x = x_ref[...]
    h = (_rms(x, x.shape[-1]) * gmix_ref[...]).astype(BF16)

    ql = (_rms(_dot(h, wql_ref[...]), MLA_Q_RANK) * gql_ref[...]).astype(BF16)
    qt = _dot_nt(wuqt_ref[...], ql)
    aq, bq = aq_ref[...], bq_ref[...]
    for hd in range(MLA_HEADS):
        rows = slice(hd * LANES, (hd + 1) * LANES)
        qh = qt[rows]
        r = lax.rsqrt(jnp.sum(qh * qh, axis=0, keepdims=True) * (1.0 / MLA_QK) + EPS)
        y = (qh * aq + _swap_row_blocks(qh, MLA_NOPE, MLA_ROPE // 2, LANES) * bq) * r
        qmt_ref[rows, :] = y.astype(BF16)

    kvl = (_rms(_dot(h, wkvl_ref[...]), MLA_KV_RANK) * gkvl_ref[...]).astype(BF16)
    kr = _dot(h, wkr_ref[...])
    kr_rot = kr * ak_ref[...] + _rotary_partner(kr, MLA_ROPE // 2) * bk_ref[...]
    kr_ss = jnp.sum(kr * kr, axis=-1, keepdims=True)
    kn = _dot(kvl, wuk_ref[...])
    _store_k_tiles(vtm_ref, _dot_nt(wuvt_ref[...], kvl))
    gkn = gkn_ref[...]
    for hd in range(MLA_HEADS):
        sl = slice(hd * LANES, (hd + 1) * LANES)
        knh = kn[:, sl]
        r = lax.rsqrt((jnp.sum(knh * knh, axis=-1, keepdims=True) + kr_ss) * (1.0 / MLA_QK) + EPS)
        km_ref[:, sl] = ((knh * gkn + kr_rot) * r).astype(BF16)

    qvt = _dot_nt(wdqvt_ref[...], h)
    _store_k_tiles(vtd_ref, qvt[DIFF_QK_WIDTH:])
    adq, bdq = adq_ref[...], bdq_ref[...]
    for hd in range(DIFF_HEADS):
        rows = slice(hd * LANES, (hd + 1) * LANES)
        qh = qvt[rows]
        t = qh * adq + _swap_row_blocks(qh, 0, DIFF_ROPE // 2, DIFF_HEAD_DIM) * bdq
        halves = []
        for f in range(2):
            part = qh[f * DIFF_HEAD_DIM:(f + 1) * DIFF_HEAD_DIM]
            r = lax.rsqrt(jnp.sum(part * part, axis=0, keepdims=True) * (1.0 / DIFF_HEAD_DIM) + EPS)
            halves.append(t[f * DIFF_HEAD_DIM:(f + 1) * DIFF_HEAD_DIM] * r)
        qdt_ref[rows, :] = jnp.concatenate(halves, axis=0).astype(BF16)

    kd = _dot(h, wdk_ref[...])
    adk, bdk = adk_ref[...], bdk_ref[...]
    for hd in range(DIFF_HEADS):
        sl = slice(hd * LANES, (hd + 1) * LANES)
        th = kd[:, sl]
        lane = lax.broadcasted_iota(jnp.int32, th.shape, 1)
        sq = th * th
        lo = jnp.sum(jnp.where(lane < DIFF_HEAD_DIM, sq, 0.0), axis=-1, keepdims=True)
        tot = jnp.sum(sq, axis=-1, keepdims=True)
        r = lax.rsqrt(jnp.where(lane < DIFF_HEAD_DIM, lo, tot - lo) * (1.0 / DIFF_HEAD_DIM) + EPS)
        kd_ref[:, sl] = ((th * adk + _rotary_partner(th, DIFF_ROPE // 2) * bdk) * r).astype(BF16)

    sgm_ref[...] = jax.nn.sigmoid(_dot(h, wgm_ref[...])).astype(BF16)
    sgd_ref[...] = jax.nn.sigmoid(_dot(h, wgd_ref[...])).astype(BF16)


def _proj_call(x2, seq_len, p):
    n, d = x2.shape
    tm = PROJ_ROWS
    pos_blocks = seq_len // tm
    row = lambda i: (i, 0)
    col = lambda i: (0, i)
    const = lambda i: (0, 0)
    weights = [p["gmix"], p["wql"], p["wkvl"], p["wkr"], p["wdk"], p["wdqvt"], p["wgm"], p["wgd"],
               p["gql"], p["wuqt"], p["gkvl"], p["wuk"], p["wuvt"], p["gkn"]]
    feature_major_tables = [p["aq"], p["bq"], p["adq"], p["bdq"]]
    token_major_tables = [p["ak"], p["bk"], p["adk"], p["bdk"]]
    in_specs = ([pl.BlockSpec((tm, d), row)]
                + [pl.BlockSpec(w.shape, const) for w in weights]
                + [pl.BlockSpec((LANES, tm), lambda i: (0, i % pos_blocks)) for _ in feature_major_tables]
                + [pl.BlockSpec((tm, LANES), lambda i: (i % pos_blocks, 0)) for _ in token_major_tables])
    tk = ATTN_K_ROWS
    k_tiles = lambda width: (pl.BlockSpec((tm // tk, width, tk), lambda i: (i, 0, 0)),
                             jax.ShapeDtypeStruct((n // tk, width, tk), BF16))
    token_major = lambda width: (pl.BlockSpec((tm, width), row), jax.ShapeDtypeStruct((n, width), BF16))
    feature_major = lambda width: (pl.BlockSpec((width, tm), col), jax.ShapeDtypeStruct((width, n), BF16))
    outs = [feature_major(MLA_HEADS * LANES), token_major(MLA_HEADS * LANES), k_tiles(MLA_HEADS * MLA_V),
            feature_major(DIFF_QK_WIDTH), token_major(DIFF_QK_WIDTH), k_tiles(DIFF_V_WIDTH),
            token_major(d), token_major(d)]
    return pl.pallas_call(
        _proj_kernel,
        grid=(n // tm,),
        in_specs=in_specs,
        out_specs=[o[0] for o in outs],
        out_shape=[o[1] for o in outs],
        compiler_params=pltpu.CompilerParams(dimension_semantics=("parallel",), vmem_limit_bytes=VMEM_LIMIT_BYTES),
        name="proj",
    )(x2, *weights, *feature_major_tables, *token_major_tables)


def _chunk_mask_t(tk, tq, diag):
    kc = lax.broadcasted_iota(jnp.int32, (tk, tq), 0) // CHUNK + diag * (tk // CHUNK)
    qc = lax.broadcasted_iota(jnp.int32, (tk, tq), 1) // CHUNK
    return kc <= qc


def _softmax_step_t(st, vt, m_ref, l_ref, acc_ref):
    m_prev = m_ref[...]
    m_new = jnp.maximum(m_prev, jnp.max(st, axis=0, keepdims=True))
    alpha = jnp.exp2(m_prev - m_new)
    pr = jnp.exp2(st - m_new)
    l_ref[...] = alpha * l_ref[...] + jnp.sum(pr, axis=0, keepdims=True)
    acc_ref[...] = alpha * acc_ref[...] + _dot(vt, pr.astype(BF16))
    m_ref[...] = m_new


def _attn_scratch(chains, dv, tq, tk):
    per_chain = [pltpu.VMEM((1, tq), F32), pltpu.VMEM((1, tq), F32), pltpu.VMEM((dv, tq), F32),
                 pltpu.VMEM((tk, tq), F32), pltpu.VMEM((tk, tq), F32)]
    return per_chain * chains


def _flash_pipeline(q_tile, scratch_refs, score_fn, value_fn, tk, tq):
    assert tq == 2 * tk
    n_chains = len(scratch_refs) // 5
    chains = [scratch_refs[5 * c:5 * c + 5] for c in range(n_chains)]
    for m_ref, l_ref, acc_ref, _, _ in chains:
        m_ref[...] = jnp.full(m_ref.shape, -jnp.inf, F32)
        l_ref[...] = jnp.zeros(l_ref.shape, F32)
        acc_ref[...] = jnp.zeros(acc_ref.shape, F32)

    def scores(t, slot):
        for c, ch in enumerate(chains):
            ch[3 + slot][...] = score_fn(c, t)

    def update(t, slot, diag=None):
        for c, ch in enumerate(chains):
            st = ch[3 + slot][...]
            if diag is not None:
                st = jnp.where(_chunk_mask_t(tk, tq, diag), st, -jnp.inf)
            _softmax_step_t(st, value_fn(c, t), ch[0], ch[1], ch[2])

    scores(0, 0)

    def pair(p, carry):
        t = 2 * p
        scores(t + 1, 1)
        update(t, 0)
        scores(t + 2, 0)
        update(t + 1, 1)
        return carry

    lax.fori_loop(0, q_tile, pair, 0)
    first_diag = 2 * q_tile
    scores(first_diag + 1, 1)
    update(first_diag, 0, diag=0)
    update(first_diag + 1, 1, diag=1)
    return [(ch[1], ch[2]) for ch in chains]


def _mla_kernel(qt_ref, k_ref, vt_ref, o_ref, *scratch_refs):
    tq, tk = ATTN_Q_ROWS, ATTN_K_ROWS

    def score_fn(c, t):
        rows = pl.ds(pl.multiple_of(t * tk, tk), tk)
        sl = slice(c * LANES, (c + 1) * LANES)
        return _dot(k_ref[rows, sl], qt_ref[sl, :])

    def value_fn(c, t):
        return vt_ref[t, c * MLA_V:(c + 1) * MLA_V, :]

    out = _flash_pipeline(pl.program_id(2), scratch_refs, score_fn, value_fn, tk, tq)
    ot = jnp.concatenate([acc_ref[...] / l_ref[...] for l_ref, acc_ref in out], axis=0)
    o_ref[...] = ot.T.astype(BF16)


def _mla_call(qmt, km, vtm, batch, seq_len):
    n = km.shape[0]
    tq, tk, hps = ATTN_Q_ROWS, ATTN_K_ROWS, MLA_HEADS_PER_STEP
    qt = seq_len // tq
    return pl.pallas_call(
        _mla_kernel,
        grid=(batch, MLA_HEADS // hps, qt),
        in_specs=[pl.BlockSpec((hps * LANES, tq), lambda b, h, i: (h, b * qt + i)),
                  pl.BlockSpec((seq_len, hps * LANES), lambda b, h, i: (b, h)),
                  pl.BlockSpec((seq_len // tk, hps * MLA_V, tk), lambda b, h, i: (b, h, 0))],
        out_specs=pl.BlockSpec((tq, hps * MLA_V), lambda b, h, i: (b * qt + i, h)),
        out_shape=jax.ShapeDtypeStruct((n, MLA_HEADS * MLA_V), BF16),
        scratch_shapes=_attn_scratch(hps, MLA_V, tq, tk),
        compiler_params=pltpu.CompilerParams(dimension_semantics=("parallel", "parallel", "arbitrary"),
                                             vmem_limit_bytes=VMEM_LIMIT_BYTES),
        name="mla_attn",
    )(qmt, km, vtm)


def _diff_kernel(lam_init, qt_ref, k_ref, vt_ref, lq1_ref, lk1_ref, lq2_ref, lk2_ref, subln_ref, o_ref,
                 *scratch_refs):
    tq, tk = ATTN_Q_ROWS, ATTN_K_ROWS
    hps = DIFF_HEADS_PER_STEP

    qs = []
    for hd in range(hps):
        qh = qt_ref[hd * LANES:(hd + 1) * LANES, :]
        zero = jnp.zeros((DIFF_HEAD_DIM, tq), BF16)
        qs += [jnp.concatenate([qh[:DIFF_HEAD_DIM], zero], axis=0), jnp.concatenate([zero, qh[DIFF_HEAD_DIM:]], axis=0)]

    def score_fn(c, t):
        rows = pl.ds(pl.multiple_of(t * tk, tk), tk)
        hd = c // 2
        return _dot(k_ref[rows, hd * LANES:(hd + 1) * LANES], qs[c])

    def value_fn(c, t):
        hd = c // 2
        return vt_ref[t, hd * DIFF_V_DIM:(hd + 1) * DIFF_V_DIM, :]

    out = _flash_pipeline(pl.program_id(2), scratch_refs, score_fn, value_fn, tk, tq)

    lam = (jnp.exp(jnp.sum(lq1_ref[...] * lk1_ref[...], axis=-1, keepdims=True))
           - jnp.exp(jnp.sum(lq2_ref[...] * lk2_ref[...], axis=-1, keepdims=True)) + lam_init)
    heads = []
    for hd in range(hps):
        (l0, a0), (l1, a1) = out[2 * hd], out[2 * hd + 1]
        ot = a0[...] / l0[...] - lam * (a1[...] / l1[...])
        ot = ot * lax.rsqrt(jnp.sum(ot * ot, axis=0, keepdims=True) * (1.0 / DIFF_V_DIM) + EPS)
        heads.append(ot * subln_ref[...] * (1.0 - lam_init))
    o_ref[...] = jnp.concatenate(heads, axis=0).T.astype(BF16)


def _diff_call(qdt, kd, vtd, lq1, lk1, lq2, lk2, subln_col, lam_init, batch, seq_len):
    n = kd.shape[0]
    tq, tk, hps = ATTN_Q_ROWS, ATTN_K_ROWS, DIFF_HEADS_PER_STEP
    qt = seq_len // tq
    small = lambda a: pl.BlockSpec(a.shape, lambda b, h, i: (0, 0))
    return pl.pallas_call(
        functools.partial(_diff_kernel, lam_init),
        grid=(batch, DIFF_HEADS // hps, qt),
        in_specs=[pl.BlockSpec((hps * LANES, tq), lambda b, h, i: (h, b * qt + i)),
                  pl.BlockSpec((seq_len, hps * LANES), lambda b, h, i: (b, h)),
                  pl.BlockSpec((seq_len // tk, hps * DIFF_V_DIM, tk), lambda b, h, i: (b, h, 0)),
                  small(lq1), small(lk1), small(lq2), small(lk2), small(subln_col)],
        out_specs=pl.BlockSpec((tq, hps * LANES), lambda b, h, i: (b * qt + i, h)),
        out_shape=jax.ShapeDtypeStruct((n, DIFF_V_WIDTH), BF16),
        scratch_shapes=_attn_scratch(2 * hps, DIFF_V_DIM, tq, tk),
        compiler_params=pltpu.CompilerParams(dimension_semantics=("parallel", "parallel", "arbitrary"),
                                             vmem_limit_bytes=VMEM_LIMIT_BYTES),
        name="diff_attn",
    )(qdt, kd, vtd, lq1, lk1, lq2, lk2, subln_col)


def _merge_kernel(x_ref, om_ref, od_ref, sgm_ref, sgd_ref, wmu_ref, wdu_ref, wout_ref, gffn_ref, wr_hi_ref,
                  wr_lo_ref, br_ref, x1_ref, h2_ref, route_ref, route_t_ref, cnt_ref):
    merged = (sgm_ref[...].astype(F32) * _dot(om_ref[...], wmu_ref[...])
              + sgd_ref[...].astype(F32) * _dot(od_ref[...], wdu_ref[...]))
    x1 = x_ref[...] + _dot(merged.astype(BF16), wout_ref[...])
    x1_ref[...] = x1
    h2 = _rms(x1, x1.shape[-1]) * gffn_ref[...]
    h2_hi = h2.astype(BF16)
    h2_ref[...] = h2_hi

    h2_lo = (h2 - h2_hi.astype(F32)).astype(BF16)
    logits = (_dot(h2_hi, wr_hi_ref[...]) + _dot(h2_lo, wr_hi_ref[...]) + _dot(h2_hi, wr_lo_ref[...])
              + br_ref[...])
    lane = lax.broadcasted_iota(jnp.int32, logits.shape, 1)
    neg = -jnp.inf
    big = jnp.int32(1 << 20)

    def top(vals):
        mx = jnp.max(vals, axis=-1, keepdims=True)
        idx = jnp.min(jnp.where(vals == mx, lane, big), axis=-1, keepdims=True)
        return mx, idx

    gl = jnp.where((lane >= N_EXPERTS) & (lane < N_EXPERTS + N_GROUPS), logits, neg)
    gmax, gidx = top(gl)
    pg_sel = 1.0 / jnp.sum(jnp.exp(gl - gmax), axis=-1, keepdims=True)
    el = jnp.where((lane < N_EXPERTS) & (lane // EXPERTS_PER_GROUP == gidx - N_EXPERTS), logits, neg)
    m1, i1 = top(el)
    m2, i2 = top(jnp.where(lane == i1, neg, el))
    e2 = jnp.exp(m2 - m1)
    w1 = pg_sel / (1.0 + e2)
    w2 = w1 * e2

    tm = logits.shape[0]
    sel = (lane == i1) | (lane == i2)
    earlier = (lax.broadcasted_iota(jnp.int32, (tm, tm), 1) < lax.broadcasted_iota(jnp.int32, (tm, tm), 0))
    rank = _dot(jnp.where(earlier, 1.0, 0.0).astype(BF16), jnp.where(sel, 1.0, 0.0).astype(BF16))
    cnt = jnp.sum(jnp.where(sel, 1.0, 0.0), axis=0, keepdims=True)
    seg = jnp.floor((cnt + (SEG_ALIGN - 1)) * (1.0 / SEG_ALIGN))
    before = (lax.broadcasted_iota(jnp.int32, (LANES, LANES), 0) < lax.broadcasted_iota(jnp.int32, (LANES, LANES), 1))
    off = _dot(jnp.broadcast_to(seg, (8, LANES)).astype(BF16), jnp.where(before, 1.0, 0.0).astype(BF16))[0:1] * SEG_ALIGN
    dest = off + rank
    d1 = jnp.sum(jnp.where(lane == i1, dest, 0.0), axis=-1, keepdims=True)
    d2 = jnp.sum(jnp.where(lane == i2, dest, 0.0), axis=-1, keepdims=True)
    route = jnp.where(lane == 0, d1, jnp.where(lane == 1, d2, jnp.where(lane == 2, w1, jnp.where(lane == 3, w2, 0.0))))
    route_ref[...] = route
    route_t_ref[0] = route.T[0:8, :]
    cnt_ref[0] = seg * SEG_ALIGN


def _merge_call(x2, om, od, sgm, sgd, p):
    n, d = x2.shape
    tm = MERGE_ROWS
    row = lambda i: (i, 0)
    const = lambda i: (0, 0)
    weights = [p["wmu"], p["wdu"], p["wout"], p["gffn"], p["wr_hi"], p["wr_lo"], p["br"]]
    return pl.pallas_call(
        _merge_kernel,
        grid=(n // tm,),
        in_specs=([pl.BlockSpec((tm, a.shape[1]), row) for a in (x2, om, od, sgm, sgd)]
                  + [pl.BlockSpec(w.shape, const) for w in weights]),
        out_specs=[pl.BlockSpec((tm, d), row), pl.BlockSpec((tm, d), row), pl.BlockSpec((tm, LANES), row),
                   pl.BlockSpec((1, 8, tm), lambda i: (i, 0, 0)), pl.BlockSpec((1, 1, LANES), lambda i: (i, 0, 0))],
        out_shape=[jax.ShapeDtypeStruct((n, d), F32), jax.ShapeDtypeStruct((n, d), BF16),
                   jax.ShapeDtypeStruct((n, LANES), F32), jax.ShapeDtypeStruct((n // tm, 8, tm), F32),
                   jax.ShapeDtypeStruct((n // tm, 1, LANES), F32)],
        compiler_params=pltpu.CompilerParams(dimension_semantics=("parallel",), vmem_limit_bytes=VMEM_LIMIT_BYTES),
        name="merge_router",
    )(x2, om, od, sgm, sgd, *weights)


def _segment_copies(i, seg_dst_ref, seg_rows_ref, tile_off_ref, global_ref, tile_ref, sem, to_global):
    def body(e, carry):
        k = i * N_EXPERTS + e
        rows = pl.multiple_of(seg_rows_ref[k], SEG_ALIGN)

        @pl.when(rows > 0)
        def _():
            g = global_ref.at[pl.ds(pl.multiple_of(seg_dst_ref[k], SEG_ALIGN), rows)]
            t = tile_ref.at[pl.ds(pl.multiple_of(tile_off_ref[k], SEG_ALIGN), rows)]
            src, dst = (t, g) if to_global else (g, t)
            pltpu.make_async_copy(src, dst, sem).start()

        return carry

    lax.fori_loop(0, N_EXPERTS, body, 0)


def _wait_rows(tile_ref, rows, sem):
    @pl.when(rows > 0)
    def _():
        view = tile_ref.at[pl.ds(0, pl.multiple_of(rows, SEG_ALIGN))]
        pltpu.make_async_copy(view, view, sem).wait()


def _zero_unused_rows(tail_dst_ref, tail_rows_ref, n_used_ref, xs_ref, zero_ref, sem, start):
    n_tiles = xs_ref.shape[0] // EXPERT_ROWS
    if start:
        zero_ref[...] = jnp.zeros(zero_ref.shape, BF16)

    def tail(e, total):
        rows = pl.multiple_of(tail_rows_ref[e], SEG_ALIGN)
        if start:
            @pl.when(rows > 0)
            def _():
                dst = xs_ref.at[pl.ds(pl.multiple_of(tail_dst_ref[e], SEG_ALIGN), rows)]
                pltpu.make_async_copy(zero_ref.at[pl.ds(0, rows)], dst, sem).start()

        return total + rows

    total = lax.fori_loop(0, N_EXPERTS, tail, 0)
    if not start:
        _wait_rows(xs_ref, total + (n_tiles - n_used_ref[0]) * EXPERT_ROWS, sem)
        return

    def unused(t, carry):
        dst = xs_ref.at[pl.ds(pl.multiple_of(t * EXPERT_ROWS, EXPERT_ROWS), EXPERT_ROWS)]
        pltpu.make_async_copy(zero_ref, dst, sem).start()
        return carry

    lax.fori_loop(n_used_ref[0], n_tiles, unused, 0)


def _sort_kernel(seg_dst_ref, seg_rows_ref, tile_off_ref, tile_rows_ref, tail_dst_ref, tail_rows_ref, n_used_ref,
                 h2_ref, route_t_ref, xs_ref, sorted_ref, zero_ref, sem, zero_sem):
    i = pl.program_id(0)
    tm = h2_ref.shape[0]

    @pl.when(i == 0)
    def _():
        _zero_unused_rows(tail_dst_ref, tail_rows_ref, n_used_ref, xs_ref, zero_ref, zero_sem, True)

    d1 = route_t_ref[0, 0:1, :].astype(jnp.int32)
    d2 = route_t_ref[0, 1:2, :].astype(jnp.int32)
    r = lax.broadcasted_iota(jnp.int32, (SORT_ROWS, tm), 0)
    perm = jnp.where((r == d1) | (r == d2), 1.0, 0.0).astype(BF16)
    slot = i % 2
    sorted_ref[slot] = _dot(perm, h2_ref[...]).astype(BF16)
    _segment_copies(i, seg_dst_ref, seg_rows_ref, tile_off_ref, xs_ref, sorted_ref.at[slot], sem.at[slot], True)

    @pl.when(i > 0)
    def _():
        _wait_rows(sorted_ref.at[1 - slot], tile_rows_ref[jnp.maximum(i - 1, 0)], sem.at[1 - slot])

    @pl.when(i == pl.num_programs(0) - 1)
    def _():
        _wait_rows(sorted_ref.at[slot], tile_rows_ref[i], sem.at[slot])
        _zero_unused_rows(tail_dst_ref, tail_rows_ref, n_used_ref, xs_ref, zero_ref, zero_sem, False)


def _sort_call(h2, route_t, sched, max_rows):
    n, d = h2.shape
    tm = ROUTE_ROWS
    return pl.pallas_call(
        _sort_kernel,
        grid_spec=pltpu.PrefetchScalarGridSpec(
            num_scalar_prefetch=7,
            grid=(n // tm,),
            in_specs=[pl.BlockSpec((tm, d), lambda i, *_: (i, 0)),
                      pl.BlockSpec((1, 8, tm), lambda i, *_: (i, 0, 0))],
            out_specs=pl.BlockSpec(memory_space=pl.ANY),
            scratch_shapes=[pltpu.VMEM((2, SORT_ROWS, d), BF16), pltpu.VMEM((EXPERT_ROWS, d), BF16),
                            pltpu.SemaphoreType.DMA((2,)), pltpu.SemaphoreType.DMA(())],
        ),
        out_shape=jax.ShapeDtypeStruct((max_rows, d), BF16),
        compiler_params=pltpu.CompilerParams(dimension_semantics=("arbitrary",), vmem_limit_bytes=VMEM_LIMIT_BYTES),
        name="moe_sort",
    )(sched["seg_dst"], sched["seg_rows"], sched["tile_off"], sched["tile_rows"], sched["tail_dst"],
      sched["tail_rows"], sched["n_used"], h2, route_t)


def _expert_kernel(tile_expert_ref, n_used_ref, xs_ref, wg_ref, wu_ref, wd_ref, ys_ref, wg_bf, wu_bf, wd_bf):
    t = pl.program_id(0)
    used = t < n_used_ref[0]

    @pl.when(used & ((t == 0) | (tile_expert_ref[t] != tile_expert_ref[jnp.maximum(t - 1, 0)])))
    def _():
        wg_bf[...] = wg_ref[0].astype(BF16)
        wu_bf[...] = wu_ref[0].astype(BF16)
        wd_bf[...] = wd_ref[0].astype(BF16)

    @pl.when(used)
    def _():
        xs = xs_ref[...]
        gate = _dot(xs, wg_bf[...])
        up = _dot(xs, wu_bf[...])
        hidden = (gate * jax.nn.sigmoid(gate) * up).astype(BF16)
        ys_ref[...] = _dot(hidden, wd_bf[...]).astype(BF16)

    @pl.when(jnp.logical_not(used))
    def _():
        ys_ref[...] = jnp.zeros(ys_ref.shape, BF16)


def _expert_call(xs, wg, wu, wd, sched):
    rows, d = xs.shape
    tr = EXPERT_ROWS
    blk = lambda t, te, nu: (jnp.minimum(t, nu[0] - 1), 0)
    wsel = lambda t, te, nu: (te[jnp.minimum(t, nu[0] - 1)], 0, 0)
    return pl.pallas_call(
        _expert_kernel,
        grid_spec=pltpu.PrefetchScalarGridSpec(
            num_scalar_prefetch=2,
            grid=(rows // tr,),
            in_specs=[pl.BlockSpec((tr, d), blk),
                      pl.BlockSpec((1, d, EXPERT_FF), wsel), pl.BlockSpec((1, d, EXPERT_FF), wsel),
                      pl.BlockSpec((1, EXPERT_FF, d), wsel)],
            out_specs=pl.BlockSpec((tr, d), lambda t, te, nu: (t, 0)),
            scratch_shapes=[pltpu.VMEM((d, EXPERT_FF), BF16), pltpu.VMEM((d, EXPERT_FF), BF16),
                            pltpu.VMEM((EXPERT_FF, d), BF16)],
        ),
        out_shape=jax.ShapeDtypeStruct((rows, d), BF16),
        compiler_params=pltpu.CompilerParams(dimension_semantics=("arbitrary",), vmem_limit_bytes=VMEM_LIMIT_BYTES),
        name="moe_experts",
    )(sched["tile_expert"], sched["n_used"], xs, wg, wu, wd)


def _combine_kernel(seg_dst_ref, seg_rows_ref, tile_off_ref, tile_rows_ref, ys_ref, route_ref, x1_ref, o_ref,
                    buf_ref, sem):
    i = pl.program_id(0)
    tm = x1_ref.shape[0]
    slot = i % 2

    def fetch(tile, into):
        buf_ref[into] = jnp.zeros(buf_ref.shape[1:], BF16)
        _segment_copies(tile, seg_dst_ref, seg_rows_ref, tile_off_ref, ys_ref, buf_ref.at[into], sem.at[into], False)

    @pl.when(i == 0)
    def _():
        fetch(i, slot)

    @pl.when(i + 1 < pl.num_programs(0))
    def _():
        fetch(i + 1, 1 - slot)

    route = route_ref[...]
    d1 = route[:, 0:1].astype(jnp.int32)
    d2 = route[:, 1:2].astype(jnp.int32)
    w1 = route[:, 2:3]
    w2 = route[:, 3:4]
    r = lax.broadcasted_iota(jnp.int32, (tm, SORT_ROWS), 1)
    weights = (jnp.where(r == d1, w1, 0.0) + jnp.where(r == d2, w2, 0.0)).astype(BF16)
    _wait_rows(buf_ref.at[slot], tile_rows_ref[i], sem.at[slot])
    o_ref[...] = x1_ref[...] + _dot(weights, buf_ref[slot])


def _combine_call(ys, route, x1, sched):
    n, d = x1.shape
    tm = ROUTE_ROWS
    return pl.pallas_call(
        _combine_kernel,
        grid_spec=pltpu.PrefetchScalarGridSpec(
            num_scalar_prefetch=4,
            grid=(n // tm,),
            in_specs=[pl.BlockSpec(memory_space=pl.ANY),
                      pl.BlockSpec((tm, LANES), lambda i, *_: (i, 0)),
                      pl.BlockSpec((tm, d), lambda i, *_: (i, 0))],
            out_specs=pl.BlockSpec((tm, d), lambda i, *_: (i, 0)),
            scratch_shapes=[pltpu.VMEM((2, SORT_ROWS, d), BF16), pltpu.SemaphoreType.DMA((2,))],
        ),
        out_shape=jax.ShapeDtypeStruct((n, d), F32),
        compiler_params=pltpu.CompilerParams(dimension_semantics=("arbitrary",), vmem_limit_bytes=VMEM_LIMIT_BYTES),
        name="moe_combine",
    )(sched["seg_dst"], sched["seg_rows"], sched["tile_off"], sched["tile_rows"], ys, route, x1)


def _schedule_kernel(cnt_ref, seg_dst_ref, tile_off_ref, tile_rows_ref, misc_ref):
    hp = functools.partial(jnp.dot, preferred_element_type=F32, precision=lax.Precision.HIGHEST)
    cnt = cnt_ref[...]
    n_tiles = cnt.shape[0]
    tile_before = jnp.where(lax.broadcasted_iota(jnp.int32, (n_tiles, n_tiles), 1)
                            < lax.broadcasted_iota(jnp.int32, (n_tiles, n_tiles), 0), 1.0, 0.0)
    expert_before = jnp.where(lax.broadcasted_iota(jnp.int32, (LANES, LANES), 0)
                              < lax.broadcasted_iota(jnp.int32, (LANES, LANES), 1), 1.0, 0.0)
    expert_rows = jnp.sum(cnt, axis=0, keepdims=True)
    region = jnp.floor((expert_rows + (EXPERT_ROWS - 1)) * (1.0 / EXPERT_ROWS)) * EXPERT_ROWS
    region_start = hp(jnp.broadcast_to(region, (8, LANES)), expert_before)[0:1]
    seg_dst_ref[...] = (region_start + hp(tile_before, cnt)).astype(jnp.int32)
    tile_off_ref[...] = hp(cnt, expert_before).astype(jnp.int32)
    tile_rows_ref[...] = jnp.broadcast_to(jnp.sum(cnt, axis=-1, keepdims=True), cnt.shape).astype(jnp.int32)
    n_used = jnp.sum(region, axis=-1, keepdims=True) * (1.0 / EXPERT_ROWS)
    row = lax.broadcasted_iota(jnp.int32, (8, LANES), 0)
    misc = jnp.where(row == 0, region_start + expert_rows,
                     jnp.where(row == 1, region - expert_rows,
                               jnp.where(row == 2, region_start + region, n_used)))
    misc_ref[...] = misc.astype(jnp.int32)


def _moe_schedule(cnt, n_tokens):
    n_tiles = cnt.shape[0]
    table = jax.ShapeDtypeStruct((n_tiles, LANES), jnp.int32)
    seg_dst, tile_off, tile_rows, misc = pl.pallas_call(
        _schedule_kernel,
        out_shape=[table, table, table, jax.ShapeDtypeStruct((8, LANES), jnp.int32)],
        name="moe_schedule",
    )(cnt.reshape(n_tiles, LANES))
    max_rows = 2 * n_tokens + n_tiles * N_EXPERTS * (SEG_ALIGN - 1) + N_EXPERTS * (EXPERT_ROWS - 1)
    max_tiles = -(-max_rows // EXPERT_ROWS)
    tile_start = jnp.arange(max_tiles, dtype=jnp.int32) * EXPERT_ROWS
    region_end = misc[2, :N_EXPERTS]
    tile_expert = jnp.minimum(jnp.sum((region_end[None, :] <= tile_start[:, None]).astype(jnp.int32), axis=1),
                              N_EXPERTS - 1)
    flat = lambda a: a[:, :N_EXPERTS].reshape(-1)
    sched = {
        "seg_dst": flat(seg_dst),
        "seg_rows": flat(cnt.reshape(n_tiles, LANES).astype(jnp.int32)),
        "tile_off": flat(tile_off),
        "tile_rows": tile_rows[:, 0],
        "tail_dst": misc[0, :N_EXPERTS],
        "tail_rows": misc[1, :N_EXPERTS],
        "tile_expert": tile_expert,
        "n_used": misc[3, :1],
    }
    return sched, max_tiles * EXPERT_ROWS


def _rotary_tables(seq_len, rot_dim, period, first, gain, scale):
    half = rot_dim // 2
    pos = jnp.arange(seq_len, dtype=F32)
    inv = 1.0 / (ROPE_THETA ** (jnp.arange(0, rot_dim, 2, dtype=F32) / rot_dim))
    ang = pos[:, None] * inv[None, :]
    cos, sin = jnp.cos(ang), jnp.sin(ang)
    lane = jnp.arange(LANES)
    rel = (lane % period) - first
    active = (rel >= 0) & (rel < rot_dim)
    idx = jnp.clip(rel, 0, rot_dim - 1) % half
    sign = jnp.where(rel < half, -1.0, 1.0)
    partner = jnp.where(active, jnp.where(rel < half, lane + half, lane - half), lane)
    c = jnp.where(active[None, :], cos[:, idx], 1.0)
    s = jnp.where(active[None, :], sin[:, idx] * sign[None, :], 0.0)
    gain = gain.astype(F32)
    return (c * gain[None, :] * scale).astype(F32), (s * gain[partner][None, :] * scale).astype(F32)


def _head_pad(w, heads, width):
    r = w.shape[0]
    w = w.reshape(r, heads, width)
    return jnp.pad(w, ((0, 0), (0, 0), (0, LANES - width))).reshape(r, heads * LANES)


def _layer_params(l, seq_len, norm_mix, w_in, mla_q_latent_norm, w_mla_uq, mla_kv_latent_norm, w_mla_ukv,
                  mla_q_gain, mla_k_gain, diff_q_gain, diff_k_gain, w_mla_up, w_diff_up, w_out, norm_ffn,
                  w_router_group, b_router_group, w_router_expert, b_router_expert):
    d = w_in.shape[1]
    sizes = (MLA_Q_RANK, MLA_KV_RANK, MLA_ROPE, DIFF_QK_WIDTH, DIFF_QK_WIDTH, DIFF_V_WIDTH, d, d)
    offs = [0]
    for s in sizes:
        offs.append(offs[-1] + s)
    wi = w_in[l]
    seg = [wi[:, offs[k]:offs[k + 1]] for k in range(len(sizes))]
    row = lambda g: g.astype(F32)[None, :]
    p = {}
    p["gmix"] = row(norm_mix[l])
    p["wql"] = seg[0].astype(BF16)
    p["wkvl"] = seg[1].astype(BF16)
    p["wkr"] = jnp.pad(seg[2], ((0, 0), (MLA_NOPE, LANES - MLA_QK))).astype(BF16)
    p["wdk"] = seg[4].astype(BF16)
    p["wdqvt"] = jnp.concatenate([seg[3].T, seg[5].T], axis=0).astype(BF16)
    p["wgm"], p["wgd"] = seg[6].astype(BF16), seg[7].astype(BF16)
    p["gql"] = row(mla_q_latent_norm[l])
    p["wuqt"] = _head_pad(w_mla_uq[l], MLA_HEADS, MLA_QK).T.astype(BF16)
    p["gkvl"] = row(mla_kv_latent_norm[l])
    ukv = w_mla_ukv[l].reshape(MLA_KV_RANK, MLA_HEADS, MLA_NOPE + MLA_V)
    p["wuk"] = _head_pad(ukv[:, :, :MLA_NOPE].reshape(MLA_KV_RANK, -1), MLA_HEADS, MLA_NOPE).astype(BF16)
    p["wuvt"] = ukv[:, :, MLA_NOPE:].reshape(MLA_KV_RANK, -1).T.astype(BF16)
    gq = jnp.pad(mla_q_gain[l], (0, LANES - MLA_QK))
    gk = jnp.pad(mla_k_gain[l], (0, LANES - MLA_QK))
    nope = jnp.arange(LANES) < MLA_NOPE
    p["gkn"] = jnp.where(nope, gk, 0.0).astype(F32)[None, :]
    aq, bq = _rotary_tables(seq_len, MLA_ROPE, LANES, MLA_NOPE, gq, LOG2E * MLA_QK ** -0.5)
    p["aq"], p["bq"] = aq.T, bq.T
    ak, bk = _rotary_tables(seq_len, MLA_ROPE, LANES, MLA_NOPE, jnp.where(nope, 0.0, gk), 1.0)
    p["ak"], p["bk"] = ak, bk
    adq, bdq = _rotary_tables(seq_len, DIFF_ROPE, DIFF_HEAD_DIM, 0, jnp.tile(diff_q_gain[l], 2),
                              LOG2E * DIFF_HEAD_DIM ** -0.5)
    p["adq"], p["bdq"] = adq.T, bdq.T
    p["adk"], p["bdk"] = _rotary_tables(seq_len, DIFF_ROPE, DIFF_HEAD_DIM, 0, jnp.tile(diff_k_gain[l], 2), 1.0)
    p["wmu"] = w_mla_up[l].astype(BF16)
    p["wdu"] = w_diff_up[l].astype(BF16)
    p["wout"] = w_out[l].astype(BF16)
    p["gffn"] = row(norm_ffn[l])
    wr = jnp.concatenate([w_router_expert[l], w_router_group[l]], axis=1).astype(F32)
    wr = jnp.pad(wr, ((0, 0), (0, LANES - wr.shape[1])))
    p["wr_hi"] = wr.astype(BF16)
    p["wr_lo"] = (wr - p["wr_hi"].astype(F32)).astype(BF16)
    br = jnp.concatenate([b_router_expert[l], b_router_group[l]]).astype(F32)
    p["br"] = jnp.pad(br, (0, LANES - br.shape[0]))[None, :]
    return p


def kernel(x, norm_mix, w_in, mla_q_latent_norm, w_mla_uq, mla_kv_latent_norm, w_mla_ukv, mla_q_gain, mla_k_gain, diff_q_gain, diff_k_gain, lambda_q1, lambda_k1, lambda_q2, lambda_k2, diff_subln, w_mla_up, w_diff_up, w_out, norm_ffn, w_router_group, b_router_group, w_router_expert, b_router_expert, w_expert_gate, w_expert_up, w_expert_down):
    batch, seq_len, d = x.shape
    x2 = x.reshape(batch * seq_len, d)
    row = lambda g: g.astype(F32)[None, :]
    for l in range(norm_mix.shape[0]):
        lam_init = 0.8 - 0.6 * math.exp(-0.3 * l)
        p = _layer_params(l, seq_len, norm_mix, w_in, mla_q_latent_norm, w_mla_uq, mla_kv_latent_norm, w_mla_ukv,
                          mla_q_gain, mla_k_gain, diff_q_gain, diff_k_gain, w_mla_up, w_diff_up, w_out, norm_ffn,
                          w_router_group, b_router_group, w_router_expert, b_router_expert)
        qmt, km, vtm, qdt, kd, vtd, sgm, sgd = _proj_call(x2, seq_len, p)
        om = _mla_call(qmt, km, vtm, batch, seq_len)
        od = _diff_call(qdt, kd, vtd, row(lambda_q1[l]), row(lambda_k1[l]), row(lambda_q2[l]), row(lambda_k2[l]),
                        diff_subln[l].astype(F32)[:, None], lam_init, batch, seq_len)
        x1, h2, route, route_t, cnt = _merge_call(x2, om, od, sgm, sgd, p)
        sched, max_rows = _moe_schedule(cnt, x2.shape[0])
        xs = _sort_call(h2, route_t, sched, max_rows)
        ys = _expert_call(xs, w_expert_gate[l], w_expert_up[l], w_expert_down[l], sched)
        x2 = _combine_call(ys, route, x1, sched)
    return x2.reshape(batch, seq_len, d)
```

```python
import functools
import math

import jax
import jax.numpy as jnp
from jax import lax
from jax.experimental import pallas as pl
from jax.experimental.pallas import tpu as pltpu

CHUNK = 64
ROPE_THETA = 500000.0
EPS = 1e-6

MLA_HEADS = 8
MLA_NOPE = 64
MLA_ROPE = 32
MLA_V = 64
MLA_QK = MLA_NOPE + MLA_ROPE
MLA_Q_RANK = 256
MLA_KV_RANK = 128

DIFF_HEADS = 4
DIFF_HEAD_DIM = 64
DIFF_V_DIM = 2 * DIFF_HEAD_DIM
DIFF_ROPE = DIFF_HEAD_DIM // 4
DIFF_QK_WIDTH = DIFF_HEADS * 2 * DIFF_HEAD_DIM
DIFF_V_WIDTH = DIFF_HEADS * DIFF_V_DIM

N_GROUPS = 4
EXPERTS_PER_GROUP = 8
N_EXPERTS = N_GROUPS * EXPERTS_PER_GROUP
EXPERT_FF = 256

LANES = 128
VMEM_LIMIT_BYTES = 48 * 1024 * 1024

PROJ_ROWS = 512
ATTN_Q_ROWS = 512
ATTN_K_ROWS = 256
MERGE_ROWS = 512
ROUTE_ROWS = MERGE_ROWS
SEG_ALIGN = 16
SORT_ROWS = 2 * ROUTE_ROWS + N_EXPERTS * SEG_ALIGN
EXPERT_ROWS = 512
MLA_HEADS_PER_STEP = 4
DIFF_HEADS_PER_STEP = 2
LOG2E = 1.4426950408889634

BF16 = jnp.bfloat16
F32 = jnp.float32


def _dot(a, b):
    return jnp.dot(a, b, preferred_element_type=F32)


def _dot_nt(a, b):
    return lax.dot_general(a, b, (((1,), (1,)), ((), ())), preferred_element_type=F32)


def _rms(x, width):
    return x * lax.rsqrt(jnp.sum(x * x, axis=-1, keepdims=True) * (1.0 / width) + EPS)


def _rotary_partner(y, half):
    lane = lax.broadcasted_iota(jnp.int32, y.shape, 1)
    up = pltpu.roll(y, LANES - half, 1)
    down = pltpu.roll(y, half, 1)
    return jnp.where((lane // half) % 2 == 0, up, down)


def _swap_row_blocks(y, first, half, period):
    parts = []
    for base in range(0, y.shape[0], period):
        a = base + first
        parts += [y[base:a], y[a + half:a + 2 * half], y[a:a + half], y[a + 2 * half:base + period]]
    return jnp.concatenate([p for p in parts if p.shape[0]], axis=0)


def _store_k_tiles(o_ref, vt):
    tk = o_ref.shape[-1]
    for c in range(o_ref.shape[0]):
        o_ref[c] = vt[:, c * tk:(c + 1) * tk].astype(BF16)


def _proj_kernel(x_ref, gmix_ref, wql_ref, wkvl_ref, wkr_ref, wdk_ref, wdqvt_ref, wgm_ref, wgd_ref,
                 gql_ref, wuqt_ref, gkvl_ref, wuk_ref, wuvt_ref, gkn_ref,
                 aq_ref, bq_ref, adq_ref, bdq_ref, ak_ref, bk_ref, adk_ref, bdk_ref,
                 qmt_ref, km_ref, vtm_ref, qdt_ref, kd_ref, vtd_ref, sgm_ref, sgd_ref):
    x = x_ref[...]
    h = (_rms(x, x.shape[-1]) * gmix_ref[...]).astype(BF16)

    ql = (_rms(_dot(h, wql_ref[...]), MLA_Q_RANK) * gql_ref[...]).astype(BF16)
    qt = _dot_nt(wuqt_ref[...], ql)
    aq, bq = aq_ref[...], bq_ref[...]
    for hd in range(MLA_HEADS):
        rows = slice(hd * LANES, (hd + 1) * LANES)
        qh = qt[rows]
        r = lax.rsqrt(jnp.sum(qh * qh, axis=0, keepdims=True) * (1.0 / MLA_QK) + EPS)
        y = (qh * aq + _swap_row_blocks(qh, MLA_NOPE, MLA_ROPE // 2, LANES) * bq) * r
        qmt_ref[rows, :] = y.astype(BF16)

    kvl = (_rms(_dot(h, wkvl_ref[...]), MLA_KV_RANK) * gkvl_ref[...]).astype(BF16)
    kr = _dot(h, wkr_ref[...])
    kr_rot = kr * ak_ref[...] + _rotary_partner(kr, MLA_ROPE // 2) * bk_ref[...]
    kr_ss = jnp.sum(kr * kr, axis=-1, keepdims=True)
    kn = _dot(kvl, wuk_ref[...])
    _store_k_tiles(vtm_ref, _dot_nt(wuvt_ref[...], kvl))
    gkn = gkn_ref[...]
    for hd in range(MLA_HEADS):
        sl = slice(hd * LANES, (hd + 1) * LANES)
        knh = kn[:, sl]
        r = lax.rsqrt((jnp.sum(knh * knh, axis=-1, keepdims=True) + kr_ss) * (1.0 / MLA_QK) + EPS)
        km_ref[:, sl] = ((knh * gkn + kr_rot) * r).astype(BF16)

    qvt = _dot_nt(wdqvt_ref[...], h)
    _store_k_tiles(vtd_ref, qvt[DIFF_QK_WIDTH:])
    adq, bdq = adq_ref[...], bdq_ref[...]
    for hd in range(DIFF_HEADS):
        rows = slice(hd * LANES, (hd + 1) * LANES)
        qh = qvt[rows]
        t = qh * adq + _swap_row_blocks(qh, 0, DIFF_ROPE // 2, DIFF_HEAD_DIM) * bdq
        halves = []
        for f in range(2):
            part = qh[f * DIFF_HEAD_DIM:(f + 1) * DIFF_HEAD_DIM]
            r = lax.rsqrt(jnp.sum(part * part, axis=0, keepdims=True) * (1.0 / DIFF_HEAD_DIM) + EPS)
            halves.append(t[f * DIFF_HEAD_DIM:(f + 1) * DIFF_HEAD_DIM] * r)
        qdt_ref[rows, :] = jnp.concatenate(halves, axis=0).astype(BF16)

    kd = _dot(h, wdk_ref[...])
    adk, bdk = adk_ref[...], bdk_ref[...]
    for hd in range(DIFF_HEADS):
        sl = slice(hd * LANES, (hd + 1) * LANES)
        th = kd[:, sl]
        lane = lax.broadcasted_iota(jnp.int32, th.shape, 1)
        sq = th * th
        lo = jnp.sum(jnp.where(lane < DIFF_HEAD_DIM, sq, 0.0), axis=-1, keepdims=True)
        tot = jnp.sum(sq, axis=-1, keepdims=True)
        r = lax.rsqrt(jnp.where(lane < DIFF_HEAD_DIM, lo, tot - lo) * (1.0 / DIFF_HEAD_DIM) + EPS)
        kd_ref[:, sl] = ((th * adk + _rotary_partner(th, DIFF_ROPE // 2) * bdk) * r).astype(BF16)

    sgm_ref[...] = jax.nn.sigmoid(_dot(h, wgm_ref[...])).astype(BF16)
    sgd_ref[...] = jax.nn.sigmoid(_dot(h, wgd_ref[...])).astype(BF16)


def _proj_call(x2, seq_len, p):
    n, d = x2.shape
    tm = PROJ_ROWS
    pos_blocks = seq_len // tm
    row = lambda i: (i, 0)
    col = lambda i: (0, i)
    const = lambda i: (0, 0)
    weights = [p["gmix"], p["wql"], p["wkvl"], p["wkr"], p["wdk"], p["wdqvt"], p["wgm"], p["wgd"],
               p["gql"], p["wuqt"], p["gkvl"], p["wuk"], p["wuvt"], p["gkn"]]
    feature_major_tables = [p["aq"], p["bq"], p["adq"], p["bdq"]]
    token_major_tables = [p["ak"], p["bk"], p["adk"], p["bdk"]]
    in_specs = ([pl.BlockSpec((tm, d), row)]
                + [pl.BlockSpec(w.shape, const) for w in weights]
                + [pl.BlockSpec((LANES, tm), lambda i: (0, i % pos_blocks)) for _ in feature_major_tables]
                + [pl.BlockSpec((tm, LANES), lambda i: (i % pos_blocks, 0)) for _ in token_major_tables])
    tk = ATTN_K_ROWS
    k_tiles = lambda width: (pl.BlockSpec((tm // tk, width, tk), lambda i: (i, 0, 0)),
                             jax.ShapeDtypeStruct((n // tk, width, tk), BF16))
    token_major = lambda width: (pl.BlockSpec((tm, width), row), jax.ShapeDtypeStruct((n, width), BF16))
    feature_major = lambda width: (pl.BlockSpec((width, tm), col), jax.ShapeDtypeStruct((width, n), BF16))
    outs = [feature_major(MLA_HEADS * LANES), token_major(MLA_HEADS * LANES), k_tiles(MLA_HEADS * MLA_V),
            feature_major(DIFF_QK_WIDTH), token_major(DIFF_QK_WIDTH), k_tiles(DIFF_V_WIDTH),
            token_major(d), token_major(d)]
    return pl.pallas_call(
        _proj_kernel,
        grid=(n // tm,),
        in_specs=in_specs,
        out_specs=[o[0] for o in outs],
        out_shape=[o[1] for o in outs],
        compiler_params=pltpu.CompilerParams(dimension_semantics=("parallel",), vmem_limit_bytes=VMEM_LIMIT_BYTES),
        name="proj",
    )(x2, *weights, *feature_major_tables, *token_major_tables)


def _chunk_mask_t(tk, tq, diag):
    kc = lax.broadcasted_iota(jnp.int32, (tk, tq), 0) // CHUNK + diag * (tk // CHUNK)
    qc = lax.broadcasted_iota(jnp.int32, (tk, tq), 1) // CHUNK
    return kc <= qc


ONES_ROWS = 16


def _with_ones_rows(vt):
    return jnp.concatenate([vt, jnp.ones((ONES_ROWS, vt.shape[1]), vt.dtype)], axis=0)


def _softmax_step_t(st, vt_ones, m_ref, acc_ref):
    m_prev = m_ref[...]
    m_new = jnp.maximum(m_prev, jnp.max(st, axis=0, keepdims=True))
    alpha = jnp.exp2(m_prev - m_new)
    pr = jnp.exp2(st - m_new)
    acc_ref[...] = alpha * acc_ref[...] + _dot(vt_ones, pr.astype(BF16))
    m_ref[...] = m_new


def _normalized(acc_ref, dv):
    acc = acc_ref[...]
    return acc[:dv] / acc[dv:dv + 1]


STATE_REFS = 4


def _attn_scratch(chains, dv, tq, tk):
    per_chain = [pltpu.VMEM((1, tq), F32), pltpu.VMEM((dv + ONES_ROWS, tq), F32),
                 pltpu.VMEM((tk, tq), F32), pltpu.VMEM((tk, tq), F32)]
    return per_chain * chains


def _flash_pipeline(q_tile, scratch_refs, score_fn, value_fn, tk, tq):
    assert tq == 2 * tk
    n_chains = len(scratch_refs) // STATE_REFS
    chains = [scratch_refs[STATE_REFS * c:STATE_REFS * (c + 1)] for c in range(n_chains)]
    for m_ref, acc_ref, _, _ in chains:
        m_ref[...] = jnp.full(m_ref.shape, -jnp.inf, F32)
        acc_ref[...] = jnp.zeros(acc_ref.shape, F32)

    def scores(t, slot):
        for c, ch in enumerate(chains):
            ch[2 + slot][...] = score_fn(c, t)

    def update(t, slot, diag=None):
        for c, ch in enumerate(chains):
            st = ch[2 + slot][...]
            if diag is not None:
                st = jnp.where(_chunk_mask_t(tk, tq, diag), st, -jnp.inf)
            _softmax_step_t(st, _with_ones_rows(value_fn(c, t)), ch[0], ch[1])

    scores(0, 0)

    def pair(p, carry):
        t = 2 * p
        scores(t + 1, 1)
        update(t, 0)
        scores(t + 2, 0)
        update(t + 1, 1)
        return carry

    lax.fori_loop(0, q_tile, pair, 0)
    first_diag = 2 * q_tile
    scores(first_diag + 1, 1)
    update(first_diag, 0, diag=0)
    update(first_diag + 1, 1, diag=1)
    return [ch[1] for ch in chains]


def _mla_kernel(qt_ref, k_ref, vt_ref, o_ref, *scratch_refs):
    tq, tk = ATTN_Q_ROWS, ATTN_K_ROWS

    def score_fn(c, t):
        rows = pl.ds(pl.multiple_of(t * tk, tk), tk)
        sl = slice(c * LANES, (c + 1) * LANES)
        return _dot(k_ref[rows, sl], qt_ref[sl, :])

    def value_fn(c, t):
        return vt_ref[t, c * MLA_V:(c + 1) * MLA_V, :]

    out = _flash_pipeline(pl.program_id(2), scratch_refs, score_fn, value_fn, tk, tq)
    ot = jnp.concatenate([_normalized(acc_ref, MLA_V) for acc_ref in out], axis=0)
    o_ref[...] = ot.T.astype(BF16)


def _mla_call(qmt, km, vtm, batch, seq_len):
    n = km.shape[0]
    tq, tk, hps = ATTN_Q_ROWS, ATTN_K_ROWS, MLA_HEADS_PER_STEP
    qt = seq_len // tq
    return pl.pallas_call(
        _mla_kernel,
        grid=(batch, MLA_HEADS // hps, qt),
        in_specs=[pl.BlockSpec((hps * LANES, tq), lambda b, h, i: (h, b * qt + i)),
                  pl.BlockSpec((seq_len, hps * LANES), lambda b, h, i: (b, h)),
                  pl.BlockSpec((seq_len // tk, hps * MLA_V, tk), lambda b, h, i: (b, h, 0))],
        out_specs=pl.BlockSpec((tq, hps * MLA_V), lambda b, h, i: (b * qt + i, h)),
        out_shape=jax.ShapeDtypeStruct((n, MLA_HEADS * MLA_V), BF16),
        scratch_shapes=_attn_scratch(hps, MLA_V, tq, tk),
        compiler_params=pltpu.CompilerParams(dimension_semantics=("parallel", "parallel", "arbitrary"),
                                             vmem_limit_bytes=VMEM_LIMIT_BYTES),
        name="mla_attn",
    )(qmt, km, vtm)


def _diff_kernel(lam_init, qt_ref, k_ref, vt_ref, lq1_ref, lk1_ref, lq2_ref, lk2_ref, subln_ref, o_ref,
                 *scratch_refs):
    tq, tk = ATTN_Q_ROWS, ATTN_K_ROWS
    hps = DIFF_HEADS_PER_STEP

    qs = []
    for hd in range(hps):
        qh = qt_ref[hd * LANES:(hd + 1) * LANES, :]
        zero = jnp.zeros((DIFF_HEAD_DIM, tq), BF16)
        qs += [jnp.concatenate([qh[:DIFF_HEAD_DIM], zero], axis=0), jnp.concatenate([zero, qh[DIFF_HEAD_DIM:]], axis=0)]

    def score_fn(c, t):
        rows = pl.ds(pl.multiple_of(t * tk, tk), tk)
        hd = c // 2
        return _dot(k_ref[rows, hd * LANES:(hd + 1) * LANES], qs[c])

    def value_fn(c, t):
        hd = c // 2
        return vt_ref[t, hd * DIFF_V_DIM:(hd + 1) * DIFF_V_DIM, :]

    out = _flash_pipeline(pl.program_id(2), scratch_refs, score_fn, value_fn, tk, tq)

    lam = (jnp.exp(jnp.sum(lq1_ref[...] * lk1_ref[...], axis=-1, keepdims=True))
           - jnp.exp(jnp.sum(lq2_ref[...] * lk2_ref[...], axis=-1, keepdims=True)) + lam_init)
    heads = []
    for hd in range(hps):
        ot = _normalized(out[2 * hd], DIFF_V_DIM) - lam * _normalized(out[2 * hd + 1], DIFF_V_DIM)
        ot = ot * lax.rsqrt(jnp.sum(ot * ot, axis=0, keepdims=True) * (1.0 / DIFF_V_DIM) + EPS)
        heads.append(ot * subln_ref[...] * (1.0 - lam_init))
    o_ref[...] = jnp.concatenate(heads, axis=0).T.astype(BF16)


def _diff_call(qdt, kd, vtd, lq1, lk1, lq2, lk2, subln_col, lam_init, batch, seq_len):
    n = kd.shape[0]
    tq, tk, hps = ATTN_Q_ROWS, ATTN_K_ROWS, DIFF_HEADS_PER_STEP
    qt = seq_len // tq
    small = lambda a: pl.BlockSpec(a.shape, lambda b, h, i: (0, 0))
    return pl.pallas_call(
        functools.partial(_diff_kernel, lam_init),
        grid=(batch, DIFF_HEADS // hps, qt),
        in_specs=[pl.BlockSpec((hps * LANES, tq), lambda b, h, i: (h, b * qt + i)),
                  pl.BlockSpec((seq_len, hps * LANES), lambda b, h, i: (b, h)),
                  pl.BlockSpec((seq_len // tk, hps * DIFF_V_DIM, tk), lambda b, h, i: (b, h, 0)),
                  small(lq1), small(lk1), small(lq2), small(lk2), small(subln_col)],
        out_specs=pl.BlockSpec((tq, hps * LANES), lambda b, h, i: (b * qt + i, h)),
        out_shape=jax.ShapeDtypeStruct((n, DIFF_V_WIDTH), BF16),
        scratch_shapes=_attn_scratch(2 * hps, DIFF_V_DIM, tq, tk),
        compiler_params=pltpu.CompilerParams(dimension_semantics=("parallel", "parallel", "arbitrary"),
                                             vmem_limit_bytes=VMEM_LIMIT_BYTES),
        name="diff_attn",
    )(qdt, kd, vtd, lq1, lk1, lq2, lk2, subln_col)


def _merge_kernel(x_ref, om_ref, od_ref, sgm_ref, sgd_ref, wmu_ref, wdu_ref, wout_ref, gffn_ref, wr_hi_ref,
                  wr_lo_ref, br_ref, x1_ref, h2_ref, route_ref, route_t_ref, cnt_ref):
    merged = (sgm_ref[...].astype(F32) * _dot(om_ref[...], wmu_ref[...])
              + sgd_ref[...].astype(F32) * _dot(od_ref[...], wdu_ref[...]))
    x1 = x_ref[...] + _dot(merged.astype(BF16), wout_ref[...])
    x1_ref[...] = x1
    h2 = _rms(x1, x1.shape[-1]) * gffn_ref[...]
    h2_hi = h2.astype(BF16)
    h2_ref[...] = h2_hi

    h2_lo = (h2 - h2_hi.astype(F32)).astype(BF16)
    logits = (_dot(h2_hi, wr_hi_ref[...]) + _dot(h2_lo, wr_hi_ref[...]) + _dot(h2_hi, wr_lo_ref[...])
              + br_ref[...])
    lane = lax.broadcasted_iota(jnp.int32, logits.shape, 1)
    neg = -jnp.inf
    big = jnp.int32(1 << 20)

    def top(vals):
        mx = jnp.max(vals, axis=-1, keepdims=True)
        idx = jnp.min(jnp.where(vals == mx, lane, big), axis=-1, keepdims=True)
        return mx, idx

    gl = jnp.where((lane >= N_EXPERTS) & (lane < N_EXPERTS + N_GROUPS), logits, neg)
    gmax, gidx = top(gl)
    pg_sel = 1.0 / jnp.sum(jnp.exp(gl - gmax), axis=-1, keepdims=True)
    el = jnp.where((lane < N_EXPERTS) & (lane // EXPERTS_PER_GROUP == gidx - N_EXPERTS), logits, neg)
    m1, i1 = top(el)
    m2, i2 = top(jnp.where(lane == i1, neg, el))
    e2 = jnp.exp(m2 - m1)
    w1 = pg_sel / (1.0 + e2)
    w2 = w1 * e2

    tm = logits.shape[0]
    sel = (lane == i1) | (lane == i2)
    earlier = (lax.broadcasted_iota(jnp.int32, (tm, tm), 1) < lax.broadcasted_iota(jnp.int32, (tm, tm), 0))
    rank = _dot(jnp.where(earlier, 1.0, 0.0).astype(BF16), jnp.where(sel, 1.0, 0.0).astype(BF16))
    cnt = jnp.sum(jnp.where(sel, 1.0, 0.0), axis=0, keepdims=True)
    seg = jnp.floor((cnt + (SEG_ALIGN - 1)) * (1.0 / SEG_ALIGN))
    before = (lax.broadcasted_iota(jnp.int32, (LANES, LANES), 0) < lax.broadcasted_iota(jnp.int32, (LANES, LANES), 1))
    off = _dot(jnp.broadcast_to(seg, (8, LANES)).astype(BF16), jnp.where(before, 1.0, 0.0).astype(BF16))[0:1] * SEG_ALIGN
    dest = off + rank
    d1 = jnp.sum(jnp.where(lane == i1, dest, 0.0), axis=-1, keepdims=True)
    d2 = jnp.sum(jnp.where(lane == i2, dest, 0.0), axis=-1, keepdims=True)
    route = jnp.where(lane == 0, d1, jnp.where(lane == 1, d2, jnp.where(lane == 2, w1, jnp.where(lane == 3, w2, 0.0))))
    route_ref[...] = route
    route_t_ref[0] = route.T[0:8, :]
    cnt_ref[0] = seg * SEG_ALIGN


def _merge_call(x2, om, od, sgm, sgd, p):
    n, d = x2.shape
    tm = MERGE_ROWS
    row = lambda i: (i, 0)
    const = lambda i: (0, 0)
    weights = [p["wmu"], p["wdu"], p["wout"], p["gffn"], p["wr_hi"], p["wr_lo"], p["br"]]
    return pl.pallas_call(
        _merge_kernel,
        grid=(n // tm,),
        in_specs=([pl.BlockSpec((tm, a.shape[1]), row) for a in (x2, om, od, sgm, sgd)]
                  + [pl.BlockSpec(w.shape, const) for w in weights]),
        out_specs=[pl.BlockSpec((tm, d), row), pl.BlockSpec((tm, d), row), pl.BlockSpec((tm, LANES), row),
                   pl.BlockSpec((1, 8, tm), lambda i: (i, 0, 0)), pl.BlockSpec((1, 1, LANES), lambda i: (i, 0, 0))],
        out_shape=[jax.ShapeDtypeStruct((n, d), F32), jax.ShapeDtypeStruct((n, d), BF16),
                   jax.ShapeDtypeStruct((n, LANES), F32), jax.ShapeDtypeStruct((n // tm, 8, tm), F32),
                   jax.ShapeDtypeStruct((n // tm, 1, LANES), F32)],
        compiler_params=pltpu.CompilerParams(dimension_semantics=("parallel",), vmem_limit_bytes=VMEM_LIMIT_BYTES),
        name="merge_router",
    )(x2, om, od, sgm, sgd, *weights)


def _segment_copies(i, seg_dst_ref, seg_rows_ref, tile_off_ref, global_ref, tile_ref, sem, to_global):
    def body(e, carry):
        k = i * N_EXPERTS + e
        rows = pl.multiple_of(seg_rows_ref[k], SEG_ALIGN)

        @pl.when(rows > 0)
        def _():
            g = global_ref.at[pl.ds(pl.multiple_of(seg_dst_ref[k], SEG_ALIGN), rows)]
            t = tile_ref.at[pl.ds(pl.multiple_of(tile_off_ref[k], SEG_ALIGN), rows)]
            src, dst = (t, g) if to_global else (g, t)
            pltpu.make_async_copy(src, dst, sem).start()

        return carry

    lax.fori_loop(0, N_EXPERTS, body, 0)


def _wait_rows(tile_ref, rows, sem):
    @pl.when(rows > 0)
    def _():
        view = tile_ref.at[pl.ds(0, pl.multiple_of(rows, SEG_ALIGN))]
        pltpu.make_async_copy(view, view, sem).wait()


def _zero_unused_rows(tail_dst_ref, tail_rows_ref, n_used_ref, xs_ref, zero_ref, sem, start):
    n_tiles = xs_ref.shape[0] // EXPERT_ROWS
    if start:
        zero_ref[...] = jnp.zeros(zero_ref.shape, BF16)

    def tail(e, total):
        rows = pl.multiple_of(tail_rows_ref[e], SEG_ALIGN)
        if start:
            @pl.when(rows > 0)
            def _():
                dst = xs_ref.at[pl.ds(pl.multiple_of(tail_dst_ref[e], SEG_ALIGN), rows)]
                pltpu.make_async_copy(zero_ref.at[pl.ds(0, rows)], dst, sem).start()

        return total + rows

    total = lax.fori_loop(0, N_EXPERTS, tail, 0)
    if not start:
        _wait_rows(xs_ref, total + (n_tiles - n_used_ref[0]) * EXPERT_ROWS, sem)
        return

    def unused(t, carry):
        dst = xs_ref.at[pl.ds(pl.multiple_of(t * EXPERT_ROWS, EXPERT_ROWS), EXPERT_ROWS)]
        pltpu.make_async_copy(zero_ref, dst, sem).start()
        return carry

    lax.fori_loop(n_used_ref[0], n_tiles, unused, 0)


def _sort_kernel(seg_dst_ref, seg_rows_ref, tile_off_ref, tile_rows_ref, tail_dst_ref, tail_rows_ref, n_used_ref,
                 h2_ref, route_t_ref, xs_ref, sorted_ref, zero_ref, sem, zero_sem):
    i = pl.program_id(0)
    tm = h2_ref.shape[0]

    @pl.when(i == 0)
    def _():
        _zero_unused_rows(tail_dst_ref, tail_rows_ref, n_used_ref, xs_ref, zero_ref, zero_sem, True)

    d1 = route_t_ref[0, 0:1, :].astype(jnp.int32)
    d2 = route_t_ref[0, 1:2, :].astype(jnp.int32)
    r = lax.broadcasted_iota(jnp.int32, (SORT_ROWS, tm), 0)
    perm = jnp.where((r == d1) | (r == d2), 1.0, 0.0).astype(BF16)
    slot = i % 2
    sorted_ref[slot] = _dot(perm, h2_ref[...]).astype(BF16)
    _segment_copies(i, seg_dst_ref, seg_rows_ref, tile_off_ref, xs_ref, sorted_ref.at[slot], sem.at[slot], True)

    @pl.when(i > 0)
    def _():
        _wait_rows(sorted_ref.at[1 - slot], tile_rows_ref[jnp.maximum(i - 1, 0)], sem.at[1 - slot])

    @pl.when(i == pl.num_programs(0) - 1)
    def _():
        _wait_rows(sorted_ref.at[slot], tile_rows_ref[i], sem.at[slot])
        _zero_unused_rows(tail_dst_ref, tail_rows_ref, n_used_ref, xs_ref, zero_ref, zero_sem, False)


def _sort_call(h2, route_t, sched, max_rows):
    n, d = h2.shape
    tm = ROUTE_ROWS
    return pl.pallas_call(
        _sort_kernel,
        grid_spec=pltpu.PrefetchScalarGridSpec(
            num_scalar_prefetch=7,
            grid=(n // tm,),
            in_specs=[pl.BlockSpec((tm, d), lambda i, *_: (i, 0)),
                      pl.BlockSpec((1, 8, tm), lambda i, *_: (i, 0, 0))],
            out_specs=pl.BlockSpec(memory_space=pl.ANY),
            scratch_shapes=[pltpu.VMEM((2, SORT_ROWS, d), BF16), pltpu.VMEM((EXPERT_ROWS, d), BF16),
                            pltpu.SemaphoreType.DMA((2,)), pltpu.SemaphoreType.DMA(())],
        ),
        out_shape=jax.ShapeDtypeStruct((max_rows, d), BF16),
        compiler_params=pltpu.CompilerParams(dimension_semantics=("arbitrary",), vmem_limit_bytes=VMEM_LIMIT_BYTES),
        name="moe_sort",
    )(sched["seg_dst"], sched["seg_rows"], sched["tile_off"], sched["tile_rows"], sched["tail_dst"],
      sched["tail_rows"], sched["n_used"], h2, route_t)


def _expert_kernel(tile_expert_ref, n_used_ref, xs_ref, wg_ref, wu_ref, wd_ref, ys_ref, wg_bf, wu_bf, wd_bf):
    t = pl.program_id(0)
    used = t < n_used_ref[0]

    @pl.when(used & ((t == 0) | (tile_expert_ref[t] != tile_expert_ref[jnp.maximum(t - 1, 0)])))
    def _():
        wg_bf[...] = wg_ref[0].astype(BF16)
        wu_bf[...] = wu_ref[0].astype(BF16)
        wd_bf[...] = wd_ref[0].astype(BF16)

    @pl.when(used)
    def _():
        xs = xs_ref[...]
        gate = _dot(xs, wg_bf[...])
        up = _dot(xs, wu_bf[...])
        hidden = (gate * jax.nn.sigmoid(gate) * up).astype(BF16)
        ys_ref[...] = _dot(hidden, wd_bf[...]).astype(BF16)

    @pl.when(jnp.logical_not(used))
    def _():
        ys_ref[...] = jnp.zeros(ys_ref.shape, BF16)


def _expert_call(xs, wg, wu, wd, sched):
    rows, d = xs.shape
    tr = EXPERT_ROWS
    blk = lambda t, te, nu: (jnp.minimum(t, nu[0] - 1), 0)
    wsel = lambda t, te, nu: (te[jnp.minimum(t, nu[0] - 1)], 0, 0)
    return pl.pallas_call(
        _expert_kernel,
        grid_spec=pltpu.PrefetchScalarGridSpec(
            num_scalar_prefetch=2,
            grid=(rows // tr,),
            in_specs=[pl.BlockSpec((tr, d), blk),
                      pl.BlockSpec((1, d, EXPERT_FF), wsel), pl.BlockSpec((1, d, EXPERT_FF), wsel),
                      pl.BlockSpec((1, EXPERT_FF, d), wsel)],
            out_specs=pl.BlockSpec((tr, d), lambda t, te, nu: (t, 0)),
            scratch_shapes=[pltpu.VMEM((d, EXPERT_FF), BF16), pltpu.VMEM((d, EXPERT_FF), BF16),
                            pltpu.VMEM((EXPERT_FF, d), BF16)],
        ),
        out_shape=jax.ShapeDtypeStruct((rows, d), BF16),
        compiler_params=pltpu.CompilerParams(dimension_semantics=("arbitrary",), vmem_limit_bytes=VMEM_LIMIT_BYTES),
        name="moe_experts",
    )(sched["tile_expert"], sched["n_used"], xs, wg, wu, wd)


def _combine_kernel(seg_dst_ref, seg_rows_ref, tile_off_ref, tile_rows_ref, ys_ref, route_ref, x1_ref, o_ref,
                    buf_ref, sem):
    i = pl.program_id(0)
    tm = x1_ref.shape[0]
    slot = i % 2

    def fetch(tile, into):
        buf_ref[into] = jnp.zeros(buf_ref.shape[1:], BF16)
        _segment_copies(tile, seg_dst_ref, seg_rows_ref, tile_off_ref, ys_ref, buf_ref.at[into], sem.at[into], False)

    @pl.when(i == 0)
    def _():
        fetch(i, slot)

    @pl.when(i + 1 < pl.num_programs(0))
    def _():
        fetch(i + 1, 1 - slot)

    route = route_ref[...]
    d1 = route[:, 0:1].astype(jnp.int32)
    d2 = route[:, 1:2].astype(jnp.int32)
    w1 = route[:, 2:3]
    w2 = route[:, 3:4]
    r = lax.broadcasted_iota(jnp.int32, (tm, SORT_ROWS), 1)
    weights = (jnp.where(r == d1, w1, 0.0) + jnp.where(r == d2, w2, 0.0)).astype(BF16)
    _wait_rows(buf_ref.at[slot], tile_rows_ref[i], sem.at[slot])
    o_ref[...] = x1_ref[...] + _dot(weights, buf_ref[slot])


def _combine_call(ys, route, x1, sched):
    n, d = x1.shape
    tm = ROUTE_ROWS
    return pl.pallas_call(
        _combine_kernel,
        grid_spec=pltpu.PrefetchScalarGridSpec(
            num_scalar_prefetch=4,
            grid=(n // tm,),
            in_specs=[pl.BlockSpec(memory_space=pl.ANY),
                      pl.BlockSpec((tm, LANES), lambda i, *_: (i, 0)),
                      pl.BlockSpec((tm, d), lambda i, *_: (i, 0))],
            out_specs=pl.BlockSpec((tm, d), lambda i, *_: (i, 0)),
            scratch_shapes=[pltpu.VMEM((2, SORT_ROWS, d), BF16), pltpu.SemaphoreType.DMA((2,))],
        ),
        out_shape=jax.ShapeDtypeStruct((n, d), F32),
        compiler_params=pltpu.CompilerParams(dimension_semantics=("arbitrary",), vmem_limit_bytes=VMEM_LIMIT_BYTES),
        name="moe_combine",
    )(sched["seg_dst"], sched["seg_rows"], sched["tile_off"], sched["tile_rows"], ys, route, x1)


def _schedule_kernel(cnt_ref, seg_dst_ref, tile_off_ref, tile_rows_ref, misc_ref):
    hp = functools.partial(jnp.dot, preferred_element_type=F32, precision=lax.Precision.HIGHEST)
    cnt = cnt_ref[...]
    n_tiles = cnt.shape[0]
    tile_before = jnp.where(lax.broadcasted_iota(jnp.int32, (n_tiles, n_tiles), 1)
                            < lax.broadcasted_iota(jnp.int32, (n_tiles, n_tiles), 0), 1.0, 0.0)
    expert_before = jnp.where(lax.broadcasted_iota(jnp.int32, (LANES, LANES), 0)
                              < lax.broadcasted_iota(jnp.int32, (LANES, LANES), 1), 1.0, 0.0)
    expert_rows = jnp.sum(cnt, axis=0, keepdims=True)
    region = jnp.floor((expert_rows + (EXPERT_ROWS - 1)) * (1.0 / EXPERT_ROWS)) * EXPERT_ROWS
    region_start = hp(jnp.broadcast_to(region, (8, LANES)), expert_before)[0:1]
    seg_dst_ref[...] = (region_start + hp(tile_before, cnt)).astype(jnp.int32)
    tile_off_ref[...] = hp(cnt, expert_before).astype(jnp.int32)
    tile_rows_ref[...] = jnp.broadcast_to(jnp.sum(cnt, axis=-1, keepdims=True), cnt.shape).astype(jnp.int32)
    n_used = jnp.sum(region, axis=-1, keepdims=True) * (1.0 / EXPERT_ROWS)
    row = lax.broadcasted_iota(jnp.int32, (8, LANES), 0)
    misc = jnp.where(row == 0, region_start + expert_rows,
                     jnp.where(row == 1, region - expert_rows,
                               jnp.where(row == 2, region_start + region, n_used)))
    misc_ref[...] = misc.astype(jnp.int32)


def _moe_schedule(cnt, n_tokens):
    n_tiles = cnt.shape[0]
    table = jax.ShapeDtypeStruct((n_tiles, LANES), jnp.int32)
    seg_dst, tile_off, tile_rows, misc = pl.pallas_call(
        _schedule_kernel,
        out_shape=[table, table, table, jax.ShapeDtypeStruct((8, LANES), jnp.int32)],
        name="moe_schedule",
    )(cnt.reshape(n_tiles, LANES))
    max_rows = 2 * n_tokens + n_tiles * N_EXPERTS * (SEG_ALIGN - 1) + N_EXPERTS * (EXPERT_ROWS - 1)
    max_tiles = -(-max_rows // EXPERT_ROWS)
    tile_start = jnp.arange(max_tiles, dtype=jnp.int32) * EXPERT_ROWS
    region_end = misc[2, :N_EXPERTS]
    tile_expert = jnp.minimum(jnp.sum((region_end[None, :] <= tile_start[:, None]).astype(jnp.int32), axis=1),
                              N_EXPERTS - 1)
    flat = lambda a: a[:, :N_EXPERTS].reshape(-1)
    sched = {
        "seg_dst": flat(seg_dst),
        "seg_rows": flat(cnt.reshape(n_tiles, LANES).astype(jnp.int32)),
        "tile_off": flat(tile_off),
        "tile_rows": tile_rows[:, 0],
        "tail_dst": misc[0, :N_EXPERTS],
        "tail_rows": misc[1, :N_EXPERTS],
        "tile_expert": tile_expert,
        "n_used": misc[3, :1],
    }
    return sched, max_tiles * EXPERT_ROWS


def _rotary_tables(seq_len, rot_dim, period, first, gain, scale):
    half = rot_dim // 2
    pos = jnp.arange(seq_len, dtype=F32)
    inv = 1.0 / (ROPE_THETA ** (jnp.arange(0, rot_dim, 2, dtype=F32) / rot_dim))
    ang = pos[:, None] * inv[None, :]
    cos, sin = jnp.cos(ang), jnp.sin(ang)
    lane = jnp.arange(LANES)
    rel = (lane % period) - first
    active = (rel >= 0) & (rel < rot_dim)
    idx = jnp.clip(rel, 0, rot_dim - 1) % half
    sign = jnp.where(rel < half, -1.0, 1.0)
    partner = jnp.where(active, jnp.where(rel < half, lane + half, lane - half), lane)
    c = jnp.where(active[None, :], cos[:, idx], 1.0)
    s = jnp.where(active[None, :], sin[:, idx] * sign[None, :], 0.0)
    gain = gain.astype(F32)
    return (c * gain[None, :] * scale).astype(F32), (s * gain[partner][None, :] * scale).astype(F32)


def _head_pad(w, heads, width):
    r = w.shape[0]
    w = w.reshape(r, heads, width)
    return jnp.pad(w, ((0, 0), (0, 0), (0, LANES - width))).reshape(r, heads * LANES)


def _layer_params(l, seq_len, norm_mix, w_in, mla_q_latent_norm, w_mla_uq, mla_kv_latent_norm, w_mla_ukv,
                  mla_q_gain, mla_k_gain, diff_q_gain, diff_k_gain, w_mla_up, w_diff_up, w_out, norm_ffn,
                  w_router_group, b_router_group, w_router_expert, b_router_expert):
    d = w_in.shape[1]
    sizes = (MLA_Q_RANK, MLA_KV_RANK, MLA_ROPE, DIFF_QK_WIDTH, DIFF_QK_WIDTH, DIFF_V_WIDTH, d, d)
    offs = [0]
    for s in sizes:
        offs.append(offs[-1] + s)
    wi = w_in[l]
    seg = [wi[:, offs[k]:offs[k + 1]] for k in range(len(sizes))]
    row = lambda g: g.astype(F32)[None, :]
    p = {}
    p["gmix"] = row(norm_mix[l])
    p["wql"] = seg[0].astype(BF16)
    p["wkvl"] = seg[1].astype(BF16)
    p["wkr"] = jnp.pad(seg[2], ((0, 0), (MLA_NOPE, LANES - MLA_QK))).astype(BF16)
    p["wdk"] = seg[4].astype(BF16)
    p["wdqvt"] = jnp.concatenate([seg[3].T, seg[5].T], axis=0).astype(BF16)
    p["wgm"], p["wgd"] = seg[6].astype(BF16), seg[7].astype(BF16)
    p["gql"] = row(mla_q_latent_norm[l])
    p["wuqt"] = _head_pad(w_mla_uq[l], MLA_HEADS, MLA_QK).T.astype(BF16)
    p["gkvl"] = row(mla_kv_latent_norm[l])
    ukv = w_mla_ukv[l].reshape(MLA_KV_RANK, MLA_HEADS, MLA_NOPE + MLA_V)
    p["wuk"] = _head_pad(ukv[:, :, :MLA_NOPE].reshape(MLA_KV_RANK, -1), MLA_HEADS, MLA_NOPE).astype(BF16)
    p["wuvt"] = ukv[:, :, MLA_NOPE:].reshape(MLA_KV_RANK, -1).T.astype(BF16)
    gq = jnp.pad(mla_q_gain[l], (0, LANES - MLA_QK))
    gk = jnp.pad(mla_k_gain[l], (0, LANES - MLA_QK))
    nope = jnp.arange(LANES) < MLA_NOPE
    p["gkn"] = jnp.where(nope, gk, 0.0).astype(F32)[None, :]
    aq, bq = _rotary_tables(seq_len, MLA_ROPE, LANES, MLA_NOPE, gq, LOG2E * MLA_QK ** -0.5)
    p["aq"], p["bq"] = aq.T, bq.T
    ak, bk = _rotary_tables(seq_len, MLA_ROPE, LANES, MLA_NOPE, jnp.where(nope, 0.0, gk), 1.0)
    p["ak"], p["bk"] = ak, bk
    adq, bdq = _rotary_tables(seq_len, DIFF_ROPE, DIFF_HEAD_DIM, 0, jnp.tile(diff_q_gain[l], 2),
                              LOG2E * DIFF_HEAD_DIM ** -0.5)
    p["adq"], p["bdq"] = adq.T, bdq.T
    p["adk"], p["bdk"] = _rotary_tables(seq_len, DIFF_ROPE, DIFF_HEAD_DIM, 0, jnp.tile(diff_k_gain[l], 2), 1.0)
    p["wmu"] = w_mla_up[l].astype(BF16)
    p["wdu"] = w_diff_up[l].astype(BF16)
    p["wout"] = w_out[l].astype(BF16)
    p["gffn"] = row(norm_ffn[l])
    wr = jnp.concatenate([w_router_expert[l], w_router_group[l]], axis=1).astype(F32)
    wr = jnp.pad(wr, ((0, 0), (0, LANES - wr.shape[1])))
    p["wr_hi"] = wr.astype(BF16)
    p["wr_lo"] = (wr - p["wr_hi"].astype(F32)).astype(BF16)
    br = jnp.concatenate([b_router_expert[l], b_router_group[l]]).astype(F32)
    p["br"] = jnp.pad(br, (0, LANES - br.shape[0]))[None, :]
    return p


def kernel(x, norm_mix, w_in, mla_q_latent_norm, w_mla_uq, mla_kv_latent_norm, w_mla_ukv, mla_q_gain, mla_k_gain, diff_q_gain, diff_k_gain, lambda_q1, lambda_k1, lambda_q2, lambda_k2, diff_subln, w_mla_up, w_diff_up, w_out, norm_ffn, w_router_group, b_router_group, w_router_expert, b_router_expert, w_expert_gate, w_expert_up, w_expert_down):
    batch, seq_len, d = x.shape
    x2 = x.reshape(batch * seq_len, d)
    row = lambda g: g.astype(F32)[None, :]
    for l in range(norm_mix.shape[0]):
        lam_init = 0.8 - 0.6 * math.exp(-0.3 * l)
        p = _layer_params(l, seq_len, norm_mix, w_in, mla_q_latent_norm, w_mla_uq, mla_kv_latent_norm, w_mla_ukv,
                          mla_q_gain, mla_k_gain, diff_q_gain, diff_k_gain, w_mla_up, w_diff_up, w_out, norm_ffn,
                          w_router_group, b_router_group, w_router_expert, b_router_expert)
        qmt, km, vtm, qdt, kd, vtd, sgm, sgd = _proj_call(x2, seq_len, p)
        om = _mla_call(qmt, km, vtm, batch, seq_len)
        od = _diff_call(qdt, kd, vtd, row(lambda_q1[l]), row(lambda_k1[l]), row(lambda_q2[l]), row(lambda_k2[l]),
                        diff_subln[l].astype(F32)[:, None], lam_init, batch, seq_len)
        x1, h2, route, route_t, cnt = _merge_call(x2, om, od, sgm, sgd, p)
        sched, max_rows = _moe_schedule(cnt, x2.shape[0])
        xs = _sort_call(h2, route_t, sched, max_rows)
        ys = _expert_call(xs, w_expert_gate[l], w_expert_up[l], w_expert_down[l], sched)
        x2 = _combine_call(ys, route, x1, sched)
    return x2.reshape(batch, seq_len, d)
```

```python
import functools
import math

import jax
import jax.numpy as jnp
from jax import lax
from jax.experimental import pallas as pl
from jax.experimental.pallas import tpu as pltpu

CHUNK = 64
ROPE_THETA = 500000.0
EPS = 1e-6

MLA_HEADS = 8
MLA_NOPE = 64
MLA_ROPE = 32
MLA_V = 64
MLA_QK = MLA_NOPE + MLA_ROPE
MLA_Q_RANK = 256
MLA_KV_RANK = 128

DIFF_HEADS = 4
DIFF_HEAD_DIM = 64
DIFF_V_DIM = 2 * DIFF_HEAD_DIM
DIFF_ROPE = DIFF_HEAD_DIM // 4
DIFF_QK_WIDTH = DIFF_HEADS * 2 * DIFF_HEAD_DIM
DIFF_V_WIDTH = DIFF_HEADS * DIFF_V_DIM

N_GROUPS = 4
EXPERTS_PER_GROUP = 8
N_EXPERTS = N_GROUPS * EXPERTS_PER_GROUP
EXPERT_FF = 256

LANES = 128
VMEM_LIMIT_BYTES = 48 * 1024 * 1024

PROJ_ROWS = 512
ATTN_Q_ROWS = 512
ATTN_K_ROWS = 256
MERGE_ROWS = 512
ROUTE_ROWS = MERGE_ROWS
SEG_ALIGN = 16
SORT_ROWS = 2 * ROUTE_ROWS + N_EXPERTS * SEG_ALIGN
EXPERT_ROWS = 512
MLA_HEADS_PER_STEP = 4
DIFF_HEADS_PER_STEP = 2
LOG2E = 1.4426950408889634

BF16 = jnp.bfloat16
F32 = jnp.float32


def _dot(a, b):
    return jnp.dot(a, b, preferred_element_type=F32)


def _dot_nt(a, b):
    return lax.dot_general(a, b, (((1,), (1,)), ((), ())), preferred_element_type=F32)


def _rms(x, width):
    return x * lax.rsqrt(jnp.sum(x * x, axis=-1, keepdims=True) * (1.0 / width) + EPS)


def _rotary_partner(y, half):
    lane = lax.broadcasted_iota(jnp.int32, y.shape, 1)
    up = pltpu.roll(y, LANES - half, 1)
    down = pltpu.roll(y, half, 1)
    return jnp.where((lane // half) % 2 == 0, up, down)


def _swap_row_blocks(y, first, half, period):
    parts = []
    for base in range(0, y.shape[0], period):
        a = base + first
        parts += [y[base:a], y[a + half:a + 2 * half], y[a:a + half], y[a + 2 * half:base + period]]
    return jnp.concatenate([p for p in parts if p.shape[0]], axis=0)


def _store_k_tiles(o_ref, vt):
    tk = o_ref.shape[-1]
    for c in range(o_ref.shape[0]):
        o_ref[c] = vt[:, c * tk:(c + 1) * tk].astype(BF16)


def _proj_kernel(x_ref, gmix_ref, wql_ref, wkvl_ref, wkr_ref, wdk_ref, wdqvt_ref, wgm_ref, wgd_ref,
                 gql_ref, wuqt_ref, gkvl_ref, wuk_ref, wuvt_ref, gkn_ref,
                 aq_ref, bq_ref, adq_ref, bdq_ref, ak_ref, bk_ref, adk_ref, bdk_ref,
                 qmt_ref, km_ref, vtm_ref, qdt_ref, kd_ref, vtd_ref, sgm_ref, sgd_ref):
    x = x_ref[...]
    h = (_rms(x, x.shape[-1]) * gmix_ref[...]).astype(BF16)

    ql = (_rms(_dot(h, wql_ref[...]), MLA_Q_RANK) * gql_ref[...]).astype(BF16)
    qt = _dot_nt(wuqt_ref[...], ql)
    aq, bq = aq_ref[...], bq_ref[...]
    for hd in range(MLA_HEADS):
        rows = slice(hd * LANES, (hd + 1) * LANES)
        qh = qt[rows]
        r = lax.rsqrt(jnp.sum(qh * qh, axis=0, keepdims=True) * (1.0 / MLA_QK) + EPS)
        y = (qh * aq + _swap_row_blocks(qh, MLA_NOPE, MLA_ROPE // 2, LANES) * bq) * r
        qmt_ref[rows, :] = y.astype(BF16)

    kvl = (_rms(_dot(h, wkvl_ref[...]), MLA_KV_RANK) * gkvl_ref[...]).astype(BF16)
    kr = _dot(h, wkr_ref[...])
    kr_rot = kr * ak_ref[...] + _rotary_partner(kr, MLA_ROPE // 2) * bk_ref[...]
    kr_ss = jnp.sum(kr * kr, axis=-1, keepdims=True)
    kn = _dot(kvl, wuk_ref[...])
    _store_k_tiles(vtm_ref, _dot_nt(wuvt_ref[...], kvl))
    gkn = gkn_ref[...]
    for hd in range(MLA_HEADS):
        sl = slice(hd * LANES, (hd + 1) * LANES)
        knh = kn[:, sl]
        r = lax.rsqrt((jnp.sum(knh * knh, axis=-1, keepdims=True) + kr_ss) * (1.0 / MLA_QK) + EPS)
        km_ref[:, sl] = ((knh * gkn + kr_rot) * r).astype(BF16)

    qvt = _dot_nt(wdqvt_ref[...], h)
    _store_k_tiles(vtd_ref, qvt[DIFF_QK_WIDTH:])
    adq, bdq = adq_ref[...], bdq_ref[...]
    for hd in range(DIFF_HEADS):
        rows = slice(hd * LANES, (hd + 1) * LANES)
        qh = qvt[rows]
        t = qh * adq + _swap_row_blocks(qh, 0, DIFF_ROPE // 2, DIFF_HEAD_DIM) * bdq
        halves = []
        for f in range(2):
            part = qh[f * DIFF_HEAD_DIM:(f + 1) * DIFF_HEAD_DIM]
            r = lax.rsqrt(jnp.sum(part * part, axis=0, keepdims=True) * (1.0 / DIFF_HEAD_DIM) + EPS)
            halves.append(t[f * DIFF_HEAD_DIM:(f + 1) * DIFF_HEAD_DIM] * r)
        qdt_ref[rows, :] = jnp.concatenate(halves, axis=0).astype(BF16)

    kd = _dot(h, wdk_ref[...])
    adk, bdk = adk_ref[...], bdk_ref[...]
    for hd in range(DIFF_HEADS):
        sl = slice(hd * LANES, (hd + 1) * LANES)
        th = kd[:, sl]
        lane = lax.broadcasted_iota(jnp.int32, th.shape, 1)
        sq = th * th
        lo = jnp.sum(jnp.where(lane < DIFF_HEAD_DIM, sq, 0.0), axis=-1, keepdims=True)
        tot = jnp.sum(sq, axis=-1, keepdims=True)
        r = lax.rsqrt(jnp.where(lane < DIFF_HEAD_DIM, lo, tot - lo) * (1.0 / DIFF_HEAD_DIM) + EPS)
        kd_ref[:, sl] = ((th * adk + _rotary_partner(th, DIFF_ROPE // 2) * bdk) * r).astype(BF16)

    sgm_ref[...] = jax.nn.sigmoid(_dot(h, wgm_ref[...])).astype(BF16)
    sgd_ref[...] = jax.nn.sigmoid(_dot(h, wgd_ref[...])).astype(BF16)


def _proj_call(x2, seq_len, p):
    n, d = x2.shape
    tm = PROJ_ROWS
    pos_blocks = seq_len // tm
    row = lambda i: (i, 0)
    col = lambda i: (0, i)
    const = lambda i: (0, 0)
    weights = [p["gmix"], p["wql"], p["wkvl"], p["wkr"], p["wdk"], p["wdqvt"], p["wgm"], p["wgd"],
               p["gql"], p["wuqt"], p["gkvl"], p["wuk"], p["wuvt"], p["gkn"]]
    feature_major_tables = [p["aq"], p["bq"], p["adq"], p["bdq"]]
    token_major_tables = [p["ak"], p["bk"], p["adk"], p["bdk"]]
    in_specs = ([pl.BlockSpec((tm, d), row)]
                + [pl.BlockSpec(w.shape, const) for w in weights]
                + [pl.BlockSpec((LANES, tm), lambda i: (0, i % pos_blocks)) for _ in feature_major_tables]
                + [pl.BlockSpec((tm, LANES), lambda i: (i % pos_blocks, 0)) for _ in token_major_tables])
    tk = ATTN_K_ROWS
    k_tiles = lambda width: (pl.BlockSpec((tm // tk, width, tk), lambda i: (i, 0, 0)),
                             jax.ShapeDtypeStruct((n // tk, width, tk), BF16))
    token_major = lambda width: (pl.BlockSpec((tm, width), row), jax.ShapeDtypeStruct((n, width), BF16))
    feature_major = lambda width: (pl.BlockSpec((width, tm), col), jax.ShapeDtypeStruct((width, n), BF16))
    outs = [feature_major(MLA_HEADS * LANES), token_major(MLA_HEADS * LANES), k_tiles(MLA_HEADS * MLA_V),
            feature_major(DIFF_QK_WIDTH), token_major(DIFF_QK_WIDTH), k_tiles(DIFF_V_WIDTH),
            token_major(d), token_major(d)]
    return pl.pallas_call(
        _proj_kernel,
        grid=(n // tm,),
        in_specs=in_specs,
        out_specs=[o[0] for o in outs],
        out_shape=[o[1] for o in outs],
        compiler_params=pltpu.CompilerParams(dimension_semantics=("parallel",), vmem_limit_bytes=VMEM_LIMIT_BYTES),
        name="proj",
    )(x2, *weights, *feature_major_tables, *token_major_tables)


def _chunk_mask_t(tk, tq, diag):
    kc = lax.broadcasted_iota(jnp.int32, (tk, tq), 0) // CHUNK + diag * (tk // CHUNK)
    qc = lax.broadcasted_iota(jnp.int32, (tk, tq), 1) // CHUNK
    return kc <= qc


ONES_ROWS = 16


def _with_ones_rows(vt):
    return jnp.concatenate([vt, jnp.ones((ONES_ROWS, vt.shape[1]), vt.dtype)], axis=0)


def _softmax_step_t(st, vt_ones, m_ref, acc_ref):
    m_prev = m_ref[...]
    m_new = jnp.maximum(m_prev, jnp.max(st, axis=0, keepdims=True))
    alpha = jnp.exp2(m_prev - m_new)
    pr = jnp.exp2(st - m_new)
    acc_ref[...] = alpha * acc_ref[...] + _dot(vt_ones, pr.astype(BF16))
    m_ref[...] = m_new


def _normalized(acc_ref, dv):
    acc = acc_ref[...]
    return acc[:dv] / acc[dv:dv + 1]


STATE_REFS = 4


def _attn_scratch(chains, dv, tq, tk):
    per_chain = [pltpu.VMEM((1, tq), F32), pltpu.VMEM((dv + ONES_ROWS, tq), F32),
                 pltpu.VMEM((tk, tq), F32), pltpu.VMEM((tk, tq), F32)]
    return per_chain * chains


def _flash_pipeline(q_tile, scratch_refs, score_fn, value_fn, tk, tq):
    assert tq == 2 * tk
    n_chains = len(scratch_refs) // STATE_REFS
    chains = [scratch_refs[STATE_REFS * c:STATE_REFS * (c + 1)] for c in range(n_chains)]
    for m_ref, acc_ref, _, _ in chains:
        m_ref[...] = jnp.full(m_ref.shape, -jnp.inf, F32)
        acc_ref[...] = jnp.zeros(acc_ref.shape, F32)

    def scores(t, slot):
        for c, ch in enumerate(chains):
            ch[2 + slot][...] = score_fn(c, t)

    def update(t, slot, diag=None):
        for c, ch in enumerate(chains):
            st = ch[2 + slot][...]
            if diag is not None:
                st = jnp.where(_chunk_mask_t(tk, tq, diag), st, -jnp.inf)
            _softmax_step_t(st, _with_ones_rows(value_fn(c, t)), ch[0], ch[1])

    scores(0, 0)

    def pair(p, carry):
        t = 2 * p
        scores(t + 1, 1)
        update(t, 0)
        scores(t + 2, 0)
        update(t + 1, 1)
        return carry

    lax.fori_loop(0, q_tile, pair, 0)
    first_diag = 2 * q_tile
    scores(first_diag + 1, 1)
    update(first_diag, 0, diag=0)
    update(first_diag + 1, 1, diag=1)
    return [ch[1] for ch in chains]


def _mla_kernel(qt_ref, k_ref, vt_ref, o_ref, *scratch_refs):
    tq, tk = ATTN_Q_ROWS, ATTN_K_ROWS

    def score_fn(c, t):
        rows = pl.ds(pl.multiple_of(t * tk, tk), tk)
        sl = slice(c * LANES, (c + 1) * LANES)
        return _dot(k_ref[rows, sl], qt_ref[sl, :])

    def value_fn(c, t):
        return vt_ref[t, c * MLA_V:(c + 1) * MLA_V, :]

    out = _flash_pipeline(pl.program_id(2), scratch_refs, score_fn, value_fn, tk, tq)
    ot = jnp.concatenate([_normalized(acc_ref, MLA_V) for acc_ref in out], axis=0)
    o_ref[...] = ot.T.astype(BF16)


def _mla_call(qmt, km, vtm, batch, seq_len):
    n = km.shape[0]
    tq, tk, hps = ATTN_Q_ROWS, ATTN_K_ROWS, MLA_HEADS_PER_STEP
    qt = seq_len // tq
    return pl.pallas_call(
        _mla_kernel,
        grid=(batch, MLA_HEADS // hps, qt),
        in_specs=[pl.BlockSpec((hps * LANES, tq), lambda b, h, i: (h, b * qt + i)),
                  pl.BlockSpec((seq_len, hps * LANES), lambda b, h, i: (b, h)),
                  pl.BlockSpec((seq_len // tk, hps * MLA_V, tk), lambda b, h, i: (b, h, 0))],
        out_specs=pl.BlockSpec((tq, hps * MLA_V), lambda b, h, i: (b * qt + i, h)),
        out_shape=jax.ShapeDtypeStruct((n, MLA_HEADS * MLA_V), BF16),
        scratch_shapes=_attn_scratch(hps, MLA_V, tq, tk),
        compiler_params=pltpu.CompilerParams(dimension_semantics=("parallel", "parallel", "arbitrary"),
                                             vmem_limit_bytes=VMEM_LIMIT_BYTES),
        name="mla_attn",
    )(qmt, km, vtm)


def _diff_kernel(lam_init, qt_ref, k_ref, vt_ref, lq1_ref, lk1_ref, lq2_ref, lk2_ref, subln_ref, o_ref,
                 *scratch_refs):
    tq, tk = ATTN_Q_ROWS, ATTN_K_ROWS
    hps = DIFF_HEADS_PER_STEP

    qs = []
    for hd in range(hps):
        qh = qt_ref[hd * LANES:(hd + 1) * LANES, :]
        zero = jnp.zeros((DIFF_HEAD_DIM, tq), BF16)
        qs += [jnp.concatenate([qh[:DIFF_HEAD_DIM], zero], axis=0), jnp.concatenate([zero, qh[DIFF_HEAD_DIM:]], axis=0)]

    def score_fn(c, t):
        rows = pl.ds(pl.multiple_of(t * tk, tk), tk)
        hd = c // 2
        return _dot(k_ref[rows, hd * LANES:(hd + 1) * LANES], qs[c])

    def value_fn(c, t):
        hd = c // 2
        return vt_ref[t, hd * DIFF_V_DIM:(hd + 1) * DIFF_V_DIM, :]

    out = _flash_pipeline(pl.program_id(2), scratch_refs, score_fn, value_fn, tk, tq)

    lam = (jnp.exp(jnp.sum(lq1_ref[...] * lk1_ref[...], axis=-1, keepdims=True))
           - jnp.exp(jnp.sum(lq2_ref[...] * lk2_ref[...], axis=-1, keepdims=True)) + lam_init)
    heads = []
    for hd in range(hps):
        ot = _normalized(out[2 * hd], DIFF_V_DIM) - lam * _normalized(out[2 * hd + 1], DIFF_V_DIM)
        ot = ot * lax.rsqrt(jnp.sum(ot * ot, axis=0, keepdims=True) * (1.0 / DIFF_V_DIM) + EPS)
        heads.append(ot * subln_ref[...] * (1.0 - lam_init))
    o_ref[...] = jnp.concatenate(heads, axis=0).T.astype(BF16)


def _diff_call(qdt, kd, vtd, lq1, lk1, lq2, lk2, subln_col, lam_init, batch, seq_len):
    n = kd.shape[0]
    tq, tk, hps = ATTN_Q_ROWS, ATTN_K_ROWS, DIFF_HEADS_PER_STEP
    qt = seq_len // tq
    small = lambda a: pl.BlockSpec(a.shape, lambda b, h, i: (0, 0))
    return pl.pallas_call(
        functools.partial(_diff_kernel, lam_init),
        grid=(batch, DIFF_HEADS // hps, qt),
        in_specs=[pl.BlockSpec((hps * LANES, tq), lambda b, h, i: (h, b * qt + i)),
                  pl.BlockSpec((seq_len, hps * LANES), lambda b, h, i: (b, h)),
                  pl.BlockSpec((seq_len // tk, hps * DIFF_V_DIM, tk), lambda b, h, i: (b, h, 0)),
                  small(lq1), small(lk1), small(lq2), small(lk2), small(subln_col)],
        out_specs=pl.BlockSpec((tq, hps * LANES), lambda b, h, i: (b * qt + i, h)),
        out_shape=jax.ShapeDtypeStruct((n, DIFF_V_WIDTH), BF16),
        scratch_shapes=_attn_scratch(2 * hps, DIFF_V_DIM, tq, tk),
        compiler_params=pltpu.CompilerParams(dimension_semantics=("parallel", "parallel", "arbitrary"),
                                             vmem_limit_bytes=VMEM_LIMIT_BYTES),
        name="diff_attn",
    )(qdt, kd, vtd, lq1, lk1, lq2, lk2, subln_col)


def _merge_kernel(x_ref, om_ref, od_ref, sgm_ref, sgd_ref, wmu_ref, wdu_ref, wout_ref, gffn_ref, wrt_ref,
                  brt_ref, x1_ref, h2_ref, route_ref, route_t_ref, cnt_ref):
    merged = (sgm_ref[...].astype(F32) * _dot(om_ref[...], wmu_ref[...])
              + sgd_ref[...].astype(F32) * _dot(od_ref[...], wdu_ref[...]))
    x1 = x_ref[...] + _dot(merged.astype(BF16), wout_ref[...])
    x1_ref[...] = x1
    h2 = _rms(x1, x1.shape[-1]) * gffn_ref[...]
    h2_hi = h2.astype(BF16)
    h2_ref[...] = h2_hi
    tm = h2.shape[0]

    h2_lo = (h2 - h2_hi.astype(F32)).astype(BF16)
    by_hi = _dot_nt(wrt_ref[...], h2_hi)
    logits = by_hi[:LANES] + by_hi[LANES:] + _dot_nt(wrt_ref[:LANES, :], h2_lo) + brt_ref[...]
    row = lax.broadcasted_iota(jnp.int32, logits.shape, 0)
    neg = -jnp.inf
    big = jnp.int32(1 << 20)

    def top(vals):
        mx = jnp.max(vals, axis=0, keepdims=True)
        idx = jnp.min(jnp.where(vals == mx, row, big), axis=0, keepdims=True)
        return mx, idx

    gl = jnp.where((row >= N_EXPERTS) & (row < N_EXPERTS + N_GROUPS), logits, neg)
    gmax, gidx = top(gl)
    pg_sel = 1.0 / jnp.sum(jnp.exp(gl - gmax), axis=0, keepdims=True)
    el = jnp.where((row < N_EXPERTS) & (row // EXPERTS_PER_GROUP == gidx - N_EXPERTS), logits, neg)
    m1, i1 = top(el)
    m2, i2 = top(jnp.where(row == i1, neg, el))
    e2 = jnp.exp(m2 - m1)
    w1 = pg_sel / (1.0 + e2)
    w2 = w1 * e2

    sel = jnp.where((row == i1) | (row == i2), 1.0, 0.0).astype(BF16)
    t_row = lax.broadcasted_iota(jnp.int32, (tm, tm), 0)
    t_col = lax.broadcasted_iota(jnp.int32, (tm, tm), 1)
    rank = _dot(sel, jnp.where(t_row < t_col, 1.0, 0.0).astype(BF16))
    cnt = _dot(sel, jnp.ones((tm, tm), BF16))
    seg = jnp.floor((cnt + (SEG_ALIGN - 1)) * (1.0 / SEG_ALIGN))
    e_row = lax.broadcasted_iota(jnp.int32, (LANES, LANES), 0)
    e_col = lax.broadcasted_iota(jnp.int32, (LANES, LANES), 1)
    off = _dot(jnp.where(e_col < e_row, 1.0, 0.0).astype(BF16), seg.astype(BF16)) * SEG_ALIGN
    dest = off + rank
    d1 = jnp.sum(jnp.where(row == i1, dest, 0.0), axis=0, keepdims=True)
    d2 = jnp.sum(jnp.where(row == i2, dest, 0.0), axis=0, keepdims=True)
    route_t = jnp.where(row == 0, d1, jnp.where(row == 1, d2, jnp.where(row == 2, w1, jnp.where(row == 3, w2, 0.0))))
    route_t_ref[0] = route_t[0:8]
    route_ref[...] = route_t.T
    cnt_ref[0] = (seg[:, :LANES] * SEG_ALIGN).T[0:1]


def _merge_call(x2, om, od, sgm, sgd, p):
    n, d = x2.shape
    tm = MERGE_ROWS
    row = lambda i: (i, 0)
    const = lambda i: (0, 0)
    weights = [p["wmu"], p["wdu"], p["wout"], p["gffn"], p["wrt"], p["brt"]]
    return pl.pallas_call(
        _merge_kernel,
        grid=(n // tm,),
        in_specs=([pl.BlockSpec((tm, a.shape[1]), row) for a in (x2, om, od, sgm, sgd)]
                  + [pl.BlockSpec(w.shape, const) for w in weights]),
        out_specs=[pl.BlockSpec((tm, d), row), pl.BlockSpec((tm, d), row), pl.BlockSpec((tm, LANES), row),
                   pl.BlockSpec((1, 8, tm), lambda i: (i, 0, 0)), pl.BlockSpec((1, 1, LANES), lambda i: (i, 0, 0))],
        out_shape=[jax.ShapeDtypeStruct((n, d), F32), jax.ShapeDtypeStruct((n, d), BF16),
                   jax.ShapeDtypeStruct((n, LANES), F32), jax.ShapeDtypeStruct((n // tm, 8, tm), F32),
                   jax.ShapeDtypeStruct((n // tm, 1, LANES), F32)],
        compiler_params=pltpu.CompilerParams(dimension_semantics=("parallel",), vmem_limit_bytes=VMEM_LIMIT_BYTES),
        name="merge_router",
    )(x2, om, od, sgm, sgd, *weights)


def _segment_copies(i, seg_dst_ref, seg_rows_ref, tile_off_ref, global_ref, tile_ref, sem, to_global):
    def body(e, carry):
        k = i * N_EXPERTS + e
        rows = pl.multiple_of(seg_rows_ref[k], SEG_ALIGN)

        @pl.when(rows > 0)
        def _():
            g = global_ref.at[pl.ds(pl.multiple_of(seg_dst_ref[k], SEG_ALIGN), rows)]
            t = tile_ref.at[pl.ds(pl.multiple_of(tile_off_ref[k], SEG_ALIGN), rows)]
            src, dst = (t, g) if to_global else (g, t)
            pltpu.make_async_copy(src, dst, sem).start()

        return carry

    lax.fori_loop(0, N_EXPERTS, body, 0)


def _wait_rows(tile_ref, rows, sem):
    @pl.when(rows > 0)
    def _():
        view = tile_ref.at[pl.ds(0, pl.multiple_of(rows, SEG_ALIGN))]
        pltpu.make_async_copy(view, view, sem).wait()


def _zero_unused_rows(tail_dst_ref, tail_rows_ref, n_used_ref, xs_ref, zero_ref, sem, start):
    n_tiles = xs_ref.shape[0] // EXPERT_ROWS
    if start:
        zero_ref[...] = jnp.zeros(zero_ref.shape, BF16)

    def tail(e, total):
        rows = pl.multiple_of(tail_rows_ref[e], SEG_ALIGN)
        if start:
            @pl.when(rows > 0)
            def _():
                dst = xs_ref.at[pl.ds(pl.multiple_of(tail_dst_ref[e], SEG_ALIGN), rows)]
                pltpu.make_async_copy(zero_ref.at[pl.ds(0, rows)], dst, sem).start()

        return total + rows

    total = lax.fori_loop(0, N_EXPERTS, tail, 0)
    if not start:
        _wait_rows(xs_ref, total + (n_tiles - n_used_ref[0]) * EXPERT_ROWS, sem)
        return

    def unused(t, carry):
        dst = xs_ref.at[pl.ds(pl.multiple_of(t * EXPERT_ROWS, EXPERT_ROWS), EXPERT_ROWS)]
        pltpu.make_async_copy(zero_ref, dst, sem).start()
        return carry

    lax.fori_loop(n_used_ref[0], n_tiles, unused, 0)


def _sort_kernel(seg_dst_ref, seg_rows_ref, tile_off_ref, tile_rows_ref, tail_dst_ref, tail_rows_ref, n_used_ref,
                 h2_ref, route_t_ref, xs_ref, sorted_ref, zero_ref, sem, zero_sem):
    i = pl.program_id(0)
    tm = h2_ref.shape[0]

    @pl.when(i == 0)
    def _():
        _zero_unused_rows(tail_dst_ref, tail_rows_ref, n_used_ref, xs_ref, zero_ref, zero_sem, True)

    d1 = route_t_ref[0, 0:1, :].astype(jnp.int32)
    d2 = route_t_ref[0, 1:2, :].astype(jnp.int32)
    r = lax.broadcasted_iota(jnp.int32, (SORT_ROWS, tm), 0)
    perm = jnp.where((r == d1) | (r == d2), 1.0, 0.0).astype(BF16)
    slot = i % 2
    sorted_ref[slot] = _dot(perm, h2_ref[...]).astype(BF16)
    _segment_copies(i, seg_dst_ref, seg_rows_ref, tile_off_ref, xs_ref, sorted_ref.at[slot], sem.at[slot], True)

    @pl.when(i > 0)
    def _():
        _wait_rows(sorted_ref.at[1 - slot], tile_rows_ref[jnp.maximum(i - 1, 0)], sem.at[1 - slot])

    @pl.when(i == pl.num_programs(0) - 1)
    def _():
        _wait_rows(sorted_ref.at[slot], tile_rows_ref[i], sem.at[slot])
        _zero_unused_rows(tail_dst_ref, tail_rows_ref, n_used_ref, xs_ref, zero_ref, zero_sem, False)


def _sort_call(h2, route_t, sched, max_rows):
    n, d = h2.shape
    tm = ROUTE_ROWS
    return pl.pallas_call(
        _sort_kernel,
        grid_spec=pltpu.PrefetchScalarGridSpec(
            num_scalar_prefetch=7,
            grid=(n // tm,),
            in_specs=[pl.BlockSpec((tm, d), lambda i, *_: (i, 0)),
                      pl.BlockSpec((1, 8, tm), lambda i, *_: (i, 0, 0))],
            out_specs=pl.BlockSpec(memory_space=pl.ANY),
            scratch_shapes=[pltpu.VMEM((2, SORT_ROWS, d), BF16), pltpu.VMEM((EXPERT_ROWS, d), BF16),
                            pltpu.SemaphoreType.DMA((2,)), pltpu.SemaphoreType.DMA(())],
        ),
        out_shape=jax.ShapeDtypeStruct((max_rows, d), BF16),
        compiler_params=pltpu.CompilerParams(dimension_semantics=("arbitrary",), vmem_limit_bytes=VMEM_LIMIT_BYTES),
        name="moe_sort",
    )(sched["seg_dst"], sched["seg_rows"], sched["tile_off"], sched["tile_rows"], sched["tail_dst"],
      sched["tail_rows"], sched["n_used"], h2, route_t)


def _expert_kernel(tile_expert_ref, n_used_ref, xs_ref, wg_ref, wu_ref, wd_ref, ys_ref, wg_bf, wu_bf, wd_bf):
    t = pl.program_id(0)
    used = t < n_used_ref[0]

    @pl.when(used & ((t == 0) | (tile_expert_ref[t] != tile_expert_ref[jnp.maximum(t - 1, 0)])))
    def _():
        wg_bf[...] = wg_ref[0].astype(BF16)
        wu_bf[...] = wu_ref[0].astype(BF16)
        wd_bf[...] = wd_ref[0].astype(BF16)

    @pl.when(used)
    def _():
        xs = xs_ref[...]
        gate = _dot(xs, wg_bf[...])
        up = _dot(xs, wu_bf[...])
        hidden = (gate * jax.nn.sigmoid(gate) * up).astype(BF16)
        ys_ref[...] = _dot(hidden, wd_bf[...]).astype(BF16)

    @pl.when(jnp.logical_not(used))
    def _():
        ys_ref[...] = jnp.zeros(ys_ref.shape, BF16)


def _expert_call(xs, wg, wu, wd, sched):
    rows, d = xs.shape
    tr = EXPERT_ROWS
    blk = lambda t, te, nu: (jnp.minimum(t, nu[0] - 1), 0)
    wsel = lambda t, te, nu: (te[jnp.minimum(t, nu[0] - 1)], 0, 0)
    return pl.pallas_call(
        _expert_kernel,
        grid_spec=pltpu.PrefetchScalarGridSpec(
            num_scalar_prefetch=2,
            grid=(rows // tr,),
            in_specs=[pl.BlockSpec((tr, d), blk),
                      pl.BlockSpec((1, d, EXPERT_FF), wsel), pl.BlockSpec((1, d, EXPERT_FF), wsel),
                      pl.BlockSpec((1, EXPERT_FF, d), wsel)],
            out_specs=pl.BlockSpec((tr, d), lambda t, te, nu: (t, 0)),
            scratch_shapes=[pltpu.VMEM((d, EXPERT_FF), BF16), pltpu.VMEM((d, EXPERT_FF), BF16),
                            pltpu.VMEM((EXPERT_FF, d), BF16)],
        ),
        out_shape=jax.ShapeDtypeStruct((rows, d), BF16),
        compiler_params=pltpu.CompilerParams(dimension_semantics=("arbitrary",), vmem_limit_bytes=VMEM_LIMIT_BYTES),
        name="moe_experts",
    )(sched["tile_expert"], sched["n_used"], xs, wg, wu, wd)


def _combine_kernel(seg_dst_ref, seg_rows_ref, tile_off_ref, tile_rows_ref, ys_ref, route_ref, x1_ref, o_ref,
                    buf_ref, sem):
    i = pl.program_id(0)
    tm = x1_ref.shape[0]
    slot = i % 2

    def fetch(tile, into):
        buf_ref[into] = jnp.zeros(buf_ref.shape[1:], BF16)
        _segment_copies(tile, seg_dst_ref, seg_rows_ref, tile_off_ref, ys_ref, buf_ref.at[into], sem.at[into], False)

    @pl.when(i == 0)
    def _():
        fetch(i, slot)

    @pl.when(i + 1 < pl.num_programs(0))
    def _():
        fetch(i + 1, 1 - slot)

    route = route_ref[...]
    d1 = route[:, 0:1].astype(jnp.int32)
    d2 = route[:, 1:2].astype(jnp.int32)
    w1 = route[:, 2:3]
    w2 = route[:, 3:4]
    r = lax.broadcasted_iota(jnp.int32, (tm, SORT_ROWS), 1)
    weights = (jnp.where(r == d1, w1, 0.0) + jnp.where(r == d2, w2, 0.0)).astype(BF16)
    _wait_rows(buf_ref.at[slot], tile_rows_ref[i], sem.at[slot])
    o_ref[...] = x1_ref[...] + _dot(weights, buf_ref[slot])


def _combine_call(ys, route, x1, sched):
    n, d = x1.shape
    tm = ROUTE_ROWS
    return pl.pallas_call(
        _combine_kernel,
        grid_spec=pltpu.PrefetchScalarGridSpec(
            num_scalar_prefetch=4,
            grid=(n // tm,),
            in_specs=[pl.BlockSpec(memory_space=pl.ANY),
                      pl.BlockSpec((tm, LANES), lambda i, *_: (i, 0)),
                      pl.BlockSpec((tm, d), lambda i, *_: (i, 0))],
            out_specs=pl.BlockSpec((tm, d), lambda i, *_: (i, 0)),
            scratch_shapes=[pltpu.VMEM((2, SORT_ROWS, d), BF16), pltpu.SemaphoreType.DMA((2,))],
        ),
        out_shape=jax.ShapeDtypeStruct((n, d), F32),
        compiler_params=pltpu.CompilerParams(dimension_semantics=("arbitrary",), vmem_limit_bytes=VMEM_LIMIT_BYTES),
        name="moe_combine",
    )(sched["seg_dst"], sched["seg_rows"], sched["tile_off"], sched["tile_rows"], ys, route, x1)


def _schedule_kernel(cnt_ref, seg_dst_ref, tile_off_ref, tile_rows_ref, misc_ref):
    hp = functools.partial(jnp.dot, preferred_element_type=F32, precision=lax.Precision.HIGHEST)
    cnt = cnt_ref[...]
    n_tiles = cnt.shape[0]
    tile_before = jnp.where(lax.broadcasted_iota(jnp.int32, (n_tiles, n_tiles), 1)
                            < lax.broadcasted_iota(jnp.int32, (n_tiles, n_tiles), 0), 1.0, 0.0)
    expert_before = jnp.where(lax.broadcasted_iota(jnp.int32, (LANES, LANES), 0)
                              < lax.broadcasted_iota(jnp.int32, (LANES, LANES), 1), 1.0, 0.0)
    expert_rows = jnp.sum(cnt, axis=0, keepdims=True)
    region = jnp.floor((expert_rows + (EXPERT_ROWS - 1)) * (1.0 / EXPERT_ROWS)) * EXPERT_ROWS
    region_start = hp(jnp.broadcast_to(region, (8, LANES)), expert_before)[0:1]
    seg_dst_ref[...] = (region_start + hp(tile_before, cnt)).astype(jnp.int32)
    tile_off_ref[...] = hp(cnt, expert_before).astype(jnp.int32)
    tile_rows_ref[...] = jnp.broadcast_to(jnp.sum(cnt, axis=-1, keepdims=True), cnt.shape).astype(jnp.int32)
    n_used = jnp.sum(region, axis=-1, keepdims=True) * (1.0 / EXPERT_ROWS)
    row = lax.broadcasted_iota(jnp.int32, (8, LANES), 0)
    misc = jnp.where(row == 0, region_start + expert_rows,
                     jnp.where(row == 1, region - expert_rows,
                               jnp.where(row == 2, region_start + region, n_used)))
    misc_ref[...] = misc.astype(jnp.int32)


def _moe_schedule(cnt, n_tokens):
    n_tiles = cnt.shape[0]
    table = jax.ShapeDtypeStruct((n_tiles, LANES), jnp.int32)
    seg_dst, tile_off, tile_rows, misc = pl.pallas_call(
        _schedule_kernel,
        out_shape=[table, table, table, jax.ShapeDtypeStruct((8, LANES), jnp.int32)],
        name="moe_schedule",
    )(cnt.reshape(n_tiles, LANES))
    max_rows = 2 * n_tokens + n_tiles * N_EXPERTS * (SEG_ALIGN - 1) + N_EXPERTS * (EXPERT_ROWS - 1)
    max_tiles = -(-max_rows // EXPERT_ROWS)
    tile_start = jnp.arange(max_tiles, dtype=jnp.int32) * EXPERT_ROWS
    region_end = misc[2, :N_EXPERTS]
    tile_expert = jnp.minimum(jnp.sum((region_end[None, :] <= tile_start[:, None]).astype(jnp.int32), axis=1),
                              N_EXPERTS - 1)
    flat = lambda a: a[:, :N_EXPERTS].reshape(-1)
    sched = {
        "seg_dst": flat(seg_dst),
        "seg_rows": flat(cnt.reshape(n_tiles, LANES).astype(jnp.int32)),
        "tile_off": flat(tile_off),
        "tile_rows": tile_rows[:, 0],
        "tail_dst": misc[0, :N_EXPERTS],
        "tail_rows": misc[1, :N_EXPERTS],
        "tile_expert": tile_expert,
        "n_used": misc[3, :1],
    }
    return sched, max_tiles * EXPERT_ROWS


def _rotary_tables(seq_len, rot_dim, period, first, gain, scale):
    half = rot_dim // 2
    pos = jnp.arange(seq_len, dtype=F32)
    inv = 1.0 / (ROPE_THETA ** (jnp.arange(0, rot_dim, 2, dtype=F32) / rot_dim))
    ang = pos[:, None] * inv[None, :]
    cos, sin = jnp.cos(ang), jnp.sin(ang)
    lane = jnp.arange(LANES)
    rel = (lane % period) - first
    active = (rel >= 0) & (rel < rot_dim)
    idx = jnp.clip(rel, 0, rot_dim - 1) % half
    sign = jnp.where(rel < half, -1.0, 1.0)
    partner = jnp.where(active, jnp.where(rel < half, lane + half, lane - half), lane)
    c = jnp.where(active[None, :], cos[:, idx], 1.0)
    s = jnp.where(active[None, :], sin[:, idx] * sign[None, :], 0.0)
    gain = gain.astype(F32)
    return (c * gain[None, :] * scale).astype(F32), (s * gain[partner][None, :] * scale).astype(F32)


def _head_pad(w, heads, width):
    r = w.shape[0]
    w = w.reshape(r, heads, width)
    return jnp.pad(w, ((0, 0), (0, 0), (0, LANES - width))).reshape(r, heads * LANES)


def _layer_params(l, seq_len, norm_mix, w_in, mla_q_latent_norm, w_mla_uq, mla_kv_latent_norm, w_mla_ukv,
                  mla_q_gain, mla_k_gain, diff_q_gain, diff_k_gain, w_mla_up, w_diff_up, w_out, norm_ffn,
                  w_router_group, b_router_group, w_router_expert, b_router_expert):
    d = w_in.shape[1]
    sizes = (MLA_Q_RANK, MLA_KV_RANK, MLA_ROPE, DIFF_QK_WIDTH, DIFF_QK_WIDTH, DIFF_V_WIDTH, d, d)
    offs = [0]
    for s in sizes:
        offs.append(offs[-1] + s)
    wi = w_in[l]
    seg = [wi[:, offs[k]:offs[k + 1]] for k in range(len(sizes))]
    row = lambda g: g.astype(F32)[None, :]
    p = {}
    p["gmix"] = row(norm_mix[l])
    p["wql"] = seg[0].astype(BF16)
    p["wkvl"] = seg[1].astype(BF16)
    p["wkr"] = jnp.pad(seg[2], ((0, 0), (MLA_NOPE, LANES - MLA_QK))).astype(BF16)
    p["wdk"] = seg[4].astype(BF16)
    p["wdqvt"] = jnp.concatenate([seg[3].T, seg[5].T], axis=0).astype(BF16)
    p["wgm"], p["wgd"] = seg[6].astype(BF16), seg[7].astype(BF16)
    p["gql"] = row(mla_q_latent_norm[l])
    p["wuqt"] = _head_pad(w_mla_uq[l], MLA_HEADS, MLA_QK).T.astype(BF16)
    p["gkvl"] = row(mla_kv_latent_norm[l])
    ukv = w_mla_ukv[l].reshape(MLA_KV_RANK, MLA_HEADS, MLA_NOPE + MLA_V)
    p["wuk"] = _head_pad(ukv[:, :, :MLA_NOPE].reshape(MLA_KV_RANK, -1), MLA_HEADS, MLA_NOPE).astype(BF16)
    p["wuvt"] = ukv[:, :, MLA_NOPE:].reshape(MLA_KV_RANK, -1).T.astype(BF16)
    gq = jnp.pad(mla_q_gain[l], (0, LANES - MLA_QK))
    gk = jnp.pad(mla_k_gain[l], (0, LANES - MLA_QK))
    nope = jnp.arange(LANES) < MLA_NOPE
    p["gkn"] = jnp.where(nope, gk, 0.0).astype(F32)[None, :]
    aq, bq = _rotary_tables(seq_len, MLA_ROPE, LANES, MLA_NOPE, gq, LOG2E * MLA_QK ** -0.5)
    p["aq"], p["bq"] = aq.T, bq.T
    ak, bk = _rotary_tables(seq_len, MLA_ROPE, LANES, MLA_NOPE, jnp.where(nope, 0.0, gk), 1.0)
    p["ak"], p["bk"] = ak, bk
    adq, bdq = _rotary_tables(seq_len, DIFF_ROPE, DIFF_HEAD_DIM, 0, jnp.tile(diff_q_gain[l], 2),
                              LOG2E * DIFF_HEAD_DIM ** -0.5)
    p["adq"], p["bdq"] = adq.T, bdq.T
    p["adk"], p["bdk"] = _rotary_tables(seq_len, DIFF_ROPE, DIFF_HEAD_DIM, 0, jnp.tile(diff_k_gain[l], 2), 1.0)
    p["wmu"] = w_mla_up[l].astype(BF16)
    p["wdu"] = w_diff_up[l].astype(BF16)
    p["wout"] = w_out[l].astype(BF16)
    p["gffn"] = row(norm_ffn[l])
    wr = jnp.concatenate([w_router_expert[l], w_router_group[l]], axis=1).astype(F32)
    wrt = jnp.pad(wr, ((0, 0), (0, LANES - wr.shape[1]))).T
    wrt_hi = wrt.astype(BF16)
    p["wrt"] = jnp.concatenate([wrt_hi, (wrt - wrt_hi.astype(F32)).astype(BF16)], axis=0)
    br = jnp.concatenate([b_router_expert[l], b_router_group[l]]).astype(F32)
    p["brt"] = jnp.broadcast_to(jnp.pad(br, (0, LANES - br.shape[0]))[:, None], (LANES, MERGE_ROWS))
    return p


def kernel(x, norm_mix, w_in, mla_q_latent_norm, w_mla_uq, mla_kv_latent_norm, w_mla_ukv, mla_q_gain, mla_k_gain, diff_q_gain, diff_k_gain, lambda_q1, lambda_k1, lambda_q2, lambda_k2, diff_subln, w_mla_up, w_diff_up, w_out, norm_ffn, w_router_group, b_router_group, w_router_expert, b_router_expert, w_expert_gate, w_expert_up, w_expert_down):
    batch, seq_len, d = x.shape
    x2 = x.reshape(batch * seq_len, d)
    row = lambda g: g.astype(F32)[None, :]
    for l in range(norm_mix.shape[0]):
        lam_init = 0.8 - 0.6 * math.exp(-0.3 * l)
        p = _layer_params(l, seq_len, norm_mix, w_in, mla_q_latent_norm, w_mla_uq, mla_kv_latent_norm, w_mla_ukv,
                          mla_q_gain, mla_k_gain, diff_q_gain, diff_k_gain, w_mla_up, w_diff_up, w_out, norm_ffn,
                          w_router_group, b_router_group, w_router_expert, b_router_expert)
        qmt, km, vtm, qdt, kd, vtd, sgm, sgd = _proj_call(x2, seq_len, p)
        om = _mla_call(qmt, km, vtm, batch, seq_len)
        od = _diff_call(qdt, kd, vtd, row(lambda_q1[l]), row(lambda_k1[l]), row(lambda_q2[l]), row(lambda_k2[l]),
                        diff_subln[l].astype(F32)[:, None], lam_init, batch, seq_len)
        x1, h2, route, route_t, cnt = _merge_call(x2, om, od, sgm, sgd, p)
        sched, max_rows = _moe_schedule(cnt, x2.shape[0])
        xs = _sort_call(h2, route_t, sched, max_rows)
        ys = _expert_call(xs, w_expert_gate[l], w_expert_up[l], w_expert_down[l], sched)
        x2 = _combine_call(ys, route, x1, sched)
    return x2.reshape(batch, seq_len, d)
```

```python
import functools
import math

import jax
import jax.numpy as jnp
from jax import lax
from jax.experimental import pallas as pl
from jax.experimental.pallas import tpu as pltpu

CHUNK = 64
ROPE_THETA = 500000.0
EPS = 1e-6

MLA_HEADS = 8
MLA_NOPE = 64
MLA_ROPE = 32
MLA_V = 64
MLA_QK = MLA_NOPE + MLA_ROPE
MLA_Q_RANK = 256
MLA_KV_RANK = 128

DIFF_HEADS = 4
DIFF_HEAD_DIM = 64
DIFF_V_DIM = 2 * DIFF_HEAD_DIM
DIFF_ROPE = DIFF_HEAD_DIM // 4
DIFF_QK_WIDTH = DIFF_HEADS * 2 * DIFF_HEAD_DIM
DIFF_V_WIDTH = DIFF_HEADS * DIFF_V_DIM

N_GROUPS = 4
EXPERTS_PER_GROUP = 8
N_EXPERTS = N_GROUPS * EXPERTS_PER_GROUP
EXPERT_FF = 256

LANES = 128
VMEM_LIMIT_BYTES = 48 * 1024 * 1024

PROJ_ROWS = 512
ATTN_Q_ROWS = 512
ATTN_K_ROWS = 256
MERGE_ROWS = 512
ROUTE_ROWS = MERGE_ROWS
SEG_ALIGN = 16
SORT_ROWS = 2 * ROUTE_ROWS + N_EXPERTS * SEG_ALIGN
SORT_ROWS_COMMON = 2 * ROUTE_ROWS + N_EXPERTS * SEG_ALIGN // 2
EXPERT_ROWS = 512
MLA_HEADS_PER_STEP = 4
DIFF_HEADS_PER_STEP = 2
LOG2E = 1.4426950408889634

BF16 = jnp.bfloat16
F32 = jnp.float32


def _dot(a, b):
    return jnp.dot(a, b, preferred_element_type=F32)


def _dot_nt(a, b):
    return lax.dot_general(a, b, (((1,), (1,)), ((), ())), preferred_element_type=F32)


def _rms(x, width):
    return x * lax.rsqrt(jnp.sum(x * x, axis=-1, keepdims=True) * (1.0 / width) + EPS)


def _rotary_partner(y, half):
    lane = lax.broadcasted_iota(jnp.int32, y.shape, 1)
    up = pltpu.roll(y, LANES - half, 1)
    down = pltpu.roll(y, half, 1)
    return jnp.where((lane // half) % 2 == 0, up, down)


def _swap_row_blocks(y, first, half, period):
    parts = []
    for base in range(0, y.shape[0], period):
        a = base + first
        parts += [y[base:a], y[a + half:a + 2 * half], y[a:a + half], y[a + 2 * half:base + period]]
    return jnp.concatenate([p for p in parts if p.shape[0]], axis=0)


def _store_k_tiles(o_ref, vt):
    tk = o_ref.shape[-1]
    for c in range(o_ref.shape[0]):
        o_ref[c] = vt[:, c * tk:(c + 1) * tk].astype(BF16)


PROJ_COLS = {}
_col = 0
for _name, _width in (("ql", MLA_Q_RANK), ("kvl", MLA_KV_RANK), ("kr", LANES), ("dk", DIFF_QK_WIDTH),
                      ("gm", 1024), ("gd", 1024)):
    PROJ_COLS[_name] = slice(_col, _col + _width)
    _col += _width
PROJ_WIDTH = _col
GAIN_ROWS = {"mix": 0, "ql": 1, "kvl": 2, "kn": 3}


def _proj_kernel(x_ref, gains_ref, w_ref, wdqvt_ref, wuqt_ref, wuk_ref, wuvt_ref, fm_ref, tm_ref,
                 qmt_ref, km_ref, vtm_ref, qdt_ref, kd_ref, vtd_ref, sgm_ref, sgd_ref):
    gain = lambda name, width: gains_ref[GAIN_ROWS[name]:GAIN_ROWS[name] + 1, 0:width]
    weight = lambda name: w_ref[:, PROJ_COLS[name]]
    fm_table = lambda k: fm_ref[k * LANES:(k + 1) * LANES, :]
    tm_table = lambda k: tm_ref[:, k * LANES:(k + 1) * LANES]
    x = x_ref[...]
    h = (_rms(x, x.shape[-1]) * gain("mix", x.shape[-1])).astype(BF16)

    ql = (_rms(_dot(h, weight("ql")), MLA_Q_RANK) * gain("ql", MLA_Q_RANK)).astype(BF16)
    qt = _dot_nt(wuqt_ref[...], ql)
    aq, bq = fm_table(0), fm_table(1)
    for hd in range(MLA_HEADS):
        rows = slice(hd * LANES, (hd + 1) * LANES)
        qh = qt[rows]
        r = lax.rsqrt(jnp.sum(qh * qh, axis=0, keepdims=True) * (1.0 / MLA_QK) + EPS)
        y = (qh * aq + _swap_row_blocks(qh, MLA_NOPE, MLA_ROPE // 2, LANES) * bq) * r
        qmt_ref[rows, :] = y.astype(BF16)

    kvl = (_rms(_dot(h, weight("kvl")), MLA_KV_RANK) * gain("kvl", MLA_KV_RANK)).astype(BF16)
    kr = _dot(h, weight("kr"))
    kr_rot = kr * tm_table(0) + _rotary_partner(kr, MLA_ROPE // 2) * tm_table(1)
    kr_ss = jnp.sum(kr * kr, axis=-1, keepdims=True)
    kn = _dot(kvl, wuk_ref[...])
    _store_k_tiles(vtm_ref, _dot_nt(wuvt_ref[...], kvl))
    gkn = gain("kn", LANES)
    for hd in range(MLA_HEADS):
        sl = slice(hd * LANES, (hd + 1) * LANES)
        knh = kn[:, sl]
        r = lax.rsqrt((jnp.sum(knh * knh, axis=-1, keepdims=True) + kr_ss) * (1.0 / MLA_QK) + EPS)
        km_ref[:, sl] = ((knh * gkn + kr_rot) * r).astype(BF16)

    qvt = _dot_nt(wdqvt_ref[...], h)
    _store_k_tiles(vtd_ref, qvt[DIFF_QK_WIDTH:])
    adq, bdq = fm_table(2), fm_table(3)
    for hd in range(DIFF_HEADS):
        rows = slice(hd * LANES, (hd + 1) * LANES)
        qh = qvt[rows]
        t = qh * adq + _swap_row_blocks(qh, 0, DIFF_ROPE // 2, DIFF_HEAD_DIM) * bdq
        halves = []
        for f in range(2):
            part = qh[f * DIFF_HEAD_DIM:(f + 1) * DIFF_HEAD_DIM]
            r = lax.rsqrt(jnp.sum(part * part, axis=0, keepdims=True) * (1.0 / DIFF_HEAD_DIM) + EPS)
            halves.append(t[f * DIFF_HEAD_DIM:(f + 1) * DIFF_HEAD_DIM] * r)
        qdt_ref[rows, :] = jnp.concatenate(halves, axis=0).astype(BF16)

    kd = _dot(h, weight("dk"))
    adk, bdk = tm_table(2), tm_table(3)
    for hd in range(DIFF_HEADS):
        sl = slice(hd * LANES, (hd + 1) * LANES)
        th = kd[:, sl]
        lane = lax.broadcasted_iota(jnp.int32, th.shape, 1)
        sq = th * th
        lo = jnp.sum(jnp.where(lane < DIFF_HEAD_DIM, sq, 0.0), axis=-1, keepdims=True)
        tot = jnp.sum(sq, axis=-1, keepdims=True)
        r = lax.rsqrt(jnp.where(lane < DIFF_HEAD_DIM, lo, tot - lo) * (1.0 / DIFF_HEAD_DIM) + EPS)
        kd_ref[:, sl] = ((th * adk + _rotary_partner(th, DIFF_ROPE // 2) * bdk) * r).astype(BF16)

    sgm_ref[...] = jax.nn.sigmoid(_dot(h, weight("gm"))).astype(BF16)
    sgd_ref[...] = jax.nn.sigmoid(_dot(h, weight("gd"))).astype(BF16)


def _proj_call(x2, seq_len, p):
    n, d = x2.shape
    tm = PROJ_ROWS
    pos_blocks = seq_len // tm
    row = lambda i: (i, 0)
    col = lambda i: (0, i)
    const = lambda i: (0, 0)
    weights = [p["proj_gains"], p["proj_w"], p["wdqvt"], p["wuqt"], p["wuk"], p["wuvt"]]
    in_specs = ([pl.BlockSpec((tm, d), row)]
                + [pl.BlockSpec(w.shape, const) for w in weights]
                + [pl.BlockSpec((4 * LANES, tm), lambda i: (0, i % pos_blocks)),
                   pl.BlockSpec((tm, 4 * LANES), lambda i: (i % pos_blocks, 0))])
    tk = ATTN_K_ROWS
    k_tiles = lambda width: (pl.BlockSpec((tm // tk, width, tk), lambda i: (i, 0, 0)),
                             jax.ShapeDtypeStruct((n // tk, width, tk), BF16))
    token_major = lambda width: (pl.BlockSpec((tm, width), row), jax.ShapeDtypeStruct((n, width), BF16))
    feature_major = lambda width: (pl.BlockSpec((width, tm), col), jax.ShapeDtypeStruct((width, n), BF16))
    outs = [feature_major(MLA_HEADS * LANES), token_major(MLA_HEADS * LANES), k_tiles(MLA_HEADS * MLA_V),
            feature_major(DIFF_QK_WIDTH), token_major(DIFF_QK_WIDTH), k_tiles(DIFF_V_WIDTH),
            token_major(d), token_major(d)]
    return pl.pallas_call(
        _proj_kernel,
        grid=(n // tm,),
        in_specs=in_specs,
        out_specs=[o[0] for o in outs],
        out_shape=[o[1] for o in outs],
        compiler_params=pltpu.CompilerParams(dimension_semantics=("parallel",), vmem_limit_bytes=VMEM_LIMIT_BYTES),
        name="proj",
    )(x2, *weights, p["tables_fm"], p["tables_tm"])


def _chunk_mask_t(tk, tq, diag):
    kc = lax.broadcasted_iota(jnp.int32, (tk, tq), 0) // CHUNK + diag * (tk // CHUNK)
    qc = lax.broadcasted_iota(jnp.int32, (tk, tq), 1) // CHUNK
    return kc <= qc


ONES_ROWS = 16


def _with_ones_rows(vt):
    return jnp.concatenate([vt, jnp.ones((ONES_ROWS, vt.shape[1]), vt.dtype)], axis=0)


def _softmax_step_t(st, vt_ones, m_ref, acc_ref):
    m_prev = m_ref[...]
    m_new = jnp.maximum(m_prev, jnp.max(st, axis=0, keepdims=True))
    alpha = jnp.exp2(m_prev - m_new)
    pr = jnp.exp2(st - m_new)
    acc_ref[...] = alpha * acc_ref[...] + _dot(vt_ones, pr.astype(BF16))
    m_ref[...] = m_new


def _normalized(acc_ref, dv):
    acc = acc_ref[...]
    return acc[:dv] / acc[dv:dv + 1]


STATE_REFS = 4


def _attn_scratch(chains, dv, tq, tk):
    per_chain = [pltpu.VMEM((1, tq), F32), pltpu.VMEM((dv + ONES_ROWS, tq), F32),
                 pltpu.VMEM((tk, tq), F32), pltpu.VMEM((tk, tq), F32)]
    return per_chain * chains


def _flash_pipeline(q_tile, scratch_refs, score_fn, value_fn, tk, tq):
    assert tq == 2 * tk
    n_chains = len(scratch_refs) // STATE_REFS
    chains = [scratch_refs[STATE_REFS * c:STATE_REFS * (c + 1)] for c in range(n_chains)]
    for m_ref, acc_ref, _, _ in chains:
        m_ref[...] = jnp.full(m_ref.shape, -jnp.inf, F32)
        acc_ref[...] = jnp.zeros(acc_ref.shape, F32)

    def scores(t, slot):
        for c, ch in enumerate(chains):
            ch[2 + slot][...] = score_fn(c, t)

    def update(t, slot, diag=None):
        for c, ch in enumerate(chains):
            st = ch[2 + slot][...]
            if diag is not None:
                st = jnp.where(_chunk_mask_t(tk, tq, diag), st, -jnp.inf)
            _softmax_step_t(st, _with_ones_rows(value_fn(c, t)), ch[0], ch[1])

    scores(0, 0)

    def pair(p, carry):
        t = 2 * p
        scores(t + 1, 1)
        update(t, 0)
        scores(t + 2, 0)
        update(t + 1, 1)
        return carry

    lax.fori_loop(0, q_tile, pair, 0)
    first_diag = 2 * q_tile
    scores(first_diag + 1, 1)
    update(first_diag, 0, diag=0)
    update(first_diag + 1, 1, diag=1)
    return [ch[1] for ch in chains]


def _mla_kernel(qt_ref, k_ref, vt_ref, o_ref, *scratch_refs):
    tq, tk = ATTN_Q_ROWS, ATTN_K_ROWS

    def score_fn(c, t):
        rows = pl.ds(pl.multiple_of(t * tk, tk), tk)
        sl = slice(c * LANES, (c + 1) * LANES)
        return _dot(k_ref[rows, sl], qt_ref[sl, :])

    def value_fn(c, t):
        return vt_ref[t, c * MLA_V:(c + 1) * MLA_V, :]

    out = _flash_pipeline(pl.program_id(2), scratch_refs, score_fn, value_fn, tk, tq)
    ot = jnp.concatenate([_normalized(acc_ref, MLA_V) for acc_ref in out], axis=0)
    o_ref[...] = ot.T.astype(BF16)


def _mla_call(qmt, km, vtm, batch, seq_len):
    n = km.shape[0]
    tq, tk, hps = ATTN_Q_ROWS, ATTN_K_ROWS, MLA_HEADS_PER_STEP
    qt = seq_len // tq
    return pl.pallas_call(
        _mla_kernel,
        grid=(batch, MLA_HEADS // hps, qt),
        in_specs=[pl.BlockSpec((hps * LANES, tq), lambda b, h, i: (h, b * qt + i)),
                  pl.BlockSpec((seq_len, hps * LANES), lambda b, h, i: (b, h)),
                  pl.BlockSpec((seq_len // tk, hps * MLA_V, tk), lambda b, h, i: (b, h, 0))],
        out_specs=pl.BlockSpec((tq, hps * MLA_V), lambda b, h, i: (b * qt + i, h)),
        out_shape=jax.ShapeDtypeStruct((n, MLA_HEADS * MLA_V), BF16),
        scratch_shapes=_attn_scratch(hps, MLA_V, tq, tk),
        compiler_params=pltpu.CompilerParams(dimension_semantics=("parallel", "parallel", "arbitrary"),
                                             vmem_limit_bytes=VMEM_LIMIT_BYTES),
        name="mla_attn",
    )(qmt, km, vtm)


def _diff_kernel(lam_init, qt_ref, k_ref, vt_ref, lq1_ref, lk1_ref, lq2_ref, lk2_ref, subln_ref, o_ref,
                 *scratch_refs):
    tq, tk = ATTN_Q_ROWS, ATTN_K_ROWS
    hps = DIFF_HEADS_PER_STEP

    qs = []
    for hd in range(hps):
        qh = qt_ref[hd * LANES:(hd + 1) * LANES, :]
        zero = jnp.zeros((DIFF_HEAD_DIM, tq), BF16)
        qs += [jnp.concatenate([qh[:DIFF_HEAD_DIM], zero], axis=0), jnp.concatenate([zero, qh[DIFF_HEAD_DIM:]], axis=0)]

    def score_fn(c, t):
        rows = pl.ds(pl.multiple_of(t * tk, tk), tk)
        hd = c // 2
        return _dot(k_ref[rows, hd * LANES:(hd + 1) * LANES], qs[c])

    def value_fn(c, t):
        hd = c // 2
        return vt_ref[t, hd * DIFF_V_DIM:(hd + 1) * DIFF_V_DIM, :]

    out = _flash_pipeline(pl.program_id(2), scratch_refs, score_fn, value_fn, tk, tq)

    lam = (jnp.exp(jnp.sum(lq1_ref[...] * lk1_ref[...], axis=-1, keepdims=True))
           - jnp.exp(jnp.sum(lq2_ref[...] * lk2_ref[...], axis=-1, keepdims=True)) + lam_init)
    heads = []
    for hd in range(hps):
        ot = _normalized(out[2 * hd], DIFF_V_DIM) - lam * _normalized(out[2 * hd + 1], DIFF_V_DIM)
        ot = ot * lax.rsqrt(jnp.sum(ot * ot, axis=0, keepdims=True) * (1.0 / DIFF_V_DIM) + EPS)
        heads.append(ot * subln_ref[...] * (1.0 - lam_init))
    o_ref[...] = jnp.concatenate(heads, axis=0).T.astype(BF16)


def _diff_call(qdt, kd, vtd, lq1, lk1, lq2, lk2, subln_col, lam_init, batch, seq_len):
    n = kd.shape[0]
    tq, tk, hps = ATTN_Q_ROWS, ATTN_K_ROWS, DIFF_HEADS_PER_STEP
    qt = seq_len // tq
    small = lambda a: pl.BlockSpec(a.shape, lambda b, h, i: (0, 0))
    return pl.pallas_call(
        functools.partial(_diff_kernel, lam_init),
        grid=(batch, DIFF_HEADS // hps, qt),
        in_specs=[pl.BlockSpec((hps * LANES, tq), lambda b, h, i: (h, b * qt + i)),
                  pl.BlockSpec((seq_len, hps * LANES), lambda b, h, i: (b, h)),
                  pl.BlockSpec((seq_len // tk, hps * DIFF_V_DIM, tk), lambda b, h, i: (b, h, 0)),
                  small(lq1), small(lk1), small(lq2), small(lk2), small(subln_col)],
        out_specs=pl.BlockSpec((tq, hps * LANES), lambda b, h, i: (b * qt + i, h)),
        out_shape=jax.ShapeDtypeStruct((n, DIFF_V_WIDTH), BF16),
        scratch_shapes=_attn_scratch(2 * hps, DIFF_V_DIM, tq, tk),
        compiler_params=pltpu.CompilerParams(dimension_semantics=("parallel", "parallel", "arbitrary"),
                                             vmem_limit_bytes=VMEM_LIMIT_BYTES),
        name="diff_attn",
    )(qdt, kd, vtd, lq1, lk1, lq2, lk2, subln_col)


def _merge_kernel(x_ref, om_ref, od_ref, sgm_ref, sgd_ref, w_ref, gffn_ref, wrt_ref,
                  brt_ref, x1_ref, h2_ref, route_ref, route_t_ref, cnt_ref):
    up_m, up_d = MLA_HEADS * MLA_V, DIFF_V_WIDTH
    merged = (sgm_ref[...].astype(F32) * _dot(om_ref[...], w_ref[0:up_m, :])
              + sgd_ref[...].astype(F32) * _dot(od_ref[...], w_ref[up_m:up_m + up_d, :]))
    x1 = x_ref[...] + _dot(merged.astype(BF16), w_ref[up_m + up_d:, :])
    x1_ref[...] = x1
    h2 = _rms(x1, x1.shape[-1]) * gffn_ref[...]
    h2_hi = h2.astype(BF16)
    h2_ref[...] = h2_hi
    tm = h2.shape[0]

    h2_lo = (h2 - h2_hi.astype(F32)).astype(BF16)
    by_hi = _dot_nt(wrt_ref[...], h2_hi)
    logits = by_hi[:LANES] + by_hi[LANES:] + _dot_nt(wrt_ref[:LANES, :], h2_lo) + brt_ref[...]
    row = lax.broadcasted_iota(jnp.int32, logits.shape, 0)
    neg = -jnp.inf
    big = jnp.int32(1 << 20)

    def top(vals):
        mx = jnp.max(vals, axis=0, keepdims=True)
        idx = jnp.min(jnp.where(vals == mx, row, big), axis=0, keepdims=True)
        return mx, idx

    gl = jnp.where((row >= N_EXPERTS) & (row < N_EXPERTS + N_GROUPS), logits, neg)
    gmax, gidx = top(gl)
    pg_sel = 1.0 / jnp.sum(jnp.exp(gl - gmax), axis=0, keepdims=True)
    el = jnp.where((row < N_EXPERTS) & (row // EXPERTS_PER_GROUP == gidx - N_EXPERTS), logits, neg)
    m1, i1 = top(el)
    m2, i2 = top(jnp.where(row == i1, neg, el))
    e2 = jnp.exp(m2 - m1)
    w1 = pg_sel / (1.0 + e2)
    w2 = w1 * e2

    sel = jnp.where((row == i1) | (row == i2), 1.0, 0.0).astype(BF16)
    t_row = lax.broadcasted_iota(jnp.int32, (tm, tm), 0)
    t_col = lax.broadcasted_iota(jnp.int32, (tm, tm), 1)
    rank = _dot(sel, jnp.where(t_row < t_col, 1.0, 0.0).astype(BF16))
    cnt = _dot(sel, jnp.ones((tm, tm), BF16))
    seg = jnp.floor((cnt + (SEG_ALIGN - 1)) * (1.0 / SEG_ALIGN))
    e_row = lax.broadcasted_iota(jnp.int32, (LANES, LANES), 0)
    e_col = lax.broadcasted_iota(jnp.int32, (LANES, LANES), 1)
    off = _dot(jnp.where(e_col < e_row, 1.0, 0.0).astype(BF16), seg.astype(BF16)) * SEG_ALIGN
    dest = off + rank
    d1 = jnp.sum(jnp.where(row == i1, dest, 0.0), axis=0, keepdims=True)
    d2 = jnp.sum(jnp.where(row == i2, dest, 0.0), axis=0, keepdims=True)
    route_t = jnp.where(row == 0, d1, jnp.where(row == 1, d2, jnp.where(row == 2, w1, jnp.where(row == 3, w2, 0.0))))
    route_t_ref[0] = route_t[0:8]
    route_ref[...] = route_t.T
    cnt_ref[0] = (seg[:, :LANES] * SEG_ALIGN).T[0:1]


def _merge_call(x2, om, od, sgm, sgd, p):
    n, d = x2.shape
    tm = MERGE_ROWS
    row = lambda i: (i, 0)
    const = lambda i: (0, 0)
    weights = [p["merge_w"], p["gffn"], p["wrt"], p["brt"]]
    return pl.pallas_call(
        _merge_kernel,
        grid=(n // tm,),
        in_specs=([pl.BlockSpec((tm, a.shape[1]), row) for a in (x2, om, od, sgm, sgd)]
                  + [pl.BlockSpec(w.shape, const) for w in weights]),
        out_specs=[pl.BlockSpec((tm, d), row), pl.BlockSpec((tm, d), row), pl.BlockSpec((tm, LANES), row),
                   pl.BlockSpec((1, 8, tm), lambda i: (i, 0, 0)), pl.BlockSpec((1, 1, LANES), lambda i: (i, 0, 0))],
        out_shape=[jax.ShapeDtypeStruct((n, d), F32), jax.ShapeDtypeStruct((n, d), BF16),
                   jax.ShapeDtypeStruct((n, LANES), F32), jax.ShapeDtypeStruct((n // tm, 8, tm), F32),
                   jax.ShapeDtypeStruct((n // tm, 1, LANES), F32)],
        compiler_params=pltpu.CompilerParams(dimension_semantics=("parallel",), vmem_limit_bytes=VMEM_LIMIT_BYTES),
        name="merge_router",
    )(x2, om, od, sgm, sgd, *weights)


def _segment_copies(i, seg_dst_ref, seg_rows_ref, tile_off_ref, global_ref, tile_ref, sem, to_global):
    def body(e, carry):
        k = i * N_EXPERTS + e
        rows = pl.multiple_of(seg_rows_ref[k], SEG_ALIGN)

        @pl.when(rows > 0)
        def _():
            g = global_ref.at[pl.ds(pl.multiple_of(seg_dst_ref[k], SEG_ALIGN), rows)]
            t = tile_ref.at[pl.ds(pl.multiple_of(tile_off_ref[k], SEG_ALIGN), rows)]
            src, dst = (t, g) if to_global else (g, t)
            pltpu.make_async_copy(src, dst, sem).start()

        return carry

    lax.fori_loop(0, N_EXPERTS, body, 0)


def _wait_rows(tile_ref, rows, sem):
    @pl.when(rows > 0)
    def _():
        view = tile_ref.at[pl.ds(0, pl.multiple_of(rows, SEG_ALIGN))]
        pltpu.make_async_copy(view, view, sem).wait()


def _zero_unused_rows(tail_dst_ref, tail_rows_ref, n_used_ref, xs_ref, zero_ref, sem, start):
    n_tiles = xs_ref.shape[0] // EXPERT_ROWS
    if start:
        zero_ref[...] = jnp.zeros(zero_ref.shape, BF16)

    def tail(e, total):
        rows = pl.multiple_of(tail_rows_ref[e], SEG_ALIGN)
        if start:
            @pl.when(rows > 0)
            def _():
                dst = xs_ref.at[pl.ds(pl.multiple_of(tail_dst_ref[e], SEG_ALIGN), rows)]
                pltpu.make_async_copy(zero_ref.at[pl.ds(0, rows)], dst, sem).start()

        return total + rows

    total = lax.fori_loop(0, N_EXPERTS, tail, 0)
    if not start:
        _wait_rows(xs_ref, total + (n_tiles - n_used_ref[0]) * EXPERT_ROWS, sem)
        return

    def unused(t, carry):
        dst = xs_ref.at[pl.ds(pl.multiple_of(t * EXPERT_ROWS, EXPERT_ROWS), EXPERT_ROWS)]
        pltpu.make_async_copy(zero_ref, dst, sem).start()
        return carry

    lax.fori_loop(n_used_ref[0], n_tiles, unused, 0)


def _sort_kernel(seg_dst_ref, seg_rows_ref, tile_off_ref, tile_rows_ref, tail_dst_ref, tail_rows_ref, n_used_ref,
                 h2_ref, route_t_ref, xs_ref, sorted_ref, zero_ref, sem, zero_sem):
    i = pl.program_id(0)
    tm = h2_ref.shape[0]

    @pl.when(i == 0)
    def _():
        _zero_unused_rows(tail_dst_ref, tail_rows_ref, n_used_ref, xs_ref, zero_ref, zero_sem, True)

    d1 = route_t_ref[0, 0:1, :].astype(jnp.int32)
    d2 = route_t_ref[0, 1:2, :].astype(jnp.int32)
    slot = i % 2

    def sort_rows(n_rows):
        r = lax.broadcasted_iota(jnp.int32, (n_rows, tm), 0)
        perm = jnp.where((r == d1) | (r == d2), 1.0, 0.0).astype(BF16)
        sorted_ref[slot, 0:n_rows] = _dot(perm, h2_ref[...]).astype(BF16)

    @pl.when(tile_rows_ref[i] <= SORT_ROWS_COMMON)
    def _():
        sort_rows(SORT_ROWS_COMMON)

    @pl.when(tile_rows_ref[i] > SORT_ROWS_COMMON)
    def _():
        sort_rows(SORT_ROWS)

    _segment_copies(i, seg_dst_ref, seg_rows_ref, tile_off_ref, xs_ref, sorted_ref.at[slot], sem.at[slot], True)

    @pl.when(i > 0)
    def _():
        _wait_rows(sorted_ref.at[1 - slot], tile_rows_ref[jnp.maximum(i - 1, 0)], sem.at[1 - slot])

    @pl.when(i == pl.num_programs(0) - 1)
    def _():
        _wait_rows(sorted_ref.at[slot], tile_rows_ref[i], sem.at[slot])
        _zero_unused_rows(tail_dst_ref, tail_rows_ref, n_used_ref, xs_ref, zero_ref, zero_sem, False)


def _sort_call(h2, route_t, sched, max_rows):
    n, d = h2.shape
    tm = ROUTE_ROWS
    return pl.pallas_call(
        _sort_kernel,
        grid_spec=pltpu.PrefetchScalarGridSpec(
            num_scalar_prefetch=7,
            grid=(n // tm,),
            in_specs=[pl.BlockSpec((tm, d), lambda i, *_: (i, 0)),
                      pl.BlockSpec((1, 8, tm), lambda i, *_: (i, 0, 0))],
            out_specs=pl.BlockSpec(memory_space=pl.ANY),
            scratch_shapes=[pltpu.VMEM((2, SORT_ROWS, d), BF16), pltpu.VMEM((EXPERT_ROWS, d), BF16),
                            pltpu.SemaphoreType.DMA((2,)), pltpu.SemaphoreType.DMA(())],
        ),
        out_shape=jax.ShapeDtypeStruct((max_rows, d), BF16),
        compiler_params=pltpu.CompilerParams(dimension_semantics=("arbitrary",), vmem_limit_bytes=VMEM_LIMIT_BYTES),
        name="moe_sort",
    )(sched["seg_dst"], sched["seg_rows"], sched["tile_off"], sched["tile_rows"], sched["tail_dst"],
      sched["tail_rows"], sched["n_used"], h2, route_t)


def _expert_kernel(tile_expert_ref, n_used_ref, xs_ref, wg_ref, wu_ref, wd_ref, ys_ref, wg_bf, wu_bf, wd_bf):
    t = pl.program_id(0)
    used = t < n_used_ref[0]

    @pl.when(used & ((t == 0) | (tile_expert_ref[t] != tile_expert_ref[jnp.maximum(t - 1, 0)])))
    def _():
        wg_bf[...] = wg_ref[0].astype(BF16)
        wu_bf[...] = wu_ref[0].astype(BF16)
        wd_bf[...] = wd_ref[0].astype(BF16)

    @pl.when(used)
    def _():
        xs = xs_ref[...]
        gate = _dot(xs, wg_bf[...])
        up = _dot(xs, wu_bf[...])
        hidden = (gate * jax.nn.sigmoid(gate) * up).astype(BF16)
        ys_ref[...] = _dot(hidden, wd_bf[...]).astype(BF16)

    @pl.when(jnp.logical_not(used))
    def _():
        ys_ref[...] = jnp.zeros(ys_ref.shape, BF16)


def _expert_call(xs, wg, wu, wd, sched):
    rows, d = xs.shape
    tr = EXPERT_ROWS
    blk = lambda t, te, nu: (jnp.minimum(t, nu[0] - 1), 0)
    wsel = lambda t, te, nu: (te[jnp.minimum(t, nu[0] - 1)], 0, 0)
    return pl.pallas_call(
        _expert_kernel,
        grid_spec=pltpu.PrefetchScalarGridSpec(
            num_scalar_prefetch=2,
            grid=(rows // tr,),
            in_specs=[pl.BlockSpec((tr, d), blk),
                      pl.BlockSpec((1, d, EXPERT_FF), wsel), pl.BlockSpec((1, d, EXPERT_FF), wsel),
                      pl.BlockSpec((1, EXPERT_FF, d), wsel)],
            out_specs=pl.BlockSpec((tr, d), lambda t, te, nu: (t, 0)),
            scratch_shapes=[pltpu.VMEM((d, EXPERT_FF), BF16), pltpu.VMEM((d, EXPERT_FF), BF16),
                            pltpu.VMEM((EXPERT_FF, d), BF16)],
        ),
        out_shape=jax.ShapeDtypeStruct((rows, d), BF16),
        compiler_params=pltpu.CompilerParams(dimension_semantics=("arbitrary",), vmem_limit_bytes=VMEM_LIMIT_BYTES),
        name="moe_experts",
    )(sched["tile_expert"], sched["n_used"], xs, wg, wu, wd)


def _combine_kernel(seg_dst_ref, seg_rows_ref, tile_off_ref, tile_rows_ref, ys_ref, route_ref, x1_ref, o_ref,
                    buf_ref, sem):
    i = pl.program_id(0)
    tm = x1_ref.shape[0]
    slot = i % 2

    def fetch(tile, into):
        buf_ref[into] = jnp.zeros(buf_ref.shape[1:], BF16)
        _segment_copies(tile, seg_dst_ref, seg_rows_ref, tile_off_ref, ys_ref, buf_ref.at[into], sem.at[into], False)

    @pl.when(i == 0)
    def _():
        fetch(i, slot)

    @pl.when(i + 1 < pl.num_programs(0))
    def _():
        fetch(i + 1, 1 - slot)

    route = route_ref[...]
    d1 = route[:, 0:1].astype(jnp.int32)
    d2 = route[:, 1:2].astype(jnp.int32)
    w1 = route[:, 2:3]
    w2 = route[:, 3:4]
    _wait_rows(buf_ref.at[slot], tile_rows_ref[i], sem.at[slot])

    def combine_rows(n_rows):
        r = lax.broadcasted_iota(jnp.int32, (tm, n_rows), 1)
        weights = (jnp.where(r == d1, w1, 0.0) + jnp.where(r == d2, w2, 0.0)).astype(BF16)
        o_ref[...] = x1_ref[...] + _dot(weights, buf_ref[slot, 0:n_rows])

    @pl.when(tile_rows_ref[i] <= SORT_ROWS_COMMON)
    def _():
        combine_rows(SORT_ROWS_COMMON)

    @pl.when(tile_rows_ref[i] > SORT_ROWS_COMMON)
    def _():
        combine_rows(SORT_ROWS)


def _combine_call(ys, route, x1, sched):
    n, d = x1.shape
    tm = ROUTE_ROWS
    return pl.pallas_call(
        _combine_kernel,
        grid_spec=pltpu.PrefetchScalarGridSpec(
            num_scalar_prefetch=4,
            grid=(n // tm,),
            in_specs=[pl.BlockSpec(memory_space=pl.ANY),
                      pl.BlockSpec((tm, LANES), lambda i, *_: (i, 0)),
                      pl.BlockSpec((tm, d), lambda i, *_: (i, 0))],
            out_specs=pl.BlockSpec((tm, d), lambda i, *_: (i, 0)),
            scratch_shapes=[pltpu.VMEM((2, SORT_ROWS, d), BF16), pltpu.SemaphoreType.DMA((2,))],
        ),
        out_shape=jax.ShapeDtypeStruct((n, d), F32),
        compiler_params=pltpu.CompilerParams(dimension_semantics=("arbitrary",), vmem_limit_bytes=VMEM_LIMIT_BYTES),
        name="moe_combine",
    )(sched["seg_dst"], sched["seg_rows"], sched["tile_off"], sched["tile_rows"], ys, route, x1)


def _schedule_kernel(cnt_ref, seg_dst_ref, tile_off_ref, tile_rows_ref, misc_ref):
    hp = functools.partial(jnp.dot, preferred_element_type=F32, precision=lax.Precision.HIGHEST)
    cnt = cnt_ref[...]
    n_tiles = cnt.shape[0]
    tile_before = jnp.where(lax.broadcasted_iota(jnp.int32, (n_tiles, n_tiles), 1)
                            < lax.broadcasted_iota(jnp.int32, (n_tiles, n_tiles), 0), 1.0, 0.0)
    expert_before = jnp.where(lax.broadcasted_iota(jnp.int32, (LANES, LANES), 0)
                              < lax.broadcasted_iota(jnp.int32, (LANES, LANES), 1), 1.0, 0.0)
    expert_rows = jnp.sum(cnt, axis=0, keepdims=True)
    region = jnp.floor((expert_rows + (EXPERT_ROWS - 1)) * (1.0 / EXPERT_ROWS)) * EXPERT_ROWS
    region_start = hp(jnp.broadcast_to(region, (8, LANES)), expert_before)[0:1]
    seg_dst_ref[...] = (region_start + hp(tile_before, cnt)).astype(jnp.int32)
    tile_off_ref[...] = hp(cnt, expert_before).astype(jnp.int32)
    tile_rows_ref[...] = jnp.broadcast_to(jnp.sum(cnt, axis=-1, keepdims=True), cnt.shape).astype(jnp.int32)
    n_used = jnp.sum(region, axis=-1, keepdims=True) * (1.0 / EXPERT_ROWS)
    row = lax.broadcasted_iota(jnp.int32, (8, LANES), 0)
    misc = jnp.where(row == 0, region_start + expert_rows,
                     jnp.where(row == 1, region - expert_rows,
                               jnp.where(row == 2, region_start + region, n_used)))
    misc_ref[...] = misc.astype(jnp.int32)


def _moe_schedule(cnt, n_tokens):
    n_tiles = cnt.shape[0]
    table = jax.ShapeDtypeStruct((n_tiles, LANES), jnp.int32)
    seg_dst, tile_off, tile_rows, misc = pl.pallas_call(
        _schedule_kernel,
        out_shape=[table, table, table, jax.ShapeDtypeStruct((8, LANES), jnp.int32)],
        name="moe_schedule",
    )(cnt.reshape(n_tiles, LANES))
    max_rows = 2 * n_tokens + n_tiles * N_EXPERTS * (SEG_ALIGN - 1) + N_EXPERTS * (EXPERT_ROWS - 1)
    max_tiles = -(-max_rows // EXPERT_ROWS)
    tile_start = jnp.arange(max_tiles, dtype=jnp.int32) * EXPERT_ROWS
    region_end = misc[2, :N_EXPERTS]
    tile_expert = jnp.minimum(jnp.sum((region_end[None, :] <= tile_start[:, None]).astype(jnp.int32), axis=1),
                              N_EXPERTS - 1)
    flat = lambda a: a[:, :N_EXPERTS].reshape(-1)
    sched = {
        "seg_dst": flat(seg_dst),
        "seg_rows": flat(cnt.reshape(n_tiles, LANES).astype(jnp.int32)),
        "tile_off": flat(tile_off),
        "tile_rows": tile_rows[:, 0],
        "tail_dst": misc[0, :N_EXPERTS],
        "tail_rows": misc[1, :N_EXPERTS],
        "tile_expert": tile_expert,
        "n_used": misc[3, :1],
    }
    return sched, max_tiles * EXPERT_ROWS


def _rotary_tables(cos, sin, period, first, gain, scale):
    seq_len, half = cos.shape
    rest = period - first - 2 * half
    reps = LANES // period
    ones, zeros = jnp.ones((seq_len, first), F32), jnp.zeros((seq_len, first), F32)
    c = jnp.concatenate([ones, cos, cos, jnp.ones((seq_len, rest), F32)] * reps, axis=1)
    s = jnp.concatenate([zeros, -sin, sin, jnp.zeros((seq_len, rest), F32)] * reps, axis=1)
    gain = gain.astype(F32)
    blocks = []
    for base in range(0, LANES, period):
        a = base + first
        blocks += [gain[base:a], gain[a + half:a + 2 * half], gain[a:a + half], gain[a + 2 * half:base + period]]
    gain_partner = jnp.concatenate(blocks)
    return c * (gain * scale)[None, :], s * (gain_partner * scale)[None, :]


def _cos_sin(seq_len, rot_dim):
    pos = jnp.arange(seq_len, dtype=F32)
    inv = 1.0 / (ROPE_THETA ** (jnp.arange(0, rot_dim, 2, dtype=F32) / rot_dim))
    ang = pos[:, None] * inv[None, :]
    return jnp.cos(ang), jnp.sin(ang)


def _head_pad(w, heads, width):
    r = w.shape[0]
    w = w.reshape(r, heads, width)
    return jnp.pad(w, ((0, 0), (0, 0), (0, LANES - width))).reshape(r, heads * LANES)


def _layer_params(l, seq_len, norm_mix, w_in, mla_q_latent_norm, w_mla_uq, mla_kv_latent_norm, w_mla_ukv,
                  mla_q_gain, mla_k_gain, diff_q_gain, diff_k_gain, w_mla_up, w_diff_up, w_out, norm_ffn,
                  w_router_group, b_router_group, w_router_expert, b_router_expert):
    d = w_in.shape[1]
    sizes = (MLA_Q_RANK, MLA_KV_RANK, MLA_ROPE, DIFF_QK_WIDTH, DIFF_QK_WIDTH, DIFF_V_WIDTH, d, d)
    offs = [0]
    for s in sizes:
        offs.append(offs[-1] + s)
    wi = w_in[l]
    seg = [wi[:, offs[k]:offs[k + 1]] for k in range(len(sizes))]
    p = {}
    wkr = jnp.pad(seg[2], ((0, 0), (MLA_NOPE, LANES - MLA_QK)))
    p["proj_w"] = jnp.concatenate([seg[0], seg[1], wkr, seg[4], seg[6], seg[7]], axis=1).astype(BF16)
    p["wdqvt"] = jnp.concatenate([seg[3].T, seg[5].T], axis=0).astype(BF16)
    p["wuqt"] = _head_pad(w_mla_uq[l], MLA_HEADS, MLA_QK).T.astype(BF16)
    ukv = w_mla_ukv[l].reshape(MLA_KV_RANK, MLA_HEADS, MLA_NOPE + MLA_V)
    p["wuk"] = _head_pad(ukv[:, :, :MLA_NOPE].reshape(MLA_KV_RANK, -1), MLA_HEADS, MLA_NOPE).astype(BF16)
    p["wuvt"] = ukv[:, :, MLA_NOPE:].reshape(MLA_KV_RANK, -1).T.astype(BF16)
    gq = jnp.pad(mla_q_gain[l], (0, LANES - MLA_QK))
    gk = jnp.pad(mla_k_gain[l], (0, LANES - MLA_QK))
    nope = jnp.arange(LANES) < MLA_NOPE
    pad_row = lambda g: jnp.pad(g.astype(F32), (0, d - g.shape[0]))
    p["proj_gains"] = jnp.stack([pad_row(norm_mix[l]), pad_row(mla_q_latent_norm[l]), pad_row(mla_kv_latent_norm[l]),
                                 pad_row(jnp.where(nope, gk, 0.0))] + [jnp.zeros((d,), F32)] * 4)
    cos_m, sin_m = _cos_sin(seq_len, MLA_ROPE)
    cos_d, sin_d = _cos_sin(seq_len, DIFF_ROPE)
    aq, bq = _rotary_tables(cos_m, sin_m, LANES, MLA_NOPE, gq, LOG2E * MLA_QK ** -0.5)
    ak, bk = _rotary_tables(cos_m, sin_m, LANES, MLA_NOPE, jnp.where(nope, 0.0, gk), 1.0)
    adq, bdq = _rotary_tables(cos_d, sin_d, DIFF_HEAD_DIM, 0, jnp.tile(diff_q_gain[l], 2),
                              LOG2E * DIFF_HEAD_DIM ** -0.5)
    adk, bdk = _rotary_tables(cos_d, sin_d, DIFF_HEAD_DIM, 0, jnp.tile(diff_k_gain[l], 2), 1.0)
    p["tables_fm"] = jnp.concatenate([aq, bq, adq, bdq], axis=1).T
    p["tables_tm"] = jnp.concatenate([ak, bk, adk, bdk], axis=1)
    p["merge_w"] = jnp.concatenate([w_mla_up[l], w_diff_up[l], w_out[l]], axis=0).astype(BF16)
    p["gffn"] = norm_ffn[l].astype(F32)[None, :]
    wr = jnp.concatenate([w_router_expert[l], w_router_group[l]], axis=1).astype(F32)
    wrt = jnp.pad(wr, ((0, 0), (0, LANES - wr.shape[1]))).T
    wrt_hi = wrt.astype(BF16)
    p["wrt"] = jnp.concatenate([wrt_hi, (wrt - wrt_hi.astype(F32)).astype(BF16)], axis=0)
    br = jnp.concatenate([b_router_expert[l], b_router_group[l]]).astype(F32)
    p["brt"] = jnp.broadcast_to(jnp.pad(br, (0, LANES - br.shape[0]))[:, None], (LANES, MERGE_ROWS))
    return p


def kernel(x, norm_mix, w_in, mla_q_latent_norm, w_mla_uq, mla_kv_latent_norm, w_mla_ukv, mla_q_gain, mla_k_gain, diff_q_gain, diff_k_gain, lambda_q1, lambda_k1, lambda_q2, lambda_k2, diff_subln, w_mla_up, w_diff_up, w_out, norm_ffn, w_router_group, b_router_group, w_router_expert, b_router_expert, w_expert_gate, w_expert_up, w_expert_down):
    batch, seq_len, d = x.shape
    x2 = x.reshape(batch * seq_len, d)
    row = lambda g: g.astype(F32)[None, :]
    for l in range(norm_mix.shape[0]):
        lam_init = 0.8 - 0.6 * math.exp(-0.3 * l)
        p = _layer_params(l, seq_len, norm_mix, w_in, mla_q_latent_norm, w_mla_uq, mla_kv_latent_norm, w_mla_ukv,
                          mla_q_gain, mla_k_gain, diff_q_gain, diff_k_gain, w_mla_up, w_diff_up, w_out, norm_ffn,
                          w_router_group, b_router_group, w_router_expert, b_router_expert)
        qmt, km, vtm, qdt, kd, vtd, sgm, sgd = _proj_call(x2, seq_len, p)
        om = _mla_call(qmt, km, vtm, batch, seq_len)
        od = _diff_call(qdt, kd, vtd, row(lambda_q1[l]), row(lambda_k1[l]), row(lambda_q2[l]), row(lambda_k2[l]),
                        diff_subln[l].astype(F32)[:, None], lam_init, batch, seq_len)
        x1, h2, route, route_t, cnt = _merge_call(x2, om, od, sgm, sgd, p)
        sched, max_rows = _moe_schedule(cnt, x2.shape[0])
        xs = _sort_call(h2, route_t, sched, max_rows)
        ys = _expert_call(xs, w_expert_gate[l], w_expert_up[l], w_expert_down[l], sched)
        x2 = _combine_call(ys, route, x1, sched)
    return x2.reshape(batch, seq_len, d)
```

```python
import functools
import math

import jax
import jax.numpy as jnp
from jax import lax
from jax.experimental import pallas as pl
from jax.experimental.pallas import tpu as pltpu

CHUNK = 64
ROPE_THETA = 500000.0
EPS = 1e-6

MLA_HEADS = 8
MLA_NOPE = 64
MLA_ROPE = 32
MLA_V = 64
MLA_QK = MLA_NOPE + MLA_ROPE
MLA_Q_RANK = 256
MLA_KV_RANK = 128

DIFF_HEADS = 4
DIFF_HEAD_DIM = 64
DIFF_V_DIM = 2 * DIFF_HEAD_DIM
DIFF_ROPE = DIFF_HEAD_DIM // 4
DIFF_QK_WIDTH = DIFF_HEADS * 2 * DIFF_HEAD_DIM
DIFF_V_WIDTH = DIFF_HEADS * DIFF_V_DIM

N_GROUPS = 4
EXPERTS_PER_GROUP = 8
N_EXPERTS = N_GROUPS * EXPERTS_PER_GROUP
EXPERT_FF = 256

LANES = 128
VMEM_LIMIT_BYTES = 48 * 1024 * 1024

PROJ_ROWS = 512
ATTN_Q_ROWS = 512
ATTN_K_ROWS = 256
MERGE_ROWS = 512
ROUTE_ROWS = MERGE_ROWS
SEG_ALIGN = 16
SORT_ROWS = 2 * ROUTE_ROWS + N_EXPERTS * SEG_ALIGN
SORT_ROWS_COMMON = 2 * ROUTE_ROWS + N_EXPERTS * SEG_ALIGN // 2
EXPERT_ROWS = 512
MLA_HEADS_PER_STEP = 4
DIFF_HEADS_PER_STEP = 2
LOG2E = 1.4426950408889634

BF16 = jnp.bfloat16
F32 = jnp.float32


def _dot(a, b):
    return jnp.dot(a, b, preferred_element_type=F32)


def _dot_nt(a, b):
    return lax.dot_general(a, b, (((1,), (1,)), ((), ())), preferred_element_type=F32)


def _rms(x, width):
    return x * lax.rsqrt(jnp.sum(x * x, axis=-1, keepdims=True) * (1.0 / width) + EPS)


def _rotary_partner(y, half):
    lane = lax.broadcasted_iota(jnp.int32, y.shape, 1)
    up = pltpu.roll(y, LANES - half, 1)
    down = pltpu.roll(y, half, 1)
    return jnp.where((lane // half) % 2 == 0, up, down)


def _swap_row_blocks(y, first, half, period):
    parts = []
    for base in range(0, y.shape[0], period):
        a = base + first
        parts += [y[base:a], y[a + half:a + 2 * half], y[a:a + half], y[a + 2 * half:base + period]]
    return jnp.concatenate([p for p in parts if p.shape[0]], axis=0)


def _store_k_tiles(o_ref, vt):
    tk = o_ref.shape[-1]
    for c in range(o_ref.shape[0]):
        o_ref[c] = vt[:, c * tk:(c + 1) * tk].astype(BF16)


def _proj_kernel(x_ref, gmix_ref, wql_ref, wkvl_ref, wkr_ref, wdk_ref, wdqvt_ref, wgm_ref, wgd_ref,
                 gql_ref, wuqt_ref, gkvl_ref, wuk_ref, wuvt_ref, gkn_ref,
                 aq_ref, bq_ref, adq_ref, bdq_ref, ak_ref, bk_ref, adk_ref, bdk_ref,
                 qmt_ref, km_ref, vtm_ref, qdt_ref, kd_ref, vtd_ref, sgm_ref, sgd_ref):
    x = x_ref[...]
    h = (_rms(x, x.shape[-1]) * gmix_ref[...]).astype(BF16)

    ql = (_rms(_dot(h, wql_ref[...]), MLA_Q_RANK) * gql_ref[...]).astype(BF16)
    qt = _dot_nt(wuqt_ref[...], ql)
    aq, bq = aq_ref[...], bq_ref[...]
    for hd in range(MLA_HEADS):
        rows = slice(hd * LANES, (hd + 1) * LANES)
        qh = qt[rows]
        r = lax.rsqrt(jnp.sum(qh * qh, axis=0, keepdims=True) * (1.0 / MLA_QK) + EPS)
        y = (qh * aq + _swap_row_blocks(qh, MLA_NOPE, MLA_ROPE // 2, LANES) * bq) * r
        qmt_ref[0, rows, :] = y.astype(BF16)

    kvl = (_rms(_dot(h, wkvl_ref[...]), MLA_KV_RANK) * gkvl_ref[...]).astype(BF16)
    kr = _dot(h, wkr_ref[...])
    kr_rot = kr * ak_ref[...] + _rotary_partner(kr, MLA_ROPE // 2) * bk_ref[...]
    kr_ss = jnp.sum(kr * kr, axis=-1, keepdims=True)
    kn = _dot(kvl, wuk_ref[...])
    _store_k_tiles(vtm_ref, _dot_nt(wuvt_ref[...], kvl))
    gkn = gkn_ref[...]
    for hd in range(MLA_HEADS):
        sl = slice(hd * LANES, (hd + 1) * LANES)
        knh = kn[:, sl]
        r = lax.rsqrt((jnp.sum(knh * knh, axis=-1, keepdims=True) + kr_ss) * (1.0 / MLA_QK) + EPS)
        km_ref[:, sl] = ((knh * gkn + kr_rot) * r).astype(BF16)

    qvt = _dot_nt(wdqvt_ref[...], h)
    _store_k_tiles(vtd_ref, qvt[DIFF_QK_WIDTH:])
    adq, bdq = adq_ref[...], bdq_ref[...]
    for hd in range(DIFF_HEADS):
        rows = slice(hd * LANES, (hd + 1) * LANES)
        qh = qvt[rows]
        t = qh * adq + _swap_row_blocks(qh, 0, DIFF_ROPE // 2, DIFF_HEAD_DIM) * bdq
        halves = []
        for f in range(2):
            part = qh[f * DIFF_HEAD_DIM:(f + 1) * DIFF_HEAD_DIM]
            r = lax.rsqrt(jnp.sum(part * part, axis=0, keepdims=True) * (1.0 / DIFF_HEAD_DIM) + EPS)
            halves.append(t[f * DIFF_HEAD_DIM:(f + 1) * DIFF_HEAD_DIM] * r)
        qdt_ref[0, rows, :] = jnp.concatenate(halves, axis=0).astype(BF16)

    kd = _dot(h, wdk_ref[...])
    adk, bdk = adk_ref[...], bdk_ref[...]
    for hd in range(DIFF_HEADS):
        sl = slice(hd * LANES, (hd + 1) * LANES)
        th = kd[:, sl]
        lane = lax.broadcasted_iota(jnp.int32, th.shape, 1)
        sq = th * th
        lo = jnp.sum(jnp.where(lane < DIFF_HEAD_DIM, sq, 0.0), axis=-1, keepdims=True)
        tot = jnp.sum(sq, axis=-1, keepdims=True)
        r = lax.rsqrt(jnp.where(lane < DIFF_HEAD_DIM, lo, tot - lo) * (1.0 / DIFF_HEAD_DIM) + EPS)
        kd_ref[:, sl] = ((th * adk + _rotary_partner(th, DIFF_ROPE // 2) * bdk) * r).astype(BF16)

    sgm_ref[...] = jax.nn.sigmoid(_dot(h, wgm_ref[...])).astype(BF16)
    sgd_ref[...] = jax.nn.sigmoid(_dot(h, wgd_ref[...])).astype(BF16)


def _proj_call(x2, seq_len, p):
    n, d = x2.shape
    tm = PROJ_ROWS
    pos_blocks = seq_len // tm
    row = lambda i: (i, 0)
    const = lambda i: (0, 0)
    weights = [p["gmix"], p["wql"], p["wkvl"], p["wkr"], p["wdk"], p["wdqvt"], p["wgm"], p["wgd"],
               p["gql"], p["wuqt"], p["gkvl"], p["wuk"], p["wuvt"], p["gkn"]]
    feature_major_tables = [p["aq"], p["bq"], p["adq"], p["bdq"]]
    token_major_tables = [p["ak"], p["bk"], p["adk"], p["bdk"]]
    in_specs = ([pl.BlockSpec((tm, d), row)]
                + [pl.BlockSpec(w.shape, const) for w in weights]
                + [pl.BlockSpec((LANES, tm), lambda i: (0, i % pos_blocks)) for _ in feature_major_tables]
                + [pl.BlockSpec((tm, LANES), lambda i: (i % pos_blocks, 0)) for _ in token_major_tables])
    tk = ATTN_K_ROWS
    k_tiles = lambda width: (pl.BlockSpec((tm // tk, width, tk), lambda i: (i, 0, 0)),
                             jax.ShapeDtypeStruct((n // tk, width, tk), BF16))
    token_major = lambda width: (pl.BlockSpec((tm, width), row), jax.ShapeDtypeStruct((n, width), BF16))
    assert tm == ATTN_Q_ROWS
    feature_major = lambda width: (pl.BlockSpec((1, width, tm), lambda i: (i, 0, 0)),
                                   jax.ShapeDtypeStruct((n // tm, width, tm), BF16))
    outs = [feature_major(MLA_HEADS * LANES), token_major(MLA_HEADS * LANES), k_tiles(MLA_HEADS * MLA_V),
            feature_major(DIFF_QK_WIDTH), token_major(DIFF_QK_WIDTH), k_tiles(DIFF_V_WIDTH),
            token_major(d), token_major(d)]
    return pl.pallas_call(
        _proj_kernel,
        grid=(n // tm,),
        in_specs=in_specs,
        out_specs=[o[0] for o in outs],
        out_shape=[o[1] for o in outs],
        compiler_params=pltpu.CompilerParams(dimension_semantics=("parallel",), vmem_limit_bytes=VMEM_LIMIT_BYTES),
        name="proj",
    )(x2, *weights, *feature_major_tables, *token_major_tables)


def _chunk_mask_t(tk, tq, diag):
    kc = lax.broadcasted_iota(jnp.int32, (tk, tq), 0) // CHUNK + diag * (tk // CHUNK)
    qc = lax.broadcasted_iota(jnp.int32, (tk, tq), 1) // CHUNK
    return kc <= qc


ONES_ROWS = 16


def _with_ones_rows(vt):
    return jnp.concatenate([vt, jnp.ones((ONES_ROWS, vt.shape[1]), vt.dtype)], axis=0)


def _softmax_step_t(st, vt_ones, m_ref, acc_ref):
    m_prev = m_ref[...]
    m_new = jnp.maximum(m_prev, jnp.max(st, axis=0, keepdims=True))
    alpha = jnp.exp2(m_prev - m_new)
    pr = jnp.exp2(st - m_new)
    acc_ref[...] = alpha * acc_ref[...] + _dot(vt_ones, pr.astype(BF16))
    m_ref[...] = m_new


def _normalized(acc_ref, dv):
    acc = acc_ref[...]
    return acc[:dv] / acc[dv:dv + 1]


STATE_REFS = 4


def _attn_scratch(chains, dv, tq, tk):
    per_chain = [pltpu.VMEM((1, tq), F32), pltpu.VMEM((dv + ONES_ROWS, tq), F32),
                 pltpu.VMEM((tk, tq), F32), pltpu.VMEM((tk, tq), F32)]
    return per_chain * chains


def _flash_attention(n_q_tiles, scratch_refs, score_fn, value_fn, finalize_fn, tk, tq):
    ratio = tq // tk
    assert tq == ratio * tk and ratio % 2 == 0
    n_chains = len(scratch_refs) // STATE_REFS
    chains = [scratch_refs[STATE_REFS * c:STATE_REFS * (c + 1)] for c in range(n_chains)]

    def scores(i, t, slot):
        for c, ch in enumerate(chains):
            ch[2 + slot][...] = score_fn(c, i, t)

    def update(t, slot, diag=None):
        for c, ch in enumerate(chains):
            st = ch[2 + slot][...]
            if diag is not None:
                st = jnp.where(_chunk_mask_t(tk, tq, diag), st, -jnp.inf)
            _softmax_step_t(st, _with_ones_rows(value_fn(c, t)), ch[0], ch[1])

    scores(0, 0, 0)

    def query_tile(i, carry):
        for m_ref, acc_ref, _, _ in chains:
            m_ref[...] = jnp.full(m_ref.shape, -jnp.inf, F32)
            acc_ref[...] = jnp.zeros(acc_ref.shape, F32)

        def pair(p, c):
            t = 2 * p
            scores(i, t + 1, 1)
            update(t, 0)
            scores(i, t + 2, 0)
            update(t + 1, 1)
            return c

        lax.fori_loop(0, i * (ratio // 2), pair, 0)
        first_diag = ratio * i
        for d in range(ratio):
            if d + 1 < ratio:
                scores(i, first_diag + d + 1, (d + 1) % 2)
            else:
                scores(jnp.minimum(i + 1, n_q_tiles - 1), 0, 0)
            update(first_diag + d, d % 2, diag=d)
        finalize_fn(i, [ch[1] for ch in chains])
        return carry

    lax.fori_loop(0, n_q_tiles, query_tile, 0)


def _mla_kernel(qt_ref, k_ref, vt_ref, o_ref, *scratch_refs):
    tq, tk = ATTN_Q_ROWS, ATTN_K_ROWS

    def score_fn(c, i, t):
        rows = pl.ds(pl.multiple_of(t * tk, tk), tk)
        sl = slice(c * LANES, (c + 1) * LANES)
        return _dot(k_ref[rows, sl], qt_ref[i, sl, :])

    def value_fn(c, t):
        return vt_ref[t, c * MLA_V:(c + 1) * MLA_V, :]

    def finalize_fn(i, accs):
        ot = jnp.concatenate([_normalized(acc_ref, MLA_V) for acc_ref in accs], axis=0)
        o_ref[pl.ds(pl.multiple_of(i * tq, tq), tq), :] = ot.T.astype(BF16)

    _flash_attention(qt_ref.shape[0], scratch_refs, score_fn, value_fn, finalize_fn, tk, tq)


def _mla_call(qmt, km, vtm, batch, seq_len):
    n = km.shape[0]
    tq, tk, hps = ATTN_Q_ROWS, ATTN_K_ROWS, MLA_HEADS_PER_STEP
    return pl.pallas_call(
        _mla_kernel,
        grid=(batch, MLA_HEADS // hps),
        in_specs=[pl.BlockSpec((seq_len // tq, hps * LANES, tq), lambda b, h: (b, h, 0)),
                  pl.BlockSpec((seq_len, hps * LANES), lambda b, h: (b, h)),
                  pl.BlockSpec((seq_len // tk, hps * MLA_V, tk), lambda b, h: (b, h, 0))],
        out_specs=pl.BlockSpec((seq_len, hps * MLA_V), lambda b, h: (b, h)),
        out_shape=jax.ShapeDtypeStruct((n, MLA_HEADS * MLA_V), BF16),
        scratch_shapes=_attn_scratch(hps, MLA_V, tq, tk),
        compiler_params=pltpu.CompilerParams(dimension_semantics=("parallel", "parallel"),
                                             vmem_limit_bytes=VMEM_LIMIT_BYTES),
        name="mla_attn",
    )(qmt, km, vtm)


def _diff_kernel(lam_init, qt_ref, k_ref, vt_ref, lq1_ref, lk1_ref, lq2_ref, lk2_ref, subln_ref, o_ref,
                 *scratch_refs):
    tq, tk = ATTN_Q_ROWS, ATTN_K_ROWS
    hps = DIFF_HEADS_PER_STEP
    zero = jnp.zeros((DIFF_HEAD_DIM, tq), BF16)

    def score_fn(c, i, t):
        rows = pl.ds(pl.multiple_of(t * tk, tk), tk)
        hd, f = c // 2, c % 2
        half = qt_ref[i, hd * LANES + f * DIFF_HEAD_DIM:hd * LANES + (f + 1) * DIFF_HEAD_DIM, :]
        q = jnp.concatenate([half, zero] if f == 0 else [zero, half], axis=0)
        return _dot(k_ref[rows, hd * LANES:(hd + 1) * LANES], q)

    def value_fn(c, t):
        hd = c // 2
        return vt_ref[t, hd * DIFF_V_DIM:(hd + 1) * DIFF_V_DIM, :]

    lam = (jnp.exp(jnp.sum(lq1_ref[...] * lk1_ref[...], axis=-1, keepdims=True))
           - jnp.exp(jnp.sum(lq2_ref[...] * lk2_ref[...], axis=-1, keepdims=True)) + lam_init)
    subln = subln_ref[...] * (1.0 - lam_init)

    def finalize_fn(i, accs):
        heads = []
        for hd in range(hps):
            ot = _normalized(accs[2 * hd], DIFF_V_DIM) - lam * _normalized(accs[2 * hd + 1], DIFF_V_DIM)
            ot = ot * lax.rsqrt(jnp.sum(ot * ot, axis=0, keepdims=True) * (1.0 / DIFF_V_DIM) + EPS)
            heads.append(ot * subln)
        o_ref[pl.ds(pl.multiple_of(i * tq, tq), tq), :] = jnp.concatenate(heads, axis=0).T.astype(BF16)

    _flash_attention(qt_ref.shape[0], scratch_refs, score_fn, value_fn, finalize_fn, tk, tq)


def _diff_call(qdt, kd, vtd, lq1, lk1, lq2, lk2, subln_col, lam_init, batch, seq_len):
    n = kd.shape[0]
    tq, tk, hps = ATTN_Q_ROWS, ATTN_K_ROWS, DIFF_HEADS_PER_STEP
    small = lambda a: pl.BlockSpec(a.shape, lambda b, h: (0, 0))
    return pl.pallas_call(
        functools.partial(_diff_kernel, lam_init),
        grid=(batch, DIFF_HEADS // hps),
        in_specs=[pl.BlockSpec((seq_len // tq, hps * LANES, tq), lambda b, h: (b, h, 0)),
                  pl.BlockSpec((seq_len, hps * LANES), lambda b, h: (b, h)),
                  pl.BlockSpec((seq_len // tk, hps * DIFF_V_DIM, tk), lambda b, h: (b, h, 0)),
                  small(lq1), small(lk1), small(lq2), small(lk2), small(subln_col)],
        out_specs=pl.BlockSpec((seq_len, hps * LANES), lambda b, h: (b, h)),
        out_shape=jax.ShapeDtypeStruct((n, DIFF_V_WIDTH), BF16),
        scratch_shapes=_attn_scratch(2 * hps, DIFF_V_DIM, tq, tk),
        compiler_params=pltpu.CompilerParams(dimension_semantics=("parallel", "parallel"),
                                             vmem_limit_bytes=VMEM_LIMIT_BYTES),
        name="diff_attn",
    )(qdt, kd, vtd, lq1, lk1, lq2, lk2, subln_col)


def _merge_kernel(x_ref, om_ref, od_ref, sgm_ref, sgd_ref, wmu_ref, wdu_ref, wout_ref, gffn_ref, wrt_ref,
                  brt_ref, x1_ref, h2_ref, route_ref, route_t_ref, cnt_ref):
    merged = (sgm_ref[...].astype(F32) * _dot(om_ref[...], wmu_ref[...])
              + sgd_ref[...].astype(F32) * _dot(od_ref[...], wdu_ref[...]))
    x1 = x_ref[...] + _dot(merged.astype(BF16), wout_ref[...])
    x1_ref[...] = x1
    h2 = _rms(x1, x1.shape[-1]) * gffn_ref[...]
    h2_hi = h2.astype(BF16)
    h2_ref[...] = h2_hi
    tm = h2.shape[0]

    h2_lo = (h2 - h2_hi.astype(F32)).astype(BF16)
    by_hi = _dot_nt(wrt_ref[...], h2_hi)
    logits = by_hi[:LANES] + by_hi[LANES:] + _dot_nt(wrt_ref[:LANES, :], h2_lo) + brt_ref[...]
    row = lax.broadcasted_iota(jnp.int32, logits.shape, 0)
    neg = -jnp.inf
    big = jnp.int32(1 << 20)

    def top(vals):
        mx = jnp.max(vals, axis=0, keepdims=True)
        idx = jnp.min(jnp.where(vals == mx, row, big), axis=0, keepdims=True)
        return mx, idx

    gl = jnp.where((row >= N_EXPERTS) & (row < N_EXPERTS + N_GROUPS), logits, neg)
    gmax, gidx = top(gl)
    pg_sel = 1.0 / jnp.sum(jnp.exp(gl - gmax), axis=0, keepdims=True)
    el = jnp.where((row < N_EXPERTS) & (row // EXPERTS_PER_GROUP == gidx - N_EXPERTS), logits, neg)
    m1, i1 = top(el)
    m2, i2 = top(jnp.where(row == i1, neg, el))
    e2 = jnp.exp(m2 - m1)
    w1 = pg_sel / (1.0 + e2)
    w2 = w1 * e2

    sel = jnp.where((row == i1) | (row == i2), 1.0, 0.0).astype(BF16)
    t_row = lax.broadcasted_iota(jnp.int32, (tm, tm), 0)
    t_col = lax.broadcasted_iota(jnp.int32, (tm, tm), 1)
    rank = _dot(sel, jnp.where(t_row < t_col, 1.0, 0.0).astype(BF16))
    cnt = _dot(sel, jnp.ones((tm, tm), BF16))
    seg = jnp.floor((cnt + (SEG_ALIGN - 1)) * (1.0 / SEG_ALIGN))
    e_row = lax.broadcasted_iota(jnp.int32, (LANES, LANES), 0)
    e_col = lax.broadcasted_iota(jnp.int32, (LANES, LANES), 1)
    off = _dot(jnp.where(e_col < e_row, 1.0, 0.0).astype(BF16), seg.astype(BF16)) * SEG_ALIGN
    dest = off + rank
    d1 = jnp.sum(jnp.where(row == i1, dest, 0.0), axis=0, keepdims=True)
    d2 = jnp.sum(jnp.where(row == i2, dest, 0.0), axis=0, keepdims=True)
    route_t = jnp.where(row == 0, d1, jnp.where(row == 1, d2, jnp.where(row == 2, w1, jnp.where(row == 3, w2, 0.0))))
    route_t_ref[0] = route_t[0:8]
    route_ref[...] = route_t.T
    cnt_ref[0] = (seg[:, :LANES] * SEG_ALIGN).T[0:1]


def _merge_call(x2, om, od, sgm, sgd, p):
    n, d = x2.shape
    tm = MERGE_ROWS
    row = lambda i: (i, 0)
    const = lambda i: (0, 0)
    weights = [p["wmu"], p["wdu"], p["wout"], p["gffn"], p["wrt"], p["brt"]]
    return pl.pallas_call(
        _merge_kernel,
        grid=(n // tm,),
        in_specs=([pl.BlockSpec((tm, a.shape[1]), row) for a in (x2, om, od, sgm, sgd)]
                  + [pl.BlockSpec(w.shape, const) for w in weights]),
        out_specs=[pl.BlockSpec((tm, d), row), pl.BlockSpec((tm, d), row), pl.BlockSpec((tm, LANES), row),
                   pl.BlockSpec((1, 8, tm), lambda i: (i, 0, 0)), pl.BlockSpec((1, 1, LANES), lambda i: (i, 0, 0))],
        out_shape=[jax.ShapeDtypeStruct((n, d), F32), jax.ShapeDtypeStruct((n, d), BF16),
                   jax.ShapeDtypeStruct((n, LANES), F32), jax.ShapeDtypeStruct((n // tm, 8, tm), F32),
                   jax.ShapeDtypeStruct((n // tm, 1, LANES), F32)],
        compiler_params=pltpu.CompilerParams(dimension_semantics=("parallel",), vmem_limit_bytes=VMEM_LIMIT_BYTES),
        name="merge_router",
    )(x2, om, od, sgm, sgd, *weights)


def _segment_copies(i, seg_dst_ref, seg_rows_ref, tile_off_ref, global_ref, tile_ref, sem, to_global):
    def body(e, carry):
        k = i * N_EXPERTS + e
        rows = pl.multiple_of(seg_rows_ref[k], SEG_ALIGN)

        @pl.when(rows > 0)
        def _():
            g = global_ref.at[pl.ds(pl.multiple_of(seg_dst_ref[k], SEG_ALIGN), rows)]
            t = tile_ref.at[pl.ds(pl.multiple_of(tile_off_ref[k], SEG_ALIGN), rows)]
            src, dst = (t, g) if to_global else (g, t)
            pltpu.make_async_copy(src, dst, sem).start()

        return carry

    lax.fori_loop(0, N_EXPERTS, body, 0)


def _wait_rows(tile_ref, rows, sem):
    @pl.when(rows > 0)
    def _():
        view = tile_ref.at[pl.ds(0, pl.multiple_of(rows, SEG_ALIGN))]
        pltpu.make_async_copy(view, view, sem).wait()


def _zero_unused_rows(tail_dst_ref, tail_rows_ref, n_used_ref, xs_ref, zero_ref, sem, start):
    n_tiles = xs_ref.shape[0] // EXPERT_ROWS
    if start:
        zero_ref[...] = jnp.zeros(zero_ref.shape, BF16)

    def tail(e, total):
        rows = pl.multiple_of(tail_rows_ref[e], SEG_ALIGN)
        if start:
            @pl.when(rows > 0)
            def _():
                dst = xs_ref.at[pl.ds(pl.multiple_of(tail_dst_ref[e], SEG_ALIGN), rows)]
                pltpu.make_async_copy(zero_ref.at[pl.ds(0, rows)], dst, sem).start()

        return total + rows

    total = lax.fori_loop(0, N_EXPERTS, tail, 0)
    if not start:
        _wait_rows(xs_ref, total + (n_tiles - n_used_ref[0]) * EXPERT_ROWS, sem)
        return

    def unused(t, carry):
        dst = xs_ref.at[pl.ds(pl.multiple_of(t * EXPERT_ROWS, EXPERT_ROWS), EXPERT_ROWS)]
        pltpu.make_async_copy(zero_ref, dst, sem).start()
        return carry

    lax.fori_loop(n_used_ref[0], n_tiles, unused, 0)


def _sort_kernel(seg_dst_ref, seg_rows_ref, tile_off_ref, tile_rows_ref, tail_dst_ref, tail_rows_ref, n_used_ref,
                 h2_ref, route_t_ref, xs_ref, sorted_ref, zero_ref, sem, zero_sem):
    i = pl.program_id(0)
    tm = h2_ref.shape[0]

    @pl.when(i == 0)
    def _():
        _zero_unused_rows(tail_dst_ref, tail_rows_ref, n_used_ref, xs_ref, zero_ref, zero_sem, True)

    d1 = route_t_ref[0, 0:1, :].astype(jnp.int32)
    d2 = route_t_ref[0, 1:2, :].astype(jnp.int32)
    slot = i % 2

    def sort_rows(n_rows):
        r = lax.broadcasted_iota(jnp.int32, (n_rows, tm), 0)
        perm = jnp.where((r == d1) | (r == d2), 1.0, 0.0).astype(BF16)
        sorted_ref[slot, 0:n_rows] = _dot(perm, h2_ref[...]).astype(BF16)

    @pl.when(tile_rows_ref[i] <= SORT_ROWS_COMMON)
    def _():
        sort_rows(SORT_ROWS_COMMON)

    @pl.when(tile_rows_ref[i] > SORT_ROWS_COMMON)
    def _():
        sort_rows(SORT_ROWS)

    _segment_copies(i, seg_dst_ref, seg_rows_ref, tile_off_ref, xs_ref, sorted_ref.at[slot], sem.at[slot], True)

    @pl.when(i > 0)
    def _():
        _wait_rows(sorted_ref.at[1 - slot], tile_rows_ref[jnp.maximum(i - 1, 0)], sem.at[1 - slot])

    @pl.when(i == pl.num_programs(0) - 1)
    def _():
        _wait_rows(sorted_ref.at[slot], tile_rows_ref[i], sem.at[slot])
        _zero_unused_rows(tail_dst_ref, tail_rows_ref, n_used_ref, xs_ref, zero_ref, zero_sem, False)


def _sort_call(h2, route_t, sched, max_rows):
    n, d = h2.shape
    tm = ROUTE_ROWS
    return pl.pallas_call(
        _sort_kernel,
        grid_spec=pltpu.PrefetchScalarGridSpec(
            num_scalar_prefetch=7,
            grid=(n // tm,),
            in_specs=[pl.BlockSpec((tm, d), lambda i, *_: (i, 0)),
                      pl.BlockSpec((1, 8, tm), lambda i, *_: (i, 0, 0))],
            out_specs=pl.BlockSpec(memory_space=pl.ANY),
            scratch_shapes=[pltpu.VMEM((2, SORT_ROWS, d), BF16), pltpu.VMEM((EXPERT_ROWS, d), BF16),
                            pltpu.SemaphoreType.DMA((2,)), pltpu.SemaphoreType.DMA(())],
        ),
        out_shape=jax.ShapeDtypeStruct((max_rows, d), BF16),
        compiler_params=pltpu.CompilerParams(dimension_semantics=("arbitrary",), vmem_limit_bytes=VMEM_LIMIT_BYTES),
        name="moe_sort",
    )(sched["seg_dst"], sched["seg_rows"], sched["tile_off"], sched["tile_rows"], sched["tail_dst"],
      sched["tail_rows"], sched["n_used"], h2, route_t)


def _expert_kernel(tile_expert_ref, n_used_ref, xs_ref, wg_ref, wu_ref, wd_ref, ys_ref, wg_bf, wu_bf, wd_bf):
    t = pl.program_id(0)
    used = t < n_used_ref[0]

    @pl.when(used & ((t == 0) | (tile_expert_ref[t] != tile_expert_ref[jnp.maximum(t - 1, 0)])))
    def _():
        wg_bf[...] = wg_ref[0].astype(BF16)
        wu_bf[...] = wu_ref[0].astype(BF16)
        wd_bf[...] = wd_ref[0].astype(BF16)

    @pl.when(used)
    def _():
        xs = xs_ref[...]
        gate = _dot(xs, wg_bf[...])
        up = _dot(xs, wu_bf[...])
        hidden = (gate * jax.nn.sigmoid(gate) * up).astype(BF16)
        ys_ref[...] = _dot(hidden, wd_bf[...]).astype(BF16)

    @pl.when(jnp.logical_not(used))
    def _():
        ys_ref[...] = jnp.zeros(ys_ref.shape, BF16)


def _expert_call(xs, wg, wu, wd, sched):
    rows, d = xs.shape
    tr = EXPERT_ROWS
    blk = lambda t, te, nu: (jnp.minimum(t, nu[0] - 1), 0)
    wsel = lambda t, te, nu: (te[jnp.minimum(t, nu[0] - 1)], 0, 0)
    return pl.pallas_call(
        _expert_kernel,
        grid_spec=pltpu.PrefetchScalarGridSpec(
            num_scalar_prefetch=2,
            grid=(rows // tr,),
            in_specs=[pl.BlockSpec((tr, d), blk),
                      pl.BlockSpec((1, d, EXPERT_FF), wsel), pl.BlockSpec((1, d, EXPERT_FF), wsel),
                      pl.BlockSpec((1, EXPERT_FF, d), wsel)],
            out_specs=pl.BlockSpec((tr, d), lambda t, te, nu: (t, 0)),
            scratch_shapes=[pltpu.VMEM((d, EXPERT_FF), BF16), pltpu.VMEM((d, EXPERT_FF), BF16),
                            pltpu.VMEM((EXPERT_FF, d), BF16)],
        ),
        out_shape=jax.ShapeDtypeStruct((rows, d), BF16),
        compiler_params=pltpu.CompilerParams(dimension_semantics=("arbitrary",), vmem_limit_bytes=VMEM_LIMIT_BYTES),
        name="moe_experts",
    )(sched["tile_expert"], sched["n_used"], xs, wg, wu, wd)


def _combine_kernel(seg_dst_ref, seg_rows_ref, tile_off_ref, tile_rows_ref, ys_ref, route_ref, x1_ref, o_ref,
                    buf_ref, sem):
    i = pl.program_id(0)
    tm = x1_ref.shape[0]
    slot = i % 2

    def fetch(tile, into):
        buf_ref[into] = jnp.zeros(buf_ref.shape[1:], BF16)
        _segment_copies(tile, seg_dst_ref, seg_rows_ref, tile_off_ref, ys_ref, buf_ref.at[into], sem.at[into], False)

    @pl.when(i == 0)
    def _():
        fetch(i, slot)

    @pl.when(i + 1 < pl.num_programs(0))
    def _():
        fetch(i + 1, 1 - slot)

    route = route_ref[...]
    d1 = route[:, 0:1].astype(jnp.int32)
    d2 = route[:, 1:2].astype(jnp.int32)
    w1 = route[:, 2:3]
    w2 = route[:, 3:4]
    _wait_rows(buf_ref.at[slot], tile_rows_ref[i], sem.at[slot])

    def combine_rows(n_rows):
        r = lax.broadcasted_iota(jnp.int32, (tm, n_rows), 1)
        weights = (jnp.where(r == d1, w1, 0.0) + jnp.where(r == d2, w2, 0.0)).astype(BF16)
        o_ref[...] = x1_ref[...] + _dot(weights, buf_ref[slot, 0:n_rows])

    @pl.when(tile_rows_ref[i] <= SORT_ROWS_COMMON)
    def _():
        combine_rows(SORT_ROWS_COMMON)

    @pl.when(tile_rows_ref[i] > SORT_ROWS_COMMON)
    def _():
        combine_rows(SORT_ROWS)


def _combine_call(ys, route, x1, sched):
    n, d = x1.shape
    tm = ROUTE_ROWS
    return pl.pallas_call(
        _combine_kernel,
        grid_spec=pltpu.PrefetchScalarGridSpec(
            num_scalar_prefetch=4,
            grid=(n // tm,),
            in_specs=[pl.BlockSpec(memory_space=pl.ANY),
                      pl.BlockSpec((tm, LANES), lambda i, *_: (i, 0)),
                      pl.BlockSpec((tm, d), lambda i, *_: (i, 0))],
            out_specs=pl.BlockSpec((tm, d), lambda i, *_: (i, 0)),
            scratch_shapes=[pltpu.VMEM((2, SORT_ROWS, d), BF16), pltpu.SemaphoreType.DMA((2,))],
        ),
        out_shape=jax.ShapeDtypeStruct((n, d), F32),
        compiler_params=pltpu.CompilerParams(dimension_semantics=("arbitrary",), vmem_limit_bytes=VMEM_LIMIT_BYTES),
        name="moe_combine",
    )(sched["seg_dst"], sched["seg_rows"], sched["tile_off"], sched["tile_rows"], ys, route, x1)


def _schedule_kernel(cnt_ref, seg_dst_ref, tile_off_ref, tile_rows_ref, misc_ref):
    hp = functools.partial(jnp.dot, preferred_element_type=F32, precision=lax.Precision.HIGHEST)
    cnt = cnt_ref[...]
    n_tiles = cnt.shape[0]
    tile_before = jnp.where(lax.broadcasted_iota(jnp.int32, (n_tiles, n_tiles), 1)
                            < lax.broadcasted_iota(jnp.int32, (n_tiles, n_tiles), 0), 1.0, 0.0)
    expert_before = jnp.where(lax.broadcasted_iota(jnp.int32, (LANES, LANES), 0)
                              < lax.broadcasted_iota(jnp.int32, (LANES, LANES), 1), 1.0, 0.0)
    expert_rows = jnp.sum(cnt, axis=0, keepdims=True)
    region = jnp.floor((expert_rows + (EXPERT_ROWS - 1)) * (1.0 / EXPERT_ROWS)) * EXPERT_ROWS
    region_start = hp(jnp.broadcast_to(region, (8, LANES)), expert_before)[0:1]
    seg_dst_ref[...] = (region_start + hp(tile_before, cnt)).astype(jnp.int32)
    tile_off_ref[...] = hp(cnt, expert_before).astype(jnp.int32)
    tile_rows_ref[...] = jnp.broadcast_to(jnp.sum(cnt, axis=-1, keepdims=True), cnt.shape).astype(jnp.int32)
    n_used = jnp.sum(region, axis=-1, keepdims=True) * (1.0 / EXPERT_ROWS)
    row = lax.broadcasted_iota(jnp.int32, (8, LANES), 0)
    misc = jnp.where(row == 0, region_start + expert_rows,
                     jnp.where(row == 1, region - expert_rows,
                               jnp.where(row == 2, region_start + region, n_used)))
    misc_ref[...] = misc.astype(jnp.int32)


def _moe_schedule(cnt, n_tokens):
    n_tiles = cnt.shape[0]
    table = jax.ShapeDtypeStruct((n_tiles, LANES), jnp.int32)
    seg_dst, tile_off, tile_rows, misc = pl.pallas_call(
        _schedule_kernel,
        out_shape=[table, table, table, jax.ShapeDtypeStruct((8, LANES), jnp.int32)],
        name="moe_schedule",
    )(cnt.reshape(n_tiles, LANES))
    max_rows = 2 * n_tokens + n_tiles * N_EXPERTS * (SEG_ALIGN - 1) + N_EXPERTS * (EXPERT_ROWS - 1)
    max_tiles = -(-max_rows // EXPERT_ROWS)
    tile_start = jnp.arange(max_tiles, dtype=jnp.int32) * EXPERT_ROWS
    region_end = misc[2, :N_EXPERTS]
    tile_expert = jnp.minimum(jnp.sum((region_end[None, :] <= tile_start[:, None]).astype(jnp.int32), axis=1),
                              N_EXPERTS - 1)
    flat = lambda a: a[:, :N_EXPERTS].reshape(-1)
    sched = {
        "seg_dst": flat(seg_dst),
        "seg_rows": flat(cnt.reshape(n_tiles, LANES).astype(jnp.int32)),
        "tile_off": flat(tile_off),
        "tile_rows": tile_rows[:, 0],
        "tail_dst": misc[0, :N_EXPERTS],
        "tail_rows": misc[1, :N_EXPERTS],
        "tile_expert": tile_expert,
        "n_used": misc[3, :1],
    }
    return sched, max_tiles * EXPERT_ROWS


def _rotary_tables(seq_len, rot_dim, period, first, gain, scale):
    half = rot_dim // 2
    pos = jnp.arange(seq_len, dtype=F32)
    inv = 1.0 / (ROPE_THETA ** (jnp.arange(0, rot_dim, 2, dtype=F32) / rot_dim))
    ang = pos[:, None] * inv[None, :]
    cos, sin = jnp.cos(ang), jnp.sin(ang)
    lane = jnp.arange(LANES)
    rel = (lane % period) - first
    active = (rel >= 0) & (rel < rot_dim)
    idx = jnp.clip(rel, 0, rot_dim - 1) % half
    sign = jnp.where(rel < half, -1.0, 1.0)
    partner = jnp.where(active, jnp.where(rel < half, lane + half, lane - half), lane)
    c = jnp.where(active[None, :], cos[:, idx], 1.0)
    s = jnp.where(active[None, :], sin[:, idx] * sign[None, :], 0.0)
    gain = gain.astype(F32)
    return (c * gain[None, :] * scale).astype(F32), (s * gain[partner][None, :] * scale).astype(F32)


def _head_pad(w, heads, width):
    r = w.shape[0]
    w = w.reshape(r, heads, width)
    return jnp.pad(w, ((0, 0), (0, 0), (0, LANES - width))).reshape(r, heads * LANES)


def _layer_params(l, seq_len, norm_mix, w_in, mla_q_latent_norm, w_mla_uq, mla_kv_latent_norm, w_mla_ukv,
                  mla_q_gain, mla_k_gain, diff_q_gain, diff_k_gain, w_mla_up, w_diff_up, w_out, norm_ffn,
                  w_router_group, b_router_group, w_router_expert, b_router_expert):
    d = w_in.shape[1]
    sizes = (MLA_Q_RANK, MLA_KV_RANK, MLA_ROPE, DIFF_QK_WIDTH, DIFF_QK_WIDTH, DIFF_V_WIDTH, d, d)
    offs = [0]
    for s in sizes:
        offs.append(offs[-1] + s)
    wi = w_in[l]
    seg = [wi[:, offs[k]:offs[k + 1]] for k in range(len(sizes))]
    row = lambda g: g.astype(F32)[None, :]
    p = {}
    p["gmix"] = row(norm_mix[l])
    p["wql"] = seg[0].astype(BF16)
    p["wkvl"] = seg[1].astype(BF16)
    p["wkr"] = jnp.pad(seg[2], ((0, 0), (MLA_NOPE, LANES - MLA_QK))).astype(BF16)
    p["wdk"] = seg[4].astype(BF16)
    p["wdqvt"] = jnp.concatenate([seg[3].T, seg[5].T], axis=0).astype(BF16)
    p["wgm"], p["wgd"] = seg[6].astype(BF16), seg[7].astype(BF16)
    p["gql"] = row(mla_q_latent_norm[l])
    p["wuqt"] = _head_pad(w_mla_uq[l], MLA_HEADS, MLA_QK).T.astype(BF16)
    p["gkvl"] = row(mla_kv_latent_norm[l])
    ukv = w_mla_ukv[l].reshape(MLA_KV_RANK, MLA_HEADS, MLA_NOPE + MLA_V)
    p["wuk"] = _head_pad(ukv[:, :, :MLA_NOPE].reshape(MLA_KV_RANK, -1), MLA_HEADS, MLA_NOPE).astype(BF16)
    p["wuvt"] = ukv[:, :, MLA_NOPE:].reshape(MLA_KV_RANK, -1).T.astype(BF16)
    gq = jnp.pad(mla_q_gain[l], (0, LANES - MLA_QK))
    gk = jnp.pad(mla_k_gain[l], (0, LANES - MLA_QK))
    nope = jnp.arange(LANES) < MLA_NOPE
    p["gkn"] = jnp.where(nope, gk, 0.0).astype(F32)[None, :]
    aq, bq = _rotary_tables(seq_len, MLA_ROPE, LANES, MLA_NOPE, gq, LOG2E * MLA_QK ** -0.5)
    p["aq"], p["bq"] = aq.T, bq.T
    ak, bk = _rotary_tables(seq_len, MLA_ROPE, LANES, MLA_NOPE, jnp.where(nope, 0.0, gk), 1.0)
    p["ak"], p["bk"] = ak, bk
    adq, bdq = _rotary_tables(seq_len, DIFF_ROPE, DIFF_HEAD_DIM, 0, jnp.tile(diff_q_gain[l], 2),
                              LOG2E * DIFF_HEAD_DIM ** -0.5)
    p["adq"], p["bdq"] = adq.T, bdq.T
    p["adk"], p["bdk"] = _rotary_tables(seq_len, DIFF_ROPE, DIFF_HEAD_DIM, 0, jnp.tile(diff_k_gain[l], 2), 1.0)
    p["wmu"] = w_mla_up[l].astype(BF16)
    p["wdu"] = w_diff_up[l].astype(BF16)
    p["wout"] = w_out[l].astype(BF16)
    p["gffn"] = row(norm_ffn[l])
    wr = jnp.concatenate([w_router_expert[l], w_router_group[l]], axis=1).astype(F32)
    wrt = jnp.pad(wr, ((0, 0), (0, LANES - wr.shape[1]))).T
    wrt_hi = wrt.astype(BF16)
    p["wrt"] = jnp.concatenate([wrt_hi, (wrt - wrt_hi.astype(F32)).astype(BF16)], axis=0)
    br = jnp.concatenate([b_router_expert[l], b_router_group[l]]).astype(F32)
    p["brt"] = jnp.broadcast_to(jnp.pad(br, (0, LANES - br.shape[0]))[:, None], (LANES, MERGE_ROWS))
    return p


def kernel(x, norm_mix, w_in, mla_q_latent_norm, w_mla_uq, mla_kv_latent_norm, w_mla_ukv, mla_q_gain, mla_k_gain, diff_q_gain, diff_k_gain, lambda_q1, lambda_k1, lambda_q2, lambda_k2, diff_subln, w_mla_up, w_diff_up, w_out, norm_ffn, w_router_group, b_router_group, w_router_expert, b_router_expert, w_expert_gate, w_expert_up, w_expert_down):
    batch, seq_len, d = x.shape
    x2 = x.reshape(batch * seq_len, d)
    row = lambda g: g.astype(F32)[None, :]
    for l in range(norm_mix.shape[0]):
        lam_init = 0.8 - 0.6 * math.exp(-0.3 * l)
        p = _layer_params(l, seq_len, norm_mix, w_in, mla_q_latent_norm, w_mla_uq, mla_kv_latent_norm, w_mla_ukv,
                          mla_q_gain, mla_k_gain, diff_q_gain, diff_k_gain, w_mla_up, w_diff_up, w_out, norm_ffn,
                          w_router_group, b_router_group, w_router_expert, b_router_expert)
        qmt, km, vtm, qdt, kd, vtd, sgm, sgd = _proj_call(x2, seq_len, p)
        om = _mla_call(qmt, km, vtm, batch, seq_len)
        od = _diff_call(qdt, kd, vtd, row(lambda_q1[l]), row(lambda_k1[l]), row(lambda_q2[l]), row(lambda_k2[l]),
                        diff_subln[l].astype(F32)[:, None], lam_init, batch, seq_len)
        x1, h2, route, route_t, cnt = _merge_call(x2, om, od, sgm, sgd, p)
        sched, max_rows = _moe_schedule(cnt, x2.shape[0])
        xs = _sort_call(h2, route_t, sched, max_rows)
        ys = _expert_call(xs, w_expert_gate[l], w_expert_up[l], w_expert_down[l], sched)
        x2 = _combine_call(ys, route, x1, sched)
    return x2.reshape(batch, seq_len, d)
```

```python
import functools
import math

import jax
import jax.numpy as jnp
from jax import lax
from jax.experimental import pallas as pl
from jax.experimental.pallas import tpu as pltpu

CHUNK = 64
ROPE_THETA = 500000.0
EPS = 1e-6

MLA_HEADS = 8
MLA_NOPE = 64
MLA_ROPE = 32
MLA_V = 64
MLA_QK = MLA_NOPE + MLA_ROPE
MLA_Q_RANK = 256
MLA_KV_RANK = 128

DIFF_HEADS = 4
DIFF_HEAD_DIM = 64
DIFF_V_DIM = 2 * DIFF_HEAD_DIM
DIFF_ROPE = DIFF_HEAD_DIM // 4
DIFF_QK_WIDTH = DIFF_HEADS * 2 * DIFF_HEAD_DIM
DIFF_V_WIDTH = DIFF_HEADS * DIFF_V_DIM

N_GROUPS = 4
EXPERTS_PER_GROUP = 8
N_EXPERTS = N_GROUPS * EXPERTS_PER_GROUP
EXPERT_FF = 256

LANES = 128
VMEM_LIMIT_BYTES = 48 * 1024 * 1024

PROJ_ROWS = 512
ATTN_Q_ROWS = 512
ATTN_K_ROWS = 256
MERGE_ROWS = 512
ROUTE_ROWS = MERGE_ROWS
SEG_ALIGN = 16
SORT_ROWS = 2 * ROUTE_ROWS + N_EXPERTS * SEG_ALIGN
SORT_ROWS_COMMON = 2 * ROUTE_ROWS + N_EXPERTS * SEG_ALIGN // 2
EXPERT_ROWS = 512
MLA_HEADS_PER_STEP = 4
DIFF_HEADS_PER_STEP = 2
LOG2E = 1.4426950408889634

BF16 = jnp.bfloat16
F32 = jnp.float32


def _dot(a, b):
    return jnp.dot(a, b, preferred_element_type=F32)


def _dot_nt(a, b):
    return lax.dot_general(a, b, (((1,), (1,)), ((), ())), preferred_element_type=F32)


def _rms(x, width):
    return x * lax.rsqrt(jnp.sum(x * x, axis=-1, keepdims=True) * (1.0 / width) + EPS)


def _rotary_partner(y, half):
    lane = lax.broadcasted_iota(jnp.int32, y.shape, 1)
    up = pltpu.roll(y, LANES - half, 1)
    down = pltpu.roll(y, half, 1)
    return jnp.where((lane // half) % 2 == 0, up, down)


def _swap_row_blocks(y, first, half, period):
    parts = []
    for base in range(0, y.shape[0], period):
        a = base + first
        parts += [y[base:a], y[a + half:a + 2 * half], y[a:a + half], y[a + 2 * half:base + period]]
    return jnp.concatenate([p for p in parts if p.shape[0]], axis=0)


def _store_k_tiles(o_ref, vt):
    tk = o_ref.shape[-1]
    for c in range(o_ref.shape[0]):
        o_ref[c] = vt[:, c * tk:(c + 1) * tk].astype(BF16)


def _proj_kernel(x_ref, gmix_ref, wql_ref, wkvl_ref, wkr_ref, wdk_ref, wdqvt_ref, wgm_ref, wgd_ref,
                 gql_ref, wuqt_ref, gkvl_ref, wuk_ref, wuvt_ref, gkn_ref,
                 aq_ref, bq_ref, adq_ref, bdq_ref, ak_ref, bk_ref, adk_ref, bdk_ref,
                 qmt_ref, km_ref, vtm_ref, qdt_ref, kd_ref, vtd_ref, sgm_ref, sgd_ref):
    x = x_ref[...]
    h = (_rms(x, x.shape[-1]) * gmix_ref[...]).astype(BF16)

    ql = (_rms(_dot(h, wql_ref[...]), MLA_Q_RANK) * gql_ref[...]).astype(BF16)
    qt = _dot_nt(wuqt_ref[...], ql)
    aq, bq = aq_ref[...], bq_ref[...]
    for hd in range(MLA_HEADS):
        rows = slice(hd * LANES, (hd + 1) * LANES)
        qh = qt[rows]
        r = lax.rsqrt(jnp.sum(qh * qh, axis=0, keepdims=True) * (1.0 / MLA_QK) + EPS)
        y = (qh * aq + _swap_row_blocks(qh, MLA_NOPE, MLA_ROPE // 2, LANES) * bq) * r
        qmt_ref[0, rows, :] = y.astype(BF16)

    kvl = (_rms(_dot(h, wkvl_ref[...]), MLA_KV_RANK) * gkvl_ref[...]).astype(BF16)
    kr = _dot(h, wkr_ref[...])
    kr_rot = kr * ak_ref[...] + _rotary_partner(kr, MLA_ROPE // 2) * bk_ref[...]
    kr_ss = jnp.sum(kr * kr, axis=-1, keepdims=True)
    kn = _dot(kvl, wuk_ref[...])
    _store_k_tiles(vtm_ref, _dot_nt(wuvt_ref[...], kvl))
    gkn = gkn_ref[...]
    for hd in range(MLA_HEADS):
        sl = slice(hd * LANES, (hd + 1) * LANES)
        knh = kn[:, sl]
        r = lax.rsqrt((jnp.sum(knh * knh, axis=-1, keepdims=True) + kr_ss) * (1.0 / MLA_QK) + EPS)
        km_ref[:, sl] = ((knh * gkn + kr_rot) * r).astype(BF16)

    qvt = _dot_nt(wdqvt_ref[...], h)
    _store_k_tiles(vtd_ref, qvt[DIFF_QK_WIDTH:])
    adq, bdq = adq_ref[...], bdq_ref[...]
    for hd in range(DIFF_HEADS):
        rows = slice(hd * LANES, (hd + 1) * LANES)
        qh = qvt[rows]
        t = qh * adq + _swap_row_blocks(qh, 0, DIFF_ROPE // 2, DIFF_HEAD_DIM) * bdq
        halves = []
        for f in range(2):
            part = qh[f * DIFF_HEAD_DIM:(f + 1) * DIFF_HEAD_DIM]
            r = lax.rsqrt(jnp.sum(part * part, axis=0, keepdims=True) * (1.0 / DIFF_HEAD_DIM) + EPS)
            halves.append(t[f * DIFF_HEAD_DIM:(f + 1) * DIFF_HEAD_DIM] * r)
        qdt_ref[0, rows, :] = jnp.concatenate(halves, axis=0).astype(BF16)

    kd = _dot(h, wdk_ref[...])
    adk, bdk = adk_ref[...], bdk_ref[...]
    for hd in range(DIFF_HEADS):
        sl = slice(hd * LANES, (hd + 1) * LANES)
        th = kd[:, sl]
        lane = lax.broadcasted_iota(jnp.int32, th.shape, 1)
        sq = th * th
        lo = jnp.sum(jnp.where(lane < DIFF_HEAD_DIM, sq, 0.0), axis=-1, keepdims=True)
        tot = jnp.sum(sq, axis=-1, keepdims=True)
        r = lax.rsqrt(jnp.where(lane < DIFF_HEAD_DIM, lo, tot - lo) * (1.0 / DIFF_HEAD_DIM) + EPS)
        kd_ref[:, sl] = ((th * adk + _rotary_partner(th, DIFF_ROPE // 2) * bdk) * r).astype(BF16)

    sgm_ref[...] = jax.nn.sigmoid(_dot(h, wgm_ref[...])).astype(BF16)
    sgd_ref[...] = jax.nn.sigmoid(_dot(h, wgd_ref[...])).astype(BF16)


def _proj_call(x2, seq_len, p):
    n, d = x2.shape
    tm = PROJ_ROWS
    pos_blocks = seq_len // tm
    row = lambda i: (i, 0)
    const = lambda i: (0, 0)
    weights = [p["gmix"], p["wql"], p["wkvl"], p["wkr"], p["wdk"], p["wdqvt"], p["wgm"], p["wgd"],
               p["gql"], p["wuqt"], p["gkvl"], p["wuk"], p["wuvt"], p["gkn"]]
    feature_major_tables = [p["aq"], p["bq"], p["adq"], p["bdq"]]
    token_major_tables = [p["ak"], p["bk"], p["adk"], p["bdk"]]
    in_specs = ([pl.BlockSpec((tm, d), row)]
                + [pl.BlockSpec(w.shape, const) for w in weights]
                + [pl.BlockSpec((LANES, tm), lambda i: (0, i % pos_blocks)) for _ in feature_major_tables]
                + [pl.BlockSpec((tm, LANES), lambda i: (i % pos_blocks, 0)) for _ in token_major_tables])
    tk = ATTN_K_ROWS
    k_tiles = lambda width: (pl.BlockSpec((tm // tk, width, tk), lambda i: (i, 0, 0)),
                             jax.ShapeDtypeStruct((n // tk, width, tk), BF16))
    token_major = lambda width: (pl.BlockSpec((tm, width), row), jax.ShapeDtypeStruct((n, width), BF16))
    assert tm == ATTN_Q_ROWS
    feature_major = lambda width: (pl.BlockSpec((1, width, tm), lambda i: (i, 0, 0)),
                                   jax.ShapeDtypeStruct((n // tm, width, tm), BF16))
    outs = [feature_major(MLA_HEADS * LANES), token_major(MLA_HEADS * LANES), k_tiles(MLA_HEADS * MLA_V),
            feature_major(DIFF_QK_WIDTH), token_major(DIFF_QK_WIDTH), k_tiles(DIFF_V_WIDTH),
            token_major(d), token_major(d)]
    return pl.pallas_call(
        _proj_kernel,
        grid=(n // tm,),
        in_specs=in_specs,
        out_specs=[o[0] for o in outs],
        out_shape=[o[1] for o in outs],
        compiler_params=pltpu.CompilerParams(dimension_semantics=("parallel",), vmem_limit_bytes=VMEM_LIMIT_BYTES),
        name="proj",
    )(x2, *weights, *feature_major_tables, *token_major_tables)


def _chunk_mask_t(tk, width):
    kc = lax.broadcasted_iota(jnp.int32, (tk, width), 0) // CHUNK
    qc = lax.broadcasted_iota(jnp.int32, (tk, width), 1) // CHUNK
    return kc <= qc


ONES_ROWS = 16


def _with_ones_rows(vt):
    return jnp.concatenate([vt, jnp.ones((ONES_ROWS, vt.shape[1]), vt.dtype)], axis=0)


def _softmax_step_t(st, vt_ones, m_ref, acc_ref, lo):
    m_prev = m_ref[:, lo:]
    m_new = jnp.maximum(m_prev, jnp.max(st, axis=0, keepdims=True))
    alpha = jnp.exp2(m_prev - m_new)
    pr = jnp.exp2(st - m_new)
    acc_ref[:, lo:] = alpha * acc_ref[:, lo:] + _dot(vt_ones, pr.astype(BF16))
    m_ref[:, lo:] = m_new


def _normalized(acc_ref, dv):
    acc = acc_ref[...]
    return acc[:dv] / acc[dv:dv + 1]


STATE_REFS = 4


def _attn_scratch(chains, dv, tq, tk):
    per_chain = [pltpu.VMEM((1, tq), F32), pltpu.VMEM((dv + ONES_ROWS, tq), F32),
                 pltpu.VMEM((tk, tq), F32), pltpu.VMEM((tk, tq), F32)]
    return per_chain * chains


def _flash_attention(n_q_tiles, scratch_refs, score_fn, value_fn, finalize_fn, tk, tq):
    ratio = tq // tk
    assert tq == ratio * tk and ratio % 2 == 0
    n_chains = len(scratch_refs) // STATE_REFS
    chains = [scratch_refs[STATE_REFS * c:STATE_REFS * (c + 1)] for c in range(n_chains)]

    def scores(i, t, slot, lo=0):
        for c, ch in enumerate(chains):
            ch[2 + slot][:, lo:] = score_fn(c, i, t, lo)

    def update(t, slot, diag=None):
        lo = 0 if diag is None else diag * tk
        for c, ch in enumerate(chains):
            st = ch[2 + slot][:, lo:]
            if diag is not None:
                st = jnp.where(_chunk_mask_t(tk, tq - lo), st, -jnp.inf)
            _softmax_step_t(st, _with_ones_rows(value_fn(c, t)), ch[0], ch[1], lo)

    scores(0, 0, 0)

    def query_tile(i, carry):
        for m_ref, acc_ref, _, _ in chains:
            m_ref[...] = jnp.full(m_ref.shape, -jnp.inf, F32)
            acc_ref[...] = jnp.zeros(acc_ref.shape, F32)

        def pair(p, c):
            t = 2 * p
            scores(i, t + 1, 1)
            update(t, 0)
            scores(i, t + 2, 0)
            update(t + 1, 1)
            return c

        lax.fori_loop(0, i * (ratio // 2), pair, 0)
        first_diag = ratio * i
        for d in range(ratio):
            if d + 1 < ratio:
                scores(i, first_diag + d + 1, (d + 1) % 2, lo=(d + 1) * tk)
            else:
                scores(jnp.minimum(i + 1, n_q_tiles - 1), 0, 0)
            update(first_diag + d, d % 2, diag=d)
        finalize_fn(i, [ch[1] for ch in chains])
        return carry

    lax.fori_loop(0, n_q_tiles, query_tile, 0)


def _mla_kernel(qt_ref, k_ref, vt_ref, o_ref, *scratch_refs):
    tq, tk = ATTN_Q_ROWS, ATTN_K_ROWS

    def score_fn(c, i, t, lo):
        rows = pl.ds(pl.multiple_of(t * tk, tk), tk)
        sl = slice(c * LANES, (c + 1) * LANES)
        return _dot(k_ref[rows, sl], qt_ref[i, sl, lo:])

    def value_fn(c, t):
        return vt_ref[t, c * MLA_V:(c + 1) * MLA_V, :]

    def finalize_fn(i, accs):
        ot = jnp.concatenate([_normalized(acc_ref, MLA_V) for acc_ref in accs], axis=0)
        o_ref[pl.ds(pl.multiple_of(i * tq, tq), tq), :] = ot.T.astype(BF16)

    _flash_attention(qt_ref.shape[0], scratch_refs, score_fn, value_fn, finalize_fn, tk, tq)


def _mla_call(qmt, km, vtm, batch, seq_len):
    n = km.shape[0]
    tq, tk, hps = ATTN_Q_ROWS, ATTN_K_ROWS, MLA_HEADS_PER_STEP
    return pl.pallas_call(
        _mla_kernel,
        grid=(batch, MLA_HEADS // hps),
        in_specs=[pl.BlockSpec((seq_len // tq, hps * LANES, tq), lambda b, h: (b, h, 0)),
                  pl.BlockSpec((seq_len, hps * LANES), lambda b, h: (b, h)),
                  pl.BlockSpec((seq_len // tk, hps * MLA_V, tk), lambda b, h: (b, h, 0))],
        out_specs=pl.BlockSpec((seq_len, hps * MLA_V), lambda b, h: (b, h)),
        out_shape=jax.ShapeDtypeStruct((n, MLA_HEADS * MLA_V), BF16),
        scratch_shapes=_attn_scratch(hps, MLA_V, tq, tk),
        compiler_params=pltpu.CompilerParams(dimension_semantics=("parallel", "parallel"),
                                             vmem_limit_bytes=VMEM_LIMIT_BYTES),
        name="mla_attn",
    )(qmt, km, vtm)


def _diff_kernel(lam_init, qt_ref, k_ref, vt_ref, lq1_ref, lk1_ref, lq2_ref, lk2_ref, subln_ref, o_ref,
                 *scratch_refs):
    tq, tk = ATTN_Q_ROWS, ATTN_K_ROWS
    hps = DIFF_HEADS_PER_STEP

    def score_fn(c, i, t, lo):
        rows = pl.ds(pl.multiple_of(t * tk, tk), tk)
        hd, f = c // 2, c % 2
        half = qt_ref[i, hd * LANES + f * DIFF_HEAD_DIM:hd * LANES + (f + 1) * DIFF_HEAD_DIM, lo:]
        zero = jnp.zeros_like(half)
        q = jnp.concatenate([half, zero] if f == 0 else [zero, half], axis=0)
        return _dot(k_ref[rows, hd * LANES:(hd + 1) * LANES], q)

    def value_fn(c, t):
        hd = c // 2
        return vt_ref[t, hd * DIFF_V_DIM:(hd + 1) * DIFF_V_DIM, :]

    lam = (jnp.exp(jnp.sum(lq1_ref[...] * lk1_ref[...], axis=-1, keepdims=True))
           - jnp.exp(jnp.sum(lq2_ref[...] * lk2_ref[...], axis=-1, keepdims=True)) + lam_init)
    subln = subln_ref[...] * (1.0 - lam_init)

    def finalize_fn(i, accs):
        heads = []
        for hd in range(hps):
            ot = _normalized(accs[2 * hd], DIFF_V_DIM) - lam * _normalized(accs[2 * hd + 1], DIFF_V_DIM)
            ot = ot * lax.rsqrt(jnp.sum(ot * ot, axis=0, keepdims=True) * (1.0 / DIFF_V_DIM) + EPS)
            heads.append(ot * subln)
        o_ref[pl.ds(pl.multiple_of(i * tq, tq), tq), :] = jnp.concatenate(heads, axis=0).T.astype(BF16)

    _flash_attention(qt_ref.shape[0], scratch_refs, score_fn, value_fn, finalize_fn, tk, tq)


def _diff_call(qdt, kd, vtd, lq1, lk1, lq2, lk2, subln_col, lam_init, batch, seq_len):
    n = kd.shape[0]
    tq, tk, hps = ATTN_Q_ROWS, ATTN_K_ROWS, DIFF_HEADS_PER_STEP
    small = lambda a: pl.BlockSpec(a.shape, lambda b, h: (0, 0))
    return pl.pallas_call(
        functools.partial(_diff_kernel, lam_init),
        grid=(batch, DIFF_HEADS // hps),
        in_specs=[pl.BlockSpec((seq_len // tq, hps * LANES, tq), lambda b, h: (b, h, 0)),
                  pl.BlockSpec((seq_len, hps * LANES), lambda b, h: (b, h)),
                  pl.BlockSpec((seq_len // tk, hps * DIFF_V_DIM, tk), lambda b, h: (b, h, 0)),
                  small(lq1), small(lk1), small(lq2), small(lk2), small(subln_col)],
        out_specs=pl.BlockSpec((seq_len, hps * LANES), lambda b, h: (b, h)),
        out_shape=jax.ShapeDtypeStruct((n, DIFF_V_WIDTH), BF16),
        scratch_shapes=_attn_scratch(2 * hps, DIFF_V_DIM, tq, tk),
        compiler_params=pltpu.CompilerParams(dimension_semantics=("parallel", "parallel"),
                                             vmem_limit_bytes=VMEM_LIMIT_BYTES),
        name="diff_attn",
    )(qdt, kd, vtd, lq1, lk1, lq2, lk2, subln_col)


def _merge_kernel(x_ref, om_ref, od_ref, sgm_ref, sgd_ref, wmu_ref, wdu_ref, wout_ref, gffn_ref, wrt_ref,
                  brt_ref, x1_ref, h2_ref, route_ref, route_t_ref, cnt_ref):
    merged = (sgm_ref[...].astype(F32) * _dot(om_ref[...], wmu_ref[...])
              + sgd_ref[...].astype(F32) * _dot(od_ref[...], wdu_ref[...]))
    x1 = x_ref[...] + _dot(merged.astype(BF16), wout_ref[...])
    x1_ref[...] = x1
    h2 = _rms(x1, x1.shape[-1]) * gffn_ref[...]
    h2_hi = h2.astype(BF16)
    h2_ref[...] = h2_hi
    tm = h2.shape[0]

    h2_lo = (h2 - h2_hi.astype(F32)).astype(BF16)
    by_hi = _dot_nt(wrt_ref[...], h2_hi)
    logits = by_hi[:LANES] + by_hi[LANES:] + _dot_nt(wrt_ref[:LANES, :], h2_lo) + brt_ref[...]
    row = lax.broadcasted_iota(jnp.int32, logits.shape, 0)
    neg = -jnp.inf
    big = jnp.int32(1 << 20)

    def top(vals):
        mx = jnp.max(vals, axis=0, keepdims=True)
        idx = jnp.min(jnp.where(vals == mx, row, big), axis=0, keepdims=True)
        return mx, idx

    gl = jnp.where((row >= N_EXPERTS) & (row < N_EXPERTS + N_GROUPS), logits, neg)
    gmax, gidx = top(gl)
    pg_sel = 1.0 / jnp.sum(jnp.exp(gl - gmax), axis=0, keepdims=True)
    el = jnp.where((row < N_EXPERTS) & (row // EXPERTS_PER_GROUP == gidx - N_EXPERTS), logits, neg)
    m1, i1 = top(el)
    m2, i2 = top(jnp.where(row == i1, neg, el))
    e2 = jnp.exp(m2 - m1)
    w1 = pg_sel / (1.0 + e2)
    w2 = w1 * e2

    sel = jnp.where((row == i1) | (row == i2), 1.0, 0.0).astype(BF16)
    t_row = lax.broadcasted_iota(jnp.int32, (tm, tm), 0)
    t_col = lax.broadcasted_iota(jnp.int32, (tm, tm), 1)
    rank = _dot(sel, jnp.where(t_row < t_col, 1.0, 0.0).astype(BF16))
    cnt = _dot(sel, jnp.ones((tm, tm), BF16))
    seg = jnp.floor((cnt + (SEG_ALIGN - 1)) * (1.0 / SEG_ALIGN))
    e_row = lax.broadcasted_iota(jnp.int32, (LANES, LANES), 0)
    e_col = lax.broadcasted_iota(jnp.int32, (LANES, LANES), 1)
    off = _dot(jnp.where(e_col < e_row, 1.0, 0.0).astype(BF16), seg.astype(BF16)) * SEG_ALIGN
    dest = off + rank
    d1 = jnp.sum(jnp.where(row == i1, dest, 0.0), axis=0, keepdims=True)
    d2 = jnp.sum(jnp.where(row == i2, dest, 0.0), axis=0, keepdims=True)
    route_t = jnp.where(row == 0, d1, jnp.where(row == 1, d2, jnp.where(row == 2, w1, jnp.where(row == 3, w2, 0.0))))
    route_t_ref[0] = route_t[0:8]
    route_ref[...] = route_t.T
    cnt_ref[0] = (seg[:, :LANES] * SEG_ALIGN).T[0:1]


def _merge_call(x2, om, od, sgm, sgd, p):
    n, d = x2.shape
    tm = MERGE_ROWS
    row = lambda i: (i, 0)
    const = lambda i: (0, 0)
    weights = [p["wmu"], p["wdu"], p["wout"], p["gffn"], p["wrt"], p["brt"]]
    return pl.pallas_call(
        _merge_kernel,
        grid=(n // tm,),
        in_specs=([pl.BlockSpec((tm, a.shape[1]), row) for a in (x2, om, od, sgm, sgd)]
                  + [pl.BlockSpec(w.shape, const) for w in weights]),
        out_specs=[pl.BlockSpec((tm, d), row), pl.BlockSpec((tm, d), row), pl.BlockSpec((tm, LANES), row),
                   pl.BlockSpec((1, 8, tm), lambda i: (i, 0, 0)), pl.BlockSpec((1, 1, LANES), lambda i: (i, 0, 0))],
        out_shape=[jax.ShapeDtypeStruct((n, d), F32), jax.ShapeDtypeStruct((n, d), BF16),
                   jax.ShapeDtypeStruct((n, LANES), F32), jax.ShapeDtypeStruct((n // tm, 8, tm), F32),
                   jax.ShapeDtypeStruct((n // tm, 1, LANES), F32)],
        compiler_params=pltpu.CompilerParams(dimension_semantics=("parallel",), vmem_limit_bytes=VMEM_LIMIT_BYTES),
        name="merge_router",
    )(x2, om, od, sgm, sgd, *weights)


def _segment_copies(i, seg_dst_ref, seg_rows_ref, tile_off_ref, global_ref, tile_ref, sem, to_global):
    def body(e, carry):
        k = i * N_EXPERTS + e
        rows = pl.multiple_of(seg_rows_ref[k], SEG_ALIGN)

        @pl.when(rows > 0)
        def _():
            g = global_ref.at[pl.ds(pl.multiple_of(seg_dst_ref[k], SEG_ALIGN), rows)]
            t = tile_ref.at[pl.ds(pl.multiple_of(tile_off_ref[k], SEG_ALIGN), rows)]
            src, dst = (t, g) if to_global else (g, t)
            pltpu.make_async_copy(src, dst, sem).start()

        return carry

    lax.fori_loop(0, N_EXPERTS, body, 0)


def _wait_rows(tile_ref, rows, sem):
    @pl.when(rows > 0)
    def _():
        view = tile_ref.at[pl.ds(0, pl.multiple_of(rows, SEG_ALIGN))]
        pltpu.make_async_copy(view, view, sem).wait()


def _zero_unused_rows(tail_dst_ref, tail_rows_ref, n_used_ref, xs_ref, zero_ref, sem, start):
    n_tiles = xs_ref.shape[0] // EXPERT_ROWS
    if start:
        zero_ref[...] = jnp.zeros(zero_ref.shape, BF16)

    def tail(e, total):
        rows = pl.multiple_of(tail_rows_ref[e], SEG_ALIGN)
        if start:
            @pl.when(rows > 0)
            def _():
                dst = xs_ref.at[pl.ds(pl.multiple_of(tail_dst_ref[e], SEG_ALIGN), rows)]
                pltpu.make_async_copy(zero_ref.at[pl.ds(0, rows)], dst, sem).start()

        return total + rows

    total = lax.fori_loop(0, N_EXPERTS, tail, 0)
    if not start:
        _wait_rows(xs_ref, total + (n_tiles - n_used_ref[0]) * EXPERT_ROWS, sem)
        return

    def unused(t, carry):
        dst = xs_ref.at[pl.ds(pl.multiple_of(t * EXPERT_ROWS, EXPERT_ROWS), EXPERT_ROWS)]
        pltpu.make_async_copy(zero_ref, dst, sem).start()
        return carry

    lax.fori_loop(n_used_ref[0], n_tiles, unused, 0)


def _sort_kernel(seg_dst_ref, seg_rows_ref, tile_off_ref, tile_rows_ref, tail_dst_ref, tail_rows_ref, n_used_ref,
                 h2_ref, route_t_ref, xs_ref, sorted_ref, zero_ref, sem, zero_sem):
    i = pl.program_id(0)
    tm = h2_ref.shape[0]

    @pl.when(i == 0)
    def _():
        _zero_unused_rows(tail_dst_ref, tail_rows_ref, n_used_ref, xs_ref, zero_ref, zero_sem, True)

    d1 = route_t_ref[0, 0:1, :].astype(jnp.int32)
    d2 = route_t_ref[0, 1:2, :].astype(jnp.int32)
    slot = i % 2

    def sort_rows(n_rows):
        r = lax.broadcasted_iota(jnp.int32, (n_rows, tm), 0)
        perm = jnp.where((r == d1) | (r == d2), 1.0, 0.0).astype(BF16)
        sorted_ref[slot, 0:n_rows] = _dot(perm, h2_ref[...]).astype(BF16)

    @pl.when(tile_rows_ref[i] <= SORT_ROWS_COMMON)
    def _():
        sort_rows(SORT_ROWS_COMMON)

    @pl.when(tile_rows_ref[i] > SORT_ROWS_COMMON)
    def _():
        sort_rows(SORT_ROWS)

    _segment_copies(i, seg_dst_ref, seg_rows_ref, tile_off_ref, xs_ref, sorted_ref.at[slot], sem.at[slot], True)

    @pl.when(i > 0)
    def _():
        _wait_rows(sorted_ref.at[1 - slot], tile_rows_ref[jnp.maximum(i - 1, 0)], sem.at[1 - slot])

    @pl.when(i == pl.num_programs(0) - 1)
    def _():
        _wait_rows(sorted_ref.at[slot], tile_rows_ref[i], sem.at[slot])
        _zero_unused_rows(tail_dst_ref, tail_rows_ref, n_used_ref, xs_ref, zero_ref, zero_sem, False)


def _sort_call(h2, route_t, sched, max_rows):
    n, d = h2.shape
    tm = ROUTE_ROWS
    return pl.pallas_call(
        _sort_kernel,
        grid_spec=pltpu.PrefetchScalarGridSpec(
            num_scalar_prefetch=7,
            grid=(n // tm,),
            in_specs=[pl.BlockSpec((tm, d), lambda i, *_: (i, 0)),
                      pl.BlockSpec((1, 8, tm), lambda i, *_: (i, 0, 0))],
            out_specs=pl.BlockSpec(memory_space=pl.ANY),
            scratch_shapes=[pltpu.VMEM((2, SORT_ROWS, d), BF16), pltpu.VMEM((EXPERT_ROWS, d), BF16),
                            pltpu.SemaphoreType.DMA((2,)), pltpu.SemaphoreType.DMA(())],
        ),
        out_shape=jax.ShapeDtypeStruct((max_rows, d), BF16),
        compiler_params=pltpu.CompilerParams(dimension_semantics=("arbitrary",), vmem_limit_bytes=VMEM_LIMIT_BYTES),
        name="moe_sort",
    )(sched["seg_dst"], sched["seg_rows"], sched["tile_off"], sched["tile_rows"], sched["tail_dst"],
      sched["tail_rows"], sched["n_used"], h2, route_t)


def _expert_kernel(tile_expert_ref, n_used_ref, xs_ref, wg_ref, wu_ref, wd_ref, ys_ref, wg_bf, wu_bf, wd_bf):
    t = pl.program_id(0)
    used = t < n_used_ref[0]

    @pl.when(used & ((t == 0) | (tile_expert_ref[t] != tile_expert_ref[jnp.maximum(t - 1, 0)])))
    def _():
        wg_bf[...] = wg_ref[0].astype(BF16)
        wu_bf[...] = wu_ref[0].astype(BF16)
        wd_bf[...] = wd_ref[0].astype(BF16)

    @pl.when(used)
    def _():
        xs = xs_ref[...]
        gate = _dot(xs, wg_bf[...])
        up = _dot(xs, wu_bf[...])
        hidden = (gate * jax.nn.sigmoid(gate) * up).astype(BF16)
        ys_ref[...] = _dot(hidden, wd_bf[...]).astype(BF16)

    @pl.when(jnp.logical_not(used))
    def _():
        ys_ref[...] = jnp.zeros(ys_ref.shape, BF16)


def _expert_call(xs, wg, wu, wd, sched):
    rows, d = xs.shape
    tr = EXPERT_ROWS
    blk = lambda t, te, nu: (jnp.minimum(t, nu[0] - 1), 0)
    wsel = lambda t, te, nu: (te[jnp.minimum(t, nu[0] - 1)], 0, 0)
    return pl.pallas_call(
        _expert_kernel,
        grid_spec=pltpu.PrefetchScalarGridSpec(
            num_scalar_prefetch=2,
            grid=(rows // tr,),
            in_specs=[pl.BlockSpec((tr, d), blk),
                      pl.BlockSpec((1, d, EXPERT_FF), wsel), pl.BlockSpec((1, d, EXPERT_FF), wsel),
                      pl.BlockSpec((1, EXPERT_FF, d), wsel)],
            out_specs=pl.BlockSpec((tr, d), lambda t, te, nu: (t, 0)),
            scratch_shapes=[pltpu.VMEM((d, EXPERT_FF), BF16), pltpu.VMEM((d, EXPERT_FF), BF16),
                            pltpu.VMEM((EXPERT_FF, d), BF16)],
        ),
        out_shape=jax.ShapeDtypeStruct((rows, d), BF16),
        compiler_params=pltpu.CompilerParams(dimension_semantics=("arbitrary",), vmem_limit_bytes=VMEM_LIMIT_BYTES),
        name="moe_experts",
    )(sched["tile_expert"], sched["n_used"], xs, wg, wu, wd)


def _combine_kernel(seg_dst_ref, seg_rows_ref, tile_off_ref, tile_rows_ref, ys_ref, route_ref, x1_ref, o_ref,
                    buf_ref, sem):
    i = pl.program_id(0)
    tm = x1_ref.shape[0]
    slot = i % 2

    def fetch(tile, into):
        buf_ref[into] = jnp.zeros(buf_ref.shape[1:], BF16)
        _segment_copies(tile, seg_dst_ref, seg_rows_ref, tile_off_ref, ys_ref, buf_ref.at[into], sem.at[into], False)

    @pl.when(i == 0)
    def _():
        fetch(i, slot)

    @pl.when(i + 1 < pl.num_programs(0))
    def _():
        fetch(i + 1, 1 - slot)

    route = route_ref[...]
    d1 = route[:, 0:1].astype(jnp.int32)
    d2 = route[:, 1:2].astype(jnp.int32)
    w1 = route[:, 2:3]
    w2 = route[:, 3:4]
    _wait_rows(buf_ref.at[slot], tile_rows_ref[i], sem.at[slot])

    def combine_rows(n_rows):
        r = lax.broadcasted_iota(jnp.int32, (tm, n_rows), 1)
        weights = (jnp.where(r == d1, w1, 0.0) + jnp.where(r == d2, w2, 0.0)).astype(BF16)
        o_ref[...] = x1_ref[...] + _dot(weights, buf_ref[slot, 0:n_rows])

    @pl.when(tile_rows_ref[i] <= SORT_ROWS_COMMON)
    def _():
        combine_rows(SORT_ROWS_COMMON)

    @pl.when(tile_rows_ref[i] > SORT_ROWS_COMMON)
    def _():
        combine_rows(SORT_ROWS)


def _combine_call(ys, route, x1, sched):
    n, d = x1.shape
    tm = ROUTE_ROWS
    return pl.pallas_call(
        _combine_kernel,
        grid_spec=pltpu.PrefetchScalarGridSpec(
            num_scalar_prefetch=4,
            grid=(n // tm,),
            in_specs=[pl.BlockSpec(memory_space=pl.ANY),
                      pl.BlockSpec((tm, LANES), lambda i, *_: (i, 0)),
                      pl.BlockSpec((tm, d), lambda i, *_: (i, 0))],
            out_specs=pl.BlockSpec((tm, d), lambda i, *_: (i, 0)),
            scratch_shapes=[pltpu.VMEM((2, SORT_ROWS, d), BF16), pltpu.SemaphoreType.DMA((2,))],
        ),
        out_shape=jax.ShapeDtypeStruct((n, d), F32),
        compiler_params=pltpu.CompilerParams(dimension_semantics=("arbitrary",), vmem_limit_bytes=VMEM_LIMIT_BYTES),
        name="moe_combine",
    )(sched["seg_dst"], sched["seg_rows"], sched["tile_off"], sched["tile_rows"], ys, route, x1)


def _schedule_kernel(cnt_ref, seg_dst_ref, tile_off_ref, tile_rows_ref, misc_ref):
    hp = functools.partial(jnp.dot, preferred_element_type=F32, precision=lax.Precision.HIGHEST)
    cnt = cnt_ref[...]
    n_tiles = cnt.shape[0]
    tile_before = jnp.where(lax.broadcasted_iota(jnp.int32, (n_tiles, n_tiles), 1)
                            < lax.broadcasted_iota(jnp.int32, (n_tiles, n_tiles), 0), 1.0, 0.0)
    expert_before = jnp.where(lax.broadcasted_iota(jnp.int32, (LANES, LANES), 0)
                              < lax.broadcasted_iota(jnp.int32, (LANES, LANES), 1), 1.0, 0.0)
    expert_rows = jnp.sum(cnt, axis=0, keepdims=True)
    region = jnp.floor((expert_rows + (EXPERT_ROWS - 1)) * (1.0 / EXPERT_ROWS)) * EXPERT_ROWS
    region_start = hp(jnp.broadcast_to(region, (8, LANES)), expert_before)[0:1]
    seg_dst_ref[...] = (region_start + hp(tile_before, cnt)).astype(jnp.int32)
    tile_off_ref[...] = hp(cnt, expert_before).astype(jnp.int32)
    tile_rows_ref[...] = jnp.broadcast_to(jnp.sum(cnt, axis=-1, keepdims=True), cnt.shape).astype(jnp.int32)
    n_used = jnp.sum(region, axis=-1, keepdims=True) * (1.0 / EXPERT_ROWS)
    row = lax.broadcasted_iota(jnp.int32, (8, LANES), 0)
    misc = jnp.where(row == 0, region_start + expert_rows,
                     jnp.where(row == 1, region - expert_rows,
                               jnp.where(row == 2, region_start + region, n_used)))
    misc_ref[...] = misc.astype(jnp.int32)


def _moe_schedule(cnt, n_tokens):
    n_tiles = cnt.shape[0]
    table = jax.ShapeDtypeStruct((n_tiles, LANES), jnp.int32)
    seg_dst, tile_off, tile_rows, misc = pl.pallas_call(
        _schedule_kernel,
        out_shape=[table, table, table, jax.ShapeDtypeStruct((8, LANES), jnp.int32)],
        name="moe_schedule",
    )(cnt.reshape(n_tiles, LANES))
    max_rows = 2 * n_tokens + n_tiles * N_EXPERTS * (SEG_ALIGN - 1) + N_EXPERTS * (EXPERT_ROWS - 1)
    max_tiles = -(-max_rows // EXPERT_ROWS)
    tile_start = jnp.arange(max_tiles, dtype=jnp.int32) * EXPERT_ROWS
    region_end = misc[2, :N_EXPERTS]
    tile_expert = jnp.minimum(jnp.sum((region_end[None, :] <= tile_start[:, None]).astype(jnp.int32), axis=1),
                              N_EXPERTS - 1)
    flat = lambda a: a[:, :N_EXPERTS].reshape(-1)
    sched = {
        "seg_dst": flat(seg_dst),
        "seg_rows": flat(cnt.reshape(n_tiles, LANES).astype(jnp.int32)),
        "tile_off": flat(tile_off),
        "tile_rows": tile_rows[:, 0],
        "tail_dst": misc[0, :N_EXPERTS],
        "tail_rows": misc[1, :N_EXPERTS],
        "tile_expert": tile_expert,
        "n_used": misc[3, :1],
    }
    return sched, max_tiles * EXPERT_ROWS


def _rotary_tables(seq_len, rot_dim, period, first, gain, scale):
    half = rot_dim // 2
    pos = jnp.arange(seq_len, dtype=F32)
    inv = 1.0 / (ROPE_THETA ** (jnp.arange(0, rot_dim, 2, dtype=F32) / rot_dim))
    ang = pos[:, None] * inv[None, :]
    cos, sin = jnp.cos(ang), jnp.sin(ang)
    lane = jnp.arange(LANES)
    rel = (lane % period) - first
    active = (rel >= 0) & (rel < rot_dim)
    idx = jnp.clip(rel, 0, rot_dim - 1) % half
    sign = jnp.where(rel < half, -1.0, 1.0)
    partner = jnp.where(active, jnp.where(rel < half, lane + half, lane - half), lane)
    c = jnp.where(active[None, :], cos[:, idx], 1.0)
    s = jnp.where(active[None, :], sin[:, idx] * sign[None, :], 0.0)
    gain = gain.astype(F32)
    return (c * gain[None, :] * scale).astype(F32), (s * gain[partner][None, :] * scale).astype(F32)


def _head_pad(w, heads, width):
    r = w.shape[0]
    w = w.reshape(r, heads, width)
    return jnp.pad(w, ((0, 0), (0, 0), (0, LANES - width))).reshape(r, heads * LANES)


def _layer_params(l, seq_len, norm_mix, w_in, mla_q_latent_norm, w_mla_uq, mla_kv_latent_norm, w_mla_ukv,
                  mla_q_gain, mla_k_gain, diff_q_gain, diff_k_gain, w_mla_up, w_diff_up, w_out, norm_ffn,
                  w_router_group, b_router_group, w_router_expert, b_router_expert):
    d = w_in.shape[1]
    sizes = (MLA_Q_RANK, MLA_KV_RANK, MLA_ROPE, DIFF_QK_WIDTH, DIFF_QK_WIDTH, DIFF_V_WIDTH, d, d)
    offs = [0]
    for s in sizes:
        offs.append(offs[-1] + s)
    wi = w_in[l]
    seg = [wi[:, offs[k]:offs[k + 1]] for k in range(len(sizes))]
    row = lambda g: g.astype(F32)[None, :]
    p = {}
    p["gmix"] = row(norm_mix[l])
    p["wql"] = seg[0].astype(BF16)
    p["wkvl"] = seg[1].astype(BF16)
    p["wkr"] = jnp.pad(seg[2], ((0, 0), (MLA_NOPE, LANES - MLA_QK))).astype(BF16)
    p["wdk"] = seg[4].astype(BF16)
    p["wdqvt"] = jnp.concatenate([seg[3].T, seg[5].T], axis=0).astype(BF16)
    p["wgm"], p["wgd"] = seg[6].astype(BF16), seg[7].astype(BF16)
    p["gql"] = row(mla_q_latent_norm[l])
    p["wuqt"] = _head_pad(w_mla_uq[l], MLA_HEADS, MLA_QK).T.astype(BF16)
    p["gkvl"] = row(mla_kv_latent_norm[l])
    ukv = w_mla_ukv[l].reshape(MLA_KV_RANK, MLA_HEADS, MLA_NOPE + MLA_V)
    p["wuk"] = _head_pad(ukv[:, :, :MLA_NOPE].reshape(MLA_KV_RANK, -1), MLA_HEADS, MLA_NOPE).astype(BF16)
    p["wuvt"] = ukv[:, :, MLA_NOPE:].reshape(MLA_KV_RANK, -1).T.astype(BF16)
    gq = jnp.pad(mla_q_gain[l], (0, LANES - MLA_QK))
    gk = jnp.pad(mla_k_gain[l], (0, LANES - MLA_QK))
    nope = jnp.arange(LANES) < MLA_NOPE
    p["gkn"] = jnp.where(nope, gk, 0.0).astype(F32)[None, :]
    aq, bq = _rotary_tables(seq_len, MLA_ROPE, LANES, MLA_NOPE, gq, LOG2E * MLA_QK ** -0.5)
    p["aq"], p["bq"] = aq.T, bq.T
    ak, bk = _rotary_tables(seq_len, MLA_ROPE, LANES, MLA_NOPE, jnp.where(nope, 0.0, gk), 1.0)
    p["ak"], p["bk"] = ak, bk
    adq, bdq = _rotary_tables(seq_len, DIFF_ROPE, DIFF_HEAD_DIM, 0, jnp.tile(diff_q_gain[l], 2),
                              LOG2E * DIFF_HEAD_DIM ** -0.5)
    p["adq"], p["bdq"] = adq.T, bdq.T
    p["adk"], p["bdk"] = _rotary_tables(seq_len, DIFF_ROPE, DIFF_HEAD_DIM, 0, jnp.tile(diff_k_gain[l], 2), 1.0)
    p["wmu"] = w_mla_up[l].astype(BF16)
    p["wdu"] = w_diff_up[l].astype(BF16)
    p["wout"] = w_out[l].astype(BF16)
    p["gffn"] = row(norm_ffn[l])
    wr = jnp.concatenate([w_router_expert[l], w_router_group[l]], axis=1).astype(F32)
    wrt = jnp.pad(wr, ((0, 0), (0, LANES - wr.shape[1]))).T
    wrt_hi = wrt.astype(BF16)
    p["wrt"] = jnp.concatenate([wrt_hi, (wrt - wrt_hi.astype(F32)).astype(BF16)], axis=0)
    br = jnp.concatenate([b_router_expert[l], b_router_group[l]]).astype(F32)
    p["brt"] = jnp.broadcast_to(jnp.pad(br, (0, LANES - br.shape[0]))[:, None], (LANES, MERGE_ROWS))
    return p


def kernel(x, norm_mix, w_in, mla_q_latent_norm, w_mla_uq, mla_kv_latent_norm, w_mla_ukv, mla_q_gain, mla_k_gain, diff_q_gain, diff_k_gain, lambda_q1, lambda_k1, lambda_q2, lambda_k2, diff_subln, w_mla_up, w_diff_up, w_out, norm_ffn, w_router_group, b_router_group, w_router_expert, b_router_expert, w_expert_gate, w_expert_up, w_expert_down):
    batch, seq_len, d = x.shape
    x2 = x.reshape(batch * seq_len, d)
    row = lambda g: g.astype(F32)[None, :]
    for l in range(norm_mix.shape[0]):
        lam_init = 0.8 - 0.6 * math.exp(-0.3 * l)
        p = _layer_params(l, seq_len, norm_mix, w_in, mla_q_latent_norm, w_mla_uq, mla_kv_latent_norm, w_mla_ukv,
                          mla_q_gain, mla_k_gain, diff_q_gain, diff_k_gain, w_mla_up, w_diff_up, w_out, norm_ffn,
                          w_router_group, b_router_group, w_router_expert, b_router_expert)
        qmt, km, vtm, qdt, kd, vtd, sgm, sgd = _proj_call(x2, seq_len, p)
        om = _mla_call(qmt, km, vtm, batch, seq_len)
        od = _diff_call(qdt, kd, vtd, row(lambda_q1[l]), row(lambda_k1[l]), row(lambda_q2[l]), row(lambda_k2[l]),
                        diff_subln[l].astype(F32)[:, None], lam_init, batch, seq_len)
        x1, h2, route, route_t, cnt = _merge_call(x2, om, od, sgm, sgd, p)
        sched, max_rows = _moe_schedule(cnt, x2.shape[0])
        xs = _sort_call(h2, route_t, sched, max_rows)
        ys = _expert_call(xs, w_expert_gate[l], w_expert_up[l], w_expert_down[l], sched)
        x2 = _combine_call(ys, route, x1, sched)
    return x2.reshape(batch, seq_len, d)
```

```python
import functools
import math

import jax
import jax.numpy as jnp
from jax import lax
from jax.experimental import pallas as pl
from jax.experimental.pallas import tpu as pltpu

CHUNK = 64
ROPE_THETA = 500000.0
EPS = 1e-6

MLA_HEADS = 8
MLA_NOPE = 64
MLA_ROPE = 32
MLA_V = 64
MLA_QK = MLA_NOPE + MLA_ROPE
MLA_Q_RANK = 256
MLA_KV_RANK = 128

DIFF_HEADS = 4
DIFF_HEAD_DIM = 64
DIFF_V_DIM = 2 * DIFF_HEAD_DIM
DIFF_ROPE = DIFF_HEAD_DIM // 4
DIFF_QK_WIDTH = DIFF_HEADS * 2 * DIFF_HEAD_DIM
DIFF_V_WIDTH = DIFF_HEADS * DIFF_V_DIM

N_GROUPS = 4
EXPERTS_PER_GROUP = 8
N_EXPERTS = N_GROUPS * EXPERTS_PER_GROUP
EXPERT_FF = 256

LANES = 128
VMEM_LIMIT_BYTES = 48 * 1024 * 1024

PROJ_ROWS = 512
ATTN_Q_ROWS = 512
ATTN_K_ROWS = 256
MERGE_ROWS = 512
ROUTE_ROWS = MERGE_ROWS
SEG_ALIGN = 16
SORT_ROWS = 2 * ROUTE_ROWS + N_EXPERTS * SEG_ALIGN
SORT_ROWS_COMMON = 2 * ROUTE_ROWS + N_EXPERTS * SEG_ALIGN // 2
EXPERT_ROWS = 512
MLA_HEADS_PER_STEP = 4
DIFF_HEADS_PER_STEP = 2
LOG2E = 1.4426950408889634

BF16 = jnp.bfloat16
F32 = jnp.float32


def _dot(a, b):
    return jnp.dot(a, b, preferred_element_type=F32)


def _dot_nt(a, b):
    return lax.dot_general(a, b, (((1,), (1,)), ((), ())), preferred_element_type=F32)


def _dot_tn(w, x):
    return lax.dot_general(w, x, (((0,), (1,)), ((), ())), preferred_element_type=F32)


def _rms(x, width):
    return x * lax.rsqrt(jnp.sum(x * x, axis=-1, keepdims=True) * (1.0 / width) + EPS)


def _rotary_partner(y, half):
    lane = lax.broadcasted_iota(jnp.int32, y.shape, 1)
    up = pltpu.roll(y, LANES - half, 1)
    down = pltpu.roll(y, half, 1)
    return jnp.where((lane // half) % 2 == 0, up, down)


def _swap_row_blocks(y, first, half, period):
    parts = []
    for base in range(0, y.shape[0], period):
        a = base + first
        parts += [y[base:a], y[a + half:a + 2 * half], y[a:a + half], y[a + 2 * half:base + period]]
    return jnp.concatenate([p for p in parts if p.shape[0]], axis=0)


def _store_k_tiles(o_ref, vt):
    tk = o_ref.shape[-1]
    for c in range(o_ref.shape[0]):
        o_ref[c] = vt[:, c * tk:(c + 1) * tk].astype(BF16)


def _proj_kernel(x_ref, gmix_ref, wql_ref, wkvl_ref, wkr_ref, wdk_ref, wdqv_ref, wgm_ref, wgd_ref,
                 gql_ref, wuq_ref, gkvl_ref, wuk_ref, wuv_ref, gkn_ref,
                 aq_ref, bq_ref, adq_ref, bdq_ref, ak_ref, bk_ref, adk_ref, bdk_ref,
                 qmt_ref, km_ref, vtm_ref, qdt_ref, kd_ref, vtd_ref, sgm_ref, sgd_ref):
    x = x_ref[...]
    h = (_rms(x, x.shape[-1]) * gmix_ref[...]).astype(BF16)

    ql = (_rms(_dot(h, wql_ref[...]), MLA_Q_RANK) * gql_ref[...]).astype(BF16)
    qt = _dot_tn(wuq_ref[...], ql)
    aq, bq = aq_ref[...], bq_ref[...]
    for hd in range(MLA_HEADS):
        rows = slice(hd * LANES, (hd + 1) * LANES)
        qh = qt[rows]
        r = lax.rsqrt(jnp.sum(qh * qh, axis=0, keepdims=True) * (1.0 / MLA_QK) + EPS)
        y = (qh * aq + _swap_row_blocks(qh, MLA_NOPE, MLA_ROPE // 2, LANES) * bq) * r
        qmt_ref[0, rows, :] = y.astype(BF16)

    kvl = (_rms(_dot(h, wkvl_ref[...]), MLA_KV_RANK) * gkvl_ref[...]).astype(BF16)
    kr = _dot(h, wkr_ref[...])
    kr_rot = kr * ak_ref[...] + _rotary_partner(kr, MLA_ROPE // 2) * bk_ref[...]
    kr_ss = jnp.sum(kr * kr, axis=-1, keepdims=True)
    kn = _dot(kvl, wuk_ref[...])
    _store_k_tiles(vtm_ref, _dot_tn(wuv_ref[...], kvl))
    gkn = gkn_ref[...]
    for hd in range(MLA_HEADS):
        sl = slice(hd * LANES, (hd + 1) * LANES)
        knh = kn[:, sl]
        r = lax.rsqrt((jnp.sum(knh * knh, axis=-1, keepdims=True) + kr_ss) * (1.0 / MLA_QK) + EPS)
        km_ref[:, sl] = ((knh * gkn + kr_rot) * r).astype(BF16)

    qvt = _dot_tn(wdqv_ref[...], h)
    _store_k_tiles(vtd_ref, qvt[DIFF_QK_WIDTH:])
    adq, bdq = adq_ref[...], bdq_ref[...]
    for hd in range(DIFF_HEADS):
        rows = slice(hd * LANES, (hd + 1) * LANES)
        qh = qvt[rows]
        t = qh * adq + _swap_row_blocks(qh, 0, DIFF_ROPE // 2, DIFF_HEAD_DIM) * bdq
        halves = []
        for f in range(2):
            part = qh[f * DIFF_HEAD_DIM:(f + 1) * DIFF_HEAD_DIM]
            r = lax.rsqrt(jnp.sum(part * part, axis=0, keepdims=True) * (1.0 / DIFF_HEAD_DIM) + EPS)
            halves.append(t[f * DIFF_HEAD_DIM:(f + 1) * DIFF_HEAD_DIM] * r)
        qdt_ref[0, rows, :] = jnp.concatenate(halves, axis=0).astype(BF16)

    kd = _dot(h, wdk_ref[...])
    adk, bdk = adk_ref[...], bdk_ref[...]
    for hd in range(DIFF_HEADS):
        sl = slice(hd * LANES, (hd + 1) * LANES)
        th = kd[:, sl]
        lane = lax.broadcasted_iota(jnp.int32, th.shape, 1)
        sq = th * th
        lo = jnp.sum(jnp.where(lane < DIFF_HEAD_DIM, sq, 0.0), axis=-1, keepdims=True)
        tot = jnp.sum(sq, axis=-1, keepdims=True)
        r = lax.rsqrt(jnp.where(lane < DIFF_HEAD_DIM, lo, tot - lo) * (1.0 / DIFF_HEAD_DIM) + EPS)
        kd_ref[:, sl] = ((th * adk + _rotary_partner(th, DIFF_ROPE // 2) * bdk) * r).astype(BF16)

    sgm_ref[...] = jax.nn.sigmoid(_dot(h, wgm_ref[...])).astype(BF16)
    sgd_ref[...] = jax.nn.sigmoid(_dot(h, wgd_ref[...])).astype(BF16)


def _proj_call(x2, seq_len, p):
    n, d = x2.shape
    tm = PROJ_ROWS
    pos_blocks = seq_len // tm
    row = lambda i: (i, 0)
    const = lambda i: (0, 0)
    weights = [p["gmix"], p["wql"], p["wkvl"], p["wkr"], p["wdk"], p["wdqv"], p["wgm"], p["wgd"],
               p["gql"], p["wuq"], p["gkvl"], p["wuk"], p["wuv"], p["gkn"]]
    feature_major_tables = [p["aq"], p["bq"], p["adq"], p["bdq"]]
    token_major_tables = [p["ak"], p["bk"], p["adk"], p["bdk"]]
    in_specs = ([pl.BlockSpec((tm, d), row)]
                + [pl.BlockSpec(w.shape, const) for w in weights]
                + [pl.BlockSpec((LANES, tm), lambda i: (0, i % pos_blocks)) for _ in feature_major_tables]
                + [pl.BlockSpec((tm, LANES), lambda i: (i % pos_blocks, 0)) for _ in token_major_tables])
    tk = ATTN_K_ROWS
    k_tiles = lambda width: (pl.BlockSpec((tm // tk, width, tk), lambda i: (i, 0, 0)),
                             jax.ShapeDtypeStruct((n // tk, width, tk), BF16))
    token_major = lambda width: (pl.BlockSpec((tm, width), row), jax.ShapeDtypeStruct((n, width), BF16))
    assert tm == ATTN_Q_ROWS
    feature_major = lambda width: (pl.BlockSpec((1, width, tm), lambda i: (i, 0, 0)),
                                   jax.ShapeDtypeStruct((n // tm, width, tm), BF16))
    outs = [feature_major(MLA_HEADS * LANES), token_major(MLA_HEADS * LANES), k_tiles(MLA_HEADS * MLA_V),
            feature_major(DIFF_QK_WIDTH), token_major(DIFF_QK_WIDTH), k_tiles(DIFF_V_WIDTH),
            token_major(d), token_major(d)]
    return pl.pallas_call(
        _proj_kernel,
        grid=(n // tm,),
        in_specs=in_specs,
        out_specs=[o[0] for o in outs],
        out_shape=[o[1] for o in outs],
        compiler_params=pltpu.CompilerParams(dimension_semantics=("parallel",), vmem_limit_bytes=VMEM_LIMIT_BYTES),
        name="proj",
    )(x2, *weights, *feature_major_tables, *token_major_tables)


def _chunk_mask_t(tk, width):
    kc = lax.broadcasted_iota(jnp.int32, (tk, width), 0) // CHUNK
    qc = lax.broadcasted_iota(jnp.int32, (tk, width), 1) // CHUNK
    return kc <= qc


ONES_ROWS = 16


def _with_ones_rows(vt):
    return jnp.concatenate([vt, jnp.ones((ONES_ROWS, vt.shape[1]), vt.dtype)], axis=0)


def _softmax_step_t(st, vt_ones, m_ref, acc_ref, lo):
    m_prev = m_ref[:, lo:]
    m_new = jnp.maximum(m_prev, jnp.max(st, axis=0, keepdims=True))
    alpha = jnp.exp2(m_prev - m_new)
    pr = jnp.exp2(st - m_new)
    acc_ref[:, lo:] = alpha * acc_ref[:, lo:] + _dot(vt_ones, pr.astype(BF16))
    m_ref[:, lo:] = m_new


def _normalized(acc_ref, dv):
    acc = acc_ref[...]
    return acc[:dv] / acc[dv:dv + 1]


STATE_REFS = 4


def _attn_scratch(chains, dv, tq, tk):
    per_chain = [pltpu.VMEM((1, tq), F32), pltpu.VMEM((dv + ONES_ROWS, tq), F32),
                 pltpu.VMEM((tk, tq), F32), pltpu.VMEM((tk, tq), F32)]
    return per_chain * chains


def _flash_attention(n_q_tiles, scratch_refs, score_fn, value_fn, finalize_fn, tk, tq):
    ratio = tq // tk
    assert tq == ratio * tk and ratio % 2 == 0
    n_chains = len(scratch_refs) // STATE_REFS
    chains = [scratch_refs[STATE_REFS * c:STATE_REFS * (c + 1)] for c in range(n_chains)]

    def scores(i, t, slot, lo=0):
        for c, ch in enumerate(chains):
            ch[2 + slot][:, lo:] = score_fn(c, i, t, lo)

    def update(t, slot, diag=None):
        lo = 0 if diag is None else diag * tk
        for c, ch in enumerate(chains):
            st = ch[2 + slot][:, lo:]
            if diag is not None:
                st = jnp.where(_chunk_mask_t(tk, tq - lo), st, -jnp.inf)
            _softmax_step_t(st, _with_ones_rows(value_fn(c, t)), ch[0], ch[1], lo)

    scores(0, 0, 0)

    def query_tile(i, carry):
        for m_ref, acc_ref, _, _ in chains:
            m_ref[...] = jnp.full(m_ref.shape, -jnp.inf, F32)
            acc_ref[...] = jnp.zeros(acc_ref.shape, F32)

        def pair(p, c):
            t = 2 * p
            scores(i, t + 1, 1)
            update(t, 0)
            scores(i, t + 2, 0)
            update(t + 1, 1)
            return c

        lax.fori_loop(0, i * (ratio // 2), pair, 0)
        first_diag = ratio * i
        for d in range(ratio):
            if d + 1 < ratio:
                scores(i, first_diag + d + 1, (d + 1) % 2, lo=(d + 1) * tk)
            else:
                scores(jnp.minimum(i + 1, n_q_tiles - 1), 0, 0)
            update(first_diag + d, d % 2, diag=d)
        finalize_fn(i, [ch[1] for ch in chains])
        return carry

    lax.fori_loop(0, n_q_tiles, query_tile, 0)


def _mla_kernel(qt_ref, k_ref, vt_ref, o_ref, *scratch_refs):
    tq, tk = ATTN_Q_ROWS, ATTN_K_ROWS

    def score_fn(c, i, t, lo):
        rows = pl.ds(pl.multiple_of(t * tk, tk), tk)
        sl = slice(c * LANES, (c + 1) * LANES)
        return _dot(k_ref[rows, sl], qt_ref[i, sl, lo:])

    def value_fn(c, t):
        return vt_ref[t, c * MLA_V:(c + 1) * MLA_V, :]

    def finalize_fn(i, accs):
        ot = jnp.concatenate([_normalized(acc_ref, MLA_V) for acc_ref in accs], axis=0)
        o_ref[pl.ds(pl.multiple_of(i * tq, tq), tq), :] = ot.T.astype(BF16)

    _flash_attention(qt_ref.shape[0], scratch_refs, score_fn, value_fn, finalize_fn, tk, tq)


def _mla_call(qmt, km, vtm, batch, seq_len):
    n = km.shape[0]
    tq, tk, hps = ATTN_Q_ROWS, ATTN_K_ROWS, MLA_HEADS_PER_STEP
    return pl.pallas_call(
        _mla_kernel,
        grid=(batch, MLA_HEADS // hps),
        in_specs=[pl.BlockSpec((seq_len // tq, hps * LANES, tq), lambda b, h: (b, h, 0)),
                  pl.BlockSpec((seq_len, hps * LANES), lambda b, h: (b, h)),
                  pl.BlockSpec((seq_len // tk, hps * MLA_V, tk), lambda b, h: (b, h, 0))],
        out_specs=pl.BlockSpec((seq_len, hps * MLA_V), lambda b, h: (b, h)),
        out_shape=jax.ShapeDtypeStruct((n, MLA_HEADS * MLA_V), BF16),
        scratch_shapes=_attn_scratch(hps, MLA_V, tq, tk),
        compiler_params=pltpu.CompilerParams(dimension_semantics=("parallel", "parallel"),
                                             vmem_limit_bytes=VMEM_LIMIT_BYTES),
        name="mla_attn",
    )(qmt, km, vtm)


def _diff_kernel(lam_init, qt_ref, k_ref, vt_ref, lq1_ref, lk1_ref, lq2_ref, lk2_ref, subln_ref, o_ref,
                 *scratch_refs):
    tq, tk = ATTN_Q_ROWS, ATTN_K_ROWS
    hps = DIFF_HEADS_PER_STEP

    def score_fn(c, i, t, lo):
        rows = pl.ds(pl.multiple_of(t * tk, tk), tk)
        hd, f = c // 2, c % 2
        half = qt_ref[i, hd * LANES + f * DIFF_HEAD_DIM:hd * LANES + (f + 1) * DIFF_HEAD_DIM, lo:]
        zero = jnp.zeros_like(half)
        q = jnp.concatenate([half, zero] if f == 0 else [zero, half], axis=0)
        return _dot(k_ref[rows, hd * LANES:(hd + 1) * LANES], q)

    def value_fn(c, t):
        hd = c // 2
        return vt_ref[t, hd * DIFF_V_DIM:(hd + 1) * DIFF_V_DIM, :]

    lam = (jnp.exp(jnp.sum(lq1_ref[...] * lk1_ref[...], axis=-1, keepdims=True))
           - jnp.exp(jnp.sum(lq2_ref[...] * lk2_ref[...], axis=-1, keepdims=True)) + lam_init)
    subln = subln_ref[...] * (1.0 - lam_init)

    def finalize_fn(i, accs):
        heads = []
        for hd in range(hps):
            ot = _normalized(accs[2 * hd], DIFF_V_DIM) - lam * _normalized(accs[2 * hd + 1], DIFF_V_DIM)
            ot = ot * lax.rsqrt(jnp.sum(ot * ot, axis=0, keepdims=True) * (1.0 / DIFF_V_DIM) + EPS)
            heads.append(ot * subln)
        o_ref[pl.ds(pl.multiple_of(i * tq, tq), tq), :] = jnp.concatenate(heads, axis=0).T.astype(BF16)

    _flash_attention(qt_ref.shape[0], scratch_refs, score_fn, value_fn, finalize_fn, tk, tq)


def _diff_call(qdt, kd, vtd, lq1, lk1, lq2, lk2, subln_col, lam_init, batch, seq_len):
    n = kd.shape[0]
    tq, tk, hps = ATTN_Q_ROWS, ATTN_K_ROWS, DIFF_HEADS_PER_STEP
    small = lambda a: pl.BlockSpec(a.shape, lambda b, h: (0, 0))
    return pl.pallas_call(
        functools.partial(_diff_kernel, lam_init),
        grid=(batch, DIFF_HEADS // hps),
        in_specs=[pl.BlockSpec((seq_len // tq, hps * LANES, tq), lambda b, h: (b, h, 0)),
                  pl.BlockSpec((seq_len, hps * LANES), lambda b, h: (b, h)),
                  pl.BlockSpec((seq_len // tk, hps * DIFF_V_DIM, tk), lambda b, h: (b, h, 0)),
                  small(lq1), small(lk1), small(lq2), small(lk2), small(subln_col)],
        out_specs=pl.BlockSpec((seq_len, hps * LANES), lambda b, h: (b, h)),
        out_shape=jax.ShapeDtypeStruct((n, DIFF_V_WIDTH), BF16),
        scratch_shapes=_attn_scratch(2 * hps, DIFF_V_DIM, tq, tk),
        compiler_params=pltpu.CompilerParams(dimension_semantics=("parallel", "parallel"),
                                             vmem_limit_bytes=VMEM_LIMIT_BYTES),
        name="diff_attn",
    )(qdt, kd, vtd, lq1, lk1, lq2, lk2, subln_col)


def _merge_kernel(x_ref, om_ref, od_ref, sgm_ref, sgd_ref, wmu_ref, wdu_ref, wout_ref, gffn_ref, wr_ref,
                  brt_ref, x1_ref, h2_ref, route_ref, route_t_ref, cnt_ref):
    merged = (sgm_ref[...].astype(F32) * _dot(om_ref[...], wmu_ref[...])
              + sgd_ref[...].astype(F32) * _dot(od_ref[...], wdu_ref[...]))
    x1 = x_ref[...] + _dot(merged.astype(BF16), wout_ref[...])
    x1_ref[...] = x1
    h2 = _rms(x1, x1.shape[-1]) * gffn_ref[...]
    h2_hi = h2.astype(BF16)
    h2_ref[...] = h2_hi
    tm = h2.shape[0]

    h2_lo = (h2 - h2_hi.astype(F32)).astype(BF16)
    by_hi = _dot_tn(wr_ref[...], h2_hi)
    logits = by_hi[:LANES] + by_hi[LANES:] + _dot_tn(wr_ref[:, :LANES], h2_lo) + brt_ref[...]
    row = lax.broadcasted_iota(jnp.int32, logits.shape, 0)
    neg = -jnp.inf
    big = jnp.int32(1 << 20)

    def top(vals):
        mx = jnp.max(vals, axis=0, keepdims=True)
        idx = jnp.min(jnp.where(vals == mx, row, big), axis=0, keepdims=True)
        return mx, idx

    gl = jnp.where((row >= N_EXPERTS) & (row < N_EXPERTS + N_GROUPS), logits, neg)
    gmax, gidx = top(gl)
    pg_sel = 1.0 / jnp.sum(jnp.exp(gl - gmax), axis=0, keepdims=True)
    el = jnp.where((row < N_EXPERTS) & (row // EXPERTS_PER_GROUP == gidx - N_EXPERTS), logits, neg)
    m1, i1 = top(el)
    m2, i2 = top(jnp.where(row == i1, neg, el))
    e2 = jnp.exp(m2 - m1)
    w1 = pg_sel / (1.0 + e2)
    w2 = w1 * e2

    sel = jnp.where((row == i1) | (row == i2), 1.0, 0.0).astype(BF16)
    t_row = lax.broadcasted_iota(jnp.int32, (tm, tm), 0)
    t_col = lax.broadcasted_iota(jnp.int32, (tm, tm), 1)
    rank = _dot(sel, jnp.where(t_row < t_col, 1.0, 0.0).astype(BF16))
    cnt = _dot(sel, jnp.ones((tm, tm), BF16))
    seg = jnp.floor((cnt + (SEG_ALIGN - 1)) * (1.0 / SEG_ALIGN))
    e_row = lax.broadcasted_iota(jnp.int32, (LANES, LANES), 0)
    e_col = lax.broadcasted_iota(jnp.int32, (LANES, LANES), 1)
    off = _dot(jnp.where(e_col < e_row, 1.0, 0.0).astype(BF16), seg.astype(BF16)) * SEG_ALIGN
    dest = off + rank
    d1 = jnp.sum(jnp.where(row == i1, dest, 0.0), axis=0, keepdims=True)
    d2 = jnp.sum(jnp.where(row == i2, dest, 0.0), axis=0, keepdims=True)
    route_t = jnp.where(row == 0, d1, jnp.where(row == 1, d2, jnp.where(row == 2, w1, jnp.where(row == 3, w2, 0.0))))
    route_t_ref[0] = route_t[0:8]
    route_ref[...] = route_t.T
    cnt_ref[0] = (seg[:, :LANES] * SEG_ALIGN).T[0:1]


def _merge_call(x2, om, od, sgm, sgd, p):
    n, d = x2.shape
    tm = MERGE_ROWS
    row = lambda i: (i, 0)
    const = lambda i: (0, 0)
    weights = [p["wmu"], p["wdu"], p["wout"], p["gffn"], p["wr"], p["brt"]]
    return pl.pallas_call(
        _merge_kernel,
        grid=(n // tm,),
        in_specs=([pl.BlockSpec((tm, a.shape[1]), row) for a in (x2, om, od, sgm, sgd)]
                  + [pl.BlockSpec(w.shape, const) for w in weights]),
        out_specs=[pl.BlockSpec((tm, d), row), pl.BlockSpec((tm, d), row), pl.BlockSpec((tm, LANES), row),
                   pl.BlockSpec((1, 8, tm), lambda i: (i, 0, 0)), pl.BlockSpec((1, 1, LANES), lambda i: (i, 0, 0))],
        out_shape=[jax.ShapeDtypeStruct((n, d), F32), jax.ShapeDtypeStruct((n, d), BF16),
                   jax.ShapeDtypeStruct((n, LANES), F32), jax.ShapeDtypeStruct((n // tm, 8, tm), F32),
                   jax.ShapeDtypeStruct((n // tm, 1, LANES), F32)],
        compiler_params=pltpu.CompilerParams(dimension_semantics=("parallel",), vmem_limit_bytes=VMEM_LIMIT_BYTES),
        name="merge_router",
    )(x2, om, od, sgm, sgd, *weights)


def _segment_copies(i, seg_dst_ref, seg_rows_ref, tile_off_ref, global_ref, tile_ref, sem, to_global):
    def body(e, carry):
        k = i * N_EXPERTS + e
        rows = pl.multiple_of(seg_rows_ref[k], SEG_ALIGN)

        @pl.when(rows > 0)
        def _():
            g = global_ref.at[pl.ds(pl.multiple_of(seg_dst_ref[k], SEG_ALIGN), rows)]
            t = tile_ref.at[pl.ds(pl.multiple_of(tile_off_ref[k], SEG_ALIGN), rows)]
            src, dst = (t, g) if to_global else (g, t)
            pltpu.make_async_copy(src, dst, sem).start()

        return carry

    lax.fori_loop(0, N_EXPERTS, body, 0)


def _wait_rows(tile_ref, rows, sem):
    @pl.when(rows > 0)
    def _():
        view = tile_ref.at[pl.ds(0, pl.multiple_of(rows, SEG_ALIGN))]
        pltpu.make_async_copy(view, view, sem).wait()


def _zero_unused_rows(tail_dst_ref, tail_rows_ref, n_used_ref, xs_ref, zero_ref, sem, start):
    n_tiles = xs_ref.shape[0] // EXPERT_ROWS
    if start:
        zero_ref[...] = jnp.zeros(zero_ref.shape, BF16)

    def tail(e, total):
        rows = pl.multiple_of(tail_rows_ref[e], SEG_ALIGN)
        if start:
            @pl.when(rows > 0)
            def _():
                dst = xs_ref.at[pl.ds(pl.multiple_of(tail_dst_ref[e], SEG_ALIGN), rows)]
                pltpu.make_async_copy(zero_ref.at[pl.ds(0, rows)], dst, sem).start()

        return total + rows

    total = lax.fori_loop(0, N_EXPERTS, tail, 0)
    if not start:
        _wait_rows(xs_ref, total + (n_tiles - n_used_ref[0]) * EXPERT_ROWS, sem)
        return

    def unused(t, carry):
        dst = xs_ref.at[pl.ds(pl.multiple_of(t * EXPERT_ROWS, EXPERT_ROWS), EXPERT_ROWS)]
        pltpu.make_async_copy(zero_ref, dst, sem).start()
        return carry

    lax.fori_loop(n_used_ref[0], n_tiles, unused, 0)


def _sort_kernel(seg_dst_ref, seg_rows_ref, tile_off_ref, tile_rows_ref, tail_dst_ref, tail_rows_ref, n_used_ref,
                 h2_ref, route_t_ref, xs_ref, sorted_ref, zero_ref, sem, zero_sem):
    i = pl.program_id(0)
    tm = h2_ref.shape[0]

    @pl.when(i == 0)
    def _():
        _zero_unused_rows(tail_dst_ref, tail_rows_ref, n_used_ref, xs_ref, zero_ref, zero_sem, True)

    d1 = route_t_ref[0, 0:1, :].astype(jnp.int32)
    d2 = route_t_ref[0, 1:2, :].astype(jnp.int32)
    slot = i % 2

    def sort_rows(n_rows):
        r = lax.broadcasted_iota(jnp.int32, (n_rows, tm), 0)
        perm = jnp.where((r == d1) | (r == d2), 1.0, 0.0).astype(BF16)
        sorted_ref[slot, 0:n_rows] = _dot(perm, h2_ref[...]).astype(BF16)

    @pl.when(tile_rows_ref[i] <= SORT_ROWS_COMMON)
    def _():
        sort_rows(SORT_ROWS_COMMON)

    @pl.when(tile_rows_ref[i] > SORT_ROWS_COMMON)
    def _():
        sort_rows(SORT_ROWS)

    _segment_copies(i, seg_dst_ref, seg_rows_ref, tile_off_ref, xs_ref, sorted_ref.at[slot], sem.at[slot], True)

    @pl.when(i > 0)
    def _():
        _wait_rows(sorted_ref.at[1 - slot], tile_rows_ref[jnp.maximum(i - 1, 0)], sem.at[1 - slot])

    @pl.when(i == pl.num_programs(0) - 1)
    def _():
        _wait_rows(sorted_ref.at[slot], tile_rows_ref[i], sem.at[slot])
        _zero_unused_rows(tail_dst_ref, tail_rows_ref, n_used_ref, xs_ref, zero_ref, zero_sem, False)


def _sort_call(h2, route_t, sched, max_rows):
    n, d = h2.shape
    tm = ROUTE_ROWS
    return pl.pallas_call(
        _sort_kernel,
        grid_spec=pltpu.PrefetchScalarGridSpec(
            num_scalar_prefetch=7,
            grid=(n // tm,),
            in_specs=[pl.BlockSpec((tm, d), lambda i, *_: (i, 0)),
                      pl.BlockSpec((1, 8, tm), lambda i, *_: (i, 0, 0))],
            out_specs=pl.BlockSpec(memory_space=pl.ANY),
            scratch_shapes=[pltpu.VMEM((2, SORT_ROWS, d), BF16), pltpu.VMEM((EXPERT_ROWS, d), BF16),
                            pltpu.SemaphoreType.DMA((2,)), pltpu.SemaphoreType.DMA(())],
        ),
        out_shape=jax.ShapeDtypeStruct((max_rows, d), BF16),
        compiler_params=pltpu.CompilerParams(dimension_semantics=("arbitrary",), vmem_limit_bytes=VMEM_LIMIT_BYTES),
        name="moe_sort",
    )(sched["seg_dst"], sched["seg_rows"], sched["tile_off"], sched["tile_rows"], sched["tail_dst"],
      sched["tail_rows"], sched["n_used"], h2, route_t)


def _expert_kernel(tile_expert_ref, n_used_ref, xs_ref, wg_ref, wu_ref, wd_ref, ys_ref, wg_bf, wu_bf, wd_bf):
    t = pl.program_id(0)
    used = t < n_used_ref[0]

    @pl.when(used & ((t == 0) | (tile_expert_ref[t] != tile_expert_ref[jnp.maximum(t - 1, 0)])))
    def _():
        wg_bf[...] = wg_ref[0].astype(BF16)
        wu_bf[...] = wu_ref[0].astype(BF16)
        wd_bf[...] = wd_ref[0].astype(BF16)

    @pl.when(used)
    def _():
        xs = xs_ref[...]
        gate = _dot(xs, wg_bf[...])
        up = _dot(xs, wu_bf[...])
        hidden = (gate * jax.nn.sigmoid(gate) * up).astype(BF16)
        ys_ref[...] = _dot(hidden, wd_bf[...]).astype(BF16)

    @pl.when(jnp.logical_not(used))
    def _():
        ys_ref[...] = jnp.zeros(ys_ref.shape, BF16)


def _expert_call(xs, wg, wu, wd, sched):
    rows, d = xs.shape
    tr = EXPERT_ROWS
    blk = lambda t, te, nu: (jnp.minimum(t, nu[0] - 1), 0)
    wsel = lambda t, te, nu: (te[jnp.minimum(t, nu[0] - 1)], 0, 0)
    return pl.pallas_call(
        _expert_kernel,
        grid_spec=pltpu.PrefetchScalarGridSpec(
            num_scalar_prefetch=2,
            grid=(rows // tr,),
            in_specs=[pl.BlockSpec((tr, d), blk),
                      pl.BlockSpec((1, d, EXPERT_FF), wsel), pl.BlockSpec((1, d, EXPERT_FF), wsel),
                      pl.BlockSpec((1, EXPERT_FF, d), wsel)],
            out_specs=pl.BlockSpec((tr, d), lambda t, te, nu: (t, 0)),
            scratch_shapes=[pltpu.VMEM((d, EXPERT_FF), BF16), pltpu.VMEM((d, EXPERT_FF), BF16),
                            pltpu.VMEM((EXPERT_FF, d), BF16)],
        ),
        out_shape=jax.ShapeDtypeStruct((rows, d), BF16),
        compiler_params=pltpu.CompilerParams(dimension_semantics=("arbitrary",), vmem_limit_bytes=VMEM_LIMIT_BYTES),
        name="moe_experts",
    )(sched["tile_expert"], sched["n_used"], xs, wg, wu, wd)


def _combine_kernel(seg_dst_ref, seg_rows_ref, tile_off_ref, tile_rows_ref, ys_ref, route_ref, x1_ref, o_ref,
                    buf_ref, sem):
    i = pl.program_id(0)
    tm = x1_ref.shape[0]
    slot = i % 2

    def fetch(tile, into):
        buf_ref[into] = jnp.zeros(buf_ref.shape[1:], BF16)
        _segment_copies(tile, seg_dst_ref, seg_rows_ref, tile_off_ref, ys_ref, buf_ref.at[into], sem.at[into], False)

    @pl.when(i == 0)
    def _():
        fetch(i, slot)

    @pl.when(i + 1 < pl.num_programs(0))
    def _():
        fetch(i + 1, 1 - slot)

    route = route_ref[...]
    d1 = route[:, 0:1].astype(jnp.int32)
    d2 = route[:, 1:2].astype(jnp.int32)
    w1 = route[:, 2:3]
    w2 = route[:, 3:4]
    _wait_rows(buf_ref.at[slot], tile_rows_ref[i], sem.at[slot])

    def combine_rows(n_rows):
        r = lax.broadcasted_iota(jnp.int32, (tm, n_rows), 1)
        weights = (jnp.where(r == d1, w1, 0.0) + jnp.where(r == d2, w2, 0.0)).astype(BF16)
        o_ref[...] = x1_ref[...] + _dot(weights, buf_ref[slot, 0:n_rows])

    @pl.when(tile_rows_ref[i] <= SORT_ROWS_COMMON)
    def _():
        combine_rows(SORT_ROWS_COMMON)

    @pl.when(tile_rows_ref[i] > SORT_ROWS_COMMON)
    def _():
        combine_rows(SORT_ROWS)


def _combine_call(ys, route, x1, sched):
    n, d = x1.shape
    tm = ROUTE_ROWS
    return pl.pallas_call(
        _combine_kernel,
        grid_spec=pltpu.PrefetchScalarGridSpec(
            num_scalar_prefetch=4,
            grid=(n // tm,),
            in_specs=[pl.BlockSpec(memory_space=pl.ANY),
                      pl.BlockSpec((tm, LANES), lambda i, *_: (i, 0)),
                      pl.BlockSpec((tm, d), lambda i, *_: (i, 0))],
            out_specs=pl.BlockSpec((tm, d), lambda i, *_: (i, 0)),
            scratch_shapes=[pltpu.VMEM((2, SORT_ROWS, d), BF16), pltpu.SemaphoreType.DMA((2,))],
        ),
        out_shape=jax.ShapeDtypeStruct((n, d), F32),
        compiler_params=pltpu.CompilerParams(dimension_semantics=("arbitrary",), vmem_limit_bytes=VMEM_LIMIT_BYTES),
        name="moe_combine",
    )(sched["seg_dst"], sched["seg_rows"], sched["tile_off"], sched["tile_rows"], ys, route, x1)


def _schedule_kernel(cnt_ref, seg_dst_ref, tile_off_ref, tile_rows_ref, misc_ref):
    hp = functools.partial(jnp.dot, preferred_element_type=F32, precision=lax.Precision.HIGHEST)
    cnt = cnt_ref[...]
    n_tiles = cnt.shape[0]
    tile_before = jnp.where(lax.broadcasted_iota(jnp.int32, (n_tiles, n_tiles), 1)
                            < lax.broadcasted_iota(jnp.int32, (n_tiles, n_tiles), 0), 1.0, 0.0)
    expert_before = jnp.where(lax.broadcasted_iota(jnp.int32, (LANES, LANES), 0)
                              < lax.broadcasted_iota(jnp.int32, (LANES, LANES), 1), 1.0, 0.0)
    expert_rows = jnp.sum(cnt, axis=0, keepdims=True)
    region = jnp.floor((expert_rows + (EXPERT_ROWS - 1)) * (1.0 / EXPERT_ROWS)) * EXPERT_ROWS
    region_start = hp(jnp.broadcast_to(region, (8, LANES)), expert_before)[0:1]
    seg_dst_ref[...] = (region_start + hp(tile_before, cnt)).astype(jnp.int32)
    tile_off_ref[...] = hp(cnt, expert_before).astype(jnp.int32)
    tile_rows_ref[...] = jnp.broadcast_to(jnp.sum(cnt, axis=-1, keepdims=True), cnt.shape).astype(jnp.int32)
    n_used = jnp.sum(region, axis=-1, keepdims=True) * (1.0 / EXPERT_ROWS)
    row = lax.broadcasted_iota(jnp.int32, (8, LANES), 0)
    misc = jnp.where(row == 0, region_start + expert_rows,
                     jnp.where(row == 1, region - expert_rows,
                               jnp.where(row == 2, region_start + region, n_used)))
    misc_ref[...] = misc.astype(jnp.int32)


def _moe_schedule(cnt, n_tokens):
    n_tiles = cnt.shape[0]
    table = jax.ShapeDtypeStruct((n_tiles, LANES), jnp.int32)
    seg_dst, tile_off, tile_rows, misc = pl.pallas_call(
        _schedule_kernel,
        out_shape=[table, table, table, jax.ShapeDtypeStruct((8, LANES), jnp.int32)],
        name="moe_schedule",
    )(cnt.reshape(n_tiles, LANES))
    max_rows = 2 * n_tokens + n_tiles * N_EXPERTS * (SEG_ALIGN - 1) + N_EXPERTS * (EXPERT_ROWS - 1)
    max_tiles = -(-max_rows // EXPERT_ROWS)
    tile_start = jnp.arange(max_tiles, dtype=jnp.int32) * EXPERT_ROWS
    region_end = misc[2, :N_EXPERTS]
    tile_expert = jnp.minimum(jnp.sum((region_end[None, :] <= tile_start[:, None]).astype(jnp.int32), axis=1),
                              N_EXPERTS - 1)
    flat = lambda a: a[:, :N_EXPERTS].reshape(-1)
    sched = {
        "seg_dst": flat(seg_dst),
        "seg_rows": flat(cnt.reshape(n_tiles, LANES).astype(jnp.int32)),
        "tile_off": flat(tile_off),
        "tile_rows": tile_rows[:, 0],
        "tail_dst": misc[0, :N_EXPERTS],
        "tail_rows": misc[1, :N_EXPERTS],
        "tile_expert": tile_expert,
        "n_used": misc[3, :1],
    }
    return sched, max_tiles * EXPERT_ROWS


def _rotary_tables(seq_len, rot_dim, period, first, gain, scale, feature_major):
    half = rot_dim // 2
    pos = jnp.arange(seq_len, dtype=F32)
    inv = 1.0 / (ROPE_THETA ** (jnp.arange(0, rot_dim, 2, dtype=F32) / rot_dim))
    lane = jnp.arange(LANES)
    rel = (lane % period) - first
    active = (rel >= 0) & (rel < rot_dim)
    idx = jnp.clip(rel, 0, rot_dim - 1) % half
    sign = jnp.where(rel < half, -1.0, 1.0)
    partner = jnp.where(active, jnp.where(rel < half, lane + half, lane - half), lane)
    gain = gain.astype(F32)
    ga, gb = gain * scale, gain[partner] * sign * scale
    if feature_major:
        ang = inv[:, None] * pos[None, :]
        c = jnp.where(active[:, None], jnp.cos(ang)[idx, :], 1.0)
        s = jnp.where(active[:, None], jnp.sin(ang)[idx, :], 0.0)
        return c * ga[:, None], s * gb[:, None]
    ang = pos[:, None] * inv[None, :]
    c = jnp.where(active[None, :], jnp.cos(ang)[:, idx], 1.0)
    s = jnp.where(active[None, :], jnp.sin(ang)[:, idx], 0.0)
    return c * ga[None, :], s * gb[None, :]


def _head_pad(w, heads, width):
    r = w.shape[0]
    w = w.reshape(r, heads, width)
    return jnp.pad(w, ((0, 0), (0, 0), (0, LANES - width))).reshape(r, heads * LANES)


def _layer_params(l, seq_len, norm_mix, w_in, mla_q_latent_norm, w_mla_uq, mla_kv_latent_norm, w_mla_ukv,
                  mla_q_gain, mla_k_gain, diff_q_gain, diff_k_gain, w_mla_up, w_diff_up, w_out, norm_ffn,
                  w_router_group, b_router_group, w_router_expert, b_router_expert):
    d = w_in.shape[1]
    sizes = (MLA_Q_RANK, MLA_KV_RANK, MLA_ROPE, DIFF_QK_WIDTH, DIFF_QK_WIDTH, DIFF_V_WIDTH, d, d)
    offs = [0]
    for s in sizes:
        offs.append(offs[-1] + s)
    wi = w_in[l]
    seg = [wi[:, offs[k]:offs[k + 1]] for k in range(len(sizes))]
    row = lambda g: g.astype(F32)[None, :]
    p = {}
    p["gmix"] = row(norm_mix[l])
    p["wql"] = seg[0].astype(BF16)
    p["wkvl"] = seg[1].astype(BF16)
    p["wkr"] = jnp.pad(seg[2], ((0, 0), (MLA_NOPE, LANES - MLA_QK))).astype(BF16)
    p["wdk"] = seg[4].astype(BF16)
    p["wdqv"] = jnp.concatenate([seg[3], seg[5]], axis=1).astype(BF16)
    p["wgm"], p["wgd"] = seg[6].astype(BF16), seg[7].astype(BF16)
    p["gql"] = row(mla_q_latent_norm[l])
    p["wuq"] = _head_pad(w_mla_uq[l], MLA_HEADS, MLA_QK).astype(BF16)
    p["gkvl"] = row(mla_kv_latent_norm[l])
    ukv = w_mla_ukv[l].reshape(MLA_KV_RANK, MLA_HEADS, MLA_NOPE + MLA_V)
    p["wuk"] = _head_pad(ukv[:, :, :MLA_NOPE].reshape(MLA_KV_RANK, -1), MLA_HEADS, MLA_NOPE).astype(BF16)
    p["wuv"] = ukv[:, :, MLA_NOPE:].reshape(MLA_KV_RANK, -1).astype(BF16)
    gq = jnp.pad(mla_q_gain[l], (0, LANES - MLA_QK))
    gk = jnp.pad(mla_k_gain[l], (0, LANES - MLA_QK))
    nope = jnp.arange(LANES) < MLA_NOPE
    p["gkn"] = jnp.where(nope, gk, 0.0).astype(F32)[None, :]
    p["aq"], p["bq"] = _rotary_tables(seq_len, MLA_ROPE, LANES, MLA_NOPE, gq, LOG2E * MLA_QK ** -0.5, True)
    p["ak"], p["bk"] = _rotary_tables(seq_len, MLA_ROPE, LANES, MLA_NOPE, jnp.where(nope, 0.0, gk), 1.0, False)
    p["adq"], p["bdq"] = _rotary_tables(seq_len, DIFF_ROPE, DIFF_HEAD_DIM, 0, jnp.tile(diff_q_gain[l], 2),
                                        LOG2E * DIFF_HEAD_DIM ** -0.5, True)
    p["adk"], p["bdk"] = _rotary_tables(seq_len, DIFF_ROPE, DIFF_HEAD_DIM, 0, jnp.tile(diff_k_gain[l], 2), 1.0,
                                        False)
    p["wmu"] = w_mla_up[l].astype(BF16)
    p["wdu"] = w_diff_up[l].astype(BF16)
    p["wout"] = w_out[l].astype(BF16)
    p["gffn"] = row(norm_ffn[l])
    wr = jnp.concatenate([w_router_expert[l], w_router_group[l]], axis=1).astype(F32)
    wr = jnp.pad(wr, ((0, 0), (0, LANES - wr.shape[1])))
    wr_hi = wr.astype(BF16)
    p["wr"] = jnp.concatenate([wr_hi, (wr - wr_hi.astype(F32)).astype(BF16)], axis=1)
    br = jnp.concatenate([b_router_expert[l], b_router_group[l]]).astype(F32)
    p["brt"] = jnp.broadcast_to(jnp.pad(br, (0, LANES - br.shape[0]))[:, None], (LANES, MERGE_ROWS))
    return p


def kernel(x, norm_mix, w_in, mla_q_latent_norm, w_mla_uq, mla_kv_latent_norm, w_mla_ukv, mla_q_gain, mla_k_gain, diff_q_gain, diff_k_gain, lambda_q1, lambda_k1, lambda_q2, lambda_k2, diff_subln, w_mla_up, w_diff_up, w_out, norm_ffn, w_router_group, b_router_group, w_router_expert, b_router_expert, w_expert_gate, w_expert_up, w_expert_down):
    batch, seq_len, d = x.shape
    x2 = x.reshape(batch * seq_len, d)
    row = lambda g: g.astype(F32)[None, :]
    for l in range(norm_mix.shape[0]):
        lam_init = 0.8 - 0.6 * math.exp(-0.3 * l)
        p = _layer_params(l, seq_len, norm_mix, w_in, mla_q_latent_norm, w_mla_uq, mla_kv_latent_norm, w_mla_ukv,
                          mla_q_gain, mla_k_gain, diff_q_gain, diff_k_gain, w_mla_up, w_diff_up, w_out, norm_ffn,
                          w_router_group, b_router_group, w_router_expert, b_router_expert)
        qmt, km, vtm, qdt, kd, vtd, sgm, sgd = _proj_call(x2, seq_len, p)
        om = _mla_call(qmt, km, vtm, batch, seq_len)
        od = _diff_call(qdt, kd, vtd, row(lambda_q1[l]), row(lambda_k1[l]), row(lambda_q2[l]), row(lambda_k2[l]),
                        diff_subln[l].astype(F32)[:, None], lam_init, batch, seq_len)
        x1, h2, route, route_t, cnt = _merge_call(x2, om, od, sgm, sgd, p)
        sched, max_rows = _moe_schedule(cnt, x2.shape[0])
        xs = _sort_call(h2, route_t, sched, max_rows)
        ys = _expert_call(xs, w_expert_gate[l], w_expert_up[l], w_expert_down[l], sched)
        x2 = _combine_call(ys, route, x1, sched)
    return x2.reshape(batch, seq_len, d)
```

```python
import functools
import math

import jax
import jax.numpy as jnp
from jax import lax
from jax.experimental import pallas as pl
from jax.experimental.pallas import tpu as pltpu

CHUNK = 64
ROPE_THETA = 500000.0
EPS = 1e-6

MLA_HEADS = 8
MLA_NOPE = 64
MLA_ROPE = 32
MLA_V = 64
MLA_QK = MLA_NOPE + MLA_ROPE
MLA_Q_RANK = 256
MLA_KV_RANK = 128

DIFF_HEADS = 4
DIFF_HEAD_DIM = 64
DIFF_V_DIM = 2 * DIFF_HEAD_DIM
DIFF_ROPE = DIFF_HEAD_DIM // 4
DIFF_QK_WIDTH = DIFF_HEADS * 2 * DIFF_HEAD_DIM
DIFF_V_WIDTH = DIFF_HEADS * DIFF_V_DIM

N_GROUPS = 4
EXPERTS_PER_GROUP = 8
N_EXPERTS = N_GROUPS * EXPERTS_PER_GROUP
EXPERT_FF = 256

LANES = 128
VMEM_LIMIT_BYTES = 48 * 1024 * 1024

PROJ_ROWS = 512
ATTN_Q_ROWS = 512
ATTN_K_ROWS = 256
MERGE_ROWS = 512
ROUTE_ROWS = MERGE_ROWS
SEG_ALIGN = 16
SORT_ROWS = 2 * ROUTE_ROWS + N_EXPERTS * SEG_ALIGN
SORT_ROWS_COMMON = 2 * ROUTE_ROWS + N_EXPERTS * SEG_ALIGN // 2
EXPERT_ROWS = 512
MLA_HEADS_PER_STEP = 4
DIFF_HEADS_PER_STEP = 2
LOG2E = 1.4426950408889634

BF16 = jnp.bfloat16
F32 = jnp.float32


def _dot(a, b):
    return jnp.dot(a, b, preferred_element_type=F32)


def _dot_nt(a, b):
    return lax.dot_general(a, b, (((1,), (1,)), ((), ())), preferred_element_type=F32)


def _rms(x, width):
    return x * lax.rsqrt(jnp.sum(x * x, axis=-1, keepdims=True) * (1.0 / width) + EPS)


def _rotary_partner(y, half):
    lane = lax.broadcasted_iota(jnp.int32, y.shape, 1)
    up = pltpu.roll(y, LANES - half, 1)
    down = pltpu.roll(y, half, 1)
    return jnp.where((lane // half) % 2 == 0, up, down)


def _swap_row_blocks(y, first, half, period):
    parts = []
    for base in range(0, y.shape[0], period):
        a = base + first
        parts += [y[base:a], y[a + half:a + 2 * half], y[a:a + half], y[a + 2 * half:base + period]]
    return jnp.concatenate([p for p in parts if p.shape[0]], axis=0)


def _store_k_tiles(o_ref, vt):
    tk = o_ref.shape[-1]
    for c in range(o_ref.shape[0]):
        o_ref[c] = vt[:, c * tk:(c + 1) * tk].astype(BF16)


def _proj_kernel(x_ref, gmix_ref, wql_ref, wkvl_ref, wkr_ref, wdk_ref, wdqvt_ref, wgm_ref, wgd_ref,
                 gql_ref, wuqt_ref, gkvl_ref, wuk_ref, wuvt_ref, gkn_ref,
                 aq_ref, bq_ref, adq_ref, bdq_ref, ak_ref, bk_ref, adk_ref, bdk_ref,
                 qmt_ref, km_ref, vtm_ref, qdt_ref, kd_ref, vtd_ref, sgm_ref, sgd_ref):
    x = x_ref[...]
    h = (_rms(x, x.shape[-1]) * gmix_ref[...]).astype(BF16)

    ql = (_rms(_dot(h, wql_ref[...]), MLA_Q_RANK) * gql_ref[...]).astype(BF16)
    qt = _dot_nt(wuqt_ref[...], ql)
    aq, bq = aq_ref[...], bq_ref[...]
    for hd in range(MLA_HEADS):
        rows = slice(hd * LANES, (hd + 1) * LANES)
        qh = qt[rows]
        r = lax.rsqrt(jnp.sum(qh * qh, axis=0, keepdims=True) * (1.0 / MLA_QK) + EPS)
        y = (qh * aq + _swap_row_blocks(qh, MLA_NOPE, MLA_ROPE // 2, LANES) * bq) * r
        qmt_ref[0, rows, :] = y.astype(BF16)

    kvl = (_rms(_dot(h, wkvl_ref[...]), MLA_KV_RANK) * gkvl_ref[...]).astype(BF16)
    kr = _dot(h, wkr_ref[...])
    kr_rot = kr * ak_ref[...] + _rotary_partner(kr, MLA_ROPE // 2) * bk_ref[...]
    kr_ss = jnp.sum(kr * kr, axis=-1, keepdims=True)
    kn = _dot(kvl, wuk_ref[...])
    _store_k_tiles(vtm_ref, _dot_nt(wuvt_ref[...], kvl))
    gkn = gkn_ref[...]
    for hd in range(MLA_HEADS):
        sl = slice(hd * LANES, (hd + 1) * LANES)
        knh = kn[:, sl]
        r = lax.rsqrt((jnp.sum(knh * knh, axis=-1, keepdims=True) + kr_ss) * (1.0 / MLA_QK) + EPS)
        km_ref[:, sl] = ((knh * gkn + kr_rot) * r).astype(BF16)

    qvt = _dot_nt(wdqvt_ref[...], h)
    _store_k_tiles(vtd_ref, qvt[DIFF_QK_WIDTH:])
    adq, bdq = adq_ref[...], bdq_ref[...]
    for hd in range(DIFF_HEADS):
        rows = slice(hd * LANES, (hd + 1) * LANES)
        qh = qvt[rows]
        t = qh * adq + _swap_row_blocks(qh, 0, DIFF_ROPE // 2, DIFF_HEAD_DIM) * bdq
        halves = []
        for f in range(2):
            part = qh[f * DIFF_HEAD_DIM:(f + 1) * DIFF_HEAD_DIM]
            r = lax.rsqrt(jnp.sum(part * part, axis=0, keepdims=True) * (1.0 / DIFF_HEAD_DIM) + EPS)
            halves.append(t[f * DIFF_HEAD_DIM:(f + 1) * DIFF_HEAD_DIM] * r)
        qdt_ref[0, rows, :] = jnp.concatenate(halves, axis=0).astype(BF16)

    kd = _dot(h, wdk_ref[...])
    adk, bdk = adk_ref[...], bdk_ref[...]
    for hd in range(DIFF_HEADS):
        sl = slice(hd * LANES, (hd + 1) * LANES)
        th = kd[:, sl]
        lane = lax.broadcasted_iota(jnp.int32, th.shape, 1)
        sq = th * th
        lo = jnp.sum(jnp.where(lane < DIFF_HEAD_DIM, sq, 0.0), axis=-1, keepdims=True)
        tot = jnp.sum(sq, axis=-1, keepdims=True)
        r = lax.rsqrt(jnp.where(lane < DIFF_HEAD_DIM, lo, tot - lo) * (1.0 / DIFF_HEAD_DIM) + EPS)
        kd_ref[:, sl] = ((th * adk + _rotary_partner(th, DIFF_ROPE // 2) * bdk) * r).astype(BF16)

    sgm_ref[...] = jax.nn.sigmoid(_dot(h, wgm_ref[...])).astype(BF16)
    sgd_ref[...] = jax.nn.sigmoid(_dot(h, wgd_ref[...])).astype(BF16)


def _proj_call(x2, seq_len, p):
    n, d = x2.shape
    tm = PROJ_ROWS
    pos_blocks = seq_len // tm
    row = lambda i: (i, 0)
    const = lambda i: (0, 0)
    weights = [p["gmix"], p["wql"], p["wkvl"], p["wkr"], p["wdk"], p["wdqvt"], p["wgm"], p["wgd"],
               p["gql"], p["wuqt"], p["gkvl"], p["wuk"], p["wuvt"], p["gkn"]]
    feature_major_tables = [p["aq"], p["bq"], p["adq"], p["bdq"]]
    token_major_tables = [p["ak"], p["bk"], p["adk"], p["bdk"]]
    in_specs = ([pl.BlockSpec((tm, d), row)]
                + [pl.BlockSpec(w.shape, const) for w in weights]
                + [pl.BlockSpec((LANES, tm), lambda i: (0, i % pos_blocks)) for _ in feature_major_tables]
                + [pl.BlockSpec((tm, LANES), lambda i: (i % pos_blocks, 0)) for _ in token_major_tables])
    tk = ATTN_K_ROWS
    k_tiles = lambda width: (pl.BlockSpec((tm // tk, width, tk), lambda i: (i, 0, 0)),
                             jax.ShapeDtypeStruct((n // tk, width, tk), BF16))
    token_major = lambda width: (pl.BlockSpec((tm, width), row), jax.ShapeDtypeStruct((n, width), BF16))
    assert tm == ATTN_Q_ROWS
    feature_major = lambda width: (pl.BlockSpec((1, width, tm), lambda i: (i, 0, 0)),
                                   jax.ShapeDtypeStruct((n // tm, width, tm), BF16))
    outs = [feature_major(MLA_HEADS * LANES), token_major(MLA_HEADS * LANES), k_tiles(MLA_HEADS * MLA_V),
            feature_major(DIFF_QK_WIDTH), token_major(DIFF_QK_WIDTH), k_tiles(DIFF_V_WIDTH),
            token_major(d), token_major(d)]
    return pl.pallas_call(
        _proj_kernel,
        grid=(n // tm,),
        in_specs=in_specs,
        out_specs=[o[0] for o in outs],
        out_shape=[o[1] for o in outs],
        compiler_params=pltpu.CompilerParams(dimension_semantics=("parallel",), vmem_limit_bytes=VMEM_LIMIT_BYTES),
        name="proj",
    )(x2, *weights, *feature_major_tables, *token_major_tables)


def _chunk_mask_t(tk, width):
    kc = lax.broadcasted_iota(jnp.int32, (tk, width), 0) // CHUNK
    qc = lax.broadcasted_iota(jnp.int32, (tk, width), 1) // CHUNK
    return kc <= qc


ONES_ROWS = 16


def _with_ones_rows(vt):
    return jnp.concatenate([vt, jnp.ones((ONES_ROWS, vt.shape[1]), vt.dtype)], axis=0)


def _softmax_step_t(st, vt_ones, m_ref, acc_ref, lo):
    m_prev = m_ref[:, lo:]
    m_new = jnp.maximum(m_prev, jnp.max(st, axis=0, keepdims=True))
    alpha = jnp.exp2(m_prev - m_new)
    pr = jnp.exp2(st - m_new)
    acc_ref[:, lo:] = alpha * acc_ref[:, lo:] + _dot(vt_ones, pr.astype(BF16))
    m_ref[:, lo:] = m_new


def _normalized(acc_ref, dv):
    acc = acc_ref[...]
    return acc[:dv] / acc[dv:dv + 1]


STATE_REFS = 4


def _attn_scratch(chains, dv, tq, tk):
    per_chain = [pltpu.VMEM((1, tq), F32), pltpu.VMEM((dv + ONES_ROWS, tq), F32),
                 pltpu.VMEM((tk, tq), F32), pltpu.VMEM((tk, tq), F32)]
    return per_chain * chains


def _flash_attention(n_q_tiles, scratch_refs, score_fn, value_fn, finalize_fn, tk, tq):
    ratio = tq // tk
    assert tq == ratio * tk and ratio % 2 == 0
    n_chains = len(scratch_refs) // STATE_REFS
    chains = [scratch_refs[STATE_REFS * c:STATE_REFS * (c + 1)] for c in range(n_chains)]

    def scores(i, t, slot, lo=0):
        for c, ch in enumerate(chains):
            ch[2 + slot][:, lo:] = score_fn(c, i, t, lo)

    def update(t, slot, diag=None):
        lo = 0 if diag is None else diag * tk
        for c, ch in enumerate(chains):
            st = ch[2 + slot][:, lo:]
            if diag is not None:
                st = jnp.where(_chunk_mask_t(tk, tq - lo), st, -jnp.inf)
            _softmax_step_t(st, _with_ones_rows(value_fn(c, t)), ch[0], ch[1], lo)

    scores(0, 0, 0)

    def query_tile(i, carry):
        for m_ref, acc_ref, _, _ in chains:
            m_ref[...] = jnp.full(m_ref.shape, -jnp.inf, F32)
            acc_ref[...] = jnp.zeros(acc_ref.shape, F32)

        def pair(p, c):
            t = 2 * p
            scores(i, t + 1, 1)
            update(t, 0)
            scores(i, t + 2, 0)
            update(t + 1, 1)
            return c

        lax.fori_loop(0, i * (ratio // 2), pair, 0)
        first_diag = ratio * i
        for d in range(ratio):
            if d + 1 < ratio:
                scores(i, first_diag + d + 1, (d + 1) % 2, lo=(d + 1) * tk)
            else:
                scores(jnp.minimum(i + 1, n_q_tiles - 1), 0, 0)
            update(first_diag + d, d % 2, diag=d)
        finalize_fn(i, [ch[1] for ch in chains])
        return carry

    lax.fori_loop(0, n_q_tiles, query_tile, 0)


def _mla_kernel(qt_ref, k_ref, vt_ref, o_ref, *scratch_refs):
    tq, tk = ATTN_Q_ROWS, ATTN_K_ROWS

    def score_fn(c, i, t, lo):
        rows = pl.ds(pl.multiple_of(t * tk, tk), tk)
        sl = slice(c * LANES, (c + 1) * LANES)
        return _dot(k_ref[rows, sl], qt_ref[i, sl, lo:])

    def value_fn(c, t):
        return vt_ref[t, c * MLA_V:(c + 1) * MLA_V, :]

    def finalize_fn(i, accs):
        ot = jnp.concatenate([_normalized(acc_ref, MLA_V) for acc_ref in accs], axis=0)
        o_ref[pl.ds(pl.multiple_of(i * tq, tq), tq), :] = ot.T.astype(BF16)

    _flash_attention(qt_ref.shape[0], scratch_refs, score_fn, value_fn, finalize_fn, tk, tq)


def _mla_call(qmt, km, vtm, batch, seq_len):
    n = km.shape[0]
    tq, tk, hps = ATTN_Q_ROWS, ATTN_K_ROWS, MLA_HEADS_PER_STEP
    return pl.pallas_call(
        _mla_kernel,
        grid=(batch, MLA_HEADS // hps),
        in_specs=[pl.BlockSpec((seq_len // tq, hps * LANES, tq), lambda b, h: (b, h, 0)),
                  pl.BlockSpec((seq_len, hps * LANES), lambda b, h: (b, h)),
                  pl.BlockSpec((seq_len // tk, hps * MLA_V, tk), lambda b, h: (b, h, 0))],
        out_specs=pl.BlockSpec((seq_len, hps * MLA_V), lambda b, h: (b, h)),
        out_shape=jax.ShapeDtypeStruct((n, MLA_HEADS * MLA_V), BF16),
        scratch_shapes=_attn_scratch(hps, MLA_V, tq, tk),
        compiler_params=pltpu.CompilerParams(dimension_semantics=("parallel", "parallel"),
                                             vmem_limit_bytes=VMEM_LIMIT_BYTES),
        name="mla_attn",
    )(qmt, km, vtm)


def _diff_kernel(lam_init, qt_ref, k_ref, vt_ref, lq1_ref, lk1_ref, lq2_ref, lk2_ref, subln_ref, o_ref,
                 *scratch_refs):
    tq, tk = ATTN_Q_ROWS, ATTN_K_ROWS
    hps = DIFF_HEADS_PER_STEP

    def score_fn(c, i, t, lo):
        rows = pl.ds(pl.multiple_of(t * tk, tk), tk)
        hd, f = c // 2, c % 2
        half = qt_ref[i, hd * LANES + f * DIFF_HEAD_DIM:hd * LANES + (f + 1) * DIFF_HEAD_DIM, lo:]
        zero = jnp.zeros_like(half)
        q = jnp.concatenate([half, zero] if f == 0 else [zero, half], axis=0)
        return _dot(k_ref[rows, hd * LANES:(hd + 1) * LANES], q)

    def value_fn(c, t):
        hd = c // 2
        return vt_ref[t, hd * DIFF_V_DIM:(hd + 1) * DIFF_V_DIM, :]

    lam = (jnp.exp(jnp.sum(lq1_ref[...] * lk1_ref[...], axis=-1, keepdims=True))
           - jnp.exp(jnp.sum(lq2_ref[...] * lk2_ref[...], axis=-1, keepdims=True)) + lam_init)
    subln = subln_ref[...] * (1.0 - lam_init)

    def finalize_fn(i, accs):
        heads = []
        for hd in range(hps):
            ot = _normalized(accs[2 * hd], DIFF_V_DIM) - lam * _normalized(accs[2 * hd + 1], DIFF_V_DIM)
            ot = ot * lax.rsqrt(jnp.sum(ot * ot, axis=0, keepdims=True) * (1.0 / DIFF_V_DIM) + EPS)
            heads.append(ot * subln)
        o_ref[pl.ds(pl.multiple_of(i * tq, tq), tq), :] = jnp.concatenate(heads, axis=0).T.astype(BF16)

    _flash_attention(qt_ref.shape[0], scratch_refs, score_fn, value_fn, finalize_fn, tk, tq)


def _diff_call(qdt, kd, vtd, lq1, lk1, lq2, lk2, subln_col, lam_init, batch, seq_len):
    n = kd.shape[0]
    tq, tk, hps = ATTN_Q_ROWS, ATTN_K_ROWS, DIFF_HEADS_PER_STEP
    small = lambda a: pl.BlockSpec(a.shape, lambda b, h: (0, 0))
    return pl.pallas_call(
        functools.partial(_diff_kernel, lam_init),
        grid=(batch, DIFF_HEADS // hps),
        in_specs=[pl.BlockSpec((seq_len // tq, hps * LANES, tq), lambda b, h: (b, h, 0)),
                  pl.BlockSpec((seq_len, hps * LANES), lambda b, h: (b, h)),
                  pl.BlockSpec((seq_len // tk, hps * DIFF_V_DIM, tk), lambda b, h: (b, h, 0)),
                  small(lq1), small(lk1), small(lq2), small(lk2), small(subln_col)],
        out_specs=pl.BlockSpec((seq_len, hps * LANES), lambda b, h: (b, h)),
        out_shape=jax.ShapeDtypeStruct((n, DIFF_V_WIDTH), BF16),
        scratch_shapes=_attn_scratch(2 * hps, DIFF_V_DIM, tq, tk),
        compiler_params=pltpu.CompilerParams(dimension_semantics=("parallel", "parallel"),
                                             vmem_limit_bytes=VMEM_LIMIT_BYTES),
        name="diff_attn",
    )(qdt, kd, vtd, lq1, lk1, lq2, lk2, subln_col)


def _merge_kernel(x_ref, om_ref, od_ref, sgm_ref, sgd_ref, wmu_ref, wdu_ref, wout_ref, gffn_ref, wrt_ref,
                  brt_ref, x1_ref, h2_ref, route_ref, route_t_ref, cnt_ref):
    merged = (sgm_ref[...].astype(F32) * _dot(om_ref[...], wmu_ref[...])
              + sgd_ref[...].astype(F32) * _dot(od_ref[...], wdu_ref[...]))
    x1 = x_ref[...] + _dot(merged.astype(BF16), wout_ref[...])
    x1_ref[...] = x1
    h2 = _rms(x1, x1.shape[-1]) * gffn_ref[...]
    h2_hi = h2.astype(BF16)
    h2_ref[...] = h2_hi
    tm = h2.shape[0]

    h2_lo = (h2 - h2_hi.astype(F32)).astype(BF16)
    by_hi = _dot_nt(wrt_ref[...], h2_hi)
    logits = by_hi[:LANES] + by_hi[LANES:] + _dot_nt(wrt_ref[:LANES, :], h2_lo) + brt_ref[...]
    row = lax.broadcasted_iota(jnp.int32, logits.shape, 0)
    neg = -jnp.inf
    big = jnp.int32(1 << 20)

    def top(vals):
        mx = jnp.max(vals, axis=0, keepdims=True)
        idx = jnp.min(jnp.where(vals == mx, row, big), axis=0, keepdims=True)
        return mx, idx

    gl = jnp.where((row >= N_EXPERTS) & (row < N_EXPERTS + N_GROUPS), logits, neg)
    gmax, gidx = top(gl)
    pg_sel = 1.0 / jnp.sum(jnp.exp(gl - gmax), axis=0, keepdims=True)
    el = jnp.where((row < N_EXPERTS) & (row // EXPERTS_PER_GROUP == gidx - N_EXPERTS), logits, neg)
    m1, i1 = top(el)
    m2, i2 = top(jnp.where(row == i1, neg, el))
    e2 = jnp.exp(m2 - m1)
    w1 = pg_sel / (1.0 + e2)
    w2 = w1 * e2

    sel = jnp.where((row == i1) | (row == i2), 1.0, 0.0).astype(BF16)
    t_row = lax.broadcasted_iota(jnp.int32, (tm, tm), 0)
    t_col = lax.broadcasted_iota(jnp.int32, (tm, tm), 1)
    rank = _dot(sel, jnp.where(t_row < t_col, 1.0, 0.0).astype(BF16))
    cnt = _dot(sel, jnp.ones((tm, tm), BF16))
    seg = jnp.floor((cnt + (SEG_ALIGN - 1)) * (1.0 / SEG_ALIGN))
    e_row = lax.broadcasted_iota(jnp.int32, (LANES, LANES), 0)
    e_col = lax.broadcasted_iota(jnp.int32, (LANES, LANES), 1)
    off = _dot(jnp.where(e_col < e_row, 1.0, 0.0).astype(BF16), seg.astype(BF16)) * SEG_ALIGN
    dest = off + rank
    d1 = jnp.sum(jnp.where(row == i1, dest, 0.0), axis=0, keepdims=True)
    d2 = jnp.sum(jnp.where(row == i2, dest, 0.0), axis=0, keepdims=True)
    route_t = jnp.where(row == 0, d1, jnp.where(row == 1, d2, jnp.where(row == 2, w1, jnp.where(row == 3, w2, 0.0))))
    route_t_ref[0] = route_t[0:8]
    route_ref[...] = route_t.T
    cnt_ref[0] = (seg[:, :LANES] * SEG_ALIGN).T[0:1]


def _merge_call(x2, om, od, sgm, sgd, p):
    n, d = x2.shape
    tm = MERGE_ROWS
    row = lambda i: (i, 0)
    const = lambda i: (0, 0)
    weights = [p["wmu"], p["wdu"], p["wout"], p["gffn"], p["wrt"], p["brt"]]
    return pl.pallas_call(
        _merge_kernel,
        grid=(n // tm,),
        in_specs=([pl.BlockSpec((tm, a.shape[1]), row) for a in (x2, om, od, sgm, sgd)]
                  + [pl.BlockSpec(w.shape, const) for w in weights]),
        out_specs=[pl.BlockSpec((tm, d), row), pl.BlockSpec((tm, d), row), pl.BlockSpec((tm, LANES), row),
                   pl.BlockSpec((1, 8, tm), lambda i: (i, 0, 0)), pl.BlockSpec((1, 1, LANES), lambda i: (i, 0, 0))],
        out_shape=[jax.ShapeDtypeStruct((n, d), F32), jax.ShapeDtypeStruct((n, d), BF16),
                   jax.ShapeDtypeStruct((n, LANES), F32), jax.ShapeDtypeStruct((n // tm, 8, tm), F32),
                   jax.ShapeDtypeStruct((n // tm, 1, LANES), F32)],
        compiler_params=pltpu.CompilerParams(dimension_semantics=("parallel",), vmem_limit_bytes=VMEM_LIMIT_BYTES),
        name="merge_router",
    )(x2, om, od, sgm, sgd, *weights)


def _segment_copies(i, seg_dst_ref, seg_rows_ref, tile_off_ref, global_ref, tile_ref, sem, to_global):
    def body(e, carry):
        k = i * N_EXPERTS + e
        rows = pl.multiple_of(seg_rows_ref[k], SEG_ALIGN)

        @pl.when(rows > 0)
        def _():
            g = global_ref.at[pl.ds(pl.multiple_of(seg_dst_ref[k], SEG_ALIGN), rows)]
            t = tile_ref.at[pl.ds(pl.multiple_of(tile_off_ref[k], SEG_ALIGN), rows)]
            src, dst = (t, g) if to_global else (g, t)
            pltpu.make_async_copy(src, dst, sem).start()

        return carry

    lax.fori_loop(0, N_EXPERTS, body, 0)


def _wait_rows(tile_ref, rows, sem):
    @pl.when(rows > 0)
    def _():
        view = tile_ref.at[pl.ds(0, pl.multiple_of(rows, SEG_ALIGN))]
        pltpu.make_async_copy(view, view, sem).wait()


def _zero_unused_rows(tail_dst_ref, tail_rows_ref, n_used_ref, xs_ref, zero_ref, sem, start):
    n_tiles = xs_ref.shape[0] // EXPERT_ROWS
    if start:
        zero_ref[...] = jnp.zeros(zero_ref.shape, BF16)

    def tail(e, total):
        rows = pl.multiple_of(tail_rows_ref[e], SEG_ALIGN)
        if start:
            @pl.when(rows > 0)
            def _():
                dst = xs_ref.at[pl.ds(pl.multiple_of(tail_dst_ref[e], SEG_ALIGN), rows)]
                pltpu.make_async_copy(zero_ref.at[pl.ds(0, rows)], dst, sem).start()

        return total + rows

    total = lax.fori_loop(0, N_EXPERTS, tail, 0)
    if not start:
        _wait_rows(xs_ref, total + (n_tiles - n_used_ref[0]) * EXPERT_ROWS, sem)
        return

    def unused(t, carry):
        dst = xs_ref.at[pl.ds(pl.multiple_of(t * EXPERT_ROWS, EXPERT_ROWS), EXPERT_ROWS)]
        pltpu.make_async_copy(zero_ref, dst, sem).start()
        return carry

    lax.fori_loop(n_used_ref[0], n_tiles, unused, 0)


def _sort_kernel(seg_dst_ref, seg_rows_ref, tile_off_ref, tile_rows_ref, tail_dst_ref, tail_rows_ref, n_used_ref,
                 h2_ref, route_t_ref, xs_ref, sorted_ref, zero_ref, sem, zero_sem):
    i = pl.program_id(0)
    tm = h2_ref.shape[0]

    @pl.when(i == 0)
    def _():
        _zero_unused_rows(tail_dst_ref, tail_rows_ref, n_used_ref, xs_ref, zero_ref, zero_sem, True)

    d1 = route_t_ref[0, 0:1, :].astype(jnp.int32)
    d2 = route_t_ref[0, 1:2, :].astype(jnp.int32)
    slot = i % 2

    def sort_rows(n_rows):
        r = lax.broadcasted_iota(jnp.int32, (n_rows, tm), 0)
        perm = jnp.where((r == d1) | (r == d2), 1.0, 0.0).astype(BF16)
        sorted_ref[slot, 0:n_rows] = _dot(perm, h2_ref[...]).astype(BF16)

    @pl.when(tile_rows_ref[i] <= SORT_ROWS_COMMON)
    def _():
        sort_rows(SORT_ROWS_COMMON)

    @pl.when(tile_rows_ref[i] > SORT_ROWS_COMMON)
    def _():
        sort_rows(SORT_ROWS)

    _segment_copies(i, seg_dst_ref, seg_rows_ref, tile_off_ref, xs_ref, sorted_ref.at[slot], sem.at[slot], True)

    @pl.when(i > 0)
    def _():
        _wait_rows(sorted_ref.at[1 - slot], tile_rows_ref[jnp.maximum(i - 1, 0)], sem.at[1 - slot])

    @pl.when(i == pl.num_programs(0) - 1)
    def _():
        _wait_rows(sorted_ref.at[slot], tile_rows_ref[i], sem.at[slot])
        _zero_unused_rows(tail_dst_ref, tail_rows_ref, n_used_ref, xs_ref, zero_ref, zero_sem, False)


def _sort_call(h2, route_t, sched, max_rows):
    n, d = h2.shape
    tm = ROUTE_ROWS
    return pl.pallas_call(
        _sort_kernel,
        grid_spec=pltpu.PrefetchScalarGridSpec(
            num_scalar_prefetch=7,
            grid=(n // tm,),
            in_specs=[pl.BlockSpec((tm, d), lambda i, *_: (i, 0)),
                      pl.BlockSpec((1, 8, tm), lambda i, *_: (i, 0, 0))],
            out_specs=pl.BlockSpec(memory_space=pl.ANY),
            scratch_shapes=[pltpu.VMEM((2, SORT_ROWS, d), BF16), pltpu.VMEM((EXPERT_ROWS, d), BF16),
                            pltpu.SemaphoreType.DMA((2,)), pltpu.SemaphoreType.DMA(())],
        ),
        out_shape=jax.ShapeDtypeStruct((max_rows, d), BF16),
        compiler_params=pltpu.CompilerParams(dimension_semantics=("arbitrary",), vmem_limit_bytes=VMEM_LIMIT_BYTES),
        name="moe_sort",
    )(sched["seg_dst"], sched["seg_rows"], sched["tile_off"], sched["tile_rows"], sched["tail_dst"],
      sched["tail_rows"], sched["n_used"], h2, route_t)


def _expert_kernel(first_tile_ref, n_tiles_ref, n_used_ref, xs_ref, wg_ref, wu_ref, wd_ref, ys_ref,
                   x_buf, y_buf, wg_bf, wu_bf, wd_bf, in_sem, out_sem, zero_sem):
    e = pl.program_id(0)
    tr = EXPERT_ROWS
    n = n_tiles_ref[e]
    total_tiles = ys_ref.shape[0] // tr
    tile_rows = lambda t: pl.ds(pl.multiple_of(t * tr, tr), tr)

    def fetch(t, slot):
        pltpu.make_async_copy(xs_ref.at[tile_rows(first_tile_ref[e] + t)], x_buf.at[slot], in_sem.at[slot]).start()

    def wait_fetch(slot):
        pltpu.make_async_copy(xs_ref.at[tile_rows(0)], x_buf.at[slot], in_sem.at[slot]).wait()

    def write_back(t, slot):
        pltpu.make_async_copy(y_buf.at[slot], ys_ref.at[tile_rows(first_tile_ref[e] + t)], out_sem.at[slot]).start()

    def wait_write_back(slot):
        pltpu.make_async_copy(y_buf.at[slot], ys_ref.at[tile_rows(0)], out_sem.at[slot]).wait()

    @pl.when(e == 0)
    def _():
        y_buf[0] = jnp.zeros(y_buf.shape[1:], BF16)

        def zero_tile(t, carry):
            pltpu.make_async_copy(y_buf.at[0], ys_ref.at[tile_rows(t)], zero_sem).start()
            return carry

        lax.fori_loop(n_used_ref[0], total_tiles, zero_tile, 0)

        def wait_zero_tile(t, carry):
            pltpu.make_async_copy(y_buf.at[0], ys_ref.at[tile_rows(0)], zero_sem).wait()
            return carry

        lax.fori_loop(n_used_ref[0], total_tiles, wait_zero_tile, 0)

    @pl.when(n > 0)
    def _():
        fetch(0, 0)
        wg_bf[...] = wg_ref[0].astype(BF16)
        wu_bf[...] = wu_ref[0].astype(BF16)
        wd_bf[...] = wd_ref[0].astype(BF16)

        def tile(t, carry):
            slot = t % 2

            @pl.when(t + 1 < n)
            def _():
                fetch(t + 1, 1 - slot)

            wait_fetch(slot)

            @pl.when(t >= 2)
            def _():
                wait_write_back(slot)

            xs = x_buf[slot]
            gate = _dot(xs, wg_bf[...])
            up = _dot(xs, wu_bf[...])
            hidden = (gate * jax.nn.sigmoid(gate) * up).astype(BF16)
            y_buf[slot] = _dot(hidden, wd_bf[...]).astype(BF16)
            write_back(t, slot)
            return carry

        lax.fori_loop(0, n, tile, 0)

        @pl.when(n >= 2)
        def _():
            wait_write_back(n % 2)

        wait_write_back((n - 1) % 2)


def _expert_call(xs, wg, wu, wd, sched):
    rows, d = xs.shape
    tr = EXPERT_ROWS
    wsel = lambda e, *_: (e, 0, 0)
    return pl.pallas_call(
        _expert_kernel,
        grid_spec=pltpu.PrefetchScalarGridSpec(
            num_scalar_prefetch=3,
            grid=(N_EXPERTS,),
            in_specs=[pl.BlockSpec(memory_space=pl.ANY),
                      pl.BlockSpec((1, d, EXPERT_FF), wsel), pl.BlockSpec((1, d, EXPERT_FF), wsel),
                      pl.BlockSpec((1, EXPERT_FF, d), wsel)],
            out_specs=pl.BlockSpec(memory_space=pl.ANY),
            scratch_shapes=[pltpu.VMEM((2, tr, d), BF16), pltpu.VMEM((2, tr, d), BF16),
                            pltpu.VMEM((d, EXPERT_FF), BF16), pltpu.VMEM((d, EXPERT_FF), BF16),
                            pltpu.VMEM((EXPERT_FF, d), BF16),
                            pltpu.SemaphoreType.DMA((2,)), pltpu.SemaphoreType.DMA((2,)), pltpu.SemaphoreType.DMA(())],
        ),
        out_shape=jax.ShapeDtypeStruct((rows, d), BF16),
        compiler_params=pltpu.CompilerParams(dimension_semantics=("arbitrary",), vmem_limit_bytes=VMEM_LIMIT_BYTES),
        name="moe_experts",
    )(sched["first_tile"], sched["n_tiles"], sched["n_used"], xs, wg, wu, wd)


def _combine_kernel(seg_dst_ref, seg_rows_ref, tile_off_ref, tile_rows_ref, ys_ref, route_ref, x1_ref, o_ref,
                    buf_ref, sem):
    i = pl.program_id(0)
    tm = x1_ref.shape[0]
    slot = i % 2

    def fetch(tile, into):
        buf_ref[into] = jnp.zeros(buf_ref.shape[1:], BF16)
        _segment_copies(tile, seg_dst_ref, seg_rows_ref, tile_off_ref, ys_ref, buf_ref.at[into], sem.at[into], False)

    @pl.when(i == 0)
    def _():
        fetch(i, slot)

    @pl.when(i + 1 < pl.num_programs(0))
    def _():
        fetch(i + 1, 1 - slot)

    route = route_ref[...]
    d1 = route[:, 0:1].astype(jnp.int32)
    d2 = route[:, 1:2].astype(jnp.int32)
    w1 = route[:, 2:3]
    w2 = route[:, 3:4]
    _wait_rows(buf_ref.at[slot], tile_rows_ref[i], sem.at[slot])

    def combine_rows(n_rows):
        r = lax.broadcasted_iota(jnp.int32, (tm, n_rows), 1)
        weights = (jnp.where(r == d1, w1, 0.0) + jnp.where(r == d2, w2, 0.0)).astype(BF16)
        o_ref[...] = x1_ref[...] + _dot(weights, buf_ref[slot, 0:n_rows])

    @pl.when(tile_rows_ref[i] <= SORT_ROWS_COMMON)
    def _():
        combine_rows(SORT_ROWS_COMMON)

    @pl.when(tile_rows_ref[i] > SORT_ROWS_COMMON)
    def _():
        combine_rows(SORT_ROWS)


def _combine_call(ys, route, x1, sched):
    n, d = x1.shape
    tm = ROUTE_ROWS
    return pl.pallas_call(
        _combine_kernel,
        grid_spec=pltpu.PrefetchScalarGridSpec(
            num_scalar_prefetch=4,
            grid=(n // tm,),
            in_specs=[pl.BlockSpec(memory_space=pl.ANY),
                      pl.BlockSpec((tm, LANES), lambda i, *_: (i, 0)),
                      pl.BlockSpec((tm, d), lambda i, *_: (i, 0))],
            out_specs=pl.BlockSpec((tm, d), lambda i, *_: (i, 0)),
            scratch_shapes=[pltpu.VMEM((2, SORT_ROWS, d), BF16), pltpu.SemaphoreType.DMA((2,))],
        ),
        out_shape=jax.ShapeDtypeStruct((n, d), F32),
        compiler_params=pltpu.CompilerParams(dimension_semantics=("arbitrary",), vmem_limit_bytes=VMEM_LIMIT_BYTES),
        name="moe_combine",
    )(sched["seg_dst"], sched["seg_rows"], sched["tile_off"], sched["tile_rows"], ys, route, x1)


def _schedule_kernel(cnt_ref, seg_dst_ref, tile_off_ref, tile_rows_ref, misc_ref):
    hp = functools.partial(jnp.dot, preferred_element_type=F32, precision=lax.Precision.HIGHEST)
    cnt = cnt_ref[...]
    n_tiles = cnt.shape[0]
    tile_before = jnp.where(lax.broadcasted_iota(jnp.int32, (n_tiles, n_tiles), 1)
                            < lax.broadcasted_iota(jnp.int32, (n_tiles, n_tiles), 0), 1.0, 0.0)
    expert_before = jnp.where(lax.broadcasted_iota(jnp.int32, (LANES, LANES), 0)
                              < lax.broadcasted_iota(jnp.int32, (LANES, LANES), 1), 1.0, 0.0)
    expert_rows = jnp.sum(cnt, axis=0, keepdims=True)
    region = jnp.floor((expert_rows + (EXPERT_ROWS - 1)) * (1.0 / EXPERT_ROWS)) * EXPERT_ROWS
    region_start = hp(jnp.broadcast_to(region, (8, LANES)), expert_before)[0:1]
    seg_dst_ref[...] = (region_start + hp(tile_before, cnt)).astype(jnp.int32)
    tile_off_ref[...] = hp(cnt, expert_before).astype(jnp.int32)
    tile_rows_ref[...] = jnp.broadcast_to(jnp.sum(cnt, axis=-1, keepdims=True), cnt.shape).astype(jnp.int32)
    n_used = jnp.sum(region, axis=-1, keepdims=True) * (1.0 / EXPERT_ROWS)
    row = lax.broadcasted_iota(jnp.int32, (8, LANES), 0)
    per_tile = 1.0 / EXPERT_ROWS
    misc = jnp.where(row == 0, region_start + expert_rows,
                     jnp.where(row == 1, region - expert_rows,
                               jnp.where(row == 2, region_start * per_tile,
                                         jnp.where(row == 3, region * per_tile, n_used))))
    misc_ref[...] = misc.astype(jnp.int32)


def _moe_schedule(cnt, n_tokens):
    n_tiles = cnt.shape[0]
    table = jax.ShapeDtypeStruct((n_tiles, LANES), jnp.int32)
    seg_dst, tile_off, tile_rows, misc = pl.pallas_call(
        _schedule_kernel,
        out_shape=[table, table, table, jax.ShapeDtypeStruct((8, LANES), jnp.int32)],
        name="moe_schedule",
    )(cnt.reshape(n_tiles, LANES))
    max_rows = 2 * n_tokens + n_tiles * N_EXPERTS * (SEG_ALIGN - 1) + N_EXPERTS * (EXPERT_ROWS - 1)
    max_tiles = -(-max_rows // EXPERT_ROWS)
    flat = lambda a: a[:, :N_EXPERTS].reshape(-1)
    sched = {
        "seg_dst": flat(seg_dst),
        "seg_rows": flat(cnt.reshape(n_tiles, LANES).astype(jnp.int32)),
        "tile_off": flat(tile_off),
        "tile_rows": tile_rows[:, 0],
        "tail_dst": misc[0, :N_EXPERTS],
        "tail_rows": misc[1, :N_EXPERTS],
        "first_tile": misc[2, :N_EXPERTS],
        "n_tiles": misc[3, :N_EXPERTS],
        "n_used": misc[4, :1],
    }
    return sched, max_tiles * EXPERT_ROWS


def _rotary_tables(seq_len, rot_dim, period, first, gain, scale):
    half = rot_dim // 2
    pos = jnp.arange(seq_len, dtype=F32)
    inv = 1.0 / (ROPE_THETA ** (jnp.arange(0, rot_dim, 2, dtype=F32) / rot_dim))
    ang = pos[:, None] * inv[None, :]
    cos, sin = jnp.cos(ang), jnp.sin(ang)
    lane = jnp.arange(LANES)
    rel = (lane % period) - first
    active = (rel >= 0) & (rel < rot_dim)
    idx = jnp.clip(rel, 0, rot_dim - 1) % half
    sign = jnp.where(rel < half, -1.0, 1.0)
    partner = jnp.where(active, jnp.where(rel < half, lane + half, lane - half), lane)
    c = jnp.where(active[None, :], cos[:, idx], 1.0)
    s = jnp.where(active[None, :], sin[:, idx] * sign[None, :], 0.0)
    gain = gain.astype(F32)
    return (c * gain[None, :] * scale).astype(F32), (s * gain[partner][None, :] * scale).astype(F32)


def _head_pad(w, heads, width):
    r = w.shape[0]
    w = w.reshape(r, heads, width)
    return jnp.pad(w, ((0, 0), (0, 0), (0, LANES - width))).reshape(r, heads * LANES)


def _layer_params(l, seq_len, norm_mix, w_in, mla_q_latent_norm, w_mla_uq, mla_kv_latent_norm, w_mla_ukv,
                  mla_q_gain, mla_k_gain, diff_q_gain, diff_k_gain, w_mla_up, w_diff_up, w_out, norm_ffn,
                  w_router_group, b_router_group, w_router_expert, b_router_expert):
    d = w_in.shape[1]
    sizes = (MLA_Q_RANK, MLA_KV_RANK, MLA_ROPE, DIFF_QK_WIDTH, DIFF_QK_WIDTH, DIFF_V_WIDTH, d, d)
    offs = [0]
    for s in sizes:
        offs.append(offs[-1] + s)
    wi = w_in[l]
    seg = [wi[:, offs[k]:offs[k + 1]] for k in range(len(sizes))]
    row = lambda g: g.astype(F32)[None, :]
    p = {}
    p["gmix"] = row(norm_mix[l])
    p["wql"] = seg[0].astype(BF16)
    p["wkvl"] = seg[1].astype(BF16)
    p["wkr"] = jnp.pad(seg[2], ((0, 0), (MLA_NOPE, LANES - MLA_QK))).astype(BF16)
    p["wdk"] = seg[4].astype(BF16)
    p["wdqvt"] = jnp.concatenate([seg[3].T, seg[5].T], axis=0).astype(BF16)
    p["wgm"], p["wgd"] = seg[6].astype(BF16), seg[7].astype(BF16)
    p["gql"] = row(mla_q_latent_norm[l])
    p["wuqt"] = _head_pad(w_mla_uq[l], MLA_HEADS, MLA_QK).T.astype(BF16)
    p["gkvl"] = row(mla_kv_latent_norm[l])
    ukv = w_mla_ukv[l].reshape(MLA_KV_RANK, MLA_HEADS, MLA_NOPE + MLA_V)
    p["wuk"] = _head_pad(ukv[:, :, :MLA_NOPE].reshape(MLA_KV_RANK, -1), MLA_HEADS, MLA_NOPE).astype(BF16)
    p["wuvt"] = ukv[:, :, MLA_NOPE:].reshape(MLA_KV_RANK, -1).T.astype(BF16)
    gq = jnp.pad(mla_q_gain[l], (0, LANES - MLA_QK))
    gk = jnp.pad(mla_k_gain[l], (0, LANES - MLA_QK))
    nope = jnp.arange(LANES) < MLA_NOPE
    p["gkn"] = jnp.where(nope, gk, 0.0).astype(F32)[None, :]
    aq, bq = _rotary_tables(seq_len, MLA_ROPE, LANES, MLA_NOPE, gq, LOG2E * MLA_QK ** -0.5)
    p["aq"], p["bq"] = aq.T, bq.T
    ak, bk = _rotary_tables(seq_len, MLA_ROPE, LANES, MLA_NOPE, jnp.where(nope, 0.0, gk), 1.0)
    p["ak"], p["bk"] = ak, bk
    adq, bdq = _rotary_tables(seq_len, DIFF_ROPE, DIFF_HEAD_DIM, 0, jnp.tile(diff_q_gain[l], 2),
                              LOG2E * DIFF_HEAD_DIM ** -0.5)
    p["adq"], p["bdq"] = adq.T, bdq.T
    p["adk"], p["bdk"] = _rotary_tables(seq_len, DIFF_ROPE, DIFF_HEAD_DIM, 0, jnp.tile(diff_k_gain[l], 2), 1.0)
    p["wmu"] = w_mla_up[l].astype(BF16)
    p["wdu"] = w_diff_up[l].astype(BF16)
    p["wout"] = w_out[l].astype(BF16)
    p["gffn"] = row(norm_ffn[l])
    wr = jnp.concatenate([w_router_expert[l], w_router_group[l]], axis=1).astype(F32)
    wrt = jnp.pad(wr, ((0, 0), (0, LANES - wr.shape[1]))).T
    wrt_hi = wrt.astype(BF16)
    p["wrt"] = jnp.concatenate([wrt_hi, (wrt - wrt_hi.astype(F32)).astype(BF16)], axis=0)
    br = jnp.concatenate([b_router_expert[l], b_router_group[l]]).astype(F32)
    p["brt"] = jnp.broadcast_to(jnp.pad(br, (0, LANES - br.shape[0]))[:, None], (LANES, MERGE_ROWS))
    return p


def kernel(x, norm_mix, w_in, mla_q_latent_norm, w_mla_uq, mla_kv_latent_norm, w_mla_ukv, mla_q_gain, mla_k_gain, diff_q_gain, diff_k_gain, lambda_q1, lambda_k1, lambda_q2, lambda_k2, diff_subln, w_mla_up, w_diff_up, w_out, norm_ffn, w_router_group, b_router_group, w_router_expert, b_router_expert, w_expert_gate, w_expert_up, w_expert_down):
    batch, seq_len, d = x.shape
    x2 = x.reshape(batch * seq_len, d)
    row = lambda g: g.astype(F32)[None, :]
    for l in range(norm_mix.shape[0]):
        lam_init = 0.8 - 0.6 * math.exp(-0.3 * l)
        p = _layer_params(l, seq_len, norm_mix, w_in, mla_q_latent_norm, w_mla_uq, mla_kv_latent_norm, w_mla_ukv,
                          mla_q_gain, mla_k_gain, diff_q_gain, diff_k_gain, w_mla_up, w_diff_up, w_out, norm_ffn,
                          w_router_group, b_router_group, w_router_expert, b_router_expert)
        qmt, km, vtm, qdt, kd, vtd, sgm, sgd = _proj_call(x2, seq_len, p)
        om = _mla_call(qmt, km, vtm, batch, seq_len)
        od = _diff_call(qdt, kd, vtd, row(lambda_q1[l]), row(lambda_k1[l]), row(lambda_q2[l]), row(lambda_k2[l]),
                        diff_subln[l].astype(F32)[:, None], lam_init, batch, seq_len)
        x1, h2, route, route_t, cnt = _merge_call(x2, om, od, sgm, sgd, p)
        sched, max_rows = _moe_schedule(cnt, x2.shape[0])
        xs = _sort_call(h2, route_t, sched, max_rows)
        ys = _expert_call(xs, w_expert_gate[l], w_expert_up[l], w_expert_down[l], sched)
        x2 = _combine_call(ys, route, x1, sched)
    return x2.reshape(batch, seq_len, d)
```

```python
import functools
import math

import jax
import jax.numpy as jnp
from jax import lax
from jax.experimental import pallas as pl
from jax.experimental.pallas import tpu as pltpu

CHUNK = 64
ROPE_THETA = 500000.0
EPS = 1e-6

MLA_HEADS = 8
MLA_NOPE = 64
MLA_ROPE = 32
MLA_V = 64
MLA_QK = MLA_NOPE + MLA_ROPE
MLA_Q_RANK = 256
MLA_KV_RANK = 128

DIFF_HEADS = 4
DIFF_HEAD_DIM = 64
DIFF_V_DIM = 2 * DIFF_HEAD_DIM
DIFF_ROPE = DIFF_HEAD_DIM // 4
DIFF_QK_WIDTH = DIFF_HEADS * 2 * DIFF_HEAD_DIM
DIFF_V_WIDTH = DIFF_HEADS * DIFF_V_DIM

N_GROUPS = 4
EXPERTS_PER_GROUP = 8
N_EXPERTS = N_GROUPS * EXPERTS_PER_GROUP
EXPERT_FF = 256

LANES = 128
VMEM_LIMIT_BYTES = 48 * 1024 * 1024

PROJ_ROWS = 512
ATTN_Q_ROWS = 512
ATTN_K_ROWS = 256
MERGE_ROWS = 512
ROUTE_ROWS = MERGE_ROWS
SEG_ALIGN = 16
SORT_ROWS = 2 * ROUTE_ROWS + N_EXPERTS * SEG_ALIGN
SORT_ROWS_COMMON = 2 * ROUTE_ROWS + N_EXPERTS * SEG_ALIGN // 2
EXPERT_ROWS = 512
MLA_HEADS_PER_STEP = 4
DIFF_HEADS_PER_STEP = 2
LOG2E = 1.4426950408889634

BF16 = jnp.bfloat16
F32 = jnp.float32


def _dot(a, b):
    return jnp.dot(a, b, preferred_element_type=F32)


def _dot_nt(a, b):
    return lax.dot_general(a, b, (((1,), (1,)), ((), ())), preferred_element_type=F32)


def _rms(x, width):
    return x * lax.rsqrt(jnp.sum(x * x, axis=-1, keepdims=True) * (1.0 / width) + EPS)


def _rotary_partner(y, half):
    lane = lax.broadcasted_iota(jnp.int32, y.shape, 1)
    up = pltpu.roll(y, LANES - half, 1)
    down = pltpu.roll(y, half, 1)
    return jnp.where((lane // half) % 2 == 0, up, down)


def _swap_row_blocks(y, first, half, period):
    parts = []
    for base in range(0, y.shape[0], period):
        a = base + first
        parts += [y[base:a], y[a + half:a + 2 * half], y[a:a + half], y[a + 2 * half:base + period]]
    return jnp.concatenate([p for p in parts if p.shape[0]], axis=0)


def _store_k_tiles(o_ref, vt):
    tk = o_ref.shape[-1]
    for c in range(o_ref.shape[0]):
        o_ref[c] = vt[:, c * tk:(c + 1) * tk].astype(BF16)


def _proj_kernel(x_ref, gmix_ref, wql_ref, wkvl_ref, wkr_ref, wdk_ref, wdqvt_ref, wgm_ref, wgd_ref,
                 gql_ref, wuqt_ref, gkvl_ref, wuk_ref, wuvt_ref, gkn_ref,
                 aq_ref, bq_ref, adq_ref, bdq_ref, ak_ref, bk_ref, adk_ref, bdk_ref,
                 qmt_ref, km_ref, vtm_ref, qdt_ref, kd_ref, vtd_ref, sgm_ref, sgd_ref):
    x = x_ref[...]
    h = (_rms(x, x.shape[-1]) * gmix_ref[...]).astype(BF16)

    ql = (_rms(_dot(h, wql_ref[...]), MLA_Q_RANK) * gql_ref[...]).astype(BF16)
    qt = _dot_nt(wuqt_ref[...], ql)
    aq, bq = aq_ref[...], bq_ref[...]
    for hd in range(MLA_HEADS):
        rows = slice(hd * LANES, (hd + 1) * LANES)
        qh = qt[rows]
        r = lax.rsqrt(jnp.sum(qh * qh, axis=0, keepdims=True) * (1.0 / MLA_QK) + EPS)
        y = (qh * aq + _swap_row_blocks(qh, MLA_NOPE, MLA_ROPE // 2, LANES) * bq) * r
        qmt_ref[0, rows, :] = y.astype(BF16)

    kvl = (_rms(_dot(h, wkvl_ref[...]), MLA_KV_RANK) * gkvl_ref[...]).astype(BF16)
    kr = _dot(h, wkr_ref[...])
    kr_rot = kr * ak_ref[...] + _rotary_partner(kr, MLA_ROPE // 2) * bk_ref[...]
    kr_ss = jnp.sum(kr * kr, axis=-1, keepdims=True)
    kn = _dot(kvl, wuk_ref[...])
    _store_k_tiles(vtm_ref, _dot_nt(wuvt_ref[...], kvl))
    gkn = gkn_ref[...]
    for hd in range(MLA_HEADS):
        sl = slice(hd * LANES, (hd + 1) * LANES)
        knh = kn[:, sl]
        r = lax.rsqrt((jnp.sum(knh * knh, axis=-1, keepdims=True) + kr_ss) * (1.0 / MLA_QK) + EPS)
        km_ref[:, sl] = ((knh * gkn + kr_rot) * r).astype(BF16)

    qvt = _dot_nt(wdqvt_ref[...], h)
    _store_k_tiles(vtd_ref, qvt[DIFF_QK_WIDTH:])
    adq, bdq = adq_ref[...], bdq_ref[...]
    for hd in range(DIFF_HEADS):
        rows = slice(hd * LANES, (hd + 1) * LANES)
        qh = qvt[rows]
        t = qh * adq + _swap_row_blocks(qh, 0, DIFF_ROPE // 2, DIFF_HEAD_DIM) * bdq
        halves = []
        for f in range(2):
            part = qh[f * DIFF_HEAD_DIM:(f + 1) * DIFF_HEAD_DIM]
            r = lax.rsqrt(jnp.sum(part * part, axis=0, keepdims=True) * (1.0 / DIFF_HEAD_DIM) + EPS)
            halves.append(t[f * DIFF_HEAD_DIM:(f + 1) * DIFF_HEAD_DIM] * r)
        qdt_ref[0, rows, :] = jnp.concatenate(halves, axis=0).astype(BF16)

    kd = _dot(h, wdk_ref[...])
    adk, bdk = adk_ref[...], bdk_ref[...]
    for hd in range(DIFF_HEADS):
        sl = slice(hd * LANES, (hd + 1) * LANES)
        th = kd[:, sl]
        lane = lax.broadcasted_iota(jnp.int32, th.shape, 1)
        sq = th * th
        lo = jnp.sum(jnp.where(lane < DIFF_HEAD_DIM, sq, 0.0), axis=-1, keepdims=True)
        tot = jnp.sum(sq, axis=-1, keepdims=True)
        r = lax.rsqrt(jnp.where(lane < DIFF_HEAD_DIM, lo, tot - lo) * (1.0 / DIFF_HEAD_DIM) + EPS)
        kd_ref[:, sl] = ((th * adk + _rotary_partner(th, DIFF_ROPE // 2) * bdk) * r).astype(BF16)

    sgm_ref[...] = jax.nn.sigmoid(_dot(h, wgm_ref[...])).astype(BF16)
    sgd_ref[...] = jax.nn.sigmoid(_dot(h, wgd_ref[...])).astype(BF16)


def _proj_call(x2, seq_len, p):
    n, d = x2.shape
    tm = PROJ_ROWS
    pos_blocks = seq_len // tm
    row = lambda i: (i, 0)
    const = lambda i: (0, 0)
    weights = [p["gmix"], p["wql"], p["wkvl"], p["wkr"], p["wdk"], p["wdqvt"], p["wgm"], p["wgd"],
               p["gql"], p["wuqt"], p["gkvl"], p["wuk"], p["wuvt"], p["gkn"]]
    feature_major_tables = [p["aq"], p["bq"], p["adq"], p["bdq"]]
    token_major_tables = [p["ak"], p["bk"], p["adk"], p["bdk"]]
    in_specs = ([pl.BlockSpec((tm, d), row)]
                + [pl.BlockSpec(w.shape, const) for w in weights]
                + [pl.BlockSpec((LANES, tm), lambda i: (0, i % pos_blocks)) for _ in feature_major_tables]
                + [pl.BlockSpec((tm, LANES), lambda i: (i % pos_blocks, 0)) for _ in token_major_tables])
    tk = ATTN_K_ROWS
    k_tiles = lambda width: (pl.BlockSpec((tm // tk, width, tk), lambda i: (i, 0, 0)),
                             jax.ShapeDtypeStruct((n // tk, width, tk), BF16))
    token_major = lambda width: (pl.BlockSpec((tm, width), row), jax.ShapeDtypeStruct((n, width), BF16))
    assert tm == ATTN_Q_ROWS
    feature_major = lambda width: (pl.BlockSpec((1, width, tm), lambda i: (i, 0, 0)),
                                   jax.ShapeDtypeStruct((n // tm, width, tm), BF16))
    outs = [feature_major(MLA_HEADS * LANES), token_major(MLA_HEADS * LANES), k_tiles(MLA_HEADS * MLA_V),
            feature_major(DIFF_QK_WIDTH), token_major(DIFF_QK_WIDTH), k_tiles(DIFF_V_WIDTH),
            token_major(d), token_major(d)]
    return pl.pallas_call(
        _proj_kernel,
        grid=(n // tm,),
        in_specs=in_specs,
        out_specs=[o[0] for o in outs],
        out_shape=[o[1] for o in outs],
        compiler_params=pltpu.CompilerParams(dimension_semantics=("parallel",), vmem_limit_bytes=VMEM_LIMIT_BYTES),
        name="proj",
    )(x2, *weights, *feature_major_tables, *token_major_tables)


def _chunk_mask_t(tk, width):
    kc = lax.broadcasted_iota(jnp.int32, (tk, width), 0) // CHUNK
    qc = lax.broadcasted_iota(jnp.int32, (tk, width), 1) // CHUNK
    return kc <= qc


ONES_ROWS = 16


def _with_ones_rows(vt):
    return jnp.concatenate([vt, jnp.ones((ONES_ROWS, vt.shape[1]), vt.dtype)], axis=0)


def _softmax_step_t(st, vt_ones, m_ref, acc_ref, lo):
    m_prev = m_ref[:, lo:]
    m_new = jnp.maximum(m_prev, jnp.max(st, axis=0, keepdims=True))
    alpha = jnp.exp2(m_prev - m_new)
    pr = jnp.exp2(st - m_new)
    acc_ref[:, lo:] = alpha * acc_ref[:, lo:] + _dot(vt_ones, pr.astype(BF16))
    m_ref[:, lo:] = m_new


def _normalized(acc_ref, dv):
    acc = acc_ref[...]
    return acc[:dv] / acc[dv:dv + 1]


STATE_REFS = 4


def _attn_scratch(chains, dv, tq, tk):
    per_chain = [pltpu.VMEM((1, tq), F32), pltpu.VMEM((dv + ONES_ROWS, tq), F32),
                 pltpu.VMEM((tk, tq), F32), pltpu.VMEM((tk, tq), F32)]
    return per_chain * chains


def _flash_attention(n_q_tiles, scratch_refs, score_fn, value_fn, finalize_fn, tk, tq):
    ratio = tq // tk
    assert tq == ratio * tk and ratio % 2 == 0
    n_chains = len(scratch_refs) // STATE_REFS
    chains = [scratch_refs[STATE_REFS * c:STATE_REFS * (c + 1)] for c in range(n_chains)]

    def scores(i, t, slot, lo=0):
        for c, ch in enumerate(chains):
            ch[2 + slot][:, lo:] = score_fn(c, i, t, lo)

    def update(t, slot, diag=None):
        lo = 0 if diag is None else diag * tk
        for c, ch in enumerate(chains):
            st = ch[2 + slot][:, lo:]
            if diag is not None:
                st = jnp.where(_chunk_mask_t(tk, tq - lo), st, -jnp.inf)
            _softmax_step_t(st, _with_ones_rows(value_fn(c, t)), ch[0], ch[1], lo)

    scores(0, 0, 0)

    def query_tile(i, carry):
        for m_ref, acc_ref, _, _ in chains:
            m_ref[...] = jnp.full(m_ref.shape, -jnp.inf, F32)
            acc_ref[...] = jnp.zeros(acc_ref.shape, F32)

        def pair(p, c):
            t = 2 * p
            scores(i, t + 1, 1)
            update(t, 0)
            scores(i, t + 2, 0)
            update(t + 1, 1)
            return c

        lax.fori_loop(0, i * (ratio // 2), pair, 0)
        first_diag = ratio * i
        for d in range(ratio):
            if d + 1 < ratio:
                scores(i, first_diag + d + 1, (d + 1) % 2, lo=(d + 1) * tk)
            else:
                scores(jnp.minimum(i + 1, n_q_tiles - 1), 0, 0)
            update(first_diag + d, d % 2, diag=d)
        finalize_fn(i, [ch[1] for ch in chains])
        return carry

    lax.fori_loop(0, n_q_tiles, query_tile, 0)


def _mla_kernel(qt_ref, k_ref, vt_ref, o_ref, *scratch_refs):
    tq, tk = ATTN_Q_ROWS, ATTN_K_ROWS

    def score_fn(c, i, t, lo):
        rows = pl.ds(pl.multiple_of(t * tk, tk), tk)
        sl = slice(c * LANES, (c + 1) * LANES)
        return _dot(k_ref[rows, sl], qt_ref[i, sl, lo:])

    def value_fn(c, t):
        return vt_ref[t, c * MLA_V:(c + 1) * MLA_V, :]

    def finalize_fn(i, accs):
        ot = jnp.concatenate([_normalized(acc_ref, MLA_V) for acc_ref in accs], axis=0)
        o_ref[pl.ds(pl.multiple_of(i * tq, tq), tq), :] = ot.T.astype(BF16)

    _flash_attention(qt_ref.shape[0], scratch_refs, score_fn, value_fn, finalize_fn, tk, tq)


def _mla_call(qmt, km, vtm, batch, seq_len):
    n = km.shape[0]
    tq, tk, hps = ATTN_Q_ROWS, ATTN_K_ROWS, MLA_HEADS_PER_STEP
    return pl.pallas_call(
        _mla_kernel,
        grid=(batch, MLA_HEADS // hps),
        in_specs=[pl.BlockSpec((seq_len // tq, hps * LANES, tq), lambda b, h: (b, h, 0)),
                  pl.BlockSpec((seq_len, hps * LANES), lambda b, h: (b, h)),
                  pl.BlockSpec((seq_len // tk, hps * MLA_V, tk), lambda b, h: (b, h, 0))],
        out_specs=pl.BlockSpec((seq_len, hps * MLA_V), lambda b, h: (b, h)),
        out_shape=jax.ShapeDtypeStruct((n, MLA_HEADS * MLA_V), BF16),
        scratch_shapes=_attn_scratch(hps, MLA_V, tq, tk),
        compiler_params=pltpu.CompilerParams(dimension_semantics=("parallel", "parallel"),
                                             vmem_limit_bytes=VMEM_LIMIT_BYTES),
        name="mla_attn",
    )(qmt, km, vtm)


def _diff_kernel(lam_init, qt_ref, k_ref, vt_ref, lq1_ref, lk1_ref, lq2_ref, lk2_ref, subln_ref, o_ref,
                 *scratch_refs):
    tq, tk = ATTN_Q_ROWS, ATTN_K_ROWS
    hps = DIFF_HEADS_PER_STEP

    def score_fn(c, i, t, lo):
        rows = pl.ds(pl.multiple_of(t * tk, tk), tk)
        hd, f = c // 2, c % 2
        half = qt_ref[i, hd * LANES + f * DIFF_HEAD_DIM:hd * LANES + (f + 1) * DIFF_HEAD_DIM, lo:]
        zero = jnp.zeros_like(half)
        q = jnp.concatenate([half, zero] if f == 0 else [zero, half], axis=0)
        return _dot(k_ref[rows, hd * LANES:(hd + 1) * LANES], q)

    def value_fn(c, t):
        hd = c // 2
        return vt_ref[t, hd * DIFF_V_DIM:(hd + 1) * DIFF_V_DIM, :]

    lam = (jnp.exp(jnp.sum(lq1_ref[...] * lk1_ref[...], axis=-1, keepdims=True))
           - jnp.exp(jnp.sum(lq2_ref[...] * lk2_ref[...], axis=-1, keepdims=True)) + lam_init)
    subln = subln_ref[...] * (1.0 - lam_init)

    def finalize_fn(i, accs):
        heads = []
        for hd in range(hps):
            ot = _normalized(accs[2 * hd], DIFF_V_DIM) - lam * _normalized(accs[2 * hd + 1], DIFF_V_DIM)
            ot = ot * lax.rsqrt(jnp.sum(ot * ot, axis=0, keepdims=True) * (1.0 / DIFF_V_DIM) + EPS)
            heads.append(ot * subln)
        o_ref[pl.ds(pl.multiple_of(i * tq, tq), tq), :] = jnp.concatenate(heads, axis=0).T.astype(BF16)

    _flash_attention(qt_ref.shape[0], scratch_refs, score_fn, value_fn, finalize_fn, tk, tq)


def _diff_call(qdt, kd, vtd, lq1, lk1, lq2, lk2, subln_col, lam_init, batch, seq_len):
    n = kd.shape[0]
    tq, tk, hps = ATTN_Q_ROWS, ATTN_K_ROWS, DIFF_HEADS_PER_STEP
    small = lambda a: pl.BlockSpec(a.shape, lambda b, h: (0, 0))
    return pl.pallas_call(
        functools.partial(_diff_kernel, lam_init),
        grid=(batch, DIFF_HEADS // hps),
        in_specs=[pl.BlockSpec((seq_len // tq, hps * LANES, tq), lambda b, h: (b, h, 0)),
                  pl.BlockSpec((seq_len, hps * LANES), lambda b, h: (b, h)),
                  pl.BlockSpec((seq_len // tk, hps * DIFF_V_DIM, tk), lambda b, h: (b, h, 0)),
                  small(lq1), small(lk1), small(lq2), small(lk2), small(subln_col)],
        out_specs=pl.BlockSpec((seq_len, hps * LANES), lambda b, h: (b, h)),
        out_shape=jax.ShapeDtypeStruct((n, DIFF_V_WIDTH), BF16),
        scratch_shapes=_attn_scratch(2 * hps, DIFF_V_DIM, tq, tk),
        compiler_params=pltpu.CompilerParams(dimension_semantics=("parallel", "parallel"),
                                             vmem_limit_bytes=VMEM_LIMIT_BYTES),
        name="diff_attn",
    )(qdt, kd, vtd, lq1, lk1, lq2, lk2, subln_col)


def _merge_kernel(x_ref, om_ref, od_ref, sgm_ref, sgd_ref, wmu_ref, wdu_ref, wout_ref, gffn_ref, wrt_ref,
                  brt_ref, x1_ref, h2_ref, route_ref, route_t_ref, cnt_ref):
    merged = (sgm_ref[...].astype(F32) * _dot(om_ref[...], wmu_ref[...])
              + sgd_ref[...].astype(F32) * _dot(od_ref[...], wdu_ref[...]))
    x1 = x_ref[...] + _dot(merged.astype(BF16), wout_ref[...])
    x1_ref[...] = x1
    h2 = _rms(x1, x1.shape[-1]) * gffn_ref[...]
    h2_hi = h2.astype(BF16)
    h2_ref[...] = h2_hi
    tm = h2.shape[0]

    h2_lo = (h2 - h2_hi.astype(F32)).astype(BF16)
    by_hi = _dot_nt(wrt_ref[...], h2_hi)
    logits = by_hi[:LANES] + by_hi[LANES:] + _dot_nt(wrt_ref[:LANES, :], h2_lo) + brt_ref[...]
    row = lax.broadcasted_iota(jnp.int32, logits.shape, 0)
    neg = -jnp.inf
    big = jnp.int32(1 << 20)

    def top(vals):
        mx = jnp.max(vals, axis=0, keepdims=True)
        idx = jnp.min(jnp.where(vals == mx, row, big), axis=0, keepdims=True)
        return mx, idx

    gl = jnp.where((row >= N_EXPERTS) & (row < N_EXPERTS + N_GROUPS), logits, neg)
    gmax, gidx = top(gl)
    pg_sel = 1.0 / jnp.sum(jnp.exp(gl - gmax), axis=0, keepdims=True)
    el = jnp.where((row < N_EXPERTS) & (row // EXPERTS_PER_GROUP == gidx - N_EXPERTS), logits, neg)
    m1, i1 = top(el)
    m2, i2 = top(jnp.where(row == i1, neg, el))
    e2 = jnp.exp(m2 - m1)
    w1 = pg_sel / (1.0 + e2)
    w2 = w1 * e2

    sel = jnp.where((row == i1) | (row == i2), 1.0, 0.0).astype(BF16)
    t_row = lax.broadcasted_iota(jnp.int32, (tm, tm), 0)
    t_col = lax.broadcasted_iota(jnp.int32, (tm, tm), 1)
    rank = _dot(sel, jnp.where(t_row < t_col, 1.0, 0.0).astype(BF16))
    cnt = _dot(sel, jnp.ones((tm, tm), BF16))
    seg = jnp.floor((cnt + (SEG_ALIGN - 1)) * (1.0 / SEG_ALIGN))
    e_row = lax.broadcasted_iota(jnp.int32, (LANES, LANES), 0)
    e_col = lax.broadcasted_iota(jnp.int32, (LANES, LANES), 1)
    off = _dot(jnp.where(e_col < e_row, 1.0, 0.0).astype(BF16), seg.astype(BF16)) * SEG_ALIGN
    dest = off + rank
    d1 = jnp.sum(jnp.where(row == i1, dest, 0.0), axis=0, keepdims=True)
    d2 = jnp.sum(jnp.where(row == i2, dest, 0.0), axis=0, keepdims=True)
    route_t = jnp.where(row == 0, d1, jnp.where(row == 1, d2, jnp.where(row == 2, w1, jnp.where(row == 3, w2, 0.0))))
    route_t_ref[0] = route_t[0:8]
    route_ref[...] = route_t.T
    cnt_ref[0] = (seg[:, :LANES] * SEG_ALIGN).T[0:1]


def _merge_call(x2, om, od, sgm, sgd, p):
    n, d = x2.shape
    tm = MERGE_ROWS
    row = lambda i: (i, 0)
    const = lambda i: (0, 0)
    weights = [p["wmu"], p["wdu"], p["wout"], p["gffn"], p["wrt"], p["brt"]]
    return pl.pallas_call(
        _merge_kernel,
        grid=(n // tm,),
        in_specs=([pl.BlockSpec((tm, a.shape[1]), row) for a in (x2, om, od, sgm, sgd)]
                  + [pl.BlockSpec(w.shape, const) for w in weights]),
        out_specs=[pl.BlockSpec((tm, d), row), pl.BlockSpec((tm, d), row), pl.BlockSpec((tm, LANES), row),
                   pl.BlockSpec((1, 8, tm), lambda i: (i, 0, 0)), pl.BlockSpec((1, 1, LANES), lambda i: (i, 0, 0))],
        out_shape=[jax.ShapeDtypeStruct((n, d), F32), jax.ShapeDtypeStruct((n, d), BF16),
                   jax.ShapeDtypeStruct((n, LANES), F32), jax.ShapeDtypeStruct((n // tm, 8, tm), F32),
                   jax.ShapeDtypeStruct((n // tm, 1, LANES), F32)],
        compiler_params=pltpu.CompilerParams(dimension_semantics=("parallel",), vmem_limit_bytes=VMEM_LIMIT_BYTES),
        name="merge_router",
    )(x2, om, od, sgm, sgd, *weights)


def _segment_copies(i, seg_dst_ref, seg_rows_ref, tile_off_ref, global_ref, tile_ref, sem, to_global):
    def body(e, carry):
        k = i * N_EXPERTS + e
        rows = pl.multiple_of(seg_rows_ref[k], SEG_ALIGN)

        @pl.when(rows > 0)
        def _():
            g = global_ref.at[pl.ds(pl.multiple_of(seg_dst_ref[k], SEG_ALIGN), rows)]
            t = tile_ref.at[pl.ds(pl.multiple_of(tile_off_ref[k], SEG_ALIGN), rows)]
            src, dst = (t, g) if to_global else (g, t)
            pltpu.make_async_copy(src, dst, sem).start()

        return carry

    lax.fori_loop(0, N_EXPERTS, body, 0)


def _wait_rows(tile_ref, rows, sem):
    @pl.when(rows > 0)
    def _():
        view = tile_ref.at[pl.ds(0, pl.multiple_of(rows, SEG_ALIGN))]
        pltpu.make_async_copy(view, view, sem).wait()


def _zero_unused_rows(tail_dst_ref, tail_rows_ref, n_used_ref, xs_ref, zero_ref, sem, start):
    n_tiles = xs_ref.shape[0] // EXPERT_ROWS
    if start:
        zero_ref[...] = jnp.zeros(zero_ref.shape, BF16)

    def tail(e, total):
        rows = pl.multiple_of(tail_rows_ref[e], SEG_ALIGN)
        if start:
            @pl.when(rows > 0)
            def _():
                dst = xs_ref.at[pl.ds(pl.multiple_of(tail_dst_ref[e], SEG_ALIGN), rows)]
                pltpu.make_async_copy(zero_ref.at[pl.ds(0, rows)], dst, sem).start()

        return total + rows

    total = lax.fori_loop(0, N_EXPERTS, tail, 0)
    if not start:
        _wait_rows(xs_ref, total + (n_tiles - n_used_ref[0]) * EXPERT_ROWS, sem)
        return

    def unused(t, carry):
        dst = xs_ref.at[pl.ds(pl.multiple_of(t * EXPERT_ROWS, EXPERT_ROWS), EXPERT_ROWS)]
        pltpu.make_async_copy(zero_ref, dst, sem).start()
        return carry

    lax.fori_loop(n_used_ref[0], n_tiles, unused, 0)


def _sort_kernel(seg_dst_ref, seg_rows_ref, tile_off_ref, tile_rows_ref, tail_dst_ref, tail_rows_ref, n_used_ref,
                 h2_ref, route_t_ref, xs_ref, sorted_ref, zero_ref, sem, zero_sem):
    i = pl.program_id(0)
    tm = h2_ref.shape[0]

    @pl.when(i == 0)
    def _():
        _zero_unused_rows(tail_dst_ref, tail_rows_ref, n_used_ref, xs_ref, zero_ref, zero_sem, True)

    d1 = route_t_ref[0, 0:1, :].astype(jnp.int32)
    d2 = route_t_ref[0, 1:2, :].astype(jnp.int32)
    slot = i % 2

    def sort_rows(n_rows):
        r = lax.broadcasted_iota(jnp.int32, (n_rows, tm), 0)
        perm = jnp.where((r == d1) | (r == d2), 1.0, 0.0).astype(BF16)
        sorted_ref[slot, 0:n_rows] = _dot(perm, h2_ref[...]).astype(BF16)

    @pl.when(tile_rows_ref[i] <= SORT_ROWS_COMMON)
    def _():
        sort_rows(SORT_ROWS_COMMON)

    @pl.when(tile_rows_ref[i] > SORT_ROWS_COMMON)
    def _():
        sort_rows(SORT_ROWS)

    _segment_copies(i, seg_dst_ref, seg_rows_ref, tile_off_ref, xs_ref, sorted_ref.at[slot], sem.at[slot], True)

    @pl.when(i > 0)
    def _():
        _wait_rows(sorted_ref.at[1 - slot], tile_rows_ref[jnp.maximum(i - 1, 0)], sem.at[1 - slot])

    @pl.when(i == pl.num_programs(0) - 1)
    def _():
        _wait_rows(sorted_ref.at[slot], tile_rows_ref[i], sem.at[slot])
        _zero_unused_rows(tail_dst_ref, tail_rows_ref, n_used_ref, xs_ref, zero_ref, zero_sem, False)


def _sort_call(h2, route_t, sched, max_rows):
    n, d = h2.shape
    tm = ROUTE_ROWS
    return pl.pallas_call(
        _sort_kernel,
        grid_spec=pltpu.PrefetchScalarGridSpec(
            num_scalar_prefetch=7,
            grid=(n // tm,),
            in_specs=[pl.BlockSpec((tm, d), lambda i, *_: (i, 0)),
                      pl.BlockSpec((1, 8, tm), lambda i, *_: (i, 0, 0))],
            out_specs=pl.BlockSpec(memory_space=pl.ANY),
            scratch_shapes=[pltpu.VMEM((2, SORT_ROWS, d), BF16), pltpu.VMEM((EXPERT_ROWS, d), BF16),
                            pltpu.SemaphoreType.DMA((2,)), pltpu.SemaphoreType.DMA(())],
        ),
        out_shape=jax.ShapeDtypeStruct((max_rows, d), BF16),
        compiler_params=pltpu.CompilerParams(dimension_semantics=("arbitrary",), vmem_limit_bytes=VMEM_LIMIT_BYTES),
        name="moe_sort",
    )(sched["seg_dst"], sched["seg_rows"], sched["tile_off"], sched["tile_rows"], sched["tail_dst"],
      sched["tail_rows"], sched["n_used"], h2, route_t)


def _expert_kernel(first_tile_ref, n_tiles_ref, n_used_ref, xs_ref, wg_ref, wu_ref, wd_ref, ys_ref,
                   x_buf, y_buf, wg_bf, wu_bf, wd_bf, carry_ref, in_sem, out_sem, zero_sem):
    e = pl.program_id(0)
    tr = EXPERT_ROWS
    n = n_tiles_ref[e]
    total_tiles = ys_ref.shape[0] // tr
    tile_rows = lambda t: pl.ds(pl.multiple_of(t * tr, tr), tr)

    def fetch(t, slot):
        pltpu.make_async_copy(xs_ref.at[tile_rows(first_tile_ref[e] + t)], x_buf.at[slot], in_sem.at[slot]).start()

    def wait_fetch(slot):
        pltpu.make_async_copy(xs_ref.at[tile_rows(0)], x_buf.at[slot], in_sem.at[slot]).wait()

    def write_back(t, slot):
        pltpu.make_async_copy(y_buf.at[slot], ys_ref.at[tile_rows(first_tile_ref[e] + t)], out_sem.at[slot]).start()

    def wait_write_back(slot):
        pltpu.make_async_copy(y_buf.at[slot], ys_ref.at[tile_rows(0)], out_sem.at[slot]).wait()

    @pl.when(e == 0)
    def _():
        carry_ref[0] = 0
        carry_ref[1] = 0
        y_buf[0] = jnp.zeros(y_buf.shape[1:], BF16)

        def zero_tile(t, carry):
            pltpu.make_async_copy(y_buf.at[0], ys_ref.at[tile_rows(t)], zero_sem).start()
            return carry

        lax.fori_loop(n_used_ref[0], total_tiles, zero_tile, 0)

        def wait_zero_tile(t, carry):
            pltpu.make_async_copy(y_buf.at[0], ys_ref.at[tile_rows(0)], zero_sem).wait()
            return carry

        lax.fori_loop(n_used_ref[0], total_tiles, wait_zero_tile, 0)

    @pl.when(n > 0)
    def _():
        first_slot = carry_ref[0]

        @pl.when(carry_ref[1] == 0)
        def _():
            fetch(0, first_slot)

        wg_bf[...] = wg_ref[0].astype(BF16)
        wu_bf[...] = wu_ref[0].astype(BF16)
        wd_bf[...] = wd_ref[0].astype(BF16)
        more = first_tile_ref[e] + n < n_used_ref[0]

        def tile(t, carry):
            x_slot = (first_slot + t) % 2
            y_slot = t % 2

            @pl.when((t + 1 < n) | more)
            def _():
                fetch(t + 1, 1 - x_slot)

            wait_fetch(x_slot)

            @pl.when(t >= 2)
            def _():
                wait_write_back(y_slot)

            xs = x_buf[x_slot]
            gate = _dot(xs, wg_bf[...])
            up = _dot(xs, wu_bf[...])
            hidden = (gate * jax.nn.sigmoid(gate) * up).astype(BF16)
            y_buf[y_slot] = _dot(hidden, wd_bf[...]).astype(BF16)
            write_back(t, y_slot)
            return carry

        lax.fori_loop(0, n, tile, 0)
        carry_ref[0] = (first_slot + n) % 2
        carry_ref[1] = more.astype(jnp.int32)

        @pl.when(n >= 2)
        def _():
            wait_write_back(n % 2)

        wait_write_back((n - 1) % 2)


def _expert_call(xs, wg, wu, wd, sched):
    rows, d = xs.shape
    tr = EXPERT_ROWS
    wsel = lambda e, *_: (e, 0, 0)
    return pl.pallas_call(
        _expert_kernel,
        grid_spec=pltpu.PrefetchScalarGridSpec(
            num_scalar_prefetch=3,
            grid=(N_EXPERTS,),
            in_specs=[pl.BlockSpec(memory_space=pl.ANY),
                      pl.BlockSpec((1, d, EXPERT_FF), wsel), pl.BlockSpec((1, d, EXPERT_FF), wsel),
                      pl.BlockSpec((1, EXPERT_FF, d), wsel)],
            out_specs=pl.BlockSpec(memory_space=pl.ANY),
            scratch_shapes=[pltpu.VMEM((2, tr, d), BF16), pltpu.VMEM((2, tr, d), BF16),
                            pltpu.VMEM((d, EXPERT_FF), BF16), pltpu.VMEM((d, EXPERT_FF), BF16),
                            pltpu.VMEM((EXPERT_FF, d), BF16), pltpu.SMEM((2,), jnp.int32),
                            pltpu.SemaphoreType.DMA((2,)), pltpu.SemaphoreType.DMA((2,)), pltpu.SemaphoreType.DMA(())],
        ),
        out_shape=jax.ShapeDtypeStruct((rows, d), BF16),
        compiler_params=pltpu.CompilerParams(dimension_semantics=("arbitrary",), vmem_limit_bytes=VMEM_LIMIT_BYTES),
        name="moe_experts",
    )(sched["first_tile"], sched["n_tiles"], sched["n_used"], xs, wg, wu, wd)


def _combine_kernel(seg_dst_ref, seg_rows_ref, tile_off_ref, tile_rows_ref, ys_ref, route_ref, x1_ref, o_ref,
                    buf_ref, sem):
    i = pl.program_id(0)
    tm = x1_ref.shape[0]
    slot = i % 2

    def fetch(tile, into):
        buf_ref[into] = jnp.zeros(buf_ref.shape[1:], BF16)
        _segment_copies(tile, seg_dst_ref, seg_rows_ref, tile_off_ref, ys_ref, buf_ref.at[into], sem.at[into], False)

    @pl.when(i == 0)
    def _():
        fetch(i, slot)

    @pl.when(i + 1 < pl.num_programs(0))
    def _():
        fetch(i + 1, 1 - slot)

    route = route_ref[...]
    d1 = route[:, 0:1].astype(jnp.int32)
    d2 = route[:, 1:2].astype(jnp.int32)
    w1 = route[:, 2:3]
    w2 = route[:, 3:4]
    _wait_rows(buf_ref.at[slot], tile_rows_ref[i], sem.at[slot])

    def combine_rows(n_rows):
        r = lax.broadcasted_iota(jnp.int32, (tm, n_rows), 1)
        weights = (jnp.where(r == d1, w1, 0.0) + jnp.where(r == d2, w2, 0.0)).astype(BF16)
        o_ref[...] = x1_ref[...] + _dot(weights, buf_ref[slot, 0:n_rows])

    @pl.when(tile_rows_ref[i] <= SORT_ROWS_COMMON)
    def _():
        combine_rows(SORT_ROWS_COMMON)

    @pl.when(tile_rows_ref[i] > SORT_ROWS_COMMON)
    def _():
        combine_rows(SORT_ROWS)


def _combine_call(ys, route, x1, sched):
    n, d = x1.shape
    tm = ROUTE_ROWS
    return pl.pallas_call(
        _combine_kernel,
        grid_spec=pltpu.PrefetchScalarGridSpec(
            num_scalar_prefetch=4,
            grid=(n // tm,),
            in_specs=[pl.BlockSpec(memory_space=pl.ANY),
                      pl.BlockSpec((tm, LANES), lambda i, *_: (i, 0)),
                      pl.BlockSpec((tm, d), lambda i, *_: (i, 0))],
            out_specs=pl.BlockSpec((tm, d), lambda i, *_: (i, 0)),
            scratch_shapes=[pltpu.VMEM((2, SORT_ROWS, d), BF16), pltpu.SemaphoreType.DMA((2,))],
        ),
        out_shape=jax.ShapeDtypeStruct((n, d), F32),
        compiler_params=pltpu.CompilerParams(dimension_semantics=("arbitrary",), vmem_limit_bytes=VMEM_LIMIT_BYTES),
        name="moe_combine",
    )(sched["seg_dst"], sched["seg_rows"], sched["tile_off"], sched["tile_rows"], ys, route, x1)


def _schedule_kernel(cnt_ref, seg_dst_ref, tile_off_ref, tile_rows_ref, misc_ref):
    hp = functools.partial(jnp.dot, preferred_element_type=F32, precision=lax.Precision.HIGHEST)
    cnt = cnt_ref[...]
    n_tiles = cnt.shape[0]
    tile_before = jnp.where(lax.broadcasted_iota(jnp.int32, (n_tiles, n_tiles), 1)
                            < lax.broadcasted_iota(jnp.int32, (n_tiles, n_tiles), 0), 1.0, 0.0)
    expert_before = jnp.where(lax.broadcasted_iota(jnp.int32, (LANES, LANES), 0)
                              < lax.broadcasted_iota(jnp.int32, (LANES, LANES), 1), 1.0, 0.0)
    expert_rows = jnp.sum(cnt, axis=0, keepdims=True)
    region = jnp.floor((expert_rows + (EXPERT_ROWS - 1)) * (1.0 / EXPERT_ROWS)) * EXPERT_ROWS
    region_start = hp(jnp.broadcast_to(region, (8, LANES)), expert_before)[0:1]
    seg_dst_ref[...] = (region_start + hp(tile_before, cnt)).astype(jnp.int32)
    tile_off_ref[...] = hp(cnt, expert_before).astype(jnp.int32)
    tile_rows_ref[...] = jnp.broadcast_to(jnp.sum(cnt, axis=-1, keepdims=True), cnt.shape).astype(jnp.int32)
    n_used = jnp.sum(region, axis=-1, keepdims=True) * (1.0 / EXPERT_ROWS)
    row = lax.broadcasted_iota(jnp.int32, (8, LANES), 0)
    per_tile = 1.0 / EXPERT_ROWS
    misc = jnp.where(row == 0, region_start + expert_rows,
                     jnp.where(row == 1, region - expert_rows,
                               jnp.where(row == 2, region_start * per_tile,
                                         jnp.where(row == 3, region * per_tile, n_used))))
    misc_ref[...] = misc.astype(jnp.int32)


def _moe_schedule(cnt, n_tokens):
    n_tiles = cnt.shape[0]
    table = jax.ShapeDtypeStruct((n_tiles, LANES), jnp.int32)
    seg_dst, tile_off, tile_rows, misc = pl.pallas_call(
        _schedule_kernel,
        out_shape=[table, table, table, jax.ShapeDtypeStruct((8, LANES), jnp.int32)],
        name="moe_schedule",
    )(cnt.reshape(n_tiles, LANES))
    max_rows = 2 * n_tokens + n_tiles * N_EXPERTS * (SEG_ALIGN - 1) + N_EXPERTS * (EXPERT_ROWS - 1)
    max_tiles = -(-max_rows // EXPERT_ROWS)
    flat = lambda a: a[:, :N_EXPERTS].reshape(-1)
    sched = {
        "seg_dst": flat(seg_dst),
        "seg_rows": flat(cnt.reshape(n_tiles, LANES).astype(jnp.int32)),
        "tile_off": flat(tile_off),
        "tile_rows": tile_rows[:, 0],
        "tail_dst": misc[0, :N_EXPERTS],
        "tail_rows": misc[1, :N_EXPERTS],
        "first_tile": misc[2, :N_EXPERTS],
        "n_tiles": misc[3, :N_EXPERTS],
        "n_used": misc[4, :1],
    }
    return sched, max_tiles * EXPERT_ROWS


def _rotary_tables(seq_len, rot_dim, period, first, gain, scale):
    half = rot_dim // 2
    pos = jnp.arange(seq_len, dtype=F32)
    inv = 1.0 / (ROPE_THETA ** (jnp.arange(0, rot_dim, 2, dtype=F32) / rot_dim))
    ang = pos[:, None] * inv[None, :]
    cos, sin = jnp.cos(ang), jnp.sin(ang)
    lane = jnp.arange(LANES)
    rel = (lane % period) - first
    active = (rel >= 0) & (rel < rot_dim)
    idx = jnp.clip(rel, 0, rot_dim - 1) % half
    sign = jnp.where(rel < half, -1.0, 1.0)
    partner = jnp.where(active, jnp.where(rel < half, lane + half, lane - half), lane)
    c = jnp.where(active[None, :], cos[:, idx], 1.0)
    s = jnp.where(active[None, :], sin[:, idx] * sign[None, :], 0.0)
    gain = gain.astype(F32)
    return (c * gain[None, :] * scale).astype(F32), (s * gain[partner][None, :] * scale).astype(F32)


def _head_pad(w, heads, width):
    r = w.shape[0]
    w = w.reshape(r, heads, width)
    return jnp.pad(w, ((0, 0), (0, 0), (0, LANES - width))).reshape(r, heads * LANES)


def _layer_params(l, seq_len, norm_mix, w_in, mla_q_latent_norm, w_mla_uq, mla_kv_latent_norm, w_mla_ukv,
                  mla_q_gain, mla_k_gain, diff_q_gain, diff_k_gain, w_mla_up, w_diff_up, w_out, norm_ffn,
                  w_router_group, b_router_group, w_router_expert, b_router_expert):
    d = w_in.shape[1]
    sizes = (MLA_Q_RANK, MLA_KV_RANK, MLA_ROPE, DIFF_QK_WIDTH, DIFF_QK_WIDTH, DIFF_V_WIDTH, d, d)
    offs = [0]
    for s in sizes:
        offs.append(offs[-1] + s)
    wi = w_in[l]
    seg = [wi[:, offs[k]:offs[k + 1]] for k in range(len(sizes))]
    row = lambda g: g.astype(F32)[None, :]
    p = {}
    p["gmix"] = row(norm_mix[l])
    p["wql"] = seg[0].astype(BF16)
    p["wkvl"] = seg[1].astype(BF16)
    p["wkr"] = jnp.pad(seg[2], ((0, 0), (MLA_NOPE, LANES - MLA_QK))).astype(BF16)
    p["wdk"] = seg[4].astype(BF16)
    p["wdqvt"] = jnp.concatenate([seg[3].T, seg[5].T], axis=0).astype(BF16)
    p["wgm"], p["wgd"] = seg[6].astype(BF16), seg[7].astype(BF16)
    p["gql"] = row(mla_q_latent_norm[l])
    p["wuqt"] = _head_pad(w_mla_uq[l], MLA_HEADS, MLA_QK).T.astype(BF16)
    p["gkvl"] = row(mla_kv_latent_norm[l])
    ukv = w_mla_ukv[l].reshape(MLA_KV_RANK, MLA_HEADS, MLA_NOPE + MLA_V)
    p["wuk"] = _head_pad(ukv[:, :, :MLA_NOPE].reshape(MLA_KV_RANK, -1), MLA_HEADS, MLA_NOPE).astype(BF16)
    p["wuvt"] = ukv[:, :, MLA_NOPE:].reshape(MLA_KV_RANK, -1).T.astype(BF16)
    gq = jnp.pad(mla_q_gain[l], (0, LANES - MLA_QK))
    gk = jnp.pad(mla_k_gain[l], (0, LANES - MLA_QK))
    nope = jnp.arange(LANES) < MLA_NOPE
    p["gkn"] = jnp.where(nope, gk, 0.0).astype(F32)[None, :]
    aq, bq = _rotary_tables(seq_len, MLA_ROPE, LANES, MLA_NOPE, gq, LOG2E * MLA_QK ** -0.5)
    p["aq"], p["bq"] = aq.T, bq.T
    ak, bk = _rotary_tables(seq_len, MLA_ROPE, LANES, MLA_NOPE, jnp.where(nope, 0.0, gk), 1.0)
    p["ak"], p["bk"] = ak, bk
    adq, bdq = _rotary_tables(seq_len, DIFF_ROPE, DIFF_HEAD_DIM, 0, jnp.tile(diff_q_gain[l], 2),
                              LOG2E * DIFF_HEAD_DIM ** -0.5)
    p["adq"], p["bdq"] = adq.T, bdq.T
    p["adk"], p["bdk"] = _rotary_tables(seq_len, DIFF_ROPE, DIFF_HEAD_DIM, 0, jnp.tile(diff_k_gain[l], 2), 1.0)
    p["wmu"] = w_mla_up[l].astype(BF16)
    p["wdu"] = w_diff_up[l].astype(BF16)
    p["wout"] = w_out[l].astype(BF16)
    p["gffn"] = row(norm_ffn[l])
    wr = jnp.concatenate([w_router_expert[l], w_router_group[l]], axis=1).astype(F32)
    wrt = jnp.pad(wr, ((0, 0), (0, LANES - wr.shape[1]))).T
    wrt_hi = wrt.astype(BF16)
    p["wrt"] = jnp.concatenate([wrt_hi, (wrt - wrt_hi.astype(F32)).astype(BF16)], axis=0)
    br = jnp.concatenate([b_router_expert[l], b_router_group[l]]).astype(F32)
    p["brt"] = jnp.broadcast_to(jnp.pad(br, (0, LANES - br.shape[0]))[:, None], (LANES, MERGE_ROWS))
    return p


def kernel(x, norm_mix, w_in, mla_q_latent_norm, w_mla_uq, mla_kv_latent_norm, w_mla_ukv, mla_q_gain, mla_k_gain, diff_q_gain, diff_k_gain, lambda_q1, lambda_k1, lambda_q2, lambda_k2, diff_subln, w_mla_up, w_diff_up, w_out, norm_ffn, w_router_group, b_router_group, w_router_expert, b_router_expert, w_expert_gate, w_expert_up, w_expert_down):
    batch, seq_len, d = x.shape
    x2 = x.reshape(batch * seq_len, d)
    row = lambda g: g.astype(F32)[None, :]
    for l in range(norm_mix.shape[0]):
        lam_init = 0.8 - 0.6 * math.exp(-0.3 * l)
        p = _layer_params(l, seq_len, norm_mix, w_in, mla_q_latent_norm, w_mla_uq, mla_kv_latent_norm, w_mla_ukv,
                          mla_q_gain, mla_k_gain, diff_q_gain, diff_k_gain, w_mla_up, w_diff_up, w_out, norm_ffn,
                          w_router_group, b_router_group, w_router_expert, b_router_expert)
        qmt, km, vtm, qdt, kd, vtd, sgm, sgd = _proj_call(x2, seq_len, p)
        om = _mla_call(qmt, km, vtm, batch, seq_len)
        od = _diff_call(qdt, kd, vtd, row(lambda_q1[l]), row(lambda_k1[l]), row(lambda_q2[l]), row(lambda_k2[l]),
                        diff_subln[l].astype(F32)[:, None], lam_init, batch, seq_len)
        x1, h2, route, route_t, cnt = _merge_call(x2, om, od, sgm, sgd, p)
        sched, max_rows = _moe_schedule(cnt, x2.shape[0])
        xs = _sort_call(h2, route_t, sched, max_rows)
        ys = _expert_call(xs, w_expert_gate[l], w_expert_up[l], w_expert_down[l], sched)
        x2 = _combine_call(ys, route, x1, sched)
    return x2.reshape(batch, seq_len, d)
```

```python
import functools
import math

import jax
import jax.numpy as jnp
from jax import lax
from jax.experimental import pallas as pl
from jax.experimental.pallas import tpu as pltpu

CHUNK = 64
ROPE_THETA = 500000.0
EPS = 1e-6

MLA_HEADS = 8
MLA_NOPE = 64
MLA_ROPE = 32
MLA_V = 64
MLA_QK = MLA_NOPE + MLA_ROPE
MLA_Q_RANK = 256
MLA_KV_RANK = 128

DIFF_HEADS = 4
DIFF_HEAD_DIM = 64
DIFF_V_DIM = 2 * DIFF_HEAD_DIM
DIFF_ROPE = DIFF_HEAD_DIM // 4
DIFF_QK_WIDTH = DIFF_HEADS * 2 * DIFF_HEAD_DIM
DIFF_V_WIDTH = DIFF_HEADS * DIFF_V_DIM

N_GROUPS = 4
EXPERTS_PER_GROUP = 8
N_EXPERTS = N_GROUPS * EXPERTS_PER_GROUP
EXPERT_FF = 256

LANES = 128
VMEM_LIMIT_BYTES = 48 * 1024 * 1024

PROJ_ROWS = 512
ATTN_Q_ROWS = 512
ATTN_K_ROWS = 256
MERGE_ROWS = 512
ROUTE_ROWS = MERGE_ROWS
SEG_ALIGN = 16
SORT_ROWS = 2 * ROUTE_ROWS + N_EXPERTS * SEG_ALIGN
SORT_ROWS_COMMON = 2 * ROUTE_ROWS + N_EXPERTS * SEG_ALIGN // 2
EXPERT_ROWS = 512
MLA_HEADS_PER_STEP = 4
DIFF_HEADS_PER_STEP = 2
LOG2E = 1.4426950408889634

BF16 = jnp.bfloat16
F32 = jnp.float32


def _dot(a, b):
    return jnp.dot(a, b, preferred_element_type=F32)


def _dot_nt(a, b):
    return lax.dot_general(a, b, (((1,), (1,)), ((), ())), preferred_element_type=F32)


def _rms(x, width):
    return x * lax.rsqrt(jnp.sum(x * x, axis=-1, keepdims=True) * (1.0 / width) + EPS)


def _rotary_partner(y, half):
    lane = lax.broadcasted_iota(jnp.int32, y.shape, 1)
    up = pltpu.roll(y, LANES - half, 1)
    down = pltpu.roll(y, half, 1)
    return jnp.where((lane // half) % 2 == 0, up, down)


def _swap_row_blocks(y, first, half, period):
    parts = []
    for base in range(0, y.shape[0], period):
        a = base + first
        parts += [y[base:a], y[a + half:a + 2 * half], y[a:a + half], y[a + 2 * half:base + period]]
    return jnp.concatenate([p for p in parts if p.shape[0]], axis=0)


def _store_k_tiles(o_ref, vt):
    tk = o_ref.shape[-1]
    for c in range(o_ref.shape[0]):
        o_ref[c] = vt[:, c * tk:(c + 1) * tk].astype(BF16)


def _proj_kernel(x_ref, gmix_ref, wql_ref, wkvl_ref, wkr_ref, wdk_ref, wdqvt_ref, wgm_ref, wgd_ref,
                 gql_ref, wuqt_ref, gkvl_ref, wuk_ref, wuvt_ref, gkn_ref,
                 aq_ref, bq_ref, adq_ref, bdq_ref, ak_ref, bk_ref, adk_ref, bdk_ref,
                 qmt_ref, km_ref, vtm_ref, qdt_ref, kd_ref, vtd_ref, sgm_ref, sgd_ref):
    x = x_ref[...]
    h = (_rms(x, x.shape[-1]) * gmix_ref[...]).astype(BF16)

    ql = (_rms(_dot(h, wql_ref[...]), MLA_Q_RANK) * gql_ref[...]).astype(BF16)
    qt = _dot_nt(wuqt_ref[...], ql)
    aq, bq = aq_ref[...], bq_ref[...]
    for hd in range(MLA_HEADS):
        rows = slice(hd * LANES, (hd + 1) * LANES)
        qh = qt[rows]
        r = lax.rsqrt(jnp.sum(qh * qh, axis=0, keepdims=True) * (1.0 / MLA_QK) + EPS)
        y = (qh * aq + _swap_row_blocks(qh, MLA_NOPE, MLA_ROPE // 2, LANES) * bq) * r
        qmt_ref[0, rows, :] = y.astype(BF16)

    kvl = (_rms(_dot(h, wkvl_ref[...]), MLA_KV_RANK) * gkvl_ref[...]).astype(BF16)
    kr = _dot(h, wkr_ref[...])
    kr_rot = kr * ak_ref[...] + _rotary_partner(kr, MLA_ROPE // 2) * bk_ref[...]
    kr_ss = jnp.sum(kr * kr, axis=-1, keepdims=True)
    kn = _dot(kvl, wuk_ref[...])
    _store_k_tiles(vtm_ref, _dot_nt(wuvt_ref[...], kvl))
    gkn = gkn_ref[...]
    for hd in range(MLA_HEADS):
        sl = slice(hd * LANES, (hd + 1) * LANES)
        knh = kn[:, sl]
        r = lax.rsqrt((jnp.sum(knh * knh, axis=-1, keepdims=True) + kr_ss) * (1.0 / MLA_QK) + EPS)
        km_ref[:, sl] = ((knh * gkn + kr_rot) * r).astype(BF16)

    qvt = _dot_nt(wdqvt_ref[...], h)
    _store_k_tiles(vtd_ref, qvt[DIFF_QK_WIDTH:])
    adq, bdq = adq_ref[...], bdq_ref[...]
    for hd in range(DIFF_HEADS):
        rows = slice(hd * LANES, (hd + 1) * LANES)
        qh = qvt[rows]
        t = qh * adq + _swap_row_blocks(qh, 0, DIFF_ROPE // 2, DIFF_HEAD_DIM) * bdq
        halves = []
        for f in range(2):
            part = qh[f * DIFF_HEAD_DIM:(f + 1) * DIFF_HEAD_DIM]
            r = lax.rsqrt(jnp.sum(part * part, axis=0, keepdims=True) * (1.0 / DIFF_HEAD_DIM) + EPS)
            halves.append(t[f * DIFF_HEAD_DIM:(f + 1) * DIFF_HEAD_DIM] * r)
        qdt_ref[0, rows, :] = jnp.concatenate(halves, axis=0).astype(BF16)

    kd = _dot(h, wdk_ref[...])
    adk, bdk = adk_ref[...], bdk_ref[...]
    for hd in range(DIFF_HEADS):
        sl = slice(hd * LANES, (hd + 1) * LANES)
        th = kd[:, sl]
        lane = lax.broadcasted_iota(jnp.int32, th.shape, 1)
        sq = th * th
        lo = jnp.sum(jnp.where(lane < DIFF_HEAD_DIM, sq, 0.0), axis=-1, keepdims=True)
        tot = jnp.sum(sq, axis=-1, keepdims=True)
        r = lax.rsqrt(jnp.where(lane < DIFF_HEAD_DIM, lo, tot - lo) * (1.0 / DIFF_HEAD_DIM) + EPS)
        kd_ref[:, sl] = ((th * adk + _rotary_partner(th, DIFF_ROPE // 2) * bdk) * r).astype(BF16)

    sgm_ref[...] = jax.nn.sigmoid(_dot(h, wgm_ref[...])).astype(BF16)
    sgd_ref[...] = jax.nn.sigmoid(_dot(h, wgd_ref[...])).astype(BF16)


def _proj_call(x2, seq_len, p):
    n, d = x2.shape
    tm = PROJ_ROWS
    pos_blocks = seq_len // tm
    row = lambda i: (i, 0)
    const = lambda i: (0, 0)
    weights = [p["gmix"], p["wql"], p["wkvl"], p["wkr"], p["wdk"], p["wdqvt"], p["wgm"], p["wgd"],
               p["gql"], p["wuqt"], p["gkvl"], p["wuk"], p["wuvt"], p["gkn"]]
    feature_major_tables = [p["aq"], p["bq"], p["adq"], p["bdq"]]
    token_major_tables = [p["ak"], p["bk"], p["adk"], p["bdk"]]
    in_specs = ([pl.BlockSpec((tm, d), row)]
                + [pl.BlockSpec(w.shape, const) for w in weights]
                + [pl.BlockSpec((LANES, tm), lambda i: (0, i % pos_blocks)) for _ in feature_major_tables]
                + [pl.BlockSpec((tm, LANES), lambda i: (i % pos_blocks, 0)) for _ in token_major_tables])
    tk = ATTN_K_ROWS
    k_tiles = lambda width: (pl.BlockSpec((tm // tk, width, tk), lambda i: (i, 0, 0)),
                             jax.ShapeDtypeStruct((n // tk, width, tk), BF16))
    token_major = lambda width: (pl.BlockSpec((tm, width), row), jax.ShapeDtypeStruct((n, width), BF16))
    assert tm == ATTN_Q_ROWS
    feature_major = lambda width: (pl.BlockSpec((1, width, tm), lambda i: (i, 0, 0)),
                                   jax.ShapeDtypeStruct((n // tm, width, tm), BF16))
    outs = [feature_major(MLA_HEADS * LANES), token_major(MLA_HEADS * LANES), k_tiles(MLA_HEADS * MLA_V),
            feature_major(DIFF_QK_WIDTH), token_major(DIFF_QK_WIDTH), k_tiles(DIFF_V_WIDTH),
            token_major(d), token_major(d)]
    return pl.pallas_call(
        _proj_kernel,
        grid=(n // tm,),
        in_specs=in_specs,
        out_specs=[o[0] for o in outs],
        out_shape=[o[1] for o in outs],
        compiler_params=pltpu.CompilerParams(dimension_semantics=("parallel",), vmem_limit_bytes=VMEM_LIMIT_BYTES),
        name="proj",
    )(x2, *weights, *feature_major_tables, *token_major_tables)


def _chunk_mask_t(tk, width):
    kc = lax.broadcasted_iota(jnp.int32, (tk, width), 0) // CHUNK
    qc = lax.broadcasted_iota(jnp.int32, (tk, width), 1) // CHUNK
    return kc <= qc


ONES_ROWS = 16


def _with_ones_rows(vt):
    return jnp.concatenate([vt, jnp.ones((ONES_ROWS, vt.shape[1]), vt.dtype)], axis=0)


def _softmax_step_t(st, vt_ones, m_ref, acc_ref, lo):
    m_prev = m_ref[:, lo:]
    m_new = jnp.maximum(m_prev, jnp.max(st, axis=0, keepdims=True))
    alpha = jnp.exp2(m_prev - m_new)
    pr = jnp.exp2(st - m_new)
    acc_ref[:, lo:] = alpha * acc_ref[:, lo:] + _dot(vt_ones, pr.astype(BF16))
    m_ref[:, lo:] = m_new


def _normalized(acc_ref, dv):
    acc = acc_ref[...]
    return acc[:dv] / acc[dv:dv + 1]


STATE_REFS = 4


def _attn_scratch(chains, dv, tq, tk):
    per_chain = [pltpu.VMEM((1, tq), F32), pltpu.VMEM((dv + ONES_ROWS, tq), F32),
                 pltpu.VMEM((tk, tq + LANES), F32), pltpu.VMEM((tk, tq + LANES), F32)]
    return per_chain * chains


def _flash_attention(n_q_tiles, scratch_refs, score_fn, value_fn, finalize_fn, tk, tq):
    ratio = tq // tk
    assert tq == ratio * tk and ratio % 2 == 0
    n_chains = len(scratch_refs) // STATE_REFS
    chains = [scratch_refs[STATE_REFS * c:STATE_REFS * (c + 1)] for c in range(n_chains)]

    def scores(i, t, slot, lo=0):
        for c, ch in enumerate(chains):
            ch[2 + slot][:, lo:tq] = score_fn(c, i, t, lo)

    def update(t, slot, diag=None):
        lo = 0 if diag is None else diag * tk
        for c, ch in enumerate(chains):
            st = ch[2 + slot][:, lo:tq]
            if diag is not None:
                st = jnp.where(_chunk_mask_t(tk, tq - lo), st, -jnp.inf)
            _softmax_step_t(st, _with_ones_rows(value_fn(c, t)), ch[0], ch[1], lo)

    scores(0, 0, 0)

    def query_tile(i, carry):
        for m_ref, acc_ref, _, _ in chains:
            m_ref[...] = jnp.full(m_ref.shape, -jnp.inf, F32)
            acc_ref[...] = jnp.zeros(acc_ref.shape, F32)

        def pair(p, c):
            t = 2 * p
            scores(i, t + 1, 1)
            update(t, 0)
            scores(i, t + 2, 0)
            update(t + 1, 1)
            return c

        lax.fori_loop(0, i * (ratio // 2), pair, 0)
        first_diag = ratio * i
        for d in range(ratio):
            if d + 1 < ratio:
                scores(i, first_diag + d + 1, (d + 1) % 2, lo=(d + 1) * tk)
            else:
                scores(jnp.minimum(i + 1, n_q_tiles - 1), 0, 0)
            update(first_diag + d, d % 2, diag=d)
        finalize_fn(i, [ch[1] for ch in chains])
        return carry

    lax.fori_loop(0, n_q_tiles, query_tile, 0)


def _mla_kernel(qt_ref, k_ref, vt_ref, o_ref, *scratch_refs):
    tq, tk = ATTN_Q_ROWS, ATTN_K_ROWS

    def score_fn(c, i, t, lo):
        rows = pl.ds(pl.multiple_of(t * tk, tk), tk)
        sl = slice(c * LANES, (c + 1) * LANES)
        return _dot(k_ref[rows, sl], qt_ref[i, sl, lo:])

    def value_fn(c, t):
        return vt_ref[t, c * MLA_V:(c + 1) * MLA_V, :]

    def finalize_fn(i, accs):
        ot = jnp.concatenate([_normalized(acc_ref, MLA_V) for acc_ref in accs], axis=0)
        o_ref[pl.ds(pl.multiple_of(i * tq, tq), tq), :] = ot.T.astype(BF16)

    _flash_attention(qt_ref.shape[0], scratch_refs, score_fn, value_fn, finalize_fn, tk, tq)


def _mla_call(qmt, km, vtm, batch, seq_len):
    n = km.shape[0]
    tq, tk, hps = ATTN_Q_ROWS, ATTN_K_ROWS, MLA_HEADS_PER_STEP
    return pl.pallas_call(
        _mla_kernel,
        grid=(batch, MLA_HEADS // hps),
        in_specs=[pl.BlockSpec((seq_len // tq, hps * LANES, tq), lambda b, h: (b, h, 0)),
                  pl.BlockSpec((seq_len, hps * LANES), lambda b, h: (b, h)),
                  pl.BlockSpec((seq_len // tk, hps * MLA_V, tk), lambda b, h: (b, h, 0))],
        out_specs=pl.BlockSpec((seq_len, hps * MLA_V), lambda b, h: (b, h)),
        out_shape=jax.ShapeDtypeStruct((n, MLA_HEADS * MLA_V), BF16),
        scratch_shapes=_attn_scratch(hps, MLA_V, tq, tk),
        compiler_params=pltpu.CompilerParams(dimension_semantics=("parallel", "parallel"),
                                             vmem_limit_bytes=VMEM_LIMIT_BYTES),
        name="mla_attn",
    )(qmt, km, vtm)


def _diff_kernel(lam_init, qt_ref, k_ref, vt_ref, lq1_ref, lk1_ref, lq2_ref, lk2_ref, subln_ref, o_ref,
                 *scratch_refs):
    tq, tk = ATTN_Q_ROWS, ATTN_K_ROWS
    hps = DIFF_HEADS_PER_STEP

    def score_fn(c, i, t, lo):
        rows = pl.ds(pl.multiple_of(t * tk, tk), tk)
        hd, f = c // 2, c % 2
        half = qt_ref[i, hd * LANES + f * DIFF_HEAD_DIM:hd * LANES + (f + 1) * DIFF_HEAD_DIM, lo:]
        zero = jnp.zeros_like(half)
        q = jnp.concatenate([half, zero] if f == 0 else [zero, half], axis=0)
        return _dot(k_ref[rows, hd * LANES:(hd + 1) * LANES], q)

    def value_fn(c, t):
        hd = c // 2
        return vt_ref[t, hd * DIFF_V_DIM:(hd + 1) * DIFF_V_DIM, :]

    lam = (jnp.exp(jnp.sum(lq1_ref[...] * lk1_ref[...], axis=-1, keepdims=True))
           - jnp.exp(jnp.sum(lq2_ref[...] * lk2_ref[...], axis=-1, keepdims=True)) + lam_init)
    subln = subln_ref[...] * (1.0 - lam_init)

    def finalize_fn(i, accs):
        heads = []
        for hd in range(hps):
            ot = _normalized(accs[2 * hd], DIFF_V_DIM) - lam * _normalized(accs[2 * hd + 1], DIFF_V_DIM)
            ot = ot * lax.rsqrt(jnp.sum(ot * ot, axis=0, keepdims=True) * (1.0 / DIFF_V_DIM) + EPS)
            heads.append(ot * subln)
        o_ref[pl.ds(pl.multiple_of(i * tq, tq), tq), :] = jnp.concatenate(heads, axis=0).T.astype(BF16)

    _flash_attention(qt_ref.shape[0], scratch_refs, score_fn, value_fn, finalize_fn, tk, tq)


def _diff_call(qdt, kd, vtd, lq1, lk1, lq2, lk2, subln_col, lam_init, batch, seq_len):
    n = kd.shape[0]
    tq, tk, hps = ATTN_Q_ROWS, ATTN_K_ROWS, DIFF_HEADS_PER_STEP
    small = lambda a: pl.BlockSpec(a.shape, lambda b, h: (0, 0))
    return pl.pallas_call(
        functools.partial(_diff_kernel, lam_init),
        grid=(batch, DIFF_HEADS // hps),
        in_specs=[pl.BlockSpec((seq_len // tq, hps * LANES, tq), lambda b, h: (b, h, 0)),
                  pl.BlockSpec((seq_len, hps * LANES), lambda b, h: (b, h)),
                  pl.BlockSpec((seq_len // tk, hps * DIFF_V_DIM, tk), lambda b, h: (b, h, 0)),
                  small(lq1), small(lk1), small(lq2), small(lk2), small(subln_col)],
        out_specs=pl.BlockSpec((seq_len, hps * LANES), lambda b, h: (b, h)),
        out_shape=jax.ShapeDtypeStruct((n, DIFF_V_WIDTH), BF16),
        scratch_shapes=_attn_scratch(2 * hps, DIFF_V_DIM, tq, tk),
        compiler_params=pltpu.CompilerParams(dimension_semantics=("parallel", "parallel"),
                                             vmem_limit_bytes=VMEM_LIMIT_BYTES),
        name="diff_attn",
    )(qdt, kd, vtd, lq1, lk1, lq2, lk2, subln_col)


def _merge_kernel(x_ref, om_ref, od_ref, sgm_ref, sgd_ref, wmu_ref, wdu_ref, wout_ref, gffn_ref, wrt_ref,
                  brt_ref, x1_ref, h2_ref, route_ref, route_t_ref, cnt_ref):
    merged = (sgm_ref[...].astype(F32) * _dot(om_ref[...], wmu_ref[...])
              + sgd_ref[...].astype(F32) * _dot(od_ref[...], wdu_ref[...]))
    x1 = x_ref[...] + _dot(merged.astype(BF16), wout_ref[...])
    x1_ref[...] = x1
    h2 = _rms(x1, x1.shape[-1]) * gffn_ref[...]
    h2_hi = h2.astype(BF16)
    h2_ref[...] = h2_hi
    tm = h2.shape[0]

    h2_lo = (h2 - h2_hi.astype(F32)).astype(BF16)
    by_hi = _dot_nt(wrt_ref[...], h2_hi)
    logits = by_hi[:LANES] + by_hi[LANES:] + _dot_nt(wrt_ref[:LANES, :], h2_lo) + brt_ref[...]
    row = lax.broadcasted_iota(jnp.int32, logits.shape, 0)
    neg = -jnp.inf
    big = jnp.int32(1 << 20)

    def top(vals):
        mx = jnp.max(vals, axis=0, keepdims=True)
        idx = jnp.min(jnp.where(vals == mx, row, big), axis=0, keepdims=True)
        return mx, idx

    gl = jnp.where((row >= N_EXPERTS) & (row < N_EXPERTS + N_GROUPS), logits, neg)
    gmax, gidx = top(gl)
    pg_sel = 1.0 / jnp.sum(jnp.exp(gl - gmax), axis=0, keepdims=True)
    el = jnp.where((row < N_EXPERTS) & (row // EXPERTS_PER_GROUP == gidx - N_EXPERTS), logits, neg)
    m1, i1 = top(el)
    m2, i2 = top(jnp.where(row == i1, neg, el))
    e2 = jnp.exp(m2 - m1)
    w1 = pg_sel / (1.0 + e2)
    w2 = w1 * e2

    sel = jnp.where((row == i1) | (row == i2), 1.0, 0.0).astype(BF16)
    t_row = lax.broadcasted_iota(jnp.int32, (tm, tm), 0)
    t_col = lax.broadcasted_iota(jnp.int32, (tm, tm), 1)
    rank = _dot(sel, jnp.where(t_row < t_col, 1.0, 0.0).astype(BF16))
    cnt = _dot(sel, jnp.ones((tm, tm), BF16))
    seg = jnp.floor((cnt + (SEG_ALIGN - 1)) * (1.0 / SEG_ALIGN))
    e_row = lax.broadcasted_iota(jnp.int32, (LANES, LANES), 0)
    e_col = lax.broadcasted_iota(jnp.int32, (LANES, LANES), 1)
    off = _dot(jnp.where(e_col < e_row, 1.0, 0.0).astype(BF16), seg.astype(BF16)) * SEG_ALIGN
    dest = off + rank
    d1 = jnp.sum(jnp.where(row == i1, dest, 0.0), axis=0, keepdims=True)
    d2 = jnp.sum(jnp.where(row == i2, dest, 0.0), axis=0, keepdims=True)
    route_t = jnp.where(row == 0, d1, jnp.where(row == 1, d2, jnp.where(row == 2, w1, jnp.where(row == 3, w2, 0.0))))
    route_t_ref[0] = route_t[0:8]
    route_ref[...] = route_t.T
    cnt_ref[0] = (seg[:, :LANES] * SEG_ALIGN).T[0:1]


def _merge_call(x2, om, od, sgm, sgd, p):
    n, d = x2.shape
    tm = MERGE_ROWS
    row = lambda i: (i, 0)
    const = lambda i: (0, 0)
    weights = [p["wmu"], p["wdu"], p["wout"], p["gffn"], p["wrt"], p["brt"]]
    return pl.pallas_call(
        _merge_kernel,
        grid=(n // tm,),
        in_specs=([pl.BlockSpec((tm, a.shape[1]), row) for a in (x2, om, od, sgm, sgd)]
                  + [pl.BlockSpec(w.shape, const) for w in weights]),
        out_specs=[pl.BlockSpec((tm, d), row), pl.BlockSpec((tm, d), row), pl.BlockSpec((tm, LANES), row),
                   pl.BlockSpec((1, 8, tm), lambda i: (i, 0, 0)), pl.BlockSpec((1, 1, LANES), lambda i: (i, 0, 0))],
        out_shape=[jax.ShapeDtypeStruct((n, d), F32), jax.ShapeDtypeStruct((n, d), BF16),
                   jax.ShapeDtypeStruct((n, LANES), F32), jax.ShapeDtypeStruct((n // tm, 8, tm), F32),
                   jax.ShapeDtypeStruct((n // tm, 1, LANES), F32)],
        compiler_params=pltpu.CompilerParams(dimension_semantics=("parallel",), vmem_limit_bytes=VMEM_LIMIT_BYTES),
        name="merge_router",
    )(x2, om, od, sgm, sgd, *weights)


def _segment_copies(i, seg_dst_ref, seg_rows_ref, tile_off_ref, global_ref, tile_ref, sem, to_global):
    def body(e, carry):
        k = i * N_EXPERTS + e
        rows = pl.multiple_of(seg_rows_ref[k], SEG_ALIGN)

        @pl.when(rows > 0)
        def _():
            g = global_ref.at[pl.ds(pl.multiple_of(seg_dst_ref[k], SEG_ALIGN), rows)]
            t = tile_ref.at[pl.ds(pl.multiple_of(tile_off_ref[k], SEG_ALIGN), rows)]
            src, dst = (t, g) if to_global else (g, t)
            pltpu.make_async_copy(src, dst, sem).start()

        return carry

    lax.fori_loop(0, N_EXPERTS, body, 0)


def _wait_rows(tile_ref, rows, sem):
    @pl.when(rows > 0)
    def _():
        view = tile_ref.at[pl.ds(0, pl.multiple_of(rows, SEG_ALIGN))]
        pltpu.make_async_copy(view, view, sem).wait()


def _zero_unused_rows(tail_dst_ref, tail_rows_ref, n_used_ref, xs_ref, zero_ref, sem, start):
    n_tiles = xs_ref.shape[0] // EXPERT_ROWS
    if start:
        zero_ref[...] = jnp.zeros(zero_ref.shape, BF16)

    def tail(e, total):
        rows = pl.multiple_of(tail_rows_ref[e], SEG_ALIGN)
        if start:
            @pl.when(rows > 0)
            def _():
                dst = xs_ref.at[pl.ds(pl.multiple_of(tail_dst_ref[e], SEG_ALIGN), rows)]
                pltpu.make_async_copy(zero_ref.at[pl.ds(0, rows)], dst, sem).start()

        return total + rows

    total = lax.fori_loop(0, N_EXPERTS, tail, 0)
    if not start:
        _wait_rows(xs_ref, total + (n_tiles - n_used_ref[0]) * EXPERT_ROWS, sem)
        return

    def unused(t, carry):
        dst = xs_ref.at[pl.ds(pl.multiple_of(t * EXPERT_ROWS, EXPERT_ROWS), EXPERT_ROWS)]
        pltpu.make_async_copy(zero_ref, dst, sem).start()
        return carry

    lax.fori_loop(n_used_ref[0], n_tiles, unused, 0)


def _sort_kernel(seg_dst_ref, seg_rows_ref, tile_off_ref, tile_rows_ref, tail_dst_ref, tail_rows_ref, n_used_ref,
                 h2_ref, route_t_ref, xs_ref, sorted_ref, zero_ref, sem, zero_sem):
    i = pl.program_id(0)
    tm = h2_ref.shape[0]

    @pl.when(i == 0)
    def _():
        _zero_unused_rows(tail_dst_ref, tail_rows_ref, n_used_ref, xs_ref, zero_ref, zero_sem, True)

    d1 = route_t_ref[0, 0:1, :].astype(jnp.int32)
    d2 = route_t_ref[0, 1:2, :].astype(jnp.int32)
    slot = i % 2

    def sort_rows(n_rows):
        r = lax.broadcasted_iota(jnp.int32, (n_rows, tm), 0)
        perm = jnp.where((r == d1) | (r == d2), 1.0, 0.0).astype(BF16)
        sorted_ref[slot, 0:n_rows] = _dot(perm, h2_ref[...]).astype(BF16)

    @pl.when(tile_rows_ref[i] <= SORT_ROWS_COMMON)
    def _():
        sort_rows(SORT_ROWS_COMMON)

    @pl.when(tile_rows_ref[i] > SORT_ROWS_COMMON)
    def _():
        sort_rows(SORT_ROWS)

    _segment_copies(i, seg_dst_ref, seg_rows_ref, tile_off_ref, xs_ref, sorted_ref.at[slot], sem.at[slot], True)

    @pl.when(i > 0)
    def _():
        _wait_rows(sorted_ref.at[1 - slot], tile_rows_ref[jnp.maximum(i - 1, 0)], sem.at[1 - slot])

    @pl.when(i == pl.num_programs(0) - 1)
    def _():
        _wait_rows(sorted_ref.at[slot], tile_rows_ref[i], sem.at[slot])
        _zero_unused_rows(tail_dst_ref, tail_rows_ref, n_used_ref, xs_ref, zero_ref, zero_sem, False)


def _sort_call(h2, route_t, sched, max_rows):
    n, d = h2.shape
    tm = ROUTE_ROWS
    return pl.pallas_call(
        _sort_kernel,
        grid_spec=pltpu.PrefetchScalarGridSpec(
            num_scalar_prefetch=7,
            grid=(n // tm,),
            in_specs=[pl.BlockSpec((tm, d), lambda i, *_: (i, 0)),
                      pl.BlockSpec((1, 8, tm), lambda i, *_: (i, 0, 0))],
            out_specs=pl.BlockSpec(memory_space=pl.ANY),
            scratch_shapes=[pltpu.VMEM((2, SORT_ROWS, d), BF16), pltpu.VMEM((EXPERT_ROWS, d), BF16),
                            pltpu.SemaphoreType.DMA((2,)), pltpu.SemaphoreType.DMA(())],
        ),
        out_shape=jax.ShapeDtypeStruct((max_rows, d), BF16),
        compiler_params=pltpu.CompilerParams(dimension_semantics=("arbitrary",), vmem_limit_bytes=VMEM_LIMIT_BYTES),
        name="moe_sort",
    )(sched["seg_dst"], sched["seg_rows"], sched["tile_off"], sched["tile_rows"], sched["tail_dst"],
      sched["tail_rows"], sched["n_used"], h2, route_t)


def _expert_kernel(tile_expert_ref, n_used_ref, xs_ref, wg_ref, wu_ref, wd_ref, ys_ref, wg_bf, wu_bf, wd_bf):
    t = pl.program_id(0)
    used = t < n_used_ref[0]

    @pl.when(used & ((t == 0) | (tile_expert_ref[t] != tile_expert_ref[jnp.maximum(t - 1, 0)])))
    def _():
        wg_bf[...] = wg_ref[0].astype(BF16)
        wu_bf[...] = wu_ref[0].astype(BF16)
        wd_bf[...] = wd_ref[0].astype(BF16)

    @pl.when(used)
    def _():
        xs = xs_ref[...]
        gate = _dot(xs, wg_bf[...])
        up = _dot(xs, wu_bf[...])
        hidden = (gate * jax.nn.sigmoid(gate) * up).astype(BF16)
        ys_ref[...] = _dot(hidden, wd_bf[...]).astype(BF16)

    @pl.when(jnp.logical_not(used))
    def _():
        ys_ref[...] = jnp.zeros(ys_ref.shape, BF16)


def _expert_call(xs, wg, wu, wd, sched):
    rows, d = xs.shape
    tr = EXPERT_ROWS
    blk = lambda t, te, nu: (jnp.minimum(t, nu[0] - 1), 0)
    wsel = lambda t, te, nu: (te[jnp.minimum(t, nu[0] - 1)], 0, 0)
    return pl.pallas_call(
        _expert_kernel,
        grid_spec=pltpu.PrefetchScalarGridSpec(
            num_scalar_prefetch=2,
            grid=(rows // tr,),
            in_specs=[pl.BlockSpec((tr, d), blk),
                      pl.BlockSpec((1, d, EXPERT_FF), wsel), pl.BlockSpec((1, d, EXPERT_FF), wsel),
                      pl.BlockSpec((1, EXPERT_FF, d), wsel)],
            out_specs=pl.BlockSpec((tr, d), lambda t, te, nu: (t, 0)),
            scratch_shapes=[pltpu.VMEM((d, EXPERT_FF), BF16), pltpu.VMEM((d, EXPERT_FF), BF16),
                            pltpu.VMEM((EXPERT_FF, d), BF16)],
        ),
        out_shape=jax.ShapeDtypeStruct((rows, d), BF16),
        compiler_params=pltpu.CompilerParams(dimension_semantics=("arbitrary",), vmem_limit_bytes=VMEM_LIMIT_BYTES),
        name="moe_experts",
    )(sched["tile_expert"], sched["n_used"], xs, wg, wu, wd)


def _combine_kernel(seg_dst_ref, seg_rows_ref, tile_off_ref, tile_rows_ref, ys_ref, route_ref, x1_ref, o_ref,
                    buf_ref, sem):
    i = pl.program_id(0)
    tm = x1_ref.shape[0]
    slot = i % 2

    def fetch(tile, into):
        buf_ref[into] = jnp.zeros(buf_ref.shape[1:], BF16)
        _segment_copies(tile, seg_dst_ref, seg_rows_ref, tile_off_ref, ys_ref, buf_ref.at[into], sem.at[into], False)

    @pl.when(i == 0)
    def _():
        fetch(i, slot)

    @pl.when(i + 1 < pl.num_programs(0))
    def _():
        fetch(i + 1, 1 - slot)

    route = route_ref[...]
    d1 = route[:, 0:1].astype(jnp.int32)
    d2 = route[:, 1:2].astype(jnp.int32)
    w1 = route[:, 2:3]
    w2 = route[:, 3:4]
    _wait_rows(buf_ref.at[slot], tile_rows_ref[i], sem.at[slot])

    def combine_rows(n_rows):
        r = lax.broadcasted_iota(jnp.int32, (tm, n_rows), 1)
        weights = (jnp.where(r == d1, w1, 0.0) + jnp.where(r == d2, w2, 0.0)).astype(BF16)
        o_ref[...] = x1_ref[...] + _dot(weights, buf_ref[slot, 0:n_rows])

    @pl.when(tile_rows_ref[i] <= SORT_ROWS_COMMON)
    def _():
        combine_rows(SORT_ROWS_COMMON)

    @pl.when(tile_rows_ref[i] > SORT_ROWS_COMMON)
    def _():
        combine_rows(SORT_ROWS)


def _combine_call(ys, route, x1, sched):
    n, d = x1.shape
    tm = ROUTE_ROWS
    return pl.pallas_call(
        _combine_kernel,
        grid_spec=pltpu.PrefetchScalarGridSpec(
            num_scalar_prefetch=4,
            grid=(n // tm,),
            in_specs=[pl.BlockSpec(memory_space=pl.ANY),
                      pl.BlockSpec((tm, LANES), lambda i, *_: (i, 0)),
                      pl.BlockSpec((tm, d), lambda i, *_: (i, 0))],
            out_specs=pl.BlockSpec((tm, d), lambda i, *_: (i, 0)),
            scratch_shapes=[pltpu.VMEM((2, SORT_ROWS, d), BF16), pltpu.SemaphoreType.DMA((2,))],
        ),
        out_shape=jax.ShapeDtypeStruct((n, d), F32),
        compiler_params=pltpu.CompilerParams(dimension_semantics=("arbitrary",), vmem_limit_bytes=VMEM_LIMIT_BYTES),
        name="moe_combine",
    )(sched["seg_dst"], sched["seg_rows"], sched["tile_off"], sched["tile_rows"], ys, route, x1)


def _schedule_kernel(cnt_ref, seg_dst_ref, tile_off_ref, tile_rows_ref, misc_ref):
    hp = functools.partial(jnp.dot, preferred_element_type=F32, precision=lax.Precision.HIGHEST)
    cnt = cnt_ref[...]
    n_tiles = cnt.shape[0]
    tile_before = jnp.where(lax.broadcasted_iota(jnp.int32, (n_tiles, n_tiles), 1)
                            < lax.broadcasted_iota(jnp.int32, (n_tiles, n_tiles), 0), 1.0, 0.0)
    expert_before = jnp.where(lax.broadcasted_iota(jnp.int32, (LANES, LANES), 0)
                              < lax.broadcasted_iota(jnp.int32, (LANES, LANES), 1), 1.0, 0.0)
    expert_rows = jnp.sum(cnt, axis=0, keepdims=True)
    region = jnp.floor((expert_rows + (EXPERT_ROWS - 1)) * (1.0 / EXPERT_ROWS)) * EXPERT_ROWS
    region_start = hp(jnp.broadcast_to(region, (8, LANES)), expert_before)[0:1]
    seg_dst_ref[...] = (region_start + hp(tile_before, cnt)).astype(jnp.int32)
    tile_off_ref[...] = hp(cnt, expert_before).astype(jnp.int32)
    tile_rows_ref[...] = jnp.broadcast_to(jnp.sum(cnt, axis=-1, keepdims=True), cnt.shape).astype(jnp.int32)
    n_used = jnp.sum(region, axis=-1, keepdims=True) * (1.0 / EXPERT_ROWS)
    row = lax.broadcasted_iota(jnp.int32, (8, LANES), 0)
    misc = jnp.where(row == 0, region_start + expert_rows,
                     jnp.where(row == 1, region - expert_rows,
                               jnp.where(row == 2, region_start + region, n_used)))
    misc_ref[...] = misc.astype(jnp.int32)


def _moe_schedule(cnt, n_tokens):
    n_tiles = cnt.shape[0]
    table = jax.ShapeDtypeStruct((n_tiles, LANES), jnp.int32)
    seg_dst, tile_off, tile_rows, misc = pl.pallas_call(
        _schedule_kernel,
        out_shape=[table, table, table, jax.ShapeDtypeStruct((8, LANES), jnp.int32)],
        name="moe_schedule",
    )(cnt.reshape(n_tiles, LANES))
    max_rows = 2 * n_tokens + n_tiles * N_EXPERTS * (SEG_ALIGN - 1) + N_EXPERTS * (EXPERT_ROWS - 1)
    max_tiles = -(-max_rows // EXPERT_ROWS)
    tile_start = jnp.arange(max_tiles, dtype=jnp.int32) * EXPERT_ROWS
    region_end = misc[2, :N_EXPERTS]
    tile_expert = jnp.minimum(jnp.sum((region_end[None, :] <= tile_start[:, None]).astype(jnp.int32), axis=1),
                              N_EXPERTS - 1)
    flat = lambda a: a[:, :N_EXPERTS].reshape(-1)
    sched = {
        "seg_dst": flat(seg_dst),
        "seg_rows": flat(cnt.reshape(n_tiles, LANES).astype(jnp.int32)),
        "tile_off": flat(tile_off),
        "tile_rows": tile_rows[:, 0],
        "tail_dst": misc[0, :N_EXPERTS],
        "tail_rows": misc[1, :N_EXPERTS],
        "tile_expert": tile_expert,
        "n_used": misc[3, :1],
    }
    return sched, max_tiles * EXPERT_ROWS


def _rotary_tables(seq_len, rot_dim, period, first, gain, scale):
    half = rot_dim // 2
    pos = jnp.arange(seq_len, dtype=F32)
    inv = 1.0 / (ROPE_THETA ** (jnp.arange(0, rot_dim, 2, dtype=F32) / rot_dim))
    ang = pos[:, None] * inv[None, :]
    cos, sin = jnp.cos(ang), jnp.sin(ang)
    lane = jnp.arange(LANES)
    rel = (lane % period) - first
    active = (rel >= 0) & (rel < rot_dim)
    idx = jnp.clip(rel, 0, rot_dim - 1) % half
    sign = jnp.where(rel < half, -1.0, 1.0)
    partner = jnp.where(active, jnp.where(rel < half, lane + half, lane - half), lane)
    c = jnp.where(active[None, :], cos[:, idx], 1.0)
    s = jnp.where(active[None, :], sin[:, idx] * sign[None, :], 0.0)
    gain = gain.astype(F32)
    return (c * gain[None, :] * scale).astype(F32), (s * gain[partner][None, :] * scale).astype(F32)


def _head_pad(w, heads, width):
    r = w.shape[0]
    w = w.reshape(r, heads, width)
    return jnp.pad(w, ((0, 0), (0, 0), (0, LANES - width))).reshape(r, heads * LANES)


def _layer_params(l, seq_len, norm_mix, w_in, mla_q_latent_norm, w_mla_uq, mla_kv_latent_norm, w_mla_ukv,
                  mla_q_gain, mla_k_gain, diff_q_gain, diff_k_gain, w_mla_up, w_diff_up, w_out, norm_ffn,
                  w_router_group, b_router_group, w_router_expert, b_router_expert):
    d = w_in.shape[1]
    sizes = (MLA_Q_RANK, MLA_KV_RANK, MLA_ROPE, DIFF_QK_WIDTH, DIFF_QK_WIDTH, DIFF_V_WIDTH, d, d)
    offs = [0]
    for s in sizes:
        offs.append(offs[-1] + s)
    wi = w_in[l]
    seg = [wi[:, offs[k]:offs[k + 1]] for k in range(len(sizes))]
    row = lambda g: g.astype(F32)[None, :]
    p = {}
    p["gmix"] = row(norm_mix[l])
    p["wql"] = seg[0].astype(BF16)
    p["wkvl"] = seg[1].astype(BF16)
    p["wkr"] = jnp.pad(seg[2], ((0, 0), (MLA_NOPE, LANES - MLA_QK))).astype(BF16)
    p["wdk"] = seg[4].astype(BF16)
    p["wdqvt"] = jnp.concatenate([seg[3].T, seg[5].T], axis=0).astype(BF16)
    p["wgm"], p["wgd"] = seg[6].astype(BF16), seg[7].astype(BF16)
    p["gql"] = row(mla_q_latent_norm[l])
    p["wuqt"] = _head_pad(w_mla_uq[l], MLA_HEADS, MLA_QK).T.astype(BF16)
    p["gkvl"] = row(mla_kv_latent_norm[l])
    ukv = w_mla_ukv[l].reshape(MLA_KV_RANK, MLA_HEADS, MLA_NOPE + MLA_V)
    p["wuk"] = _head_pad(ukv[:, :, :MLA_NOPE].reshape(MLA_KV_RANK, -1), MLA_HEADS, MLA_NOPE).astype(BF16)
    p["wuvt"] = ukv[:, :, MLA_NOPE:].reshape(MLA_KV_RANK, -1).T.astype(BF16)
    gq = jnp.pad(mla_q_gain[l], (0, LANES - MLA_QK))
    gk = jnp.pad(mla_k_gain[l], (0, LANES - MLA_QK))
    nope = jnp.arange(LANES) < MLA_NOPE
    p["gkn"] = jnp.where(nope, gk, 0.0).astype(F32)[None, :]
    aq, bq = _rotary_tables(seq_len, MLA_ROPE, LANES, MLA_NOPE, gq, LOG2E * MLA_QK ** -0.5)
    p["aq"], p["bq"] = aq.T, bq.T
    ak, bk = _rotary_tables(seq_len, MLA_ROPE, LANES, MLA_NOPE, jnp.where(nope, 0.0, gk), 1.0)
    p["ak"], p["bk"] = ak, bk
    adq, bdq = _rotary_tables(seq_len, DIFF_ROPE, DIFF_HEAD_DIM, 0, jnp.tile(diff_q_gain[l], 2),
                              LOG2E * DIFF_HEAD_DIM ** -0.5)
    p["adq"], p["bdq"] = adq.T, bdq.T
    p["adk"], p["bdk"] = _rotary_tables(seq_len, DIFF_ROPE, DIFF_HEAD_DIM, 0, jnp.tile(diff_k_gain[l], 2), 1.0)
    p["wmu"] = w_mla_up[l].astype(BF16)
    p["wdu"] = w_diff_up[l].astype(BF16)
    p["wout"] = w_out[l].astype(BF16)
    p["gffn"] = row(norm_ffn[l])
    wr = jnp.concatenate([w_router_expert[l], w_router_group[l]], axis=1).astype(F32)
    wrt = jnp.pad(wr, ((0, 0), (0, LANES - wr.shape[1]))).T
    wrt_hi = wrt.astype(BF16)
    p["wrt"] = jnp.concatenate([wrt_hi, (wrt - wrt_hi.astype(F32)).astype(BF16)], axis=0)
    br = jnp.concatenate([b_router_expert[l], b_router_group[l]]).astype(F32)
    p["brt"] = jnp.broadcast_to(jnp.pad(br, (0, LANES - br.shape[0]))[:, None], (LANES, MERGE_ROWS))
    return p


def kernel(x, norm_mix, w_in, mla_q_latent_norm, w_mla_uq, mla_kv_latent_norm, w_mla_ukv, mla_q_gain, mla_k_gain, diff_q_gain, diff_k_gain, lambda_q1, lambda_k1, lambda_q2, lambda_k2, diff_subln, w_mla_up, w_diff_up, w_out, norm_ffn, w_router_group, b_router_group, w_router_expert, b_router_expert, w_expert_gate, w_expert_up, w_expert_down):
    batch, seq_len, d = x.shape
    x2 = x.reshape(batch * seq_len, d)
    row = lambda g: g.astype(F32)[None, :]
    for l in range(norm_mix.shape[0]):
        lam_init = 0.8 - 0.6 * math.exp(-0.3 * l)
        p = _layer_params(l, seq_len, norm_mix, w_in, mla_q_latent_norm, w_mla_uq, mla_kv_latent_norm, w_mla_ukv,
                          mla_q_gain, mla_k_gain, diff_q_gain, diff_k_gain, w_mla_up, w_diff_up, w_out, norm_ffn,
                          w_router_group, b_router_group, w_router_expert, b_router_expert)
        qmt, km, vtm, qdt, kd, vtd, sgm, sgd = _proj_call(x2, seq_len, p)
        om = _mla_call(qmt, km, vtm, batch, seq_len)
        od = _diff_call(qdt, kd, vtd, row(lambda_q1[l]), row(lambda_k1[l]), row(lambda_q2[l]), row(lambda_k2[l]),
                        diff_subln[l].astype(F32)[:, None], lam_init, batch, seq_len)
        x1, h2, route, route_t, cnt = _merge_call(x2, om, od, sgm, sgd, p)
        sched, max_rows = _moe_schedule(cnt, x2.shape[0])
        xs = _sort_call(h2, route_t, sched, max_rows)
        ys = _expert_call(xs, w_expert_gate[l], w_expert_up[l], w_expert_down[l], sched)
        x2 = _combine_call(ys, route, x1, sched)
    return x2.reshape(batch, seq_len, d)
```

```python
import functools
import math

import jax
import jax.numpy as jnp
from jax import lax
from jax.experimental import pallas as pl
from jax.experimental.pallas import tpu as pltpu

CHUNK = 64
ROPE_THETA = 500000.0
EPS = 1e-6

MLA_HEADS = 8
MLA_NOPE = 64
MLA_ROPE = 32
MLA_V = 64
MLA_QK = MLA_NOPE + MLA_ROPE
MLA_Q_RANK = 256
MLA_KV_RANK = 128

DIFF_HEADS = 4
DIFF_HEAD_DIM = 64
DIFF_V_DIM = 2 * DIFF_HEAD_DIM
DIFF_ROPE = DIFF_HEAD_DIM // 4
DIFF_QK_WIDTH = DIFF_HEADS * 2 * DIFF_HEAD_DIM
DIFF_V_WIDTH = DIFF_HEADS * DIFF_V_DIM

N_GROUPS = 4
EXPERTS_PER_GROUP = 8
N_EXPERTS = N_GROUPS * EXPERTS_PER_GROUP
EXPERT_FF = 256

LANES = 128
VMEM_LIMIT_BYTES = 48 * 1024 * 1024

PROJ_ROWS = 512
ATTN_Q_ROWS = 512
ATTN_K_ROWS = 256
MERGE_ROWS = 512
ROUTE_ROWS = MERGE_ROWS
SEG_ALIGN = 16
SORT_ROWS = 2 * ROUTE_ROWS + N_EXPERTS * SEG_ALIGN
SORT_ROWS_COMMON = 2 * ROUTE_ROWS + N_EXPERTS * SEG_ALIGN // 2
EXPERT_ROWS = 512
MLA_HEADS_PER_STEP = 4
DIFF_HEADS_PER_STEP = 2
LOG2E = 1.4426950408889634

BF16 = jnp.bfloat16
F32 = jnp.float32


def _dot(a, b):
    return jnp.dot(a, b, preferred_element_type=F32)


def _dot_nt(a, b):
    return lax.dot_general(a, b, (((1,), (1,)), ((), ())), preferred_element_type=F32)


def _rms(x, width):
    return x * lax.rsqrt(jnp.sum(x * x, axis=-1, keepdims=True) * (1.0 / width) + EPS)


def _rotary_partner(y, half):
    lane = lax.broadcasted_iota(jnp.int32, y.shape, 1)
    up = pltpu.roll(y, LANES - half, 1)
    down = pltpu.roll(y, half, 1)
    return jnp.where((lane // half) % 2 == 0, up, down)


def _swap_row_blocks(y, first, half, period):
    parts = []
    for base in range(0, y.shape[0], period):
        a = base + first
        parts += [y[base:a], y[a + half:a + 2 * half], y[a:a + half], y[a + 2 * half:base + period]]
    return jnp.concatenate([p for p in parts if p.shape[0]], axis=0)


def _store_k_tiles(o_ref, vt):
    tk = o_ref.shape[-1]
    for c in range(o_ref.shape[0]):
        o_ref[c] = vt[:, c * tk:(c + 1) * tk].astype(BF16)


def _proj_kernel(x_ref, gmix_ref, wql_ref, wkvl_ref, wkr_ref, wdk_ref, wdqvt_ref, wgm_ref, wgd_ref,
                 gql_ref, wuqt_ref, gkvl_ref, wuk_ref, wuvt_ref, gkn_ref,
                 aq_ref, bq_ref, adq_ref, bdq_ref, ak_ref, bk_ref, adk_ref, bdk_ref,
                 qmt_ref, km_ref, vtm_ref, qdt_ref, kd_ref, vtd_ref, sgm_ref, sgd_ref):
    x = x_ref[...]
    h = (_rms(x, x.shape[-1]) * gmix_ref[...]).astype(BF16)

    ql = (_rms(_dot(h, wql_ref[...]), MLA_Q_RANK) * gql_ref[...]).astype(BF16)
    qt = _dot_nt(wuqt_ref[...], ql)
    aq, bq = aq_ref[...], bq_ref[...]
    for hd in range(MLA_HEADS):
        rows = slice(hd * LANES, (hd + 1) * LANES)
        qh = qt[rows]
        r = lax.rsqrt(jnp.sum(qh * qh, axis=0, keepdims=True) * (1.0 / MLA_QK) + EPS)
        y = (qh * aq + _swap_row_blocks(qh, MLA_NOPE, MLA_ROPE // 2, LANES) * bq) * r
        qmt_ref[0, rows, :] = y.astype(BF16)

    kvl = (_rms(_dot(h, wkvl_ref[...]), MLA_KV_RANK) * gkvl_ref[...]).astype(BF16)
    kr = _dot(h, wkr_ref[...])
    kr_rot = kr * ak_ref[...] + _rotary_partner(kr, MLA_ROPE // 2) * bk_ref[...]
    kr_ss = jnp.sum(kr * kr, axis=-1, keepdims=True)
    kn = _dot(kvl, wuk_ref[...])
    _store_k_tiles(vtm_ref, _dot_nt(wuvt_ref[...], kvl))
    gkn = gkn_ref[...]
    for hd in range(MLA_HEADS):
        sl = slice(hd * LANES, (hd + 1) * LANES)
        knh = kn[:, sl]
        r = lax.rsqrt((jnp.sum(knh * knh, axis=-1, keepdims=True) + kr_ss) * (1.0 / MLA_QK) + EPS)
        km_ref[:, sl] = ((knh * gkn + kr_rot) * r).astype(BF16)

    qvt = _dot_nt(wdqvt_ref[...], h)
    _store_k_tiles(vtd_ref, qvt[DIFF_QK_WIDTH:])
    adq, bdq = adq_ref[...], bdq_ref[...]
    for hd in range(DIFF_HEADS):
        rows = slice(hd * LANES, (hd + 1) * LANES)
        qh = qvt[rows]
        t = qh * adq + _swap_row_blocks(qh, 0, DIFF_ROPE // 2, DIFF_HEAD_DIM) * bdq
        halves = []
        for f in range(2):
            part = qh[f * DIFF_HEAD_DIM:(f + 1) * DIFF_HEAD_DIM]
            r = lax.rsqrt(jnp.sum(part * part, axis=0, keepdims=True) * (1.0 / DIFF_HEAD_DIM) + EPS)
            halves.append(t[f * DIFF_HEAD_DIM:(f + 1) * DIFF_HEAD_DIM] * r)
        qdt_ref[0, rows, :] = jnp.concatenate(halves, axis=0).astype(BF16)

    kd = _dot(h, wdk_ref[...])
    adk, bdk = adk_ref[...], bdk_ref[...]
    for hd in range(DIFF_HEADS):
        sl = slice(hd * LANES, (hd + 1) * LANES)
        th = kd[:, sl]
        lane = lax.broadcasted_iota(jnp.int32, th.shape, 1)
        sq = th * th
        lo = jnp.sum(jnp.where(lane < DIFF_HEAD_DIM, sq, 0.0), axis=-1, keepdims=True)
        tot = jnp.sum(sq, axis=-1, keepdims=True)
        r = lax.rsqrt(jnp.where(lane < DIFF_HEAD_DIM, lo, tot - lo) * (1.0 / DIFF_HEAD_DIM) + EPS)
        kd_ref[:, sl] = ((th * adk + _rotary_partner(th, DIFF_ROPE // 2) * bdk) * r).astype(BF16)

    sgm_ref[...] = jax.nn.sigmoid(_dot(h, wgm_ref[...])).astype(BF16)
    sgd_ref[...] = jax.nn.sigmoid(_dot(h, wgd_ref[...])).astype(BF16)


def _proj_call(x2, seq_len, p):
    n, d = x2.shape
    tm = PROJ_ROWS
    pos_blocks = seq_len // tm
    row = lambda i: (i, 0)
    const = lambda i: (0, 0)
    weights = [p["gmix"], p["wql"], p["wkvl"], p["wkr"], p["wdk"], p["wdqvt"], p["wgm"], p["wgd"],
               p["gql"], p["wuqt"], p["gkvl"], p["wuk"], p["wuvt"], p["gkn"]]
    feature_major_tables = [p["aq"], p["bq"], p["adq"], p["bdq"]]
    token_major_tables = [p["ak"], p["bk"], p["adk"], p["bdk"]]
    in_specs = ([pl.BlockSpec((tm, d), row)]
                + [pl.BlockSpec(w.shape, const) for w in weights]
                + [pl.BlockSpec((LANES, tm), lambda i: (0, i % pos_blocks)) for _ in feature_major_tables]
                + [pl.BlockSpec((tm, LANES), lambda i: (i % pos_blocks, 0)) for _ in token_major_tables])
    tk = ATTN_K_ROWS
    k_tiles = lambda width: (pl.BlockSpec((tm // tk, width, tk), lambda i: (i, 0, 0)),
                             jax.ShapeDtypeStruct((n // tk, width, tk), BF16))
    token_major = lambda width: (pl.BlockSpec((tm, width), row), jax.ShapeDtypeStruct((n, width), BF16))
    assert tm == ATTN_Q_ROWS
    feature_major = lambda width: (pl.BlockSpec((1, width, tm), lambda i: (i, 0, 0)),
                                   jax.ShapeDtypeStruct((n // tm, width, tm), BF16))
    outs = [feature_major(MLA_HEADS * LANES), token_major(MLA_HEADS * LANES), k_tiles(MLA_HEADS * MLA_V),
            feature_major(DIFF_QK_WIDTH), token_major(DIFF_QK_WIDTH), k_tiles(DIFF_V_WIDTH),
            token_major(d), token_major(d)]
    return pl.pallas_call(
        _proj_kernel,
        grid=(n // tm,),
        in_specs=in_specs,
        out_specs=[o[0] for o in outs],
        out_shape=[o[1] for o in outs],
        compiler_params=pltpu.CompilerParams(dimension_semantics=("parallel",), vmem_limit_bytes=VMEM_LIMIT_BYTES),
        name="proj",
    )(x2, *weights, *feature_major_tables, *token_major_tables)


def _chunk_mask_t(tk, width):
    kc = lax.broadcasted_iota(jnp.int32, (tk, width), 0) // CHUNK
    qc = lax.broadcasted_iota(jnp.int32, (tk, width), 1) // CHUNK
    return kc <= qc


ONES_ROWS = 16


def _with_ones_rows(vt):
    return jnp.concatenate([vt, jnp.ones((ONES_ROWS, vt.shape[1]), vt.dtype)], axis=0)


def _softmax_step_t(st, vt_ones, m_ref, acc_ref, lo):
    m_prev = m_ref[:, lo:]
    m_new = jnp.maximum(m_prev, jnp.max(st, axis=0, keepdims=True))
    alpha = jnp.exp2(m_prev - m_new)
    pr = jnp.exp2(st - m_new)
    acc_ref[:, lo:] = alpha * acc_ref[:, lo:] + _dot(vt_ones, pr.astype(BF16))
    m_ref[:, lo:] = m_new


def _normalized(acc_ref, dv):
    acc = acc_ref[...]
    return acc[:dv] / acc[dv:dv + 1]


STATE_REFS = 4


def _attn_scratch(chains, dv, tq, tk):
    per_chain = [pltpu.VMEM((1, tq), F32), pltpu.VMEM((dv + ONES_ROWS, tq), F32),
                 pltpu.VMEM((tk, tq), F32), pltpu.VMEM((tk, tq), F32)]
    return per_chain * chains


def _flash_attention(n_q_tiles, scratch_refs, score_fn, value_fn, finalize_fn, tk, tq):
    ratio = tq // tk
    assert tq == ratio * tk and ratio % 2 == 0
    n_chains = len(scratch_refs) // STATE_REFS
    chains = [scratch_refs[STATE_REFS * c:STATE_REFS * (c + 1)] for c in range(n_chains)]

    def scores(i, t, slot, lo=0):
        for c, ch in enumerate(chains):
            ch[2 + slot][:, lo:] = score_fn(c, i, t, lo)

    def update(t, slot, diag=None):
        lo = 0 if diag is None else diag * tk
        for c, ch in enumerate(chains):
            st = ch[2 + slot][:, lo:]
            if diag is not None:
                st = jnp.where(_chunk_mask_t(tk, tq - lo), st, -jnp.inf)
            _softmax_step_t(st, _with_ones_rows(value_fn(c, t)), ch[0], ch[1], lo)

    scores(0, 0, 0)

    def query_tile(i, carry):
        for m_ref, acc_ref, _, _ in chains:
            m_ref[...] = jnp.full(m_ref.shape, -jnp.inf, F32)
            acc_ref[...] = jnp.zeros(acc_ref.shape, F32)

        def pair(p, c):
            t = 2 * p
            scores(i, t + 1, 1)
            update(t, 0)
            scores(i, t + 2, 0)
            update(t + 1, 1)
            return c

        lax.fori_loop(0, i * (ratio // 2), pair, 0)
        first_diag = ratio * i
        for d in range(ratio):
            if d + 1 < ratio:
                scores(i, first_diag + d + 1, (d + 1) % 2, lo=(d + 1) * tk)
            else:
                scores(jnp.minimum(i + 1, n_q_tiles - 1), 0, 0)
            update(first_diag + d, d % 2, diag=d)
        finalize_fn(i, [ch[1] for ch in chains])
        return carry

    lax.fori_loop(0, n_q_tiles, query_tile, 0)


def _mla_kernel(qt_ref, k_ref, vt_ref, o_ref, *scratch_refs):
    tq, tk = ATTN_Q_ROWS, ATTN_K_ROWS

    def score_fn(c, i, t, lo):
        rows = pl.ds(pl.multiple_of(t * tk, tk), tk)
        sl = slice(c * LANES, (c + 1) * LANES)
        return _dot(k_ref[rows, sl], qt_ref[i, sl, lo:])

    def value_fn(c, t):
        return vt_ref[t, c * MLA_V:(c + 1) * MLA_V, :]

    def finalize_fn(i, accs):
        ot = jnp.concatenate([_normalized(acc_ref, MLA_V) for acc_ref in accs], axis=0)
        o_ref[pl.ds(pl.multiple_of(i * tq, tq), tq), :] = ot.T.astype(BF16)

    _flash_attention(qt_ref.shape[0], scratch_refs, score_fn, value_fn, finalize_fn, tk, tq)


def _mla_call(qmt, km, vtm, batch, seq_len):
    n = km.shape[0]
    tq, tk, hps = ATTN_Q_ROWS, ATTN_K_ROWS, MLA_HEADS_PER_STEP
    return pl.pallas_call(
        _mla_kernel,
        grid=(batch, MLA_HEADS // hps),
        in_specs=[pl.BlockSpec((seq_len // tq, hps * LANES, tq), lambda b, h: (b, h, 0)),
                  pl.BlockSpec((seq_len, hps * LANES), lambda b, h: (b, h)),
                  pl.BlockSpec((seq_len // tk, hps * MLA_V, tk), lambda b, h: (b, h, 0))],
        out_specs=pl.BlockSpec((seq_len, hps * MLA_V), lambda b, h: (b, h)),
        out_shape=jax.ShapeDtypeStruct((n, MLA_HEADS * MLA_V), BF16),
        scratch_shapes=_attn_scratch(hps, MLA_V, tq, tk),
        compiler_params=pltpu.CompilerParams(dimension_semantics=("parallel", "parallel"),
                                             vmem_limit_bytes=VMEM_LIMIT_BYTES),
        name="mla_attn",
    )(qmt, km, vtm)


def _diff_kernel(lam_init, qt_ref, k_ref, vt_ref, lq1_ref, lk1_ref, lq2_ref, lk2_ref, subln_ref, o_ref,
                 *scratch_refs):
    tq, tk = ATTN_Q_ROWS, ATTN_K_ROWS
    hps = DIFF_HEADS_PER_STEP

    def score_fn(c, i, t, lo):
        rows = pl.ds(pl.multiple_of(t * tk, tk), tk)
        hd, f = c // 2, c % 2
        half = qt_ref[i, hd * LANES + f * DIFF_HEAD_DIM:hd * LANES + (f + 1) * DIFF_HEAD_DIM, lo:]
        zero = jnp.zeros_like(half)
        q = jnp.concatenate([half, zero] if f == 0 else [zero, half], axis=0)
        return _dot(k_ref[rows, hd * LANES:(hd + 1) * LANES], q)

    def value_fn(c, t):
        hd = c // 2
        return vt_ref[t, hd * DIFF_V_DIM:(hd + 1) * DIFF_V_DIM, :]

    lam = (jnp.exp(jnp.sum(lq1_ref[...] * lk1_ref[...], axis=-1, keepdims=True))
           - jnp.exp(jnp.sum(lq2_ref[...] * lk2_ref[...], axis=-1, keepdims=True)) + lam_init)
    subln = subln_ref[...] * (1.0 - lam_init)

    def finalize_fn(i, accs):
        heads = []
        for hd in range(hps):
            ot = _normalized(accs[2 * hd], DIFF_V_DIM) - lam * _normalized(accs[2 * hd + 1], DIFF_V_DIM)
            ot = ot * lax.rsqrt(jnp.sum(ot * ot, axis=0, keepdims=True) * (1.0 / DIFF_V_DIM) + EPS)
            heads.append(ot * subln)
        o_ref[pl.ds(pl.multiple_of(i * tq, tq), tq), :] = jnp.concatenate(heads, axis=0).T.astype(BF16)

    _flash_attention(qt_ref.shape[0], scratch_refs, score_fn, value_fn, finalize_fn, tk, tq)


def _diff_call(qdt, kd, vtd, lq1, lk1, lq2, lk2, subln_col, lam_init, batch, seq_len):
    n = kd.shape[0]
    tq, tk, hps = ATTN_Q_ROWS, ATTN_K_ROWS, DIFF_HEADS_PER_STEP
    small = lambda a: pl.BlockSpec(a.shape, lambda b, h: (0, 0))
    return pl.pallas_call(
        functools.partial(_diff_kernel, lam_init),
        grid=(batch, DIFF_HEADS // hps),
        in_specs=[pl.BlockSpec((seq_len // tq, hps * LANES, tq), lambda b, h: (b, h, 0)),
                  pl.BlockSpec((seq_len, hps * LANES), lambda b, h: (b, h)),
                  pl.BlockSpec((seq_len // tk, hps * DIFF_V_DIM, tk), lambda b, h: (b, h, 0)),
                  small(lq1), small(lk1), small(lq2), small(lk2), small(subln_col)],
        out_specs=pl.BlockSpec((seq_len, hps * LANES), lambda b, h: (b, h)),
        out_shape=jax.ShapeDtypeStruct((n, DIFF_V_WIDTH), BF16),
        scratch_shapes=_attn_scratch(2 * hps, DIFF_V_DIM, tq, tk),
        compiler_params=pltpu.CompilerParams(dimension_semantics=("parallel", "parallel"),
                                             vmem_limit_bytes=VMEM_LIMIT_BYTES),
        name="diff_attn",
    )(qdt, kd, vtd, lq1, lk1, lq2, lk2, subln_col)


def _merge_kernel(x_ref, om_ref, od_ref, sgm_ref, sgd_ref, wmu_ref, wdu_ref, wout_ref, gffn_ref, wrt_ref,
                  brt_ref, x1_ref, h2_ref, route_ref, route_t_ref, cnt_ref):
    merged = (sgm_ref[...].astype(F32) * _dot(om_ref[...], wmu_ref[...])
              + sgd_ref[...].astype(F32) * _dot(od_ref[...], wdu_ref[...]))
    x1 = x_ref[...] + _dot(merged.astype(BF16), wout_ref[...])
    x1_ref[...] = x1
    h2 = _rms(x1, x1.shape[-1]) * gffn_ref[...]
    h2_hi = h2.astype(BF16)
    h2_ref[...] = h2_hi
    tm = h2.shape[0]

    h2_lo = (h2 - h2_hi.astype(F32)).astype(BF16)
    by_hi = _dot_nt(wrt_ref[...], h2_hi)
    logits = by_hi[:LANES] + by_hi[LANES:] + _dot_nt(wrt_ref[:LANES, :], h2_lo) + brt_ref[...]
    row = lax.broadcasted_iota(jnp.int32, logits.shape, 0)
    neg = -jnp.inf
    big = jnp.int32(1 << 20)

    def top(vals):
        mx = jnp.max(vals, axis=0, keepdims=True)
        idx = jnp.min(jnp.where(vals == mx, row, big), axis=0, keepdims=True)
        return mx, idx

    gl = jnp.where((row >= N_EXPERTS) & (row < N_EXPERTS + N_GROUPS), logits, neg)
    gmax, gidx = top(gl)
    pg_sel = 1.0 / jnp.sum(jnp.exp(gl - gmax), axis=0, keepdims=True)
    el = jnp.where((row < N_EXPERTS) & (row // EXPERTS_PER_GROUP == gidx - N_EXPERTS), logits, neg)
    m1, i1 = top(el)
    m2, i2 = top(jnp.where(row == i1, neg, el))
    e2 = jnp.exp(m2 - m1)
    w1 = pg_sel / (1.0 + e2)
    w2 = w1 * e2

    sel = jnp.where((row == i1) | (row == i2), 1.0, 0.0).astype(BF16)
    t_row = lax.broadcasted_iota(jnp.int32, (tm, tm), 0)
    t_col = lax.broadcasted_iota(jnp.int32, (tm, tm), 1)
    rank = _dot(sel, jnp.where(t_row < t_col, 1.0, 0.0).astype(BF16))
    cnt = _dot(sel, jnp.ones((tm, tm), BF16))
    seg = jnp.floor((cnt + (SEG_ALIGN - 1)) * (1.0 / SEG_ALIGN))
    e_row = lax.broadcasted_iota(jnp.int32, (LANES, LANES), 0)
    e_col = lax.broadcasted_iota(jnp.int32, (LANES, LANES), 1)
    off = _dot(jnp.where(e_col < e_row, 1.0, 0.0).astype(BF16), seg.astype(BF16)) * SEG_ALIGN
    dest = off + rank
    d1 = jnp.sum(jnp.where(row == i1, dest, 0.0), axis=0, keepdims=True)
    d2 = jnp.sum(jnp.where(row == i2, dest, 0.0), axis=0, keepdims=True)
    route_t = jnp.where(row == 0, d1, jnp.where(row == 1, d2, jnp.where(row == 2, w1, jnp.where(row == 3, w2, 0.0))))
    route_t_ref[0] = route_t[0:8]
    route_ref[...] = route_t.T
    cnt_ref[0] = (seg[:, :LANES] * SEG_ALIGN).T[0:1]


def _merge_call(x2, om, od, sgm, sgd, p):
    n, d = x2.shape
    tm = MERGE_ROWS
    row = lambda i: (i, 0)
    const = lambda i: (0, 0)
    weights = [p["wmu"], p["wdu"], p["wout"], p["gffn"], p["wrt"], p["brt"]]
    return pl.pallas_call(
        _merge_kernel,
        grid=(n // tm,),
        in_specs=([pl.BlockSpec((tm, a.shape[1]), row) for a in (x2, om, od, sgm, sgd)]
                  + [pl.BlockSpec(w.shape, const) for w in weights]),
        out_specs=[pl.BlockSpec((tm, d), row), pl.BlockSpec((tm, d), row), pl.BlockSpec((tm, LANES), row),
                   pl.BlockSpec((1, 8, tm), lambda i: (i, 0, 0)), pl.BlockSpec((1, 1, LANES), lambda i: (i, 0, 0))],
        out_shape=[jax.ShapeDtypeStruct((n, d), F32), jax.ShapeDtypeStruct((n, d), BF16),
                   jax.ShapeDtypeStruct((n, LANES), F32), jax.ShapeDtypeStruct((n // tm, 8, tm), F32),
                   jax.ShapeDtypeStruct((n // tm, 1, LANES), F32)],
        compiler_params=pltpu.CompilerParams(dimension_semantics=("parallel",), vmem_limit_bytes=VMEM_LIMIT_BYTES),
        name="merge_router",
    )(x2, om, od, sgm, sgd, *weights)


def _segment_copies(i, seg_dst_ref, seg_rows_ref, tile_off_ref, global_ref, tile_ref, sem, to_global):
    def body(e, carry):
        k = i * N_EXPERTS + e
        rows = pl.multiple_of(seg_rows_ref[k], SEG_ALIGN)

        @pl.when(rows > 0)
        def _():
            g = global_ref.at[pl.ds(pl.multiple_of(seg_dst_ref[k], SEG_ALIGN), rows)]
            t = tile_ref.at[pl.ds(pl.multiple_of(tile_off_ref[k], SEG_ALIGN), rows)]
            src, dst = (t, g) if to_global else (g, t)
            pltpu.make_async_copy(src, dst, sem).start()

        return carry

    lax.fori_loop(0, N_EXPERTS, body, 0)


def _wait_rows(tile_ref, rows, sem):
    @pl.when(rows > 0)
    def _():
        view = tile_ref.at[pl.ds(0, pl.multiple_of(rows, SEG_ALIGN))]
        pltpu.make_async_copy(view, view, sem).wait()


def _zero_unused_rows(tail_dst_ref, tail_rows_ref, n_used_ref, xs_ref, zero_ref, sem, start):
    n_tiles = xs_ref.shape[0] // EXPERT_ROWS
    if start:
        zero_ref[...] = jnp.zeros(zero_ref.shape, BF16)

    def tail(e, total):
        rows = pl.multiple_of(tail_rows_ref[e], SEG_ALIGN)
        if start:
            @pl.when(rows > 0)
            def _():
                dst = xs_ref.at[pl.ds(pl.multiple_of(tail_dst_ref[e], SEG_ALIGN), rows)]
                pltpu.make_async_copy(zero_ref.at[pl.ds(0, rows)], dst, sem).start()

        return total + rows

    total = lax.fori_loop(0, N_EXPERTS, tail, 0)
    if not start:
        _wait_rows(xs_ref, total + (n_tiles - n_used_ref[0]) * EXPERT_ROWS, sem)
        return

    def unused(t, carry):
        dst = xs_ref.at[pl.ds(pl.multiple_of(t * EXPERT_ROWS, EXPERT_ROWS), EXPERT_ROWS)]
        pltpu.make_async_copy(zero_ref, dst, sem).start()
        return carry

    lax.fori_loop(n_used_ref[0], n_tiles, unused, 0)


def _sort_kernel(seg_dst_ref, seg_rows_ref, tile_off_ref, tile_rows_ref, tail_dst_ref, tail_rows_ref, n_used_ref,
                 h2_ref, route_t_ref, xs_ref, sorted_ref, zero_ref, sem, zero_sem):
    i = pl.program_id(0)
    tm = h2_ref.shape[0]

    @pl.when(i == 0)
    def _():
        _zero_unused_rows(tail_dst_ref, tail_rows_ref, n_used_ref, xs_ref, zero_ref, zero_sem, True)

    d1 = route_t_ref[0, 0:1, :].astype(jnp.int32)
    d2 = route_t_ref[0, 1:2, :].astype(jnp.int32)
    slot = i % 2

    def sort_rows(n_rows):
        r = lax.broadcasted_iota(jnp.int32, (n_rows, tm), 0)
        perm = jnp.where((r == d1) | (r == d2), 1.0, 0.0).astype(BF16)
        sorted_ref[slot, 0:n_rows] = _dot(perm, h2_ref[...]).astype(BF16)

    @pl.when(tile_rows_ref[i] <= SORT_ROWS_COMMON)
    def _():
        sort_rows(SORT_ROWS_COMMON)

    @pl.when(tile_rows_ref[i] > SORT_ROWS_COMMON)
    def _():
        sort_rows(SORT_ROWS)

    _segment_copies(i, seg_dst_ref, seg_rows_ref, tile_off_ref, xs_ref, sorted_ref.at[slot], sem.at[slot], True)

    @pl.when(i > 0)
    def _():
        _wait_rows(sorted_ref.at[1 - slot], tile_rows_ref[jnp.maximum(i - 1, 0)], sem.at[1 - slot])

    @pl.when(i == pl.num_programs(0) - 1)
    def _():
        _wait_rows(sorted_ref.at[slot], tile_rows_ref[i], sem.at[slot])
        _zero_unused_rows(tail_dst_ref, tail_rows_ref, n_used_ref, xs_ref, zero_ref, zero_sem, False)


def _sort_call(h2, route_t, sched, max_rows):
    n, d = h2.shape
    tm = ROUTE_ROWS
    return pl.pallas_call(
        _sort_kernel,
        grid_spec=pltpu.PrefetchScalarGridSpec(
            num_scalar_prefetch=7,
            grid=(n // tm,),
            in_specs=[pl.BlockSpec((tm, d), lambda i, *_: (i, 0)),
                      pl.BlockSpec((1, 8, tm), lambda i, *_: (i, 0, 0))],
            out_specs=pl.BlockSpec(memory_space=pl.ANY),
            scratch_shapes=[pltpu.VMEM((2, SORT_ROWS, d), BF16), pltpu.VMEM((EXPERT_ROWS, d), BF16),
                            pltpu.SemaphoreType.DMA((2,)), pltpu.SemaphoreType.DMA(())],
        ),
        out_shape=jax.ShapeDtypeStruct((max_rows, d), BF16),
        compiler_params=pltpu.CompilerParams(dimension_semantics=("arbitrary",), vmem_limit_bytes=VMEM_LIMIT_BYTES),
        name="moe_sort",
    )(sched["seg_dst"], sched["seg_rows"], sched["tile_off"], sched["tile_rows"], sched["tail_dst"],
      sched["tail_rows"], sched["n_used"], h2, route_t)


def _expert_kernel(tile_expert_ref, n_used_ref, xs_ref, wg_ref, wu_ref, wd_ref, ys_ref, wg_bf, wu_bf, wd_bf):
    t = pl.program_id(0)
    used = t < n_used_ref[0]

    @pl.when(used & ((t == 0) | (tile_expert_ref[t] != tile_expert_ref[jnp.maximum(t - 1, 0)])))
    def _():
        wg_bf[...] = wg_ref[0].astype(BF16)
        wu_bf[...] = wu_ref[0].astype(BF16)
        wd_bf[...] = wd_ref[0].astype(BF16)

    @pl.when(used)
    def _():
        xs = xs_ref[...]
        gate = _dot(xs, wg_bf[...])
        up = _dot(xs, wu_bf[...])
        hidden = (gate * jax.nn.sigmoid(gate) * up).astype(BF16)
        ys_ref[...] = _dot(hidden, wd_bf[...]).astype(BF16)


def _expert_call(xs, wg, wu, wd, sched):
    rows, d = xs.shape
    tr = EXPERT_ROWS
    blk = lambda t, te, nu: (jnp.minimum(t, nu[0] - 1), 0)
    wsel = lambda t, te, nu: (te[jnp.minimum(t, nu[0] - 1)], 0, 0)
    return pl.pallas_call(
        _expert_kernel,
        grid_spec=pltpu.PrefetchScalarGridSpec(
            num_scalar_prefetch=2,
            grid=(rows // tr,),
            in_specs=[pl.BlockSpec((tr, d), blk),
                      pl.BlockSpec((1, d, EXPERT_FF), wsel), pl.BlockSpec((1, d, EXPERT_FF), wsel),
                      pl.BlockSpec((1, EXPERT_FF, d), wsel)],
            out_specs=pl.BlockSpec((tr, d), blk),
            scratch_shapes=[pltpu.VMEM((d, EXPERT_FF), BF16), pltpu.VMEM((d, EXPERT_FF), BF16),
                            pltpu.VMEM((EXPERT_FF, d), BF16)],
        ),
        out_shape=jax.ShapeDtypeStruct((rows, d), BF16),
        input_output_aliases={2: 0},
        compiler_params=pltpu.CompilerParams(dimension_semantics=("arbitrary",), vmem_limit_bytes=VMEM_LIMIT_BYTES),
        name="moe_experts",
    )(sched["tile_expert"], sched["n_used"], xs, wg, wu, wd)


def _combine_kernel(seg_dst_ref, seg_rows_ref, tile_off_ref, tile_rows_ref, ys_ref, route_ref, x1_ref, o_ref,
                    buf_ref, sem):
    i = pl.program_id(0)
    tm = x1_ref.shape[0]
    slot = i % 2

    def fetch(tile, into):
        buf_ref[into] = jnp.zeros(buf_ref.shape[1:], BF16)
        _segment_copies(tile, seg_dst_ref, seg_rows_ref, tile_off_ref, ys_ref, buf_ref.at[into], sem.at[into], False)

    @pl.when(i == 0)
    def _():
        fetch(i, slot)

    @pl.when(i + 1 < pl.num_programs(0))
    def _():
        fetch(i + 1, 1 - slot)

    route = route_ref[...]
    d1 = route[:, 0:1].astype(jnp.int32)
    d2 = route[:, 1:2].astype(jnp.int32)
    w1 = route[:, 2:3]
    w2 = route[:, 3:4]
    _wait_rows(buf_ref.at[slot], tile_rows_ref[i], sem.at[slot])

    def combine_rows(n_rows):
        r = lax.broadcasted_iota(jnp.int32, (tm, n_rows), 1)
        weights = (jnp.where(r == d1, w1, 0.0) + jnp.where(r == d2, w2, 0.0)).astype(BF16)
        o_ref[...] = x1_ref[...] + _dot(weights, buf_ref[slot, 0:n_rows])

    @pl.when(tile_rows_ref[i] <= SORT_ROWS_COMMON)
    def _():
        combine_rows(SORT_ROWS_COMMON)

    @pl.when(tile_rows_ref[i] > SORT_ROWS_COMMON)
    def _():
        combine_rows(SORT_ROWS)


def _combine_call(ys, route, x1, sched):
    n, d = x1.shape
    tm = ROUTE_ROWS
    return pl.pallas_call(
        _combine_kernel,
        grid_spec=pltpu.PrefetchScalarGridSpec(
            num_scalar_prefetch=4,
            grid=(n // tm,),
            in_specs=[pl.BlockSpec(memory_space=pl.ANY),
                      pl.BlockSpec((tm, LANES), lambda i, *_: (i, 0)),
                      pl.BlockSpec((tm, d), lambda i, *_: (i, 0))],
            out_specs=pl.BlockSpec((tm, d), lambda i, *_: (i, 0)),
            scratch_shapes=[pltpu.VMEM((2, SORT_ROWS, d), BF16), pltpu.SemaphoreType.DMA((2,))],
        ),
        out_shape=jax.ShapeDtypeStruct((n, d), F32),
        compiler_params=pltpu.CompilerParams(dimension_semantics=("arbitrary",), vmem_limit_bytes=VMEM_LIMIT_BYTES),
        name="moe_combine",
    )(sched["seg_dst"], sched["seg_rows"], sched["tile_off"], sched["tile_rows"], ys, route, x1)


def _schedule_kernel(cnt_ref, seg_dst_ref, tile_off_ref, tile_rows_ref, misc_ref):
    hp = functools.partial(jnp.dot, preferred_element_type=F32, precision=lax.Precision.HIGHEST)
    cnt = cnt_ref[...]
    n_tiles = cnt.shape[0]
    tile_before = jnp.where(lax.broadcasted_iota(jnp.int32, (n_tiles, n_tiles), 1)
                            < lax.broadcasted_iota(jnp.int32, (n_tiles, n_tiles), 0), 1.0, 0.0)
    expert_before = jnp.where(lax.broadcasted_iota(jnp.int32, (LANES, LANES), 0)
                              < lax.broadcasted_iota(jnp.int32, (LANES, LANES), 1), 1.0, 0.0)
    expert_rows = jnp.sum(cnt, axis=0, keepdims=True)
    region = jnp.floor((expert_rows + (EXPERT_ROWS - 1)) * (1.0 / EXPERT_ROWS)) * EXPERT_ROWS
    region_start = hp(jnp.broadcast_to(region, (8, LANES)), expert_before)[0:1]
    seg_dst_ref[...] = (region_start + hp(tile_before, cnt)).astype(jnp.int32)
    tile_off_ref[...] = hp(cnt, expert_before).astype(jnp.int32)
    tile_rows_ref[...] = jnp.broadcast_to(jnp.sum(cnt, axis=-1, keepdims=True), cnt.shape).astype(jnp.int32)
    n_used = jnp.sum(region, axis=-1, keepdims=True) * (1.0 / EXPERT_ROWS)
    row = lax.broadcasted_iota(jnp.int32, (8, LANES), 0)
    misc = jnp.where(row == 0, region_start + expert_rows,
                     jnp.where(row == 1, region - expert_rows,
                               jnp.where(row == 2, region_start + region, n_used)))
    misc_ref[...] = misc.astype(jnp.int32)


def _moe_schedule(cnt, n_tokens):
    n_tiles = cnt.shape[0]
    table = jax.ShapeDtypeStruct((n_tiles, LANES), jnp.int32)
    seg_dst, tile_off, tile_rows, misc = pl.pallas_call(
        _schedule_kernel,
        out_shape=[table, table, table, jax.ShapeDtypeStruct((8, LANES), jnp.int32)],
        name="moe_schedule",
    )(cnt.reshape(n_tiles, LANES))
    max_rows = 2 * n_tokens + n_tiles * N_EXPERTS * (SEG_ALIGN - 1) + N_EXPERTS * (EXPERT_ROWS - 1)
    max_tiles = -(-max_rows // EXPERT_ROWS)
    tile_start = jnp.arange(max_tiles, dtype=jnp.int32) * EXPERT_ROWS
    region_end = misc[2, :N_EXPERTS]
    tile_expert = jnp.minimum(jnp.sum((region_end[None, :] <= tile_start[:, None]).astype(jnp.int32), axis=1),
                              N_EXPERTS - 1)
    flat = lambda a: a[:, :N_EXPERTS].reshape(-1)
    sched = {
        "seg_dst": flat(seg_dst),
        "seg_rows": flat(cnt.reshape(n_tiles, LANES).astype(jnp.int32)),
        "tile_off": flat(tile_off),
        "tile_rows": tile_rows[:, 0],
        "tail_dst": misc[0, :N_EXPERTS],
        "tail_rows": misc[1, :N_EXPERTS],
        "tile_expert": tile_expert,
        "n_used": misc[3, :1],
    }
    return sched, max_tiles * EXPERT_ROWS


def _rotary_tables(seq_len, rot_dim, period, first, gain, scale):
    half = rot_dim // 2
    pos = jnp.arange(seq_len, dtype=F32)
    inv = 1.0 / (ROPE_THETA ** (jnp.arange(0, rot_dim, 2, dtype=F32) / rot_dim))
    ang = pos[:, None] * inv[None, :]
    cos, sin = jnp.cos(ang), jnp.sin(ang)
    lane = jnp.arange(LANES)
    rel = (lane % period) - first
    active = (rel >= 0) & (rel < rot_dim)
    idx = jnp.clip(rel, 0, rot_dim - 1) % half
    sign = jnp.where(rel < half, -1.0, 1.0)
    partner = jnp.where(active, jnp.where(rel < half, lane + half, lane - half), lane)
    c = jnp.where(active[None, :], cos[:, idx], 1.0)
    s = jnp.where(active[None, :], sin[:, idx] * sign[None, :], 0.0)
    gain = gain.astype(F32)
    return (c * gain[None, :] * scale).astype(F32), (s * gain[partner][None, :] * scale).astype(F32)


def _head_pad(w, heads, width):
    r = w.shape[0]
    w = w.reshape(r, heads, width)
    return jnp.pad(w, ((0, 0), (0, 0), (0, LANES - width))).reshape(r, heads * LANES)


def _layer_params(l, seq_len, norm_mix, w_in, mla_q_latent_norm, w_mla_uq, mla_kv_latent_norm, w_mla_ukv,
                  mla_q_gain, mla_k_gain, diff_q_gain, diff_k_gain, w_mla_up, w_diff_up, w_out, norm_ffn,
                  w_router_group, b_router_group, w_router_expert, b_router_expert):
    d = w_in.shape[1]
    sizes = (MLA_Q_RANK, MLA_KV_RANK, MLA_ROPE, DIFF_QK_WIDTH, DIFF_QK_WIDTH, DIFF_V_WIDTH, d, d)
    offs = [0]
    for s in sizes:
        offs.append(offs[-1] + s)
    wi = w_in[l]
    seg = [wi[:, offs[k]:offs[k + 1]] for k in range(len(sizes))]
    row = lambda g: g.astype(F32)[None, :]
    p = {}
    p["gmix"] = row(norm_mix[l])
    p["wql"] = seg[0].astype(BF16)
    p["wkvl"] = seg[1].astype(BF16)
    p["wkr"] = jnp.pad(seg[2], ((0, 0), (MLA_NOPE, LANES - MLA_QK))).astype(BF16)
    p["wdk"] = seg[4].astype(BF16)
    p["wdqvt"] = jnp.concatenate([seg[3].T, seg[5].T], axis=0).astype(BF16)
    p["wgm"], p["wgd"] = seg[6].astype(BF16), seg[7].astype(BF16)
    p["gql"] = row(mla_q_latent_norm[l])
    p["wuqt"] = _head_pad(w_mla_uq[l], MLA_HEADS, MLA_QK).T.astype(BF16)
    p["gkvl"] = row(mla_kv_latent_norm[l])
    ukv = w_mla_ukv[l].reshape(MLA_KV_RANK, MLA_HEADS, MLA_NOPE + MLA_V)
    p["wuk"] = _head_pad(ukv[:, :, :MLA_NOPE].reshape(MLA_KV_RANK, -1), MLA_HEADS, MLA_NOPE).astype(BF16)
    p["wuvt"] = ukv[:, :, MLA_NOPE:].reshape(MLA_KV_RANK, -1).T.astype(BF16)
    gq = jnp.pad(mla_q_gain[l], (0, LANES - MLA_QK))
    gk = jnp.pad(mla_k_gain[l], (0, LANES - MLA_QK))
    nope = jnp.arange(LANES) < MLA_NOPE
    p["gkn"] = jnp.where(nope, gk, 0.0).astype(F32)[None, :]
    aq, bq = _rotary_tables(seq_len, MLA_ROPE, LANES, MLA_NOPE, gq, LOG2E * MLA_QK ** -0.5)
    p["aq"], p["bq"] = aq.T, bq.T
    ak, bk = _rotary_tables(seq_len, MLA_ROPE, LANES, MLA_NOPE, jnp.where(nope, 0.0, gk), 1.0)
    p["ak"], p["bk"] = ak, bk
    adq, bdq = _rotary_tables(seq_len, DIFF_ROPE, DIFF_HEAD_DIM, 0, jnp.tile(diff_q_gain[l], 2),
                              LOG2E * DIFF_HEAD_DIM ** -0.5)
    p["adq"], p["bdq"] = adq.T, bdq.T
    p["adk"], p["bdk"] = _rotary_tables(seq_len, DIFF_ROPE, DIFF_HEAD_DIM, 0, jnp.tile(diff_k_gain[l], 2), 1.0)
    p["wmu"] = w_mla_up[l].astype(BF16)
    p["wdu"] = w_diff_up[l].astype(BF16)
    p["wout"] = w_out[l].astype(BF16)
    p["gffn"] = row(norm_ffn[l])
    wr = jnp.concatenate([w_router_expert[l], w_router_group[l]], axis=1).astype(F32)
    wrt = jnp.pad(wr, ((0, 0), (0, LANES - wr.shape[1]))).T
    wrt_hi = wrt.astype(BF16)
    p["wrt"] = jnp.concatenate([wrt_hi, (wrt - wrt_hi.astype(F32)).astype(BF16)], axis=0)
    br = jnp.concatenate([b_router_expert[l], b_router_group[l]]).astype(F32)
    p["brt"] = jnp.broadcast_to(jnp.pad(br, (0, LANES - br.shape[0]))[:, None], (LANES, MERGE_ROWS))
    return p


def kernel(x, norm_mix, w_in, mla_q_latent_norm, w_mla_uq, mla_kv_latent_norm, w_mla_ukv, mla_q_gain, mla_k_gain, diff_q_gain, diff_k_gain, lambda_q1, lambda_k1, lambda_q2, lambda_k2, diff_subln, w_mla_up, w_diff_up, w_out, norm_ffn, w_router_group, b_router_group, w_router_expert, b_router_expert, w_expert_gate, w_expert_up, w_expert_down):
    batch, seq_len, d = x.shape
    x2 = x.reshape(batch * seq_len, d)
    row = lambda g: g.astype(F32)[None, :]
    for l in range(norm_mix.shape[0]):
        lam_init = 0.8 - 0.6 * math.exp(-0.3 * l)
        p = _layer_params(l, seq_len, norm_mix, w_in, mla_q_latent_norm, w_mla_uq, mla_kv_latent_norm, w_mla_ukv,
                          mla_q_gain, mla_k_gain, diff_q_gain, diff_k_gain, w_mla_up, w_diff_up, w_out, norm_ffn,
                          w_router_group, b_router_group, w_router_expert, b_router_expert)
        qmt, km, vtm, qdt, kd, vtd, sgm, sgd = _proj_call(x2, seq_len, p)
        om = _mla_call(qmt, km, vtm, batch, seq_len)
        od = _diff_call(qdt, kd, vtd, row(lambda_q1[l]), row(lambda_k1[l]), row(lambda_q2[l]), row(lambda_k2[l]),
                        diff_subln[l].astype(F32)[:, None], lam_init, batch, seq_len)
        x1, h2, route, route_t, cnt = _merge_call(x2, om, od, sgm, sgd, p)
        sched, max_rows = _moe_schedule(cnt, x2.shape[0])
        xs = _sort_call(h2, route_t, sched, max_rows)
        ys = _expert_call(xs, w_expert_gate[l], w_expert_up[l], w_expert_down[l], sched)
        x2 = _combine_call(ys, route, x1, sched)
    return x2.reshape(batch, seq_len, d)
```

```python
import functools
import math

import jax
import jax.numpy as jnp
from jax import lax
from jax.experimental import pallas as pl
from jax.experimental.pallas import tpu as pltpu

CHUNK = 64
ROPE_THETA = 500000.0
EPS = 1e-6

MLA_HEADS = 8
MLA_NOPE = 64
MLA_ROPE = 32
MLA_V = 64
MLA_QK = MLA_NOPE + MLA_ROPE
MLA_Q_RANK = 256
MLA_KV_RANK = 128

DIFF_HEADS = 4
DIFF_HEAD_DIM = 64
DIFF_V_DIM = 2 * DIFF_HEAD_DIM
DIFF_ROPE = DIFF_HEAD_DIM // 4
DIFF_QK_WIDTH = DIFF_HEADS * 2 * DIFF_HEAD_DIM
DIFF_V_WIDTH = DIFF_HEADS * DIFF_V_DIM

N_GROUPS = 4
EXPERTS_PER_GROUP = 8
N_EXPERTS = N_GROUPS * EXPERTS_PER_GROUP
EXPERT_FF = 256

LANES = 128
VMEM_LIMIT_BYTES = 48 * 1024 * 1024
DIFF_VMEM_LIMIT_BYTES = 56 * 1024 * 1024

PROJ_ROWS = 512
ATTN_Q_ROWS = 512
ATTN_K_ROWS = 256
MERGE_ROWS = 512
ROUTE_ROWS = MERGE_ROWS
SEG_ALIGN = 16
SORT_ROWS = 2 * ROUTE_ROWS + N_EXPERTS * SEG_ALIGN
SORT_ROWS_COMMON = 2 * ROUTE_ROWS + N_EXPERTS * SEG_ALIGN // 2
EXPERT_ROWS = 512
MLA_HEADS_PER_STEP = 4
DIFF_HEADS_PER_STEP = 4
LOG2E = 1.4426950408889634

BF16 = jnp.bfloat16
F32 = jnp.float32


def _dot(a, b):
    return jnp.dot(a, b, preferred_element_type=F32)


def _dot_nt(a, b):
    return lax.dot_general(a, b, (((1,), (1,)), ((), ())), preferred_element_type=F32)


def _rms(x, width):
    return x * lax.rsqrt(jnp.sum(x * x, axis=-1, keepdims=True) * (1.0 / width) + EPS)


def _rotary_partner(y, half):
    lane = lax.broadcasted_iota(jnp.int32, y.shape, 1)
    up = pltpu.roll(y, LANES - half, 1)
    down = pltpu.roll(y, half, 1)
    return jnp.where((lane // half) % 2 == 0, up, down)


def _swap_row_blocks(y, first, half, period):
    parts = []
    for base in range(0, y.shape[0], period):
        a = base + first
        parts += [y[base:a], y[a + half:a + 2 * half], y[a:a + half], y[a + 2 * half:base + period]]
    return jnp.concatenate([p for p in parts if p.shape[0]], axis=0)


def _store_k_tiles(o_ref, vt):
    tk = o_ref.shape[-1]
    for c in range(o_ref.shape[0]):
        o_ref[c] = vt[:, c * tk:(c + 1) * tk].astype(BF16)


def _proj_kernel(x_ref, gmix_ref, wql_ref, wkvl_ref, wkr_ref, wdk_ref, wdqvt_ref, wgm_ref, wgd_ref,
                 gql_ref, wuqt_ref, gkvl_ref, wuk_ref, wuvt_ref, gkn_ref,
                 aq_ref, bq_ref, adq_ref, bdq_ref, ak_ref, bk_ref, adk_ref, bdk_ref,
                 qmt_ref, km_ref, vtm_ref, qdt_ref, kd_ref, vtd_ref, sgm_ref, sgd_ref):
    x = x_ref[...]
    h = (_rms(x, x.shape[-1]) * gmix_ref[...]).astype(BF16)

    ql = (_rms(_dot(h, wql_ref[...]), MLA_Q_RANK) * gql_ref[...]).astype(BF16)
    qt = _dot_nt(wuqt_ref[...], ql)
    aq, bq = aq_ref[...], bq_ref[...]
    for hd in range(MLA_HEADS):
        rows = slice(hd * LANES, (hd + 1) * LANES)
        qh = qt[rows]
        r = lax.rsqrt(jnp.sum(qh * qh, axis=0, keepdims=True) * (1.0 / MLA_QK) + EPS)
        y = (qh * aq + _swap_row_blocks(qh, MLA_NOPE, MLA_ROPE // 2, LANES) * bq) * r
        qmt_ref[0, rows, :] = y.astype(BF16)

    kvl = (_rms(_dot(h, wkvl_ref[...]), MLA_KV_RANK) * gkvl_ref[...]).astype(BF16)
    kr = _dot(h, wkr_ref[...])
    kr_rot = kr * ak_ref[...] + _rotary_partner(kr, MLA_ROPE // 2) * bk_ref[...]
    kr_ss = jnp.sum(kr * kr, axis=-1, keepdims=True)
    kn = _dot(kvl, wuk_ref[...])
    _store_k_tiles(vtm_ref, _dot_nt(wuvt_ref[...], kvl))
    gkn = gkn_ref[...]
    for hd in range(MLA_HEADS):
        sl = slice(hd * LANES, (hd + 1) * LANES)
        knh = kn[:, sl]
        r = lax.rsqrt((jnp.sum(knh * knh, axis=-1, keepdims=True) + kr_ss) * (1.0 / MLA_QK) + EPS)
        km_ref[:, sl] = ((knh * gkn + kr_rot) * r).astype(BF16)

    qvt = _dot_nt(wdqvt_ref[...], h)
    _store_k_tiles(vtd_ref, qvt[DIFF_QK_WIDTH:])
    adq, bdq = adq_ref[...], bdq_ref[...]
    for hd in range(DIFF_HEADS):
        rows = slice(hd * LANES, (hd + 1) * LANES)
        qh = qvt[rows]
        t = qh * adq + _swap_row_blocks(qh, 0, DIFF_ROPE // 2, DIFF_HEAD_DIM) * bdq
        halves = []
        for f in range(2):
            part = qh[f * DIFF_HEAD_DIM:(f + 1) * DIFF_HEAD_DIM]
            r = lax.rsqrt(jnp.sum(part * part, axis=0, keepdims=True) * (1.0 / DIFF_HEAD_DIM) + EPS)
            halves.append(t[f * DIFF_HEAD_DIM:(f + 1) * DIFF_HEAD_DIM] * r)
        qdt_ref[0, rows, :] = jnp.concatenate(halves, axis=0).astype(BF16)

    kd = _dot(h, wdk_ref[...])
    adk, bdk = adk_ref[...], bdk_ref[...]
    for hd in range(DIFF_HEADS):
        sl = slice(hd * LANES, (hd + 1) * LANES)
        th = kd[:, sl]
        lane = lax.broadcasted_iota(jnp.int32, th.shape, 1)
        sq = th * th
        lo = jnp.sum(jnp.where(lane < DIFF_HEAD_DIM, sq, 0.0), axis=-1, keepdims=True)
        tot = jnp.sum(sq, axis=-1, keepdims=True)
        r = lax.rsqrt(jnp.where(lane < DIFF_HEAD_DIM, lo, tot - lo) * (1.0 / DIFF_HEAD_DIM) + EPS)
        kd_ref[:, sl] = ((th * adk + _rotary_partner(th, DIFF_ROPE // 2) * bdk) * r).astype(BF16)

    sgm_ref[...] = jax.nn.sigmoid(_dot(h, wgm_ref[...])).astype(BF16)
    sgd_ref[...] = jax.nn.sigmoid(_dot(h, wgd_ref[...])).astype(BF16)


def _proj_call(x2, seq_len, p):
    n, d = x2.shape
    tm = PROJ_ROWS
    pos_blocks = seq_len // tm
    row = lambda i: (i, 0)
    const = lambda i: (0, 0)
    weights = [p["gmix"], p["wql"], p["wkvl"], p["wkr"], p["wdk"], p["wdqvt"], p["wgm"], p["wgd"],
               p["gql"], p["wuqt"], p["gkvl"], p["wuk"], p["wuvt"], p["gkn"]]
    feature_major_tables = [p["aq"], p["bq"], p["adq"], p["bdq"]]
    token_major_tables = [p["ak"], p["bk"], p["adk"], p["bdk"]]
    in_specs = ([pl.BlockSpec((tm, d), row)]
                + [pl.BlockSpec(w.shape, const) for w in weights]
                + [pl.BlockSpec((LANES, tm), lambda i: (0, i % pos_blocks)) for _ in feature_major_tables]
                + [pl.BlockSpec((tm, LANES), lambda i: (i % pos_blocks, 0)) for _ in token_major_tables])
    tk = ATTN_K_ROWS
    k_tiles = lambda width: (pl.BlockSpec((tm // tk, width, tk), lambda i: (i, 0, 0)),
                             jax.ShapeDtypeStruct((n // tk, width, tk), BF16))
    token_major = lambda width: (pl.BlockSpec((tm, width), row), jax.ShapeDtypeStruct((n, width), BF16))
    assert tm == ATTN_Q_ROWS
    feature_major = lambda width: (pl.BlockSpec((1, width, tm), lambda i: (i, 0, 0)),
                                   jax.ShapeDtypeStruct((n // tm, width, tm), BF16))
    outs = [feature_major(MLA_HEADS * LANES), token_major(MLA_HEADS * LANES), k_tiles(MLA_HEADS * MLA_V),
            feature_major(DIFF_QK_WIDTH), token_major(DIFF_QK_WIDTH), k_tiles(DIFF_V_WIDTH),
            token_major(d), token_major(d)]
    return pl.pallas_call(
        _proj_kernel,
        grid=(n // tm,),
        in_specs=in_specs,
        out_specs=[o[0] for o in outs],
        out_shape=[o[1] for o in outs],
        compiler_params=pltpu.CompilerParams(dimension_semantics=("parallel",), vmem_limit_bytes=VMEM_LIMIT_BYTES),
        name="proj",
    )(x2, *weights, *feature_major_tables, *token_major_tables)


def _chunk_mask_t(tk, width):
    kc = lax.broadcasted_iota(jnp.int32, (tk, width), 0) // CHUNK
    qc = lax.broadcasted_iota(jnp.int32, (tk, width), 1) // CHUNK
    return kc <= qc


ONES_ROWS = 16


def _with_ones_rows(vt):
    return jnp.concatenate([vt, jnp.ones((ONES_ROWS, vt.shape[1]), vt.dtype)], axis=0)


def _softmax_step_t(st, vt_ones, m_ref, acc_ref, lo):
    m_prev = m_ref[:, lo:]
    m_new = jnp.maximum(m_prev, jnp.max(st, axis=0, keepdims=True))
    alpha = jnp.exp2(m_prev - m_new)
    pr = jnp.exp2(st - m_new)
    acc_ref[:, lo:] = alpha * acc_ref[:, lo:] + _dot(vt_ones, pr.astype(BF16))
    m_ref[:, lo:] = m_new


def _normalized(acc_ref, dv):
    acc = acc_ref[...]
    return acc[:dv] / acc[dv:dv + 1]


STATE_REFS = 4


def _attn_scratch(chains, dv, tq, tk):
    per_chain = [pltpu.VMEM((1, tq), F32), pltpu.VMEM((dv + ONES_ROWS, tq), F32),
                 pltpu.VMEM((tk, tq), F32), pltpu.VMEM((tk, tq), F32)]
    return per_chain * chains


def _flash_attention(n_q_tiles, scratch_refs, score_fn, value_fn, finalize_fn, tk, tq):
    ratio = tq // tk
    assert tq == ratio * tk and ratio % 2 == 0
    n_chains = len(scratch_refs) // STATE_REFS
    chains = [scratch_refs[STATE_REFS * c:STATE_REFS * (c + 1)] for c in range(n_chains)]

    def scores(i, t, slot, lo=0):
        for c, ch in enumerate(chains):
            ch[2 + slot][:, lo:] = score_fn(c, i, t, lo)

    def update(t, slot, diag=None):
        lo = 0 if diag is None else diag * tk
        for c, ch in enumerate(chains):
            st = ch[2 + slot][:, lo:]
            if diag is not None:
                st = jnp.where(_chunk_mask_t(tk, tq - lo), st, -jnp.inf)
            _softmax_step_t(st, _with_ones_rows(value_fn(c, t)), ch[0], ch[1], lo)

    scores(0, 0, 0)

    def query_tile(i, carry):
        for m_ref, acc_ref, _, _ in chains:
            m_ref[...] = jnp.full(m_ref.shape, -jnp.inf, F32)
            acc_ref[...] = jnp.zeros(acc_ref.shape, F32)

        def pair(p, c):
            t = 2 * p
            scores(i, t + 1, 1)
            update(t, 0)
            scores(i, t + 2, 0)
            update(t + 1, 1)
            return c

        lax.fori_loop(0, i * (ratio // 2), pair, 0)
        first_diag = ratio * i
        for d in range(ratio):
            if d + 1 < ratio:
                scores(i, first_diag + d + 1, (d + 1) % 2, lo=(d + 1) * tk)
            else:
                scores(jnp.minimum(i + 1, n_q_tiles - 1), 0, 0)
            update(first_diag + d, d % 2, diag=d)
        finalize_fn(i, [ch[1] for ch in chains])
        return carry

    lax.fori_loop(0, n_q_tiles, query_tile, 0)


def _mla_kernel(qt_ref, k_ref, vt_ref, o_ref, *scratch_refs):
    tq, tk = ATTN_Q_ROWS, ATTN_K_ROWS

    def score_fn(c, i, t, lo):
        rows = pl.ds(pl.multiple_of(t * tk, tk), tk)
        sl = slice(c * LANES, (c + 1) * LANES)
        return _dot(k_ref[rows, sl], qt_ref[i, sl, lo:])

    def value_fn(c, t):
        return vt_ref[t, c * MLA_V:(c + 1) * MLA_V, :]

    def finalize_fn(i, accs):
        ot = jnp.concatenate([_normalized(acc_ref, MLA_V) for acc_ref in accs], axis=0)
        o_ref[pl.ds(pl.multiple_of(i * tq, tq), tq), :] = ot.T.astype(BF16)

    _flash_attention(qt_ref.shape[0], scratch_refs, score_fn, value_fn, finalize_fn, tk, tq)


def _mla_call(qmt, km, vtm, batch, seq_len):
    n = km.shape[0]
    tq, tk, hps = ATTN_Q_ROWS, ATTN_K_ROWS, MLA_HEADS_PER_STEP
    return pl.pallas_call(
        _mla_kernel,
        grid=(batch, MLA_HEADS // hps),
        in_specs=[pl.BlockSpec((seq_len // tq, hps * LANES, tq), lambda b, h: (b, h, 0)),
                  pl.BlockSpec((seq_len, hps * LANES), lambda b, h: (b, h)),
                  pl.BlockSpec((seq_len // tk, hps * MLA_V, tk), lambda b, h: (b, h, 0))],
        out_specs=pl.BlockSpec((seq_len, hps * MLA_V), lambda b, h: (b, h)),
        out_shape=jax.ShapeDtypeStruct((n, MLA_HEADS * MLA_V), BF16),
        scratch_shapes=_attn_scratch(hps, MLA_V, tq, tk),
        compiler_params=pltpu.CompilerParams(dimension_semantics=("parallel", "parallel"),
                                             vmem_limit_bytes=VMEM_LIMIT_BYTES),
        name="mla_attn",
    )(qmt, km, vtm)


def _diff_kernel(lam_init, qt_ref, k_ref, vt_ref, lq1_ref, lk1_ref, lq2_ref, lk2_ref, subln_ref, o_ref,
                 *scratch_refs):
    tq, tk = ATTN_Q_ROWS, ATTN_K_ROWS
    hps = DIFF_HEADS_PER_STEP

    def score_fn(c, i, t, lo):
        rows = pl.ds(pl.multiple_of(t * tk, tk), tk)
        hd, f = c // 2, c % 2
        half = qt_ref[i, hd * LANES + f * DIFF_HEAD_DIM:hd * LANES + (f + 1) * DIFF_HEAD_DIM, lo:]
        zero = jnp.zeros_like(half)
        q = jnp.concatenate([half, zero] if f == 0 else [zero, half], axis=0)
        return _dot(k_ref[rows, hd * LANES:(hd + 1) * LANES], q)

    def value_fn(c, t):
        hd = c // 2
        return vt_ref[t, hd * DIFF_V_DIM:(hd + 1) * DIFF_V_DIM, :]

    lam = (jnp.exp(jnp.sum(lq1_ref[...] * lk1_ref[...], axis=-1, keepdims=True))
           - jnp.exp(jnp.sum(lq2_ref[...] * lk2_ref[...], axis=-1, keepdims=True)) + lam_init)
    subln = subln_ref[...] * (1.0 - lam_init)

    def finalize_fn(i, accs):
        heads = []
        for hd in range(hps):
            ot = _normalized(accs[2 * hd], DIFF_V_DIM) - lam * _normalized(accs[2 * hd + 1], DIFF_V_DIM)
            ot = ot * lax.rsqrt(jnp.sum(ot * ot, axis=0, keepdims=True) * (1.0 / DIFF_V_DIM) + EPS)
            heads.append(ot * subln)
        o_ref[pl.ds(pl.multiple_of(i * tq, tq), tq), :] = jnp.concatenate(heads, axis=0).T.astype(BF16)

    _flash_attention(qt_ref.shape[0], scratch_refs, score_fn, value_fn, finalize_fn, tk, tq)


def _diff_call(qdt, kd, vtd, lq1, lk1, lq2, lk2, subln_col, lam_init, batch, seq_len):
    n = kd.shape[0]
    tq, tk, hps = ATTN_Q_ROWS, ATTN_K_ROWS, DIFF_HEADS_PER_STEP
    small = lambda a: pl.BlockSpec(a.shape, lambda b, h: (0, 0))
    return pl.pallas_call(
        functools.partial(_diff_kernel, lam_init),
        grid=(batch, DIFF_HEADS // hps),
        in_specs=[pl.BlockSpec((seq_len // tq, hps * LANES, tq), lambda b, h: (b, h, 0)),
                  pl.BlockSpec((seq_len, hps * LANES), lambda b, h: (b, h)),
                  pl.BlockSpec((seq_len // tk, hps * DIFF_V_DIM, tk), lambda b, h: (b, h, 0)),
                  small(lq1), small(lk1), small(lq2), small(lk2), small(subln_col)],
        out_specs=pl.BlockSpec((seq_len, hps * LANES), lambda b, h: (b, h)),
        out_shape=jax.ShapeDtypeStruct((n, DIFF_V_WIDTH), BF16),
        scratch_shapes=_attn_scratch(2 * hps, DIFF_V_DIM, tq, tk),
        compiler_params=pltpu.CompilerParams(dimension_semantics=("parallel", "parallel"),
                                             vmem_limit_bytes=DIFF_VMEM_LIMIT_BYTES),
        name="diff_attn",
    )(qdt, kd, vtd, lq1, lk1, lq2, lk2, subln_col)


def _merge_kernel(x_ref, om_ref, od_ref, sgm_ref, sgd_ref, wmu_ref, wdu_ref, wout_ref, gffn_ref, wrt_ref,
                  brt_ref, x1_ref, h2_ref, route_ref, route_t_ref, cnt_ref):
    merged = (sgm_ref[...].astype(F32) * _dot(om_ref[...], wmu_ref[...])
              + sgd_ref[...].astype(F32) * _dot(od_ref[...], wdu_ref[...]))
    x1 = x_ref[...] + _dot(merged.astype(BF16), wout_ref[...])
    x1_ref[...] = x1
    h2 = _rms(x1, x1.shape[-1]) * gffn_ref[...]
    h2_hi = h2.astype(BF16)
    h2_ref[...] = h2_hi
    tm = h2.shape[0]

    h2_lo = (h2 - h2_hi.astype(F32)).astype(BF16)
    by_hi = _dot_nt(wrt_ref[...], h2_hi)
    logits = by_hi[:LANES] + by_hi[LANES:] + _dot_nt(wrt_ref[:LANES, :], h2_lo) + brt_ref[...]
    row = lax.broadcasted_iota(jnp.int32, logits.shape, 0)
    neg = -jnp.inf
    big = jnp.int32(1 << 20)

    def top(vals):
        mx = jnp.max(vals, axis=0, keepdims=True)
        idx = jnp.min(jnp.where(vals == mx, row, big), axis=0, keepdims=True)
        return mx, idx

    gl = jnp.where((row >= N_EXPERTS) & (row < N_EXPERTS + N_GROUPS), logits, neg)
    gmax, gidx = top(gl)
    pg_sel = 1.0 / jnp.sum(jnp.exp(gl - gmax), axis=0, keepdims=True)
    el = jnp.where((row < N_EXPERTS) & (row // EXPERTS_PER_GROUP == gidx - N_EXPERTS), logits, neg)
    m1, i1 = top(el)
    m2, i2 = top(jnp.where(row == i1, neg, el))
    e2 = jnp.exp(m2 - m1)
    w1 = pg_sel / (1.0 + e2)
    w2 = w1 * e2

    sel = jnp.where((row == i1) | (row == i2), 1.0, 0.0).astype(BF16)
    t_row = lax.broadcasted_iota(jnp.int32, (tm, tm), 0)
    t_col = lax.broadcasted_iota(jnp.int32, (tm, tm), 1)
    rank = _dot(sel, jnp.where(t_row < t_col, 1.0, 0.0).astype(BF16))
    cnt = _dot(sel, jnp.ones((tm, tm), BF16))
    seg = jnp.floor((cnt + (SEG_ALIGN - 1)) * (1.0 / SEG_ALIGN))
    e_row = lax.broadcasted_iota(jnp.int32, (LANES, LANES), 0)
    e_col = lax.broadcasted_iota(jnp.int32, (LANES, LANES), 1)
    off = _dot(jnp.where(e_col < e_row, 1.0, 0.0).astype(BF16), seg.astype(BF16)) * SEG_ALIGN
    dest = off + rank
    d1 = jnp.sum(jnp.where(row == i1, dest, 0.0), axis=0, keepdims=True)
    d2 = jnp.sum(jnp.where(row == i2, dest, 0.0), axis=0, keepdims=True)
    route_t = jnp.where(row == 0, d1, jnp.where(row == 1, d2, jnp.where(row == 2, w1, jnp.where(row == 3, w2, 0.0))))
    route_t_ref[0] = route_t[0:8]
    route_ref[...] = route_t.T
    cnt_ref[0] = (seg[:, :LANES] * SEG_ALIGN).T[0:1]


def _merge_call(x2, om, od, sgm, sgd, p):
    n, d = x2.shape
    tm = MERGE_ROWS
    row = lambda i: (i, 0)
    const = lambda i: (0, 0)
    weights = [p["wmu"], p["wdu"], p["wout"], p["gffn"], p["wrt"], p["brt"]]
    return pl.pallas_call(
        _merge_kernel,
        grid=(n // tm,),
        in_specs=([pl.BlockSpec((tm, a.shape[1]), row) for a in (x2, om, od, sgm, sgd)]
                  + [pl.BlockSpec(w.shape, const) for w in weights]),
        out_specs=[pl.BlockSpec((tm, d), row), pl.BlockSpec((tm, d), row), pl.BlockSpec((tm, LANES), row),
                   pl.BlockSpec((1, 8, tm), lambda i: (i, 0, 0)), pl.BlockSpec((1, 1, LANES), lambda i: (i, 0, 0))],
        out_shape=[jax.ShapeDtypeStruct((n, d), F32), jax.ShapeDtypeStruct((n, d), BF16),
                   jax.ShapeDtypeStruct((n, LANES), F32), jax.ShapeDtypeStruct((n // tm, 8, tm), F32),
                   jax.ShapeDtypeStruct((n // tm, 1, LANES), F32)],
        compiler_params=pltpu.CompilerParams(dimension_semantics=("parallel",), vmem_limit_bytes=VMEM_LIMIT_BYTES),
        name="merge_router",
    )(x2, om, od, sgm, sgd, *weights)


def _segment_copies(i, seg_dst_ref, seg_rows_ref, tile_off_ref, global_ref, tile_ref, sem, to_global):
    def body(e, carry):
        k = i * N_EXPERTS + e
        rows = pl.multiple_of(seg_rows_ref[k], SEG_ALIGN)

        @pl.when(rows > 0)
        def _():
            g = global_ref.at[pl.ds(pl.multiple_of(seg_dst_ref[k], SEG_ALIGN), rows)]
            t = tile_ref.at[pl.ds(pl.multiple_of(tile_off_ref[k], SEG_ALIGN), rows)]
            src, dst = (t, g) if to_global else (g, t)
            pltpu.make_async_copy(src, dst, sem).start()

        return carry

    lax.fori_loop(0, N_EXPERTS, body, 0)


def _wait_rows(tile_ref, rows, sem):
    @pl.when(rows > 0)
    def _():
        view = tile_ref.at[pl.ds(0, pl.multiple_of(rows, SEG_ALIGN))]
        pltpu.make_async_copy(view, view, sem).wait()


def _zero_unused_rows(tail_dst_ref, tail_rows_ref, n_used_ref, xs_ref, zero_ref, sem, start):
    n_tiles = xs_ref.shape[0] // EXPERT_ROWS
    if start:
        zero_ref[...] = jnp.zeros(zero_ref.shape, BF16)

    def tail(e, total):
        rows = pl.multiple_of(tail_rows_ref[e], SEG_ALIGN)
        if start:
            @pl.when(rows > 0)
            def _():
                dst = xs_ref.at[pl.ds(pl.multiple_of(tail_dst_ref[e], SEG_ALIGN), rows)]
                pltpu.make_async_copy(zero_ref.at[pl.ds(0, rows)], dst, sem).start()

        return total + rows

    total = lax.fori_loop(0, N_EXPERTS, tail, 0)
    if not start:
        _wait_rows(xs_ref, total + (n_tiles - n_used_ref[0]) * EXPERT_ROWS, sem)
        return

    def unused(t, carry):
        dst = xs_ref.at[pl.ds(pl.multiple_of(t * EXPERT_ROWS, EXPERT_ROWS), EXPERT_ROWS)]
        pltpu.make_async_copy(zero_ref, dst, sem).start()
        return carry

    lax.fori_loop(n_used_ref[0], n_tiles, unused, 0)


def _sort_kernel(seg_dst_ref, seg_rows_ref, tile_off_ref, tile_rows_ref, tail_dst_ref, tail_rows_ref, n_used_ref,
                 h2_ref, route_t_ref, xs_ref, sorted_ref, zero_ref, sem, zero_sem):
    i = pl.program_id(0)
    tm = h2_ref.shape[0]

    @pl.when(i == 0)
    def _():
        _zero_unused_rows(tail_dst_ref, tail_rows_ref, n_used_ref, xs_ref, zero_ref, zero_sem, True)

    d1 = route_t_ref[0, 0:1, :].astype(jnp.int32)
    d2 = route_t_ref[0, 1:2, :].astype(jnp.int32)
    slot = i % 2

    def sort_rows(n_rows):
        r = lax.broadcasted_iota(jnp.int32, (n_rows, tm), 0)
        perm = jnp.where((r == d1) | (r == d2), 1.0, 0.0).astype(BF16)
        sorted_ref[slot, 0:n_rows] = _dot(perm, h2_ref[...]).astype(BF16)

    @pl.when(tile_rows_ref[i] <= SORT_ROWS_COMMON)
    def _():
        sort_rows(SORT_ROWS_COMMON)

    @pl.when(tile_rows_ref[i] > SORT_ROWS_COMMON)
    def _():
        sort_rows(SORT_ROWS)

    _segment_copies(i, seg_dst_ref, seg_rows_ref, tile_off_ref, xs_ref, sorted_ref.at[slot], sem.at[slot], True)

    @pl.when(i > 0)
    def _():
        _wait_rows(sorted_ref.at[1 - slot], tile_rows_ref[jnp.maximum(i - 1, 0)], sem.at[1 - slot])

    @pl.when(i == pl.num_programs(0) - 1)
    def _():
        _wait_rows(sorted_ref.at[slot], tile_rows_ref[i], sem.at[slot])
        _zero_unused_rows(tail_dst_ref, tail_rows_ref, n_used_ref, xs_ref, zero_ref, zero_sem, False)


def _sort_call(h2, route_t, sched, max_rows):
    n, d = h2.shape
    tm = ROUTE_ROWS
    return pl.pallas_call(
        _sort_kernel,
        grid_spec=pltpu.PrefetchScalarGridSpec(
            num_scalar_prefetch=7,
            grid=(n // tm,),
            in_specs=[pl.BlockSpec((tm, d), lambda i, *_: (i, 0)),
                      pl.BlockSpec((1, 8, tm), lambda i, *_: (i, 0, 0))],
            out_specs=pl.BlockSpec(memory_space=pl.ANY),
            scratch_shapes=[pltpu.VMEM((2, SORT_ROWS, d), BF16), pltpu.VMEM((EXPERT_ROWS, d), BF16),
                            pltpu.SemaphoreType.DMA((2,)), pltpu.SemaphoreType.DMA(())],
        ),
        out_shape=jax.ShapeDtypeStruct((max_rows, d), BF16),
        compiler_params=pltpu.CompilerParams(dimension_semantics=("arbitrary",), vmem_limit_bytes=VMEM_LIMIT_BYTES),
        name="moe_sort",
    )(sched["seg_dst"], sched["seg_rows"], sched["tile_off"], sched["tile_rows"], sched["tail_dst"],
      sched["tail_rows"], sched["n_used"], h2, route_t)


def _expert_kernel(tile_expert_ref, n_used_ref, xs_ref, wg_ref, wu_ref, wd_ref, ys_ref, wg_bf, wu_bf, wd_bf):
    t = pl.program_id(0)
    used = t < n_used_ref[0]

    @pl.when(used & ((t == 0) | (tile_expert_ref[t] != tile_expert_ref[jnp.maximum(t - 1, 0)])))
    def _():
        wg_bf[...] = wg_ref[0].astype(BF16)
        wu_bf[...] = wu_ref[0].astype(BF16)
        wd_bf[...] = wd_ref[0].astype(BF16)

    @pl.when(used)
    def _():
        xs = xs_ref[...]
        gate = _dot(xs, wg_bf[...])
        up = _dot(xs, wu_bf[...])
        hidden = (gate * jax.nn.sigmoid(gate) * up).astype(BF16)
        ys_ref[...] = _dot(hidden, wd_bf[...]).astype(BF16)


def _expert_call(xs, wg, wu, wd, sched):
    rows, d = xs.shape
    tr = EXPERT_ROWS
    blk = lambda t, te, nu: (jnp.minimum(t, nu[0] - 1), 0)
    wsel = lambda t, te, nu: (te[jnp.minimum(t, nu[0] - 1)], 0, 0)
    return pl.pallas_call(
        _expert_kernel,
        grid_spec=pltpu.PrefetchScalarGridSpec(
            num_scalar_prefetch=2,
            grid=(rows // tr,),
            in_specs=[pl.BlockSpec((tr, d), blk),
                      pl.BlockSpec((1, d, EXPERT_FF), wsel), pl.BlockSpec((1, d, EXPERT_FF), wsel),
                      pl.BlockSpec((1, EXPERT_FF, d), wsel)],
            out_specs=pl.BlockSpec((tr, d), blk),
            scratch_shapes=[pltpu.VMEM((d, EXPERT_FF), BF16), pltpu.VMEM((d, EXPERT_FF), BF16),
                            pltpu.VMEM((EXPERT_FF, d), BF16)],
        ),
        out_shape=jax.ShapeDtypeStruct((rows, d), BF16),
        input_output_aliases={2: 0},
        compiler_params=pltpu.CompilerParams(dimension_semantics=("arbitrary",), vmem_limit_bytes=VMEM_LIMIT_BYTES),
        name="moe_experts",
    )(sched["tile_expert"], sched["n_used"], xs, wg, wu, wd)


def _combine_kernel(seg_dst_ref, seg_rows_ref, tile_off_ref, tile_rows_ref, ys_ref, route_ref, x1_ref, o_ref,
                    buf_ref, sem):
    i = pl.program_id(0)
    tm = x1_ref.shape[0]
    slot = i % 2

    def fetch(tile, into):
        buf_ref[into] = jnp.zeros(buf_ref.shape[1:], BF16)
        _segment_copies(tile, seg_dst_ref, seg_rows_ref, tile_off_ref, ys_ref, buf_ref.at[into], sem.at[into], False)

    @pl.when(i == 0)
    def _():
        fetch(i, slot)

    @pl.when(i + 1 < pl.num_programs(0))
    def _():
        fetch(i + 1, 1 - slot)

    route = route_ref[...]
    d1 = route[:, 0:1].astype(jnp.int32)
    d2 = route[:, 1:2].astype(jnp.int32)
    w1 = route[:, 2:3]
    w2 = route[:, 3:4]
    _wait_rows(buf_ref.at[slot], tile_rows_ref[i], sem.at[slot])

    def combine_rows(n_rows):
        r = lax.broadcasted_iota(jnp.int32, (tm, n_rows), 1)
        weights = (jnp.where(r == d1, w1, 0.0) + jnp.where(r == d2, w2, 0.0)).astype(BF16)
        o_ref[...] = x1_ref[...] + _dot(weights, buf_ref[slot, 0:n_rows])

    @pl.when(tile_rows_ref[i] <= SORT_ROWS_COMMON)
    def _():
        combine_rows(SORT_ROWS_COMMON)

    @pl.when(tile_rows_ref[i] > SORT_ROWS_COMMON)
    def _():
        combine_rows(SORT_ROWS)


def _combine_call(ys, route, x1, sched):
    n, d = x1.shape
    tm = ROUTE_ROWS
    return pl.pallas_call(
        _combine_kernel,
        grid_spec=pltpu.PrefetchScalarGridSpec(
            num_scalar_prefetch=4,
            grid=(n // tm,),
            in_specs=[pl.BlockSpec(memory_space=pl.ANY),
                      pl.BlockSpec((tm, LANES), lambda i, *_: (i, 0)),
                      pl.BlockSpec((tm, d), lambda i, *_: (i, 0))],
            out_specs=pl.BlockSpec((tm, d), lambda i, *_: (i, 0)),
            scratch_shapes=[pltpu.VMEM((2, SORT_ROWS, d), BF16), pltpu.SemaphoreType.DMA((2,))],
        ),
        out_shape=jax.ShapeDtypeStruct((n, d), F32),
        compiler_params=pltpu.CompilerParams(dimension_semantics=("arbitrary",), vmem_limit_bytes=VMEM_LIMIT_BYTES),
        name="moe_combine",
    )(sched["seg_dst"], sched["seg_rows"], sched["tile_off"], sched["tile_rows"], ys, route, x1)


def _schedule_kernel(cnt_ref, seg_dst_ref, tile_off_ref, tile_rows_ref, misc_ref):
    hp = functools.partial(jnp.dot, preferred_element_type=F32, precision=lax.Precision.HIGHEST)
    cnt = cnt_ref[...]
    n_tiles = cnt.shape[0]
    tile_before = jnp.where(lax.broadcasted_iota(jnp.int32, (n_tiles, n_tiles), 1)
                            < lax.broadcasted_iota(jnp.int32, (n_tiles, n_tiles), 0), 1.0, 0.0)
    expert_before = jnp.where(lax.broadcasted_iota(jnp.int32, (LANES, LANES), 0)
                              < lax.broadcasted_iota(jnp.int32, (LANES, LANES), 1), 1.0, 0.0)
    expert_rows = jnp.sum(cnt, axis=0, keepdims=True)
    region = jnp.floor((expert_rows + (EXPERT_ROWS - 1)) * (1.0 / EXPERT_ROWS)) * EXPERT_ROWS
    region_start = hp(jnp.broadcast_to(region, (8, LANES)), expert_before)[0:1]
    seg_dst_ref[...] = (region_start + hp(tile_before, cnt)).astype(jnp.int32)
    tile_off_ref[...] = hp(cnt, expert_before).astype(jnp.int32)
    tile_rows_ref[...] = jnp.broadcast_to(jnp.sum(cnt, axis=-1, keepdims=True), cnt.shape).astype(jnp.int32)
    n_used = jnp.sum(region, axis=-1, keepdims=True) * (1.0 / EXPERT_ROWS)
    row = lax.broadcasted_iota(jnp.int32, (8, LANES), 0)
    misc = jnp.where(row == 0, region_start + expert_rows,
                     jnp.where(row == 1, region - expert_rows,
                               jnp.where(row == 2, region_start + region, n_used)))
    misc_ref[...] = misc.astype(jnp.int32)


def _moe_schedule(cnt, n_tokens):
    n_tiles = cnt.shape[0]
    table = jax.ShapeDtypeStruct((n_tiles, LANES), jnp.int32)
    seg_dst, tile_off, tile_rows, misc = pl.pallas_call(
        _schedule_kernel,
        out_shape=[table, table, table, jax.ShapeDtypeStruct((8, LANES), jnp.int32)],
        name="moe_schedule",
    )(cnt.reshape(n_tiles, LANES))
    max_rows = 2 * n_tokens + n_tiles * N_EXPERTS * (SEG_ALIGN - 1) + N_EXPERTS * (EXPERT_ROWS - 1)
    max_tiles = -(-max_rows // EXPERT_ROWS)
    tile_start = jnp.arange(max_tiles, dtype=jnp.int32) * EXPERT_ROWS
    region_end = misc[2, :N_EXPERTS]
    tile_expert = jnp.minimum(jnp.sum((region_end[None, :] <= tile_start[:, None]).astype(jnp.int32), axis=1),
                              N_EXPERTS - 1)
    flat = lambda a: a[:, :N_EXPERTS].reshape(-1)
    sched = {
        "seg_dst": flat(seg_dst),
        "seg_rows": flat(cnt.reshape(n_tiles, LANES).astype(jnp.int32)),
        "tile_off": flat(tile_off),
        "tile_rows": tile_rows[:, 0],
        "tail_dst": misc[0, :N_EXPERTS],
        "tail_rows": misc[1, :N_EXPERTS],
        "tile_expert": tile_expert,
        "n_used": misc[3, :1],
    }
    return sched, max_tiles * EXPERT_ROWS


def _rotary_tables(seq_len, rot_dim, period, first, gain, scale):
    half = rot_dim // 2
    pos = jnp.arange(seq_len, dtype=F32)
    inv = 1.0 / (ROPE_THETA ** (jnp.arange(0, rot_dim, 2, dtype=F32) / rot_dim))
    ang = pos[:, None] * inv[None, :]
    cos, sin = jnp.cos(ang), jnp.sin(ang)
    lane = jnp.arange(LANES)
    rel = (lane % period) - first
    active = (rel >= 0) & (rel < rot_dim)
    idx = jnp.clip(rel, 0, rot_dim - 1) % half
    sign = jnp.where(rel < half, -1.0, 1.0)
    partner = jnp.where(active, jnp.where(rel < half, lane + half, lane - half), lane)
    c = jnp.where(active[None, :], cos[:, idx], 1.0)
    s = jnp.where(active[None, :], sin[:, idx] * sign[None, :], 0.0)
    gain = gain.astype(F32)
    return (c * gain[None, :] * scale).astype(F32), (s * gain[partner][None, :] * scale).astype(F32)


def _head_pad(w, heads, width):
    r = w.shape[0]
    w = w.reshape(r, heads, width)
    return jnp.pad(w, ((0, 0), (0, 0), (0, LANES - width))).reshape(r, heads * LANES)


def _layer_params(l, seq_len, norm_mix, w_in, mla_q_latent_norm, w_mla_uq, mla_kv_latent_norm, w_mla_ukv,
                  mla_q_gain, mla_k_gain, diff_q_gain, diff_k_gain, w_mla_up, w_diff_up, w_out, norm_ffn,
                  w_router_group, b_router_group, w_router_expert, b_router_expert):
    d = w_in.shape[1]
    sizes = (MLA_Q_RANK, MLA_KV_RANK, MLA_ROPE, DIFF_QK_WIDTH, DIFF_QK_WIDTH, DIFF_V_WIDTH, d, d)
    offs = [0]
    for s in sizes:
        offs.append(offs[-1] + s)
    wi = w_in[l]
    seg = [wi[:, offs[k]:offs[k + 1]] for k in range(len(sizes))]
    row = lambda g: g.astype(F32)[None, :]
    p = {}
    p["gmix"] = row(norm_mix[l])
    p["wql"] = seg[0].astype(BF16)
    p["wkvl"] = seg[1].astype(BF16)
    p["wkr"] = jnp.pad(seg[2], ((0, 0), (MLA_NOPE, LANES - MLA_QK))).astype(BF16)
    p["wdk"] = seg[4].astype(BF16)
    p["wdqvt"] = jnp.concatenate([seg[3].T, seg[5].T], axis=0).astype(BF16)
    p["wgm"], p["wgd"] = seg[6].astype(BF16), seg[7].astype(BF16)
    p["gql"] = row(mla_q_latent_norm[l])
    p["wuqt"] = _head_pad(w_mla_uq[l], MLA_HEADS, MLA_QK).T.astype(BF16)
    p["gkvl"] = row(mla_kv_latent_norm[l])
    ukv = w_mla_ukv[l].reshape(MLA_KV_RANK, MLA_HEADS, MLA_NOPE + MLA_V)
    p["wuk"] = _head_pad(ukv[:, :, :MLA_NOPE].reshape(MLA_KV_RANK, -1), MLA_HEADS, MLA_NOPE).astype(BF16)
    p["wuvt"] = ukv[:, :, MLA_NOPE:].reshape(MLA_KV_RANK, -1).T.astype(BF16)
    gq = jnp.pad(mla_q_gain[l], (0, LANES - MLA_QK))
    gk = jnp.pad(mla_k_gain[l], (0, LANES - MLA_QK))
    nope = jnp.arange(LANES) < MLA_NOPE
    p["gkn"] = jnp.where(nope, gk, 0.0).astype(F32)[None, :]
    aq, bq = _rotary_tables(seq_len, MLA_ROPE, LANES, MLA_NOPE, gq, LOG2E * MLA_QK ** -0.5)
    p["aq"], p["bq"] = aq.T, bq.T
    ak, bk = _rotary_tables(seq_len, MLA_ROPE, LANES, MLA_NOPE, jnp.where(nope, 0.0, gk), 1.0)
    p["ak"], p["bk"] = ak, bk
    adq, bdq = _rotary_tables(seq_len, DIFF_ROPE, DIFF_HEAD_DIM, 0, jnp.tile(diff_q_gain[l], 2),
                              LOG2E * DIFF_HEAD_DIM ** -0.5)
    p["adq"], p["bdq"] = adq.T, bdq.T
    p["adk"], p["bdk"] = _rotary_tables(seq_len, DIFF_ROPE, DIFF_HEAD_DIM, 0, jnp.tile(diff_k_gain[l], 2), 1.0)
    p["wmu"] = w_mla_up[l].astype(BF16)
    p["wdu"] = w_diff_up[l].astype(BF16)
    p["wout"] = w_out[l].astype(BF16)
    p["gffn"] = row(norm_ffn[l])
    wr = jnp.concatenate([w_router_expert[l], w_router_group[l]], axis=1).astype(F32)
    wrt = jnp.pad(wr, ((0, 0), (0, LANES - wr.shape[1]))).T
    wrt_hi = wrt.astype(BF16)
    p["wrt"] = jnp.concatenate([wrt_hi, (wrt - wrt_hi.astype(F32)).astype(BF16)], axis=0)
    br = jnp.concatenate([b_router_expert[l], b_router_group[l]]).astype(F32)
    p["brt"] = jnp.broadcast_to(jnp.pad(br, (0, LANES - br.shape[0]))[:, None], (LANES, MERGE_ROWS))
    return p


def kernel(x, norm_mix, w_in, mla_q_latent_norm, w_mla_uq, mla_kv_latent_norm, w_mla_ukv, mla_q_gain, mla_k_gain, diff_q_gain, diff_k_gain, lambda_q1, lambda_k1, lambda_q2, lambda_k2, diff_subln, w_mla_up, w_diff_up, w_out, norm_ffn, w_router_group, b_router_group, w_router_expert, b_router_expert, w_expert_gate, w_expert_up, w_expert_down):
    batch, seq_len, d = x.shape
    x2 = x.reshape(batch * seq_len, d)
    row = lambda g: g.astype(F32)[None, :]
    for l in range(norm_mix.shape[0]):
        lam_init = 0.8 - 0.6 * math.exp(-0.3 * l)
        p = _layer_params(l, seq_len, norm_mix, w_in, mla_q_latent_norm, w_mla_uq, mla_kv_latent_norm, w_mla_ukv,
                          mla_q_gain, mla_k_gain, diff_q_gain, diff_k_gain, w_mla_up, w_diff_up, w_out, norm_ffn,
                          w_router_group, b_router_group, w_router_expert, b_router_expert)
        qmt, km, vtm, qdt, kd, vtd, sgm, sgd = _proj_call(x2, seq_len, p)
        om = _mla_call(qmt, km, vtm, batch, seq_len)
        od = _diff_call(qdt, kd, vtd, row(lambda_q1[l]), row(lambda_k1[l]), row(lambda_q2[l]), row(lambda_k2[l]),
                        diff_subln[l].astype(F32)[:, None], lam_init, batch, seq_len)
        x1, h2, route, route_t, cnt = _merge_call(x2, om, od, sgm, sgd, p)
        sched, max_rows = _moe_schedule(cnt, x2.shape[0])
        xs = _sort_call(h2, route_t, sched, max_rows)
        ys = _expert_call(xs, w_expert_gate[l], w_expert_up[l], w_expert_down[l], sched)
        x2 = _combine_call(ys, route, x1, sched)
    return x2.reshape(batch, seq_len, d)
```

```python
import functools
import math

import jax
import jax.numpy as jnp
from jax import lax
from jax.experimental import pallas as pl
from jax.experimental.pallas import tpu as pltpu

CHUNK = 64
ROPE_THETA = 500000.0
EPS = 1e-6

MLA_HEADS = 8
MLA_NOPE = 64
MLA_ROPE = 32
MLA_V = 64
MLA_QK = MLA_NOPE + MLA_ROPE
MLA_Q_RANK = 256
MLA_KV_RANK = 128

DIFF_HEADS = 4
DIFF_HEAD_DIM = 64
DIFF_V_DIM = 2 * DIFF_HEAD_DIM
DIFF_ROPE = DIFF_HEAD_DIM // 4
DIFF_QK_WIDTH = DIFF_HEADS * 2 * DIFF_HEAD_DIM
DIFF_V_WIDTH = DIFF_HEADS * DIFF_V_DIM

N_GROUPS = 4
EXPERTS_PER_GROUP = 8
N_EXPERTS = N_GROUPS * EXPERTS_PER_GROUP
EXPERT_FF = 256

LANES = 128
VMEM_LIMIT_BYTES = 48 * 1024 * 1024
ATTN_VMEM_LIMIT_BYTES = 56 * 1024 * 1024

PROJ_ROWS = 512
ATTN_Q_ROWS = 512
ATTN_K_ROWS = 256
MERGE_ROWS = 512
ROUTE_ROWS = MERGE_ROWS
SEG_ALIGN = 16
SORT_ROWS = 2 * ROUTE_ROWS + N_EXPERTS * SEG_ALIGN
SORT_ROWS_COMMON = 2 * ROUTE_ROWS + N_EXPERTS * SEG_ALIGN // 2
EXPERT_ROWS = 512
MLA_HEADS_PER_STEP = 8
DIFF_HEADS_PER_STEP = 4
LOG2E = 1.4426950408889634

BF16 = jnp.bfloat16
F32 = jnp.float32


def _dot(a, b):
    return jnp.dot(a, b, preferred_element_type=F32)


def _dot_nt(a, b):
    return lax.dot_general(a, b, (((1,), (1,)), ((), ())), preferred_element_type=F32)


def _rms(x, width):
    return x * lax.rsqrt(jnp.sum(x * x, axis=-1, keepdims=True) * (1.0 / width) + EPS)


def _rotary_partner(y, half):
    lane = lax.broadcasted_iota(jnp.int32, y.shape, 1)
    up = pltpu.roll(y, LANES - half, 1)
    down = pltpu.roll(y, half, 1)
    return jnp.where((lane // half) % 2 == 0, up, down)


def _swap_row_blocks(y, first, half, period):
    parts = []
    for base in range(0, y.shape[0], period):
        a = base + first
        parts += [y[base:a], y[a + half:a + 2 * half], y[a:a + half], y[a + 2 * half:base + period]]
    return jnp.concatenate([p for p in parts if p.shape[0]], axis=0)


def _store_k_tiles(o_ref, vt):
    tk = o_ref.shape[-1]
    for c in range(o_ref.shape[0]):
        o_ref[c] = vt[:, c * tk:(c + 1) * tk].astype(BF16)


def _proj_kernel(x_ref, gmix_ref, wql_ref, wkvl_ref, wkr_ref, wdk_ref, wdqvt_ref, wgm_ref, wgd_ref,
                 gql_ref, wuqt_ref, gkvl_ref, wuk_ref, wuvt_ref, gkn_ref,
                 aq_ref, bq_ref, adq_ref, bdq_ref, ak_ref, bk_ref, adk_ref, bdk_ref,
                 qmt_ref, km_ref, vtm_ref, qdt_ref, kd_ref, vtd_ref, sgm_ref, sgd_ref):
    x = x_ref[...]
    h = (_rms(x, x.shape[-1]) * gmix_ref[...]).astype(BF16)

    ql = (_rms(_dot(h, wql_ref[...]), MLA_Q_RANK) * gql_ref[...]).astype(BF16)
    qt = _dot_nt(wuqt_ref[...], ql)
    aq, bq = aq_ref[...], bq_ref[...]
    for hd in range(MLA_HEADS):
        rows = slice(hd * LANES, (hd + 1) * LANES)
        qh = qt[rows]
        r = lax.rsqrt(jnp.sum(qh * qh, axis=0, keepdims=True) * (1.0 / MLA_QK) + EPS)
        y = (qh * aq + _swap_row_blocks(qh, MLA_NOPE, MLA_ROPE // 2, LANES) * bq) * r
        qmt_ref[0, rows, :] = y.astype(BF16)

    kvl = (_rms(_dot(h, wkvl_ref[...]), MLA_KV_RANK) * gkvl_ref[...]).astype(BF16)
    kr = _dot(h, wkr_ref[...])
    kr_rot = kr * ak_ref[...] + _rotary_partner(kr, MLA_ROPE // 2) * bk_ref[...]
    kr_ss = jnp.sum(kr * kr, axis=-1, keepdims=True)
    kn = _dot(kvl, wuk_ref[...])
    _store_k_tiles(vtm_ref, _dot_nt(wuvt_ref[...], kvl))
    gkn = gkn_ref[...]
    for hd in range(MLA_HEADS):
        sl = slice(hd * LANES, (hd + 1) * LANES)
        knh = kn[:, sl]
        r = lax.rsqrt((jnp.sum(knh * knh, axis=-1, keepdims=True) + kr_ss) * (1.0 / MLA_QK) + EPS)
        km_ref[:, sl] = ((knh * gkn + kr_rot) * r).astype(BF16)

    qvt = _dot_nt(wdqvt_ref[...], h)
    _store_k_tiles(vtd_ref, qvt[DIFF_QK_WIDTH:])
    adq, bdq = adq_ref[...], bdq_ref[...]
    for hd in range(DIFF_HEADS):
        rows = slice(hd * LANES, (hd + 1) * LANES)
        qh = qvt[rows]
        t = qh * adq + _swap_row_blocks(qh, 0, DIFF_ROPE // 2, DIFF_HEAD_DIM) * bdq
        halves = []
        for f in range(2):
            part = qh[f * DIFF_HEAD_DIM:(f + 1) * DIFF_HEAD_DIM]
            r = lax.rsqrt(jnp.sum(part * part, axis=0, keepdims=True) * (1.0 / DIFF_HEAD_DIM) + EPS)
            halves.append(t[f * DIFF_HEAD_DIM:(f + 1) * DIFF_HEAD_DIM] * r)
        qdt_ref[0, rows, :] = jnp.concatenate(halves, axis=0).astype(BF16)

    kd = _dot(h, wdk_ref[...])
    adk, bdk = adk_ref[...], bdk_ref[...]
    for hd in range(DIFF_HEADS):
        sl = slice(hd * LANES, (hd + 1) * LANES)
        th = kd[:, sl]
        lane = lax.broadcasted_iota(jnp.int32, th.shape, 1)
        sq = th * th
        lo = jnp.sum(jnp.where(lane < DIFF_HEAD_DIM, sq, 0.0), axis=-1, keepdims=True)
        tot = jnp.sum(sq, axis=-1, keepdims=True)
        r = lax.rsqrt(jnp.where(lane < DIFF_HEAD_DIM, lo, tot - lo) * (1.0 / DIFF_HEAD_DIM) + EPS)
        kd_ref[:, sl] = ((th * adk + _rotary_partner(th, DIFF_ROPE // 2) * bdk) * r).astype(BF16)

    sgm_ref[...] = jax.nn.sigmoid(_dot(h, wgm_ref[...])).astype(BF16)
    sgd_ref[...] = jax.nn.sigmoid(_dot(h, wgd_ref[...])).astype(BF16)


def _proj_call(x2, seq_len, p):
    n, d = x2.shape
    tm = PROJ_ROWS
    pos_blocks = seq_len // tm
    row = lambda i: (i, 0)
    const = lambda i: (0, 0)
    weights = [p["gmix"], p["wql"], p["wkvl"], p["wkr"], p["wdk"], p["wdqvt"], p["wgm"], p["wgd"],
               p["gql"], p["wuqt"], p["gkvl"], p["wuk"], p["wuvt"], p["gkn"]]
    feature_major_tables = [p["aq"], p["bq"], p["adq"], p["bdq"]]
    token_major_tables = [p["ak"], p["bk"], p["adk"], p["bdk"]]
    in_specs = ([pl.BlockSpec((tm, d), row)]
                + [pl.BlockSpec(w.shape, const) for w in weights]
                + [pl.BlockSpec((LANES, tm), lambda i: (0, i % pos_blocks)) for _ in feature_major_tables]
                + [pl.BlockSpec((tm, LANES), lambda i: (i % pos_blocks, 0)) for _ in token_major_tables])
    tk = ATTN_K_ROWS
    k_tiles = lambda width: (pl.BlockSpec((tm // tk, width, tk), lambda i: (i, 0, 0)),
                             jax.ShapeDtypeStruct((n // tk, width, tk), BF16))
    token_major = lambda width: (pl.BlockSpec((tm, width), row), jax.ShapeDtypeStruct((n, width), BF16))
    assert tm == ATTN_Q_ROWS
    feature_major = lambda width: (pl.BlockSpec((1, width, tm), lambda i: (i, 0, 0)),
                                   jax.ShapeDtypeStruct((n // tm, width, tm), BF16))
    outs = [feature_major(MLA_HEADS * LANES), token_major(MLA_HEADS * LANES), k_tiles(MLA_HEADS * MLA_V),
            feature_major(DIFF_QK_WIDTH), token_major(DIFF_QK_WIDTH), k_tiles(DIFF_V_WIDTH),
            token_major(d), token_major(d)]
    return pl.pallas_call(
        _proj_kernel,
        grid=(n // tm,),
        in_specs=in_specs,
        out_specs=[o[0] for o in outs],
        out_shape=[o[1] for o in outs],
        compiler_params=pltpu.CompilerParams(dimension_semantics=("parallel",), vmem_limit_bytes=VMEM_LIMIT_BYTES),
        name="proj",
    )(x2, *weights, *feature_major_tables, *token_major_tables)


def _chunk_mask_t(tk, width):
    kc = lax.broadcasted_iota(jnp.int32, (tk, width), 0) // CHUNK
    qc = lax.broadcasted_iota(jnp.int32, (tk, width), 1) // CHUNK
    return kc <= qc


ONES_ROWS = 16


def _with_ones_rows(vt):
    return jnp.concatenate([vt, jnp.ones((ONES_ROWS, vt.shape[1]), vt.dtype)], axis=0)


def _softmax_step_t(st, vt_ones, m_ref, acc_ref, lo):
    m_prev = m_ref[:, lo:]
    m_new = jnp.maximum(m_prev, jnp.max(st, axis=0, keepdims=True))
    alpha = jnp.exp2(m_prev - m_new)
    pr = jnp.exp2(st - m_new)
    acc_ref[:, lo:] = alpha * acc_ref[:, lo:] + _dot(vt_ones, pr.astype(BF16))
    m_ref[:, lo:] = m_new


def _normalized(acc_ref, dv):
    acc = acc_ref[...]
    return acc[:dv] / acc[dv:dv + 1]


STATE_REFS = 4


def _attn_scratch(chains, dv, tq, tk):
    per_chain = [pltpu.VMEM((1, tq), F32), pltpu.VMEM((dv + ONES_ROWS, tq), F32),
                 pltpu.VMEM((tk, tq), F32), pltpu.VMEM((tk, tq), F32)]
    return per_chain * chains


def _flash_attention(n_q_tiles, scratch_refs, score_fn, value_fn, finalize_fn, tk, tq):
    ratio = tq // tk
    assert tq == ratio * tk and ratio % 2 == 0
    n_chains = len(scratch_refs) // STATE_REFS
    chains = [scratch_refs[STATE_REFS * c:STATE_REFS * (c + 1)] for c in range(n_chains)]

    def scores(i, t, slot, lo=0):
        for c, ch in enumerate(chains):
            ch[2 + slot][:, lo:] = score_fn(c, i, t, lo)

    def update(t, slot, diag=None):
        lo = 0 if diag is None else diag * tk
        for c, ch in enumerate(chains):
            st = ch[2 + slot][:, lo:]
            if diag is not None:
                st = jnp.where(_chunk_mask_t(tk, tq - lo), st, -jnp.inf)
            _softmax_step_t(st, _with_ones_rows(value_fn(c, t)), ch[0], ch[1], lo)

    scores(0, 0, 0)

    def query_tile(i, carry):
        for m_ref, acc_ref, _, _ in chains:
            m_ref[...] = jnp.full(m_ref.shape, -jnp.inf, F32)
            acc_ref[...] = jnp.zeros(acc_ref.shape, F32)

        def pair(p, c):
            t = 2 * p
            scores(i, t + 1, 1)
            update(t, 0)
            scores(i, t + 2, 0)
            update(t + 1, 1)
            return c

        lax.fori_loop(0, i * (ratio // 2), pair, 0)
        first_diag = ratio * i
        for d in range(ratio):
            if d + 1 < ratio:
                scores(i, first_diag + d + 1, (d + 1) % 2, lo=(d + 1) * tk)
            else:
                scores(jnp.minimum(i + 1, n_q_tiles - 1), 0, 0)
            update(first_diag + d, d % 2, diag=d)
        finalize_fn(i, [ch[1] for ch in chains])
        return carry

    lax.fori_loop(0, n_q_tiles, query_tile, 0)


def _mla_kernel(qt_ref, k_ref, vt_ref, o_ref, *scratch_refs):
    tq, tk = ATTN_Q_ROWS, ATTN_K_ROWS

    def score_fn(c, i, t, lo):
        rows = pl.ds(pl.multiple_of(t * tk, tk), tk)
        sl = slice(c * LANES, (c + 1) * LANES)
        return _dot(k_ref[rows, sl], qt_ref[i, sl, lo:])

    def value_fn(c, t):
        return vt_ref[t, c * MLA_V:(c + 1) * MLA_V, :]

    def finalize_fn(i, accs):
        ot = jnp.concatenate([_normalized(acc_ref, MLA_V) for acc_ref in accs], axis=0)
        o_ref[pl.ds(pl.multiple_of(i * tq, tq), tq), :] = ot.T.astype(BF16)

    _flash_attention(qt_ref.shape[0], scratch_refs, score_fn, value_fn, finalize_fn, tk, tq)


def _mla_call(qmt, km, vtm, batch, seq_len):
    n = km.shape[0]
    tq, tk, hps = ATTN_Q_ROWS, ATTN_K_ROWS, MLA_HEADS_PER_STEP
    return pl.pallas_call(
        _mla_kernel,
        grid=(batch, MLA_HEADS // hps),
        in_specs=[pl.BlockSpec((seq_len // tq, hps * LANES, tq), lambda b, h: (b, h, 0),
                               pipeline_mode=pl.Buffered(1)),
                  pl.BlockSpec((seq_len, hps * LANES), lambda b, h: (b, h)),
                  pl.BlockSpec((seq_len // tk, hps * MLA_V, tk), lambda b, h: (b, h, 0))],
        out_specs=pl.BlockSpec((seq_len, hps * MLA_V), lambda b, h: (b, h)),
        out_shape=jax.ShapeDtypeStruct((n, MLA_HEADS * MLA_V), BF16),
        scratch_shapes=_attn_scratch(hps, MLA_V, tq, tk),
        compiler_params=pltpu.CompilerParams(dimension_semantics=("parallel", "parallel"),
                                             vmem_limit_bytes=ATTN_VMEM_LIMIT_BYTES),
        name="mla_attn",
    )(qmt, km, vtm)


def _diff_kernel(lam_init, qt_ref, k_ref, vt_ref, lq1_ref, lk1_ref, lq2_ref, lk2_ref, subln_ref, o_ref,
                 *scratch_refs):
    tq, tk = ATTN_Q_ROWS, ATTN_K_ROWS
    hps = DIFF_HEADS_PER_STEP

    def score_fn(c, i, t, lo):
        rows = pl.ds(pl.multiple_of(t * tk, tk), tk)
        hd, f = c // 2, c % 2
        half = qt_ref[i, hd * LANES + f * DIFF_HEAD_DIM:hd * LANES + (f + 1) * DIFF_HEAD_DIM, lo:]
        zero = jnp.zeros_like(half)
        q = jnp.concatenate([half, zero] if f == 0 else [zero, half], axis=0)
        return _dot(k_ref[rows, hd * LANES:(hd + 1) * LANES], q)

    def value_fn(c, t):
        hd = c // 2
        return vt_ref[t, hd * DIFF_V_DIM:(hd + 1) * DIFF_V_DIM, :]

    lam = (jnp.exp(jnp.sum(lq1_ref[...] * lk1_ref[...], axis=-1, keepdims=True))
           - jnp.exp(jnp.sum(lq2_ref[...] * lk2_ref[...], axis=-1, keepdims=True)) + lam_init)
    subln = subln_ref[...] * (1.0 - lam_init)

    def finalize_fn(i, accs):
        heads = []
        for hd in range(hps):
            ot = _normalized(accs[2 * hd], DIFF_V_DIM) - lam * _normalized(accs[2 * hd + 1], DIFF_V_DIM)
            ot = ot * lax.rsqrt(jnp.sum(ot * ot, axis=0, keepdims=True) * (1.0 / DIFF_V_DIM) + EPS)
            heads.append(ot * subln)
        o_ref[pl.ds(pl.multiple_of(i * tq, tq), tq), :] = jnp.concatenate(heads, axis=0).T.astype(BF16)

    _flash_attention(qt_ref.shape[0], scratch_refs, score_fn, value_fn, finalize_fn, tk, tq)


def _diff_call(qdt, kd, vtd, lq1, lk1, lq2, lk2, subln_col, lam_init, batch, seq_len):
    n = kd.shape[0]
    tq, tk, hps = ATTN_Q_ROWS, ATTN_K_ROWS, DIFF_HEADS_PER_STEP
    small = lambda a: pl.BlockSpec(a.shape, lambda b, h: (0, 0))
    return pl.pallas_call(
        functools.partial(_diff_kernel, lam_init),
        grid=(batch, DIFF_HEADS // hps),
        in_specs=[pl.BlockSpec((seq_len // tq, hps * LANES, tq), lambda b, h: (b, h, 0)),
                  pl.BlockSpec((seq_len, hps * LANES), lambda b, h: (b, h)),
                  pl.BlockSpec((seq_len // tk, hps * DIFF_V_DIM, tk), lambda b, h: (b, h, 0)),
                  small(lq1), small(lk1), small(lq2), small(lk2), small(subln_col)],
        out_specs=pl.BlockSpec((seq_len, hps * LANES), lambda b, h: (b, h)),
        out_shape=jax.ShapeDtypeStruct((n, DIFF_V_WIDTH), BF16),
        scratch_shapes=_attn_scratch(2 * hps, DIFF_V_DIM, tq, tk),
        compiler_params=pltpu.CompilerParams(dimension_semantics=("parallel", "parallel"),
                                             vmem_limit_bytes=ATTN_VMEM_LIMIT_BYTES),
        name="diff_attn",
    )(qdt, kd, vtd, lq1, lk1, lq2, lk2, subln_col)


def _merge_kernel(x_ref, om_ref, od_ref, sgm_ref, sgd_ref, wmu_ref, wdu_ref, wout_ref, gffn_ref, wrt_ref,
                  brt_ref, x1_ref, h2_ref, route_ref, route_t_ref, cnt_ref):
    merged = (sgm_ref[...].astype(F32) * _dot(om_ref[...], wmu_ref[...])
              + sgd_ref[...].astype(F32) * _dot(od_ref[...], wdu_ref[...]))
    x1 = x_ref[...] + _dot(merged.astype(BF16), wout_ref[...])
    x1_ref[...] = x1
    h2 = _rms(x1, x1.shape[-1]) * gffn_ref[...]
    h2_hi = h2.astype(BF16)
    h2_ref[...] = h2_hi
    tm = h2.shape[0]

    h2_lo = (h2 - h2_hi.astype(F32)).astype(BF16)
    by_hi = _dot_nt(wrt_ref[...], h2_hi)
    logits = by_hi[:LANES] + by_hi[LANES:] + _dot_nt(wrt_ref[:LANES, :], h2_lo) + brt_ref[...]
    row = lax.broadcasted_iota(jnp.int32, logits.shape, 0)
    neg = -jnp.inf
    big = jnp.int32(1 << 20)

    def top(vals):
        mx = jnp.max(vals, axis=0, keepdims=True)
        idx = jnp.min(jnp.where(vals == mx, row, big), axis=0, keepdims=True)
        return mx, idx

    gl = jnp.where((row >= N_EXPERTS) & (row < N_EXPERTS + N_GROUPS), logits, neg)
    gmax, gidx = top(gl)
    pg_sel = 1.0 / jnp.sum(jnp.exp(gl - gmax), axis=0, keepdims=True)
    el = jnp.where((row < N_EXPERTS) & (row // EXPERTS_PER_GROUP == gidx - N_EXPERTS), logits, neg)
    m1, i1 = top(el)
    m2, i2 = top(jnp.where(row == i1, neg, el))
    e2 = jnp.exp(m2 - m1)
    w1 = pg_sel / (1.0 + e2)
    w2 = w1 * e2

    sel = jnp.where((row == i1) | (row == i2), 1.0, 0.0).astype(BF16)
    t_row = lax.broadcasted_iota(jnp.int32, (tm, tm), 0)
    t_col = lax.broadcasted_iota(jnp.int32, (tm, tm), 1)
    rank = _dot(sel, jnp.where(t_row < t_col, 1.0, 0.0).astype(BF16))
    cnt = _dot(sel, jnp.ones((tm, tm), BF16))
    seg = jnp.floor((cnt + (SEG_ALIGN - 1)) * (1.0 / SEG_ALIGN))
    e_row = lax.broadcasted_iota(jnp.int32, (LANES, LANES), 0)
    e_col = lax.broadcasted_iota(jnp.int32, (LANES, LANES), 1)
    off = _dot(jnp.where(e_col < e_row, 1.0, 0.0).astype(BF16), seg.astype(BF16)) * SEG_ALIGN
    dest = off + rank
    d1 = jnp.sum(jnp.where(row == i1, dest, 0.0), axis=0, keepdims=True)
    d2 = jnp.sum(jnp.where(row == i2, dest, 0.0), axis=0, keepdims=True)
    route_t = jnp.where(row == 0, d1, jnp.where(row == 1, d2, jnp.where(row == 2, w1, jnp.where(row == 3, w2, 0.0))))
    route_t_ref[0] = route_t[0:8]
    route_ref[...] = route_t.T
    cnt_ref[0] = (seg[:, :LANES] * SEG_ALIGN).T[0:1]


def _merge_call(x2, om, od, sgm, sgd, p):
    n, d = x2.shape
    tm = MERGE_ROWS
    row = lambda i: (i, 0)
    const = lambda i: (0, 0)
    weights = [p["wmu"], p["wdu"], p["wout"], p["gffn"], p["wrt"], p["brt"]]
    return pl.pallas_call(
        _merge_kernel,
        grid=(n // tm,),
        in_specs=([pl.BlockSpec((tm, a.shape[1]), row) for a in (x2, om, od, sgm, sgd)]
                  + [pl.BlockSpec(w.shape, const) for w in weights]),
        out_specs=[pl.BlockSpec((tm, d), row), pl.BlockSpec((tm, d), row), pl.BlockSpec((tm, LANES), row),
                   pl.BlockSpec((1, 8, tm), lambda i: (i, 0, 0)), pl.BlockSpec((1, 1, LANES), lambda i: (i, 0, 0))],
        out_shape=[jax.ShapeDtypeStruct((n, d), F32), jax.ShapeDtypeStruct((n, d), BF16),
                   jax.ShapeDtypeStruct((n, LANES), F32), jax.ShapeDtypeStruct((n // tm, 8, tm), F32),
                   jax.ShapeDtypeStruct((n // tm, 1, LANES), F32)],
        compiler_params=pltpu.CompilerParams(dimension_semantics=("parallel",), vmem_limit_bytes=VMEM_LIMIT_BYTES),
        name="merge_router",
    )(x2, om, od, sgm, sgd, *weights)


def _segment_copies(i, seg_dst_ref, seg_rows_ref, tile_off_ref, global_ref, tile_ref, sem, to_global):
    def body(e, carry):
        k = i * N_EXPERTS + e
        rows = pl.multiple_of(seg_rows_ref[k], SEG_ALIGN)

        @pl.when(rows > 0)
        def _():
            g = global_ref.at[pl.ds(pl.multiple_of(seg_dst_ref[k], SEG_ALIGN), rows)]
            t = tile_ref.at[pl.ds(pl.multiple_of(tile_off_ref[k], SEG_ALIGN), rows)]
            src, dst = (t, g) if to_global else (g, t)
            pltpu.make_async_copy(src, dst, sem).start()

        return carry

    lax.fori_loop(0, N_EXPERTS, body, 0)


def _wait_rows(tile_ref, rows, sem):
    @pl.when(rows > 0)
    def _():
        view = tile_ref.at[pl.ds(0, pl.multiple_of(rows, SEG_ALIGN))]
        pltpu.make_async_copy(view, view, sem).wait()


def _zero_unused_rows(tail_dst_ref, tail_rows_ref, n_used_ref, xs_ref, zero_ref, sem, start):
    n_tiles = xs_ref.shape[0] // EXPERT_ROWS
    if start:
        zero_ref[...] = jnp.zeros(zero_ref.shape, BF16)

    def tail(e, total):
        rows = pl.multiple_of(tail_rows_ref[e], SEG_ALIGN)
        if start:
            @pl.when(rows > 0)
            def _():
                dst = xs_ref.at[pl.ds(pl.multiple_of(tail_dst_ref[e], SEG_ALIGN), rows)]
                pltpu.make_async_copy(zero_ref.at[pl.ds(0, rows)], dst, sem).start()

        return total + rows

    total = lax.fori_loop(0, N_EXPERTS, tail, 0)
    if not start:
        _wait_rows(xs_ref, total + (n_tiles - n_used_ref[0]) * EXPERT_ROWS, sem)
        return

    def unused(t, carry):
        dst = xs_ref.at[pl.ds(pl.multiple_of(t * EXPERT_ROWS, EXPERT_ROWS), EXPERT_ROWS)]
        pltpu.make_async_copy(zero_ref, dst, sem).start()
        return carry

    lax.fori_loop(n_used_ref[0], n_tiles, unused, 0)


def _sort_kernel(seg_dst_ref, seg_rows_ref, tile_off_ref, tile_rows_ref, tail_dst_ref, tail_rows_ref, n_used_ref,
                 h2_ref, route_t_ref, xs_ref, sorted_ref, zero_ref, sem, zero_sem):
    i = pl.program_id(0)
    tm = h2_ref.shape[0]

    @pl.when(i == 0)
    def _():
        _zero_unused_rows(tail_dst_ref, tail_rows_ref, n_used_ref, xs_ref, zero_ref, zero_sem, True)

    d1 = route_t_ref[0, 0:1, :].astype(jnp.int32)
    d2 = route_t_ref[0, 1:2, :].astype(jnp.int32)
    slot = i % 2

    def sort_rows(n_rows):
        r = lax.broadcasted_iota(jnp.int32, (n_rows, tm), 0)
        perm = jnp.where((r == d1) | (r == d2), 1.0, 0.0).astype(BF16)
        sorted_ref[slot, 0:n_rows] = _dot(perm, h2_ref[...]).astype(BF16)

    @pl.when(tile_rows_ref[i] <= SORT_ROWS_COMMON)
    def _():
        sort_rows(SORT_ROWS_COMMON)

    @pl.when(tile_rows_ref[i] > SORT_ROWS_COMMON)
    def _():
        sort_rows(SORT_ROWS)

    _segment_copies(i, seg_dst_ref, seg_rows_ref, tile_off_ref, xs_ref, sorted_ref.at[slot], sem.at[slot], True)

    @pl.when(i > 0)
    def _():
        _wait_rows(sorted_ref.at[1 - slot], tile_rows_ref[jnp.maximum(i - 1, 0)], sem.at[1 - slot])

    @pl.when(i == pl.num_programs(0) - 1)
    def _():
        _wait_rows(sorted_ref.at[slot], tile_rows_ref[i], sem.at[slot])
        _zero_unused_rows(tail_dst_ref, tail_rows_ref, n_used_ref, xs_ref, zero_ref, zero_sem, False)


def _sort_call(h2, route_t, sched, max_rows):
    n, d = h2.shape
    tm = ROUTE_ROWS
    return pl.pallas_call(
        _sort_kernel,
        grid_spec=pltpu.PrefetchScalarGridSpec(
            num_scalar_prefetch=7,
            grid=(n // tm,),
            in_specs=[pl.BlockSpec((tm, d), lambda i, *_: (i, 0)),
                      pl.BlockSpec((1, 8, tm), lambda i, *_: (i, 0, 0))],
            out_specs=pl.BlockSpec(memory_space=pl.ANY),
            scratch_shapes=[pltpu.VMEM((2, SORT_ROWS, d), BF16), pltpu.VMEM((EXPERT_ROWS, d), BF16),
                            pltpu.SemaphoreType.DMA((2,)), pltpu.SemaphoreType.DMA(())],
        ),
        out_shape=jax.ShapeDtypeStruct((max_rows, d), BF16),
        compiler_params=pltpu.CompilerParams(dimension_semantics=("arbitrary",), vmem_limit_bytes=VMEM_LIMIT_BYTES),
        name="moe_sort",
    )(sched["seg_dst"], sched["seg_rows"], sched["tile_off"], sched["tile_rows"], sched["tail_dst"],
      sched["tail_rows"], sched["n_used"], h2, route_t)


def _expert_kernel(tile_expert_ref, n_used_ref, xs_ref, wg_ref, wu_ref, wd_ref, ys_ref, wg_bf, wu_bf, wd_bf):
    t = pl.program_id(0)
    used = t < n_used_ref[0]

    @pl.when(used & ((t == 0) | (tile_expert_ref[t] != tile_expert_ref[jnp.maximum(t - 1, 0)])))
    def _():
        wg_bf[...] = wg_ref[0].astype(BF16)
        wu_bf[...] = wu_ref[0].astype(BF16)
        wd_bf[...] = wd_ref[0].astype(BF16)

    @pl.when(used)
    def _():
        xs = xs_ref[...]
        gate = _dot(xs, wg_bf[...])
        up = _dot(xs, wu_bf[...])
        hidden = (gate * jax.nn.sigmoid(gate) * up).astype(BF16)
        ys_ref[...] = _dot(hidden, wd_bf[...]).astype(BF16)


def _expert_call(xs, wg, wu, wd, sched):
    rows, d = xs.shape
    tr = EXPERT_ROWS
    blk = lambda t, te, nu: (jnp.minimum(t, nu[0] - 1), 0)
    wsel = lambda t, te, nu: (te[jnp.minimum(t, nu[0] - 1)], 0, 0)
    return pl.pallas_call(
        _expert_kernel,
        grid_spec=pltpu.PrefetchScalarGridSpec(
            num_scalar_prefetch=2,
            grid=(rows // tr,),
            in_specs=[pl.BlockSpec((tr, d), blk),
                      pl.BlockSpec((1, d, EXPERT_FF), wsel), pl.BlockSpec((1, d, EXPERT_FF), wsel),
                      pl.BlockSpec((1, EXPERT_FF, d), wsel)],
            out_specs=pl.BlockSpec((tr, d), blk),
            scratch_shapes=[pltpu.VMEM((d, EXPERT_FF), BF16), pltpu.VMEM((d, EXPERT_FF), BF16),
                            pltpu.VMEM((EXPERT_FF, d), BF16)],
        ),
        out_shape=jax.ShapeDtypeStruct((rows, d), BF16),
        input_output_aliases={2: 0},
        compiler_params=pltpu.CompilerParams(dimension_semantics=("arbitrary",), vmem_limit_bytes=VMEM_LIMIT_BYTES),
        name="moe_experts",
    )(sched["tile_expert"], sched["n_used"], xs, wg, wu, wd)


def _combine_kernel(seg_dst_ref, seg_rows_ref, tile_off_ref, tile_rows_ref, ys_ref, route_ref, x1_ref, o_ref,
                    buf_ref, sem):
    i = pl.program_id(0)
    tm = x1_ref.shape[0]
    slot = i % 2

    def fetch(tile, into):
        buf_ref[into] = jnp.zeros(buf_ref.shape[1:], BF16)
        _segment_copies(tile, seg_dst_ref, seg_rows_ref, tile_off_ref, ys_ref, buf_ref.at[into], sem.at[into], False)

    @pl.when(i == 0)
    def _():
        fetch(i, slot)

    @pl.when(i + 1 < pl.num_programs(0))
    def _():
        fetch(i + 1, 1 - slot)

    route = route_ref[...]
    d1 = route[:, 0:1].astype(jnp.int32)
    d2 = route[:, 1:2].astype(jnp.int32)
    w1 = route[:, 2:3]
    w2 = route[:, 3:4]
    _wait_rows(buf_ref.at[slot], tile_rows_ref[i], sem.at[slot])

    def combine_rows(n_rows):
        r = lax.broadcasted_iota(jnp.int32, (tm, n_rows), 1)
        weights = (jnp.where(r == d1, w1, 0.0) + jnp.where(r == d2, w2, 0.0)).astype(BF16)
        o_ref[...] = x1_ref[...] + _dot(weights, buf_ref[slot, 0:n_rows])

    @pl.when(tile_rows_ref[i] <= SORT_ROWS_COMMON)
    def _():
        combine_rows(SORT_ROWS_COMMON)

    @pl.when(tile_rows_ref[i] > SORT_ROWS_COMMON)
    def _():
        combine_rows(SORT_ROWS)


def _combine_call(ys, route, x1, sched):
    n, d = x1.shape
    tm = ROUTE_ROWS
    return pl.pallas_call(
        _combine_kernel,
        grid_spec=pltpu.PrefetchScalarGridSpec(
            num_scalar_prefetch=4,
            grid=(n // tm,),
            in_specs=[pl.BlockSpec(memory_space=pl.ANY),
                      pl.BlockSpec((tm, LANES), lambda i, *_: (i, 0)),
                      pl.BlockSpec((tm, d), lambda i, *_: (i, 0))],
            out_specs=pl.BlockSpec((tm, d), lambda i, *_: (i, 0)),
            scratch_shapes=[pltpu.VMEM((2, SORT_ROWS, d), BF16), pltpu.SemaphoreType.DMA((2,))],
        ),
        out_shape=jax.ShapeDtypeStruct((n, d), F32),
        compiler_params=pltpu.CompilerParams(dimension_semantics=("arbitrary",), vmem_limit_bytes=VMEM_LIMIT_BYTES),
        name="moe_combine",
    )(sched["seg_dst"], sched["seg_rows"], sched["tile_off"], sched["tile_rows"], ys, route, x1)


def _schedule_kernel(cnt_ref, seg_dst_ref, tile_off_ref, tile_rows_ref, misc_ref):
    hp = functools.partial(jnp.dot, preferred_element_type=F32, precision=lax.Precision.HIGHEST)
    cnt = cnt_ref[...]
    n_tiles = cnt.shape[0]
    tile_before = jnp.where(lax.broadcasted_iota(jnp.int32, (n_tiles, n_tiles), 1)
                            < lax.broadcasted_iota(jnp.int32, (n_tiles, n_tiles), 0), 1.0, 0.0)
    expert_before = jnp.where(lax.broadcasted_iota(jnp.int32, (LANES, LANES), 0)
                              < lax.broadcasted_iota(jnp.int32, (LANES, LANES), 1), 1.0, 0.0)
    expert_rows = jnp.sum(cnt, axis=0, keepdims=True)
    region = jnp.floor((expert_rows + (EXPERT_ROWS - 1)) * (1.0 / EXPERT_ROWS)) * EXPERT_ROWS
    region_start = hp(jnp.broadcast_to(region, (8, LANES)), expert_before)[0:1]
    seg_dst_ref[...] = (region_start + hp(tile_before, cnt)).astype(jnp.int32)
    tile_off_ref[...] = hp(cnt, expert_before).astype(jnp.int32)
    tile_rows_ref[...] = jnp.broadcast_to(jnp.sum(cnt, axis=-1, keepdims=True), cnt.shape).astype(jnp.int32)
    n_used = jnp.sum(region, axis=-1, keepdims=True) * (1.0 / EXPERT_ROWS)
    row = lax.broadcasted_iota(jnp.int32, (8, LANES), 0)
    misc = jnp.where(row == 0, region_start + expert_rows,
                     jnp.where(row == 1, region - expert_rows,
                               jnp.where(row == 2, region_start + region, n_used)))
    misc_ref[...] = misc.astype(jnp.int32)


def _moe_schedule(cnt, n_tokens):
    n_tiles = cnt.shape[0]
    table = jax.ShapeDtypeStruct((n_tiles, LANES), jnp.int32)
    seg_dst, tile_off, tile_rows, misc = pl.pallas_call(
        _schedule_kernel,
        out_shape=[table, table, table, jax.ShapeDtypeStruct((8, LANES), jnp.int32)],
        name="moe_schedule",
    )(cnt.reshape(n_tiles, LANES))
    max_rows = 2 * n_tokens + n_tiles * N_EXPERTS * (SEG_ALIGN - 1) + N_EXPERTS * (EXPERT_ROWS - 1)
    max_tiles = -(-max_rows // EXPERT_ROWS)
    tile_start = jnp.arange(max_tiles, dtype=jnp.int32) * EXPERT_ROWS
    region_end = misc[2, :N_EXPERTS]
    tile_expert = jnp.minimum(jnp.sum((region_end[None, :] <= tile_start[:, None]).astype(jnp.int32), axis=1),
                              N_EXPERTS - 1)
    flat = lambda a: a[:, :N_EXPERTS].reshape(-1)
    sched = {
        "seg_dst": flat(seg_dst),
        "seg_rows": flat(cnt.reshape(n_tiles, LANES).astype(jnp.int32)),
        "tile_off": flat(tile_off),
        "tile_rows": tile_rows[:, 0],
        "tail_dst": misc[0, :N_EXPERTS],
        "tail_rows": misc[1, :N_EXPERTS],
        "tile_expert": tile_expert,
        "n_used": misc[3, :1],
    }
    return sched, max_tiles * EXPERT_ROWS


def _rotary_tables(seq_len, rot_dim, period, first, gain, scale):
    half = rot_dim // 2
    pos = jnp.arange(seq_len, dtype=F32)
    inv = 1.0 / (ROPE_THETA ** (jnp.arange(0, rot_dim, 2, dtype=F32) / rot_dim))
    ang = pos[:, None] * inv[None, :]
    cos, sin = jnp.cos(ang), jnp.sin(ang)
    lane = jnp.arange(LANES)
    rel = (lane % period) - first
    active = (rel >= 0) & (rel < rot_dim)
    idx = jnp.clip(rel, 0, rot_dim - 1) % half
    sign = jnp.where(rel < half, -1.0, 1.0)
    partner = jnp.where(active, jnp.where(rel < half, lane + half, lane - half), lane)
    c = jnp.where(active[None, :], cos[:, idx], 1.0)
    s = jnp.where(active[None, :], sin[:, idx] * sign[None, :], 0.0)
    gain = gain.astype(F32)
    return (c * gain[None, :] * scale).astype(F32), (s * gain[partner][None, :] * scale).astype(F32)


def _head_pad(w, heads, width):
    r = w.shape[0]
    w = w.reshape(r, heads, width)
    return jnp.pad(w, ((0, 0), (0, 0), (0, LANES - width))).reshape(r, heads * LANES)


def _layer_params(l, seq_len, norm_mix, w_in, mla_q_latent_norm, w_mla_uq, mla_kv_latent_norm, w_mla_ukv,
                  mla_q_gain, mla_k_gain, diff_q_gain, diff_k_gain, w_mla_up, w_diff_up, w_out, norm_ffn,
                  w_router_group, b_router_group, w_router_expert, b_router_expert):
    d = w_in.shape[1]
    sizes = (MLA_Q_RANK, MLA_KV_RANK, MLA_ROPE, DIFF_QK_WIDTH, DIFF_QK_WIDTH, DIFF_V_WIDTH, d, d)
    offs = [0]
    for s in sizes:
        offs.append(offs[-1] + s)
    wi = w_in[l]
    seg = [wi[:, offs[k]:offs[k + 1]] for k in range(len(sizes))]
    row = lambda g: g.astype(F32)[None, :]
    p = {}
    p["gmix"] = row(norm_mix[l])
    p["wql"] = seg[0].astype(BF16)
    p["wkvl"] = seg[1].astype(BF16)
    p["wkr"] = jnp.pad(seg[2], ((0, 0), (MLA_NOPE, LANES - MLA_QK))).astype(BF16)
    p["wdk"] = seg[4].astype(BF16)
    p["wdqvt"] = jnp.concatenate([seg[3].T, seg[5].T], axis=0).astype(BF16)
    p["wgm"], p["wgd"] = seg[6].astype(BF16), seg[7].astype(BF16)
    p["gql"] = row(mla_q_latent_norm[l])
    p["wuqt"] = _head_pad(w_mla_uq[l], MLA_HEADS, MLA_QK).T.astype(BF16)
    p["gkvl"] = row(mla_kv_latent_norm[l])
    ukv = w_mla_ukv[l].reshape(MLA_KV_RANK, MLA_HEADS, MLA_NOPE + MLA_V)
    p["wuk"] = _head_pad(ukv[:, :, :MLA_NOPE].reshape(MLA_KV_RANK, -1), MLA_HEADS, MLA_NOPE).astype(BF16)
    p["wuvt"] = ukv[:, :, MLA_NOPE:].reshape(MLA_KV_RANK, -1).T.astype(BF16)
    gq = jnp.pad(mla_q_gain[l], (0, LANES - MLA_QK))
    gk = jnp.pad(mla_k_gain[l], (0, LANES - MLA_QK))
    nope = jnp.arange(LANES) < MLA_NOPE
    p["gkn"] = jnp.where(nope, gk, 0.0).astype(F32)[None, :]
    aq, bq = _rotary_tables(seq_len, MLA_ROPE, LANES, MLA_NOPE, gq, LOG2E * MLA_QK ** -0.5)
    p["aq"], p["bq"] = aq.T, bq.T
    ak, bk = _rotary_tables(seq_len, MLA_ROPE, LANES, MLA_NOPE, jnp.where(nope, 0.0, gk), 1.0)
    p["ak"], p["bk"] = ak, bk
    adq, bdq = _rotary_tables(seq_len, DIFF_ROPE, DIFF_HEAD_DIM, 0, jnp.tile(diff_q_gain[l], 2),
                              LOG2E * DIFF_HEAD_DIM ** -0.5)
    p["adq"], p["bdq"] = adq.T, bdq.T
    p["adk"], p["bdk"] = _rotary_tables(seq_len, DIFF_ROPE, DIFF_HEAD_DIM, 0, jnp.tile(diff_k_gain[l], 2), 1.0)
    p["wmu"] = w_mla_up[l].astype(BF16)
    p["wdu"] = w_diff_up[l].astype(BF16)
    p["wout"] = w_out[l].astype(BF16)
    p["gffn"] = row(norm_ffn[l])
    wr = jnp.concatenate([w_router_expert[l], w_router_group[l]], axis=1).astype(F32)
    wrt = jnp.pad(wr, ((0, 0), (0, LANES - wr.shape[1]))).T
    wrt_hi = wrt.astype(BF16)
    p["wrt"] = jnp.concatenate([wrt_hi, (wrt - wrt_hi.astype(F32)).astype(BF16)], axis=0)
    br = jnp.concatenate([b_router_expert[l], b_router_group[l]]).astype(F32)
    p["brt"] = jnp.broadcast_to(jnp.pad(br, (0, LANES - br.shape[0]))[:, None], (LANES, MERGE_ROWS))
    return p


def kernel(x, norm_mix, w_in, mla_q_latent_norm, w_mla_uq, mla_kv_latent_norm, w_mla_ukv, mla_q_gain, mla_k_gain, diff_q_gain, diff_k_gain, lambda_q1, lambda_k1, lambda_q2, lambda_k2, diff_subln, w_mla_up, w_diff_up, w_out, norm_ffn, w_router_group, b_router_group, w_router_expert, b_router_expert, w_expert_gate, w_expert_up, w_expert_down):
    batch, seq_len, d = x.shape
    x2 = x.reshape(batch * seq_len, d)
    row = lambda g: g.astype(F32)[None, :]
    for l in range(norm_mix.shape[0]):
        lam_init = 0.8 - 0.6 * math.exp(-0.3 * l)
        p = _layer_params(l, seq_len, norm_mix, w_in, mla_q_latent_norm, w_mla_uq, mla_kv_latent_norm, w_mla_ukv,
                          mla_q_gain, mla_k_gain, diff_q_gain, diff_k_gain, w_mla_up, w_diff_up, w_out, norm_ffn,
                          w_router_group, b_router_group, w_router_expert, b_router_expert)
        qmt, km, vtm, qdt, kd, vtd, sgm, sgd = _proj_call(x2, seq_len, p)
        om = _mla_call(qmt, km, vtm, batch, seq_len)
        od = _diff_call(qdt, kd, vtd, row(lambda_q1[l]), row(lambda_k1[l]), row(lambda_q2[l]), row(lambda_k2[l]),
                        diff_subln[l].astype(F32)[:, None], lam_init, batch, seq_len)
        x1, h2, route, route_t, cnt = _merge_call(x2, om, od, sgm, sgd, p)
        sched, max_rows = _moe_schedule(cnt, x2.shape[0])
        xs = _sort_call(h2, route_t, sched, max_rows)
        ys = _expert_call(xs, w_expert_gate[l], w_expert_up[l], w_expert_down[l], sched)
        x2 = _combine_call(ys, route, x1, sched)
    return x2.reshape(batch, seq_len, d)
```

```python
import functools
import math

import jax
import jax.numpy as jnp
from jax import lax
from jax.experimental import pallas as pl
from jax.experimental.pallas import tpu as pltpu

CHUNK = 64
ROPE_THETA = 500000.0
EPS = 1e-6

MLA_HEADS = 8
MLA_NOPE = 64
MLA_ROPE = 32
MLA_V = 64
MLA_QK = MLA_NOPE + MLA_ROPE
MLA_Q_RANK = 256
MLA_KV_RANK = 128

DIFF_HEADS = 4
DIFF_HEAD_DIM = 64
DIFF_V_DIM = 2 * DIFF_HEAD_DIM
DIFF_ROPE = DIFF_HEAD_DIM // 4
DIFF_QK_WIDTH = DIFF_HEADS * 2 * DIFF_HEAD_DIM
DIFF_V_WIDTH = DIFF_HEADS * DIFF_V_DIM

N_GROUPS = 4
EXPERTS_PER_GROUP = 8
N_EXPERTS = N_GROUPS * EXPERTS_PER_GROUP
EXPERT_FF = 256

LANES = 128
VMEM_LIMIT_BYTES = 48 * 1024 * 1024
ATTN_VMEM_LIMIT_BYTES = 56 * 1024 * 1024

PROJ_ROWS = 512
ATTN_Q_ROWS = 512
ATTN_K_ROWS = 256
MERGE_ROWS = 1024
ROUTE_ROWS = 512
SEG_ALIGN = 16
SORT_ROWS = 2 * ROUTE_ROWS + N_EXPERTS * SEG_ALIGN
SORT_ROWS_COMMON = 2 * ROUTE_ROWS + N_EXPERTS * SEG_ALIGN // 2
EXPERT_ROWS = 512
MLA_HEADS_PER_STEP = 8
DIFF_HEADS_PER_STEP = 4
LOG2E = 1.4426950408889634

BF16 = jnp.bfloat16
F32 = jnp.float32


def _dot(a, b):
    return jnp.dot(a, b, preferred_element_type=F32)


def _dot_nt(a, b):
    return lax.dot_general(a, b, (((1,), (1,)), ((), ())), preferred_element_type=F32)


def _rms(x, width):
    return x * lax.rsqrt(jnp.sum(x * x, axis=-1, keepdims=True) * (1.0 / width) + EPS)


def _rotary_partner(y, half):
    lane = lax.broadcasted_iota(jnp.int32, y.shape, 1)
    up = pltpu.roll(y, LANES - half, 1)
    down = pltpu.roll(y, half, 1)
    return jnp.where((lane // half) % 2 == 0, up, down)


def _swap_row_blocks(y, first, half, period):
    parts = []
    for base in range(0, y.shape[0], period):
        a = base + first
        parts += [y[base:a], y[a + half:a + 2 * half], y[a:a + half], y[a + 2 * half:base + period]]
    return jnp.concatenate([p for p in parts if p.shape[0]], axis=0)


def _store_k_tiles(o_ref, vt):
    tk = o_ref.shape[-1]
    for c in range(o_ref.shape[0]):
        o_ref[c] = vt[:, c * tk:(c + 1) * tk].astype(BF16)


def _proj_kernel(x_ref, gmix_ref, wql_ref, wkvl_ref, wkr_ref, wdk_ref, wdqvt_ref, wgm_ref, wgd_ref,
                 gql_ref, wuqt_ref, gkvl_ref, wuk_ref, wuvt_ref, gkn_ref,
                 aq_ref, bq_ref, adq_ref, bdq_ref, ak_ref, bk_ref, adk_ref, bdk_ref,
                 qmt_ref, km_ref, vtm_ref, qdt_ref, kd_ref, vtd_ref, sgm_ref, sgd_ref):
    x = x_ref[...]
    h = (_rms(x, x.shape[-1]) * gmix_ref[...]).astype(BF16)

    ql = (_rms(_dot(h, wql_ref[...]), MLA_Q_RANK) * gql_ref[...]).astype(BF16)
    qt = _dot_nt(wuqt_ref[...], ql)
    aq, bq = aq_ref[...], bq_ref[...]
    for hd in range(MLA_HEADS):
        rows = slice(hd * LANES, (hd + 1) * LANES)
        qh = qt[rows]
        r = lax.rsqrt(jnp.sum(qh * qh, axis=0, keepdims=True) * (1.0 / MLA_QK) + EPS)
        y = (qh * aq + _swap_row_blocks(qh, MLA_NOPE, MLA_ROPE // 2, LANES) * bq) * r
        qmt_ref[0, rows, :] = y.astype(BF16)

    kvl = (_rms(_dot(h, wkvl_ref[...]), MLA_KV_RANK) * gkvl_ref[...]).astype(BF16)
    kr = _dot(h, wkr_ref[...])
    kr_rot = kr * ak_ref[...] + _rotary_partner(kr, MLA_ROPE // 2) * bk_ref[...]
    kr_ss = jnp.sum(kr * kr, axis=-1, keepdims=True)
    kn = _dot(kvl, wuk_ref[...])
    _store_k_tiles(vtm_ref, _dot_nt(wuvt_ref[...], kvl))
    gkn = gkn_ref[...]
    for hd in range(MLA_HEADS):
        sl = slice(hd * LANES, (hd + 1) * LANES)
        knh = kn[:, sl]
        r = lax.rsqrt((jnp.sum(knh * knh, axis=-1, keepdims=True) + kr_ss) * (1.0 / MLA_QK) + EPS)
        km_ref[:, sl] = ((knh * gkn + kr_rot) * r).astype(BF16)

    qvt = _dot_nt(wdqvt_ref[...], h)
    _store_k_tiles(vtd_ref, qvt[DIFF_QK_WIDTH:])
    adq, bdq = adq_ref[...], bdq_ref[...]
    for hd in range(DIFF_HEADS):
        rows = slice(hd * LANES, (hd + 1) * LANES)
        qh = qvt[rows]
        t = qh * adq + _swap_row_blocks(qh, 0, DIFF_ROPE // 2, DIFF_HEAD_DIM) * bdq
        halves = []
        for f in range(2):
            part = qh[f * DIFF_HEAD_DIM:(f + 1) * DIFF_HEAD_DIM]
            r = lax.rsqrt(jnp.sum(part * part, axis=0, keepdims=True) * (1.0 / DIFF_HEAD_DIM) + EPS)
            halves.append(t[f * DIFF_HEAD_DIM:(f + 1) * DIFF_HEAD_DIM] * r)
        qdt_ref[0, rows, :] = jnp.concatenate(halves, axis=0).astype(BF16)

    kd = _dot(h, wdk_ref[...])
    adk, bdk = adk_ref[...], bdk_ref[...]
    for hd in range(DIFF_HEADS):
        sl = slice(hd * LANES, (hd + 1) * LANES)
        th = kd[:, sl]
        lane = lax.broadcasted_iota(jnp.int32, th.shape, 1)
        sq = th * th
        lo = jnp.sum(jnp.where(lane < DIFF_HEAD_DIM, sq, 0.0), axis=-1, keepdims=True)
        tot = jnp.sum(sq, axis=-1, keepdims=True)
        r = lax.rsqrt(jnp.where(lane < DIFF_HEAD_DIM, lo, tot - lo) * (1.0 / DIFF_HEAD_DIM) + EPS)
        kd_ref[:, sl] = ((th * adk + _rotary_partner(th, DIFF_ROPE // 2) * bdk) * r).astype(BF16)

    sgm_ref[...] = jax.nn.sigmoid(_dot(h, wgm_ref[...])).astype(BF16)
    sgd_ref[...] = jax.nn.sigmoid(_dot(h, wgd_ref[...])).astype(BF16)


def _proj_call(x2, seq_len, p):
    n, d = x2.shape
    tm = PROJ_ROWS
    pos_blocks = seq_len // tm
    row = lambda i: (i, 0)
    const = lambda i: (0, 0)
    weights = [p["gmix"], p["wql"], p["wkvl"], p["wkr"], p["wdk"], p["wdqvt"], p["wgm"], p["wgd"],
               p["gql"], p["wuqt"], p["gkvl"], p["wuk"], p["wuvt"], p["gkn"]]
    feature_major_tables = [p["aq"], p["bq"], p["adq"], p["bdq"]]
    token_major_tables = [p["ak"], p["bk"], p["adk"], p["bdk"]]
    in_specs = ([pl.BlockSpec((tm, d), row)]
                + [pl.BlockSpec(w.shape, const) for w in weights]
                + [pl.BlockSpec((LANES, tm), lambda i: (0, i % pos_blocks)) for _ in feature_major_tables]
                + [pl.BlockSpec((tm, LANES), lambda i: (i % pos_blocks, 0)) for _ in token_major_tables])
    tk = ATTN_K_ROWS
    k_tiles = lambda width: (pl.BlockSpec((tm // tk, width, tk), lambda i: (i, 0, 0)),
                             jax.ShapeDtypeStruct((n // tk, width, tk), BF16))
    token_major = lambda width: (pl.BlockSpec((tm, width), row), jax.ShapeDtypeStruct((n, width), BF16))
    assert tm == ATTN_Q_ROWS
    feature_major = lambda width: (pl.BlockSpec((1, width, tm), lambda i: (i, 0, 0)),
                                   jax.ShapeDtypeStruct((n // tm, width, tm), BF16))
    outs = [feature_major(MLA_HEADS * LANES), token_major(MLA_HEADS * LANES), k_tiles(MLA_HEADS * MLA_V),
            feature_major(DIFF_QK_WIDTH), token_major(DIFF_QK_WIDTH), k_tiles(DIFF_V_WIDTH),
            token_major(d), token_major(d)]
    return pl.pallas_call(
        _proj_kernel,
        grid=(n // tm,),
        in_specs=in_specs,
        out_specs=[o[0] for o in outs],
        out_shape=[o[1] for o in outs],
        compiler_params=pltpu.CompilerParams(dimension_semantics=("parallel",), vmem_limit_bytes=VMEM_LIMIT_BYTES),
        name="proj",
    )(x2, *weights, *feature_major_tables, *token_major_tables)


def _chunk_mask_t(tk, width):
    kc = lax.broadcasted_iota(jnp.int32, (tk, width), 0) // CHUNK
    qc = lax.broadcasted_iota(jnp.int32, (tk, width), 1) // CHUNK
    return kc <= qc


ONES_ROWS = 16


def _with_ones_rows(vt):
    return jnp.concatenate([vt, jnp.ones((ONES_ROWS, vt.shape[1]), vt.dtype)], axis=0)


def _softmax_step_t(st, vt_ones, m_ref, acc_ref, lo):
    m_prev = m_ref[:, lo:]
    m_new = jnp.maximum(m_prev, jnp.max(st, axis=0, keepdims=True))
    alpha = jnp.exp2(m_prev - m_new)
    pr = jnp.exp2(st - m_new)
    acc_ref[:, lo:] = alpha * acc_ref[:, lo:] + _dot(vt_ones, pr.astype(BF16))
    m_ref[:, lo:] = m_new


def _normalized(acc_ref, dv):
    acc = acc_ref[...]
    return acc[:dv] / acc[dv:dv + 1]


STATE_REFS = 4


def _attn_scratch(chains, dv, tq, tk):
    per_chain = [pltpu.VMEM((1, tq), F32), pltpu.VMEM((dv + ONES_ROWS, tq), F32),
                 pltpu.VMEM((tk, tq), F32), pltpu.VMEM((tk, tq), F32)]
    return per_chain * chains


def _flash_attention(n_q_tiles, scratch_refs, score_fn, value_fn, finalize_fn, tk, tq):
    ratio = tq // tk
    assert tq == ratio * tk and ratio % 2 == 0
    n_chains = len(scratch_refs) // STATE_REFS
    chains = [scratch_refs[STATE_REFS * c:STATE_REFS * (c + 1)] for c in range(n_chains)]

    def scores(i, t, slot, lo=0):
        for c, ch in enumerate(chains):
            ch[2 + slot][:, lo:] = score_fn(c, i, t, lo)

    def update(t, slot, diag=None):
        lo = 0 if diag is None else diag * tk
        for c, ch in enumerate(chains):
            st = ch[2 + slot][:, lo:]
            if diag is not None:
                st = jnp.where(_chunk_mask_t(tk, tq - lo), st, -jnp.inf)
            _softmax_step_t(st, _with_ones_rows(value_fn(c, t)), ch[0], ch[1], lo)

    scores(0, 0, 0)

    def query_tile(i, carry):
        for m_ref, acc_ref, _, _ in chains:
            m_ref[...] = jnp.full(m_ref.shape, -jnp.inf, F32)
            acc_ref[...] = jnp.zeros(acc_ref.shape, F32)

        def pair(p, c):
            t = 2 * p
            scores(i, t + 1, 1)
            update(t, 0)
            scores(i, t + 2, 0)
            update(t + 1, 1)
            return c

        lax.fori_loop(0, i * (ratio // 2), pair, 0)
        first_diag = ratio * i
        for d in range(ratio):
            if d + 1 < ratio:
                scores(i, first_diag + d + 1, (d + 1) % 2, lo=(d + 1) * tk)
            else:
                scores(jnp.minimum(i + 1, n_q_tiles - 1), 0, 0)
            update(first_diag + d, d % 2, diag=d)
        finalize_fn(i, [ch[1] for ch in chains])
        return carry

    lax.fori_loop(0, n_q_tiles, query_tile, 0)


def _mla_kernel(qt_ref, k_ref, vt_ref, o_ref, *scratch_refs):
    tq, tk = ATTN_Q_ROWS, ATTN_K_ROWS

    def score_fn(c, i, t, lo):
        rows = pl.ds(pl.multiple_of(t * tk, tk), tk)
        sl = slice(c * LANES, (c + 1) * LANES)
        return _dot(k_ref[rows, sl], qt_ref[i, sl, lo:])

    def value_fn(c, t):
        return vt_ref[t, c * MLA_V:(c + 1) * MLA_V, :]

    def finalize_fn(i, accs):
        ot = jnp.concatenate([_normalized(acc_ref, MLA_V) for acc_ref in accs], axis=0)
        o_ref[pl.ds(pl.multiple_of(i * tq, tq), tq), :] = ot.T.astype(BF16)

    _flash_attention(qt_ref.shape[0], scratch_refs, score_fn, value_fn, finalize_fn, tk, tq)


def _mla_call(qmt, km, vtm, batch, seq_len):
    n = km.shape[0]
    tq, tk, hps = ATTN_Q_ROWS, ATTN_K_ROWS, MLA_HEADS_PER_STEP
    return pl.pallas_call(
        _mla_kernel,
        grid=(batch, MLA_HEADS // hps),
        in_specs=[pl.BlockSpec((seq_len // tq, hps * LANES, tq), lambda b, h: (b, h, 0),
                               pipeline_mode=pl.Buffered(1)),
                  pl.BlockSpec((seq_len, hps * LANES), lambda b, h: (b, h)),
                  pl.BlockSpec((seq_len // tk, hps * MLA_V, tk), lambda b, h: (b, h, 0))],
        out_specs=pl.BlockSpec((seq_len, hps * MLA_V), lambda b, h: (b, h)),
        out_shape=jax.ShapeDtypeStruct((n, MLA_HEADS * MLA_V), BF16),
        scratch_shapes=_attn_scratch(hps, MLA_V, tq, tk),
        compiler_params=pltpu.CompilerParams(dimension_semantics=("parallel", "parallel"),
                                             vmem_limit_bytes=ATTN_VMEM_LIMIT_BYTES),
        name="mla_attn",
    )(qmt, km, vtm)


def _diff_kernel(lam_init, qt_ref, k_ref, vt_ref, lq1_ref, lk1_ref, lq2_ref, lk2_ref, subln_ref, o_ref,
                 *scratch_refs):
    tq, tk = ATTN_Q_ROWS, ATTN_K_ROWS
    hps = DIFF_HEADS_PER_STEP

    def score_fn(c, i, t, lo):
        rows = pl.ds(pl.multiple_of(t * tk, tk), tk)
        hd, f = c // 2, c % 2
        half = qt_ref[i, hd * LANES + f * DIFF_HEAD_DIM:hd * LANES + (f + 1) * DIFF_HEAD_DIM, lo:]
        zero = jnp.zeros_like(half)
        q = jnp.concatenate([half, zero] if f == 0 else [zero, half], axis=0)
        return _dot(k_ref[rows, hd * LANES:(hd + 1) * LANES], q)

    def value_fn(c, t):
        hd = c // 2
        return vt_ref[t, hd * DIFF_V_DIM:(hd + 1) * DIFF_V_DIM, :]

    lam = (jnp.exp(jnp.sum(lq1_ref[...] * lk1_ref[...], axis=-1, keepdims=True))
           - jnp.exp(jnp.sum(lq2_ref[...] * lk2_ref[...], axis=-1, keepdims=True)) + lam_init)
    subln = subln_ref[...] * (1.0 - lam_init)

    def finalize_fn(i, accs):
        heads = []
        for hd in range(hps):
            ot = _normalized(accs[2 * hd], DIFF_V_DIM) - lam * _normalized(accs[2 * hd + 1], DIFF_V_DIM)
            ot = ot * lax.rsqrt(jnp.sum(ot * ot, axis=0, keepdims=True) * (1.0 / DIFF_V_DIM) + EPS)
            heads.append(ot * subln)
        o_ref[pl.ds(pl.multiple_of(i * tq, tq), tq), :] = jnp.concatenate(heads, axis=0).T.astype(BF16)

    _flash_attention(qt_ref.shape[0], scratch_refs, score_fn, value_fn, finalize_fn, tk, tq)


def _diff_call(qdt, kd, vtd, lq1, lk1, lq2, lk2, subln_col, lam_init, batch, seq_len):
    n = kd.shape[0]
    tq, tk, hps = ATTN_Q_ROWS, ATTN_K_ROWS, DIFF_HEADS_PER_STEP
    small = lambda a: pl.BlockSpec(a.shape, lambda b, h: (0, 0))
    return pl.pallas_call(
        functools.partial(_diff_kernel, lam_init),
        grid=(batch, DIFF_HEADS // hps),
        in_specs=[pl.BlockSpec((seq_len // tq, hps * LANES, tq), lambda b, h: (b, h, 0)),
                  pl.BlockSpec((seq_len, hps * LANES), lambda b, h: (b, h)),
                  pl.BlockSpec((seq_len // tk, hps * DIFF_V_DIM, tk), lambda b, h: (b, h, 0)),
                  small(lq1), small(lk1), small(lq2), small(lk2), small(subln_col)],
        out_specs=pl.BlockSpec((seq_len, hps * LANES), lambda b, h: (b, h)),
        out_shape=jax.ShapeDtypeStruct((n, DIFF_V_WIDTH), BF16),
        scratch_shapes=_attn_scratch(2 * hps, DIFF_V_DIM, tq, tk),
        compiler_params=pltpu.CompilerParams(dimension_semantics=("parallel", "parallel"),
                                             vmem_limit_bytes=ATTN_VMEM_LIMIT_BYTES),
        name="diff_attn",
    )(qdt, kd, vtd, lq1, lk1, lq2, lk2, subln_col)


def _merge_kernel(x_ref, om_ref, od_ref, sgm_ref, sgd_ref, wmu_ref, wdu_ref, wout_ref, gffn_ref, wrt_ref,
                  brt_ref, x1_ref, h2_ref, route_ref, route_t_ref, cnt_ref):
    merged = (sgm_ref[...].astype(F32) * _dot(om_ref[...], wmu_ref[...])
              + sgd_ref[...].astype(F32) * _dot(od_ref[...], wdu_ref[...]))
    x1 = x_ref[...] + _dot(merged.astype(BF16), wout_ref[...])
    x1_ref[...] = x1
    h2 = _rms(x1, x1.shape[-1]) * gffn_ref[...]
    h2_hi = h2.astype(BF16)
    h2_ref[...] = h2_hi
    for half in range(h2.shape[0] // ROUTE_ROWS):
        rows = slice(half * ROUTE_ROWS, (half + 1) * ROUTE_ROWS)
        route_t, seg_rows = _route_tile(h2[rows], h2_hi[rows], wrt_ref, brt_ref)
        route_t_ref[half] = route_t[0:8]
        route_ref[rows, :] = route_t.T
        cnt_ref[half] = seg_rows


def _route_tile(h2, h2_hi, wrt_ref, brt_ref):
    tm = h2.shape[0]

    h2_lo = (h2 - h2_hi.astype(F32)).astype(BF16)
    by_hi = _dot_nt(wrt_ref[...], h2_hi)
    logits = by_hi[:LANES] + by_hi[LANES:] + _dot_nt(wrt_ref[:LANES, :], h2_lo) + brt_ref[...]
    row = lax.broadcasted_iota(jnp.int32, logits.shape, 0)
    neg = -jnp.inf
    big = jnp.int32(1 << 20)

    def top(vals):
        mx = jnp.max(vals, axis=0, keepdims=True)
        idx = jnp.min(jnp.where(vals == mx, row, big), axis=0, keepdims=True)
        return mx, idx

    gl = jnp.where((row >= N_EXPERTS) & (row < N_EXPERTS + N_GROUPS), logits, neg)
    gmax, gidx = top(gl)
    pg_sel = 1.0 / jnp.sum(jnp.exp(gl - gmax), axis=0, keepdims=True)
    el = jnp.where((row < N_EXPERTS) & (row // EXPERTS_PER_GROUP == gidx - N_EXPERTS), logits, neg)
    m1, i1 = top(el)
    m2, i2 = top(jnp.where(row == i1, neg, el))
    e2 = jnp.exp(m2 - m1)
    w1 = pg_sel / (1.0 + e2)
    w2 = w1 * e2

    sel = jnp.where((row == i1) | (row == i2), 1.0, 0.0).astype(BF16)
    t_row = lax.broadcasted_iota(jnp.int32, (tm, tm), 0)
    t_col = lax.broadcasted_iota(jnp.int32, (tm, tm), 1)
    rank = _dot(sel, jnp.where(t_row < t_col, 1.0, 0.0).astype(BF16))
    cnt = _dot(sel, jnp.ones((tm, tm), BF16))
    seg = jnp.floor((cnt + (SEG_ALIGN - 1)) * (1.0 / SEG_ALIGN))
    e_row = lax.broadcasted_iota(jnp.int32, (LANES, LANES), 0)
    e_col = lax.broadcasted_iota(jnp.int32, (LANES, LANES), 1)
    off = _dot(jnp.where(e_col < e_row, 1.0, 0.0).astype(BF16), seg.astype(BF16)) * SEG_ALIGN
    dest = off + rank
    d1 = jnp.sum(jnp.where(row == i1, dest, 0.0), axis=0, keepdims=True)
    d2 = jnp.sum(jnp.where(row == i2, dest, 0.0), axis=0, keepdims=True)
    route_t = jnp.where(row == 0, d1, jnp.where(row == 1, d2, jnp.where(row == 2, w1, jnp.where(row == 3, w2, 0.0))))
    return route_t, (seg[:, :LANES] * SEG_ALIGN).T[0:1]


def _merge_call(x2, om, od, sgm, sgd, p):
    n, d = x2.shape
    tm, tr = MERGE_ROWS, ROUTE_ROWS
    row = lambda i: (i, 0)
    const = lambda i: (0, 0)
    weights = [p["wmu"], p["wdu"], p["wout"], p["gffn"], p["wrt"], p["brt"]]
    return pl.pallas_call(
        _merge_kernel,
        grid=(n // tm,),
        in_specs=([pl.BlockSpec((tm, a.shape[1]), row) for a in (x2, om, od, sgm, sgd)]
                  + [pl.BlockSpec(w.shape, const) for w in weights]),
        out_specs=[pl.BlockSpec((tm, d), row), pl.BlockSpec((tm, d), row), pl.BlockSpec((tm, LANES), row),
                   pl.BlockSpec((tm // tr, 8, tr), lambda i: (i, 0, 0)),
                   pl.BlockSpec((tm // tr, 1, LANES), lambda i: (i, 0, 0))],
        out_shape=[jax.ShapeDtypeStruct((n, d), F32), jax.ShapeDtypeStruct((n, d), BF16),
                   jax.ShapeDtypeStruct((n, LANES), F32), jax.ShapeDtypeStruct((n // tr, 8, tr), F32),
                   jax.ShapeDtypeStruct((n // tr, 1, LANES), F32)],
        compiler_params=pltpu.CompilerParams(dimension_semantics=("parallel",), vmem_limit_bytes=VMEM_LIMIT_BYTES),
        name="merge_router",
    )(x2, om, od, sgm, sgd, *weights)


def _segment_copies(i, seg_dst_ref, seg_rows_ref, tile_off_ref, global_ref, tile_ref, sem, to_global):
    def body(e, carry):
        k = i * N_EXPERTS + e
        rows = pl.multiple_of(seg_rows_ref[k], SEG_ALIGN)

        @pl.when(rows > 0)
        def _():
            g = global_ref.at[pl.ds(pl.multiple_of(seg_dst_ref[k], SEG_ALIGN), rows)]
            t = tile_ref.at[pl.ds(pl.multiple_of(tile_off_ref[k], SEG_ALIGN), rows)]
            src, dst = (t, g) if to_global else (g, t)
            pltpu.make_async_copy(src, dst, sem).start()

        return carry

    lax.fori_loop(0, N_EXPERTS, body, 0)


def _wait_rows(tile_ref, rows, sem):
    @pl.when(rows > 0)
    def _():
        view = tile_ref.at[pl.ds(0, pl.multiple_of(rows, SEG_ALIGN))]
        pltpu.make_async_copy(view, view, sem).wait()


def _zero_unused_rows(tail_dst_ref, tail_rows_ref, n_used_ref, xs_ref, zero_ref, sem, start):
    n_tiles = xs_ref.shape[0] // EXPERT_ROWS
    if start:
        zero_ref[...] = jnp.zeros(zero_ref.shape, BF16)

    def tail(e, total):
        rows = pl.multiple_of(tail_rows_ref[e], SEG_ALIGN)
        if start:
            @pl.when(rows > 0)
            def _():
                dst = xs_ref.at[pl.ds(pl.multiple_of(tail_dst_ref[e], SEG_ALIGN), rows)]
                pltpu.make_async_copy(zero_ref.at[pl.ds(0, rows)], dst, sem).start()

        return total + rows

    total = lax.fori_loop(0, N_EXPERTS, tail, 0)
    if not start:
        _wait_rows(xs_ref, total + (n_tiles - n_used_ref[0]) * EXPERT_ROWS, sem)
        return

    def unused(t, carry):
        dst = xs_ref.at[pl.ds(pl.multiple_of(t * EXPERT_ROWS, EXPERT_ROWS), EXPERT_ROWS)]
        pltpu.make_async_copy(zero_ref, dst, sem).start()
        return carry

    lax.fori_loop(n_used_ref[0], n_tiles, unused, 0)


def _sort_kernel(seg_dst_ref, seg_rows_ref, tile_off_ref, tile_rows_ref, tail_dst_ref, tail_rows_ref, n_used_ref,
                 h2_ref, route_t_ref, xs_ref, sorted_ref, zero_ref, sem, zero_sem):
    i = pl.program_id(0)
    tm = h2_ref.shape[0]

    @pl.when(i == 0)
    def _():
        _zero_unused_rows(tail_dst_ref, tail_rows_ref, n_used_ref, xs_ref, zero_ref, zero_sem, True)

    d1 = route_t_ref[0, 0:1, :].astype(jnp.int32)
    d2 = route_t_ref[0, 1:2, :].astype(jnp.int32)
    slot = i % 2

    def sort_rows(n_rows):
        r = lax.broadcasted_iota(jnp.int32, (n_rows, tm), 0)
        perm = jnp.where((r == d1) | (r == d2), 1.0, 0.0).astype(BF16)
        sorted_ref[slot, 0:n_rows] = _dot(perm, h2_ref[...]).astype(BF16)

    @pl.when(tile_rows_ref[i] <= SORT_ROWS_COMMON)
    def _():
        sort_rows(SORT_ROWS_COMMON)

    @pl.when(tile_rows_ref[i] > SORT_ROWS_COMMON)
    def _():
        sort_rows(SORT_ROWS)

    _segment_copies(i, seg_dst_ref, seg_rows_ref, tile_off_ref, xs_ref, sorted_ref.at[slot], sem.at[slot], True)

    @pl.when(i > 0)
    def _():
        _wait_rows(sorted_ref.at[1 - slot], tile_rows_ref[jnp.maximum(i - 1, 0)], sem.at[1 - slot])

    @pl.when(i == pl.num_programs(0) - 1)
    def _():
        _wait_rows(sorted_ref.at[slot], tile_rows_ref[i], sem.at[slot])
        _zero_unused_rows(tail_dst_ref, tail_rows_ref, n_used_ref, xs_ref, zero_ref, zero_sem, False)


def _sort_call(h2, route_t, sched, max_rows):
    n, d = h2.shape
    tm = ROUTE_ROWS
    return pl.pallas_call(
        _sort_kernel,
        grid_spec=pltpu.PrefetchScalarGridSpec(
            num_scalar_prefetch=7,
            grid=(n // tm,),
            in_specs=[pl.BlockSpec((tm, d), lambda i, *_: (i, 0)),
                      pl.BlockSpec((1, 8, tm), lambda i, *_: (i, 0, 0))],
            out_specs=pl.BlockSpec(memory_space=pl.ANY),
            scratch_shapes=[pltpu.VMEM((2, SORT_ROWS, d), BF16), pltpu.VMEM((EXPERT_ROWS, d), BF16),
                            pltpu.SemaphoreType.DMA((2,)), pltpu.SemaphoreType.DMA(())],
        ),
        out_shape=jax.ShapeDtypeStruct((max_rows, d), BF16),
        compiler_params=pltpu.CompilerParams(dimension_semantics=("arbitrary",), vmem_limit_bytes=VMEM_LIMIT_BYTES),
        name="moe_sort",
    )(sched["seg_dst"], sched["seg_rows"], sched["tile_off"], sched["tile_rows"], sched["tail_dst"],
      sched["tail_rows"], sched["n_used"], h2, route_t)


def _expert_kernel(tile_expert_ref, n_used_ref, xs_ref, wg_ref, wu_ref, wd_ref, ys_ref, wg_bf, wu_bf, wd_bf):
    t = pl.program_id(0)
    used = t < n_used_ref[0]

    @pl.when(used & ((t == 0) | (tile_expert_ref[t] != tile_expert_ref[jnp.maximum(t - 1, 0)])))
    def _():
        wg_bf[...] = wg_ref[0].astype(BF16)
        wu_bf[...] = wu_ref[0].astype(BF16)
        wd_bf[...] = wd_ref[0].astype(BF16)

    @pl.when(used)
    def _():
        xs = xs_ref[...]
        gate = _dot(xs, wg_bf[...])
        up = _dot(xs, wu_bf[...])
        hidden = (gate * jax.nn.sigmoid(gate) * up).astype(BF16)
        ys_ref[...] = _dot(hidden, wd_bf[...]).astype(BF16)


def _expert_call(xs, wg, wu, wd, sched):
    rows, d = xs.shape
    tr = EXPERT_ROWS
    blk = lambda t, te, nu: (jnp.minimum(t, nu[0] - 1), 0)
    wsel = lambda t, te, nu: (te[jnp.minimum(t, nu[0] - 1)], 0, 0)
    return pl.pallas_call(
        _expert_kernel,
        grid_spec=pltpu.PrefetchScalarGridSpec(
            num_scalar_prefetch=2,
            grid=(rows // tr,),
            in_specs=[pl.BlockSpec((tr, d), blk),
                      pl.BlockSpec((1, d, EXPERT_FF), wsel), pl.BlockSpec((1, d, EXPERT_FF), wsel),
                      pl.BlockSpec((1, EXPERT_FF, d), wsel)],
            out_specs=pl.BlockSpec((tr, d), blk),
            scratch_shapes=[pltpu.VMEM((d, EXPERT_FF), BF16), pltpu.VMEM((d, EXPERT_FF), BF16),
                            pltpu.VMEM((EXPERT_FF, d), BF16)],
        ),
        out_shape=jax.ShapeDtypeStruct((rows, d), BF16),
        input_output_aliases={2: 0},
        compiler_params=pltpu.CompilerParams(dimension_semantics=("arbitrary",), vmem_limit_bytes=VMEM_LIMIT_BYTES),
        name="moe_experts",
    )(sched["tile_expert"], sched["n_used"], xs, wg, wu, wd)


def _combine_kernel(seg_dst_ref, seg_rows_ref, tile_off_ref, tile_rows_ref, ys_ref, route_ref, x1_ref, o_ref,
                    buf_ref, sem):
    i = pl.program_id(0)
    tm = x1_ref.shape[0]
    slot = i % 2

    def fetch(tile, into):
        buf_ref[into] = jnp.zeros(buf_ref.shape[1:], BF16)
        _segment_copies(tile, seg_dst_ref, seg_rows_ref, tile_off_ref, ys_ref, buf_ref.at[into], sem.at[into], False)

    @pl.when(i == 0)
    def _():
        fetch(i, slot)

    @pl.when(i + 1 < pl.num_programs(0))
    def _():
        fetch(i + 1, 1 - slot)

    route = route_ref[...]
    d1 = route[:, 0:1].astype(jnp.int32)
    d2 = route[:, 1:2].astype(jnp.int32)
    w1 = route[:, 2:3]
    w2 = route[:, 3:4]
    _wait_rows(buf_ref.at[slot], tile_rows_ref[i], sem.at[slot])

    def combine_rows(n_rows):
        r = lax.broadcasted_iota(jnp.int32, (tm, n_rows), 1)
        weights = (jnp.where(r == d1, w1, 0.0) + jnp.where(r == d2, w2, 0.0)).astype(BF16)
        o_ref[...] = x1_ref[...] + _dot(weights, buf_ref[slot, 0:n_rows])

    @pl.when(tile_rows_ref[i] <= SORT_ROWS_COMMON)
    def _():
        combine_rows(SORT_ROWS_COMMON)

    @pl.when(tile_rows_ref[i] > SORT_ROWS_COMMON)
    def _():
        combine_rows(SORT_ROWS)


def _combine_call(ys, route, x1, sched):
    n, d = x1.shape
    tm = ROUTE_ROWS
    return pl.pallas_call(
        _combine_kernel,
        grid_spec=pltpu.PrefetchScalarGridSpec(
            num_scalar_prefetch=4,
            grid=(n // tm,),
            in_specs=[pl.BlockSpec(memory_space=pl.ANY),
                      pl.BlockSpec((tm, LANES), lambda i, *_: (i, 0)),
                      pl.BlockSpec((tm, d), lambda i, *_: (i, 0))],
            out_specs=pl.BlockSpec((tm, d), lambda i, *_: (i, 0)),
            scratch_shapes=[pltpu.VMEM((2, SORT_ROWS, d), BF16), pltpu.SemaphoreType.DMA((2,))],
        ),
        out_shape=jax.ShapeDtypeStruct((n, d), F32),
        compiler_params=pltpu.CompilerParams(dimension_semantics=("arbitrary",), vmem_limit_bytes=VMEM_LIMIT_BYTES),
        name="moe_combine",
    )(sched["seg_dst"], sched["seg_rows"], sched["tile_off"], sched["tile_rows"], ys, route, x1)


def _schedule_kernel(cnt_ref, seg_dst_ref, tile_off_ref, tile_rows_ref, misc_ref):
    hp = functools.partial(jnp.dot, preferred_element_type=F32, precision=lax.Precision.HIGHEST)
    cnt = cnt_ref[...]
    n_tiles = cnt.shape[0]
    tile_before = jnp.where(lax.broadcasted_iota(jnp.int32, (n_tiles, n_tiles), 1)
                            < lax.broadcasted_iota(jnp.int32, (n_tiles, n_tiles), 0), 1.0, 0.0)
    expert_before = jnp.where(lax.broadcasted_iota(jnp.int32, (LANES, LANES), 0)
                              < lax.broadcasted_iota(jnp.int32, (LANES, LANES), 1), 1.0, 0.0)
    expert_rows = jnp.sum(cnt, axis=0, keepdims=True)
    region = jnp.floor((expert_rows + (EXPERT_ROWS - 1)) * (1.0 / EXPERT_ROWS)) * EXPERT_ROWS
    region_start = hp(jnp.broadcast_to(region, (8, LANES)), expert_before)[0:1]
    seg_dst_ref[...] = (region_start + hp(tile_before, cnt)).astype(jnp.int32)
    tile_off_ref[...] = hp(cnt, expert_before).astype(jnp.int32)
    tile_rows_ref[...] = jnp.broadcast_to(jnp.sum(cnt, axis=-1, keepdims=True), cnt.shape).astype(jnp.int32)
    n_used = jnp.sum(region, axis=-1, keepdims=True) * (1.0 / EXPERT_ROWS)
    row = lax.broadcasted_iota(jnp.int32, (8, LANES), 0)
    misc = jnp.where(row == 0, region_start + expert_rows,
                     jnp.where(row == 1, region - expert_rows,
                               jnp.where(row == 2, region_start + region, n_used)))
    misc_ref[...] = misc.astype(jnp.int32)


def _moe_schedule(cnt, n_tokens):
    n_tiles = cnt.shape[0]
    table = jax.ShapeDtypeStruct((n_tiles, LANES), jnp.int32)
    seg_dst, tile_off, tile_rows, misc = pl.pallas_call(
        _schedule_kernel,
        out_shape=[table, table, table, jax.ShapeDtypeStruct((8, LANES), jnp.int32)],
        name="moe_schedule",
    )(cnt.reshape(n_tiles, LANES))
    max_rows = 2 * n_tokens + n_tiles * N_EXPERTS * (SEG_ALIGN - 1) + N_EXPERTS * (EXPERT_ROWS - 1)
    max_tiles = -(-max_rows // EXPERT_ROWS)
    tile_start = jnp.arange(max_tiles, dtype=jnp.int32) * EXPERT_ROWS
    region_end = misc[2, :N_EXPERTS]
    tile_expert = jnp.minimum(jnp.sum((region_end[None, :] <= tile_start[:, None]).astype(jnp.int32), axis=1),
                              N_EXPERTS - 1)
    flat = lambda a: a[:, :N_EXPERTS].reshape(-1)
    sched = {
        "seg_dst": flat(seg_dst),
        "seg_rows": flat(cnt.reshape(n_tiles, LANES).astype(jnp.int32)),
        "tile_off": flat(tile_off),
        "tile_rows": tile_rows[:, 0],
        "tail_dst": misc[0, :N_EXPERTS],
        "tail_rows": misc[1, :N_EXPERTS],
        "tile_expert": tile_expert,
        "n_used": misc[3, :1],
    }
    return sched, max_tiles * EXPERT_ROWS


def _rotary_tables(seq_len, rot_dim, period, first, gain, scale):
    half = rot_dim // 2
    pos = jnp.arange(seq_len, dtype=F32)
    inv = 1.0 / (ROPE_THETA ** (jnp.arange(0, rot_dim, 2, dtype=F32) / rot_dim))
    ang = pos[:, None] * inv[None, :]
    cos, sin = jnp.cos(ang), jnp.sin(ang)
    lane = jnp.arange(LANES)
    rel = (lane % period) - first
    active = (rel >= 0) & (rel < rot_dim)
    idx = jnp.clip(rel, 0, rot_dim - 1) % half
    sign = jnp.where(rel < half, -1.0, 1.0)
    partner = jnp.where(active, jnp.where(rel < half, lane + half, lane - half), lane)
    c = jnp.where(active[None, :], cos[:, idx], 1.0)
    s = jnp.where(active[None, :], sin[:, idx] * sign[None, :], 0.0)
    gain = gain.astype(F32)
    return (c * gain[None, :] * scale).astype(F32), (s * gain[partner][None, :] * scale).astype(F32)


def _head_pad(w, heads, width):
    r = w.shape[0]
    w = w.reshape(r, heads, width)
    return jnp.pad(w, ((0, 0), (0, 0), (0, LANES - width))).reshape(r, heads * LANES)


def _layer_params(l, seq_len, norm_mix, w_in, mla_q_latent_norm, w_mla_uq, mla_kv_latent_norm, w_mla_ukv,
                  mla_q_gain, mla_k_gain, diff_q_gain, diff_k_gain, w_mla_up, w_diff_up, w_out, norm_ffn,
                  w_router_group, b_router_group, w_router_expert, b_router_expert):
    d = w_in.shape[1]
    sizes = (MLA_Q_RANK, MLA_KV_RANK, MLA_ROPE, DIFF_QK_WIDTH, DIFF_QK_WIDTH, DIFF_V_WIDTH, d, d)
    offs = [0]
    for s in sizes:
        offs.append(offs[-1] + s)
    wi = w_in[l]
    seg = [wi[:, offs[k]:offs[k + 1]] for k in range(len(sizes))]
    row = lambda g: g.astype(F32)[None, :]
    p = {}
    p["gmix"] = row(norm_mix[l])
    p["wql"] = seg[0].astype(BF16)
    p["wkvl"] = seg[1].astype(BF16)
    p["wkr"] = jnp.pad(seg[2], ((0, 0), (MLA_NOPE, LANES - MLA_QK))).astype(BF16)
    p["wdk"] = seg[4].astype(BF16)
    p["wdqvt"] = jnp.concatenate([seg[3].T, seg[5].T], axis=0).astype(BF16)
    p["wgm"], p["wgd"] = seg[6].astype(BF16), seg[7].astype(BF16)
    p["gql"] = row(mla_q_latent_norm[l])
    p["wuqt"] = _head_pad(w_mla_uq[l], MLA_HEADS, MLA_QK).T.astype(BF16)
    p["gkvl"] = row(mla_kv_latent_norm[l])
    ukv = w_mla_ukv[l].reshape(MLA_KV_RANK, MLA_HEADS, MLA_NOPE + MLA_V)
    p["wuk"] = _head_pad(ukv[:, :, :MLA_NOPE].reshape(MLA_KV_RANK, -1), MLA_HEADS, MLA_NOPE).astype(BF16)
    p["wuvt"] = ukv[:, :, MLA_NOPE:].reshape(MLA_KV_RANK, -1).T.astype(BF16)
    gq = jnp.pad(mla_q_gain[l], (0, LANES - MLA_QK))
    gk = jnp.pad(mla_k_gain[l], (0, LANES - MLA_QK))
    nope = jnp.arange(LANES) < MLA_NOPE
    p["gkn"] = jnp.where(nope, gk, 0.0).astype(F32)[None, :]
    aq, bq = _rotary_tables(seq_len, MLA_ROPE, LANES, MLA_NOPE, gq, LOG2E * MLA_QK ** -0.5)
    p["aq"], p["bq"] = aq.T, bq.T
    ak, bk = _rotary_tables(seq_len, MLA_ROPE, LANES, MLA_NOPE, jnp.where(nope, 0.0, gk), 1.0)
    p["ak"], p["bk"] = ak, bk
    adq, bdq = _rotary_tables(seq_len, DIFF_ROPE, DIFF_HEAD_DIM, 0, jnp.tile(diff_q_gain[l], 2),
                              LOG2E * DIFF_HEAD_DIM ** -0.5)
    p["adq"], p["bdq"] = adq.T, bdq.T
    p["adk"], p["bdk"] = _rotary_tables(seq_len, DIFF_ROPE, DIFF_HEAD_DIM, 0, jnp.tile(diff_k_gain[l], 2), 1.0)
    p["wmu"] = w_mla_up[l].astype(BF16)
    p["wdu"] = w_diff_up[l].astype(BF16)
    p["wout"] = w_out[l].astype(BF16)
    p["gffn"] = row(norm_ffn[l])
    wr = jnp.concatenate([w_router_expert[l], w_router_group[l]], axis=1).astype(F32)
    wrt = jnp.pad(wr, ((0, 0), (0, LANES - wr.shape[1]))).T
    wrt_hi = wrt.astype(BF16)
    p["wrt"] = jnp.concatenate([wrt_hi, (wrt - wrt_hi.astype(F32)).astype(BF16)], axis=0)
    br = jnp.concatenate([b_router_expert[l], b_router_group[l]]).astype(F32)
    p["brt"] = jnp.broadcast_to(jnp.pad(br, (0, LANES - br.shape[0]))[:, None], (LANES, ROUTE_ROWS))
    return p


def kernel(x, norm_mix, w_in, mla_q_latent_norm, w_mla_uq, mla_kv_latent_norm, w_mla_ukv, mla_q_gain, mla_k_gain, diff_q_gain, diff_k_gain, lambda_q1, lambda_k1, lambda_q2, lambda_k2, diff_subln, w_mla_up, w_diff_up, w_out, norm_ffn, w_router_group, b_router_group, w_router_expert, b_router_expert, w_expert_gate, w_expert_up, w_expert_down):
    batch, seq_len, d = x.shape
    x2 = x.reshape(batch * seq_len, d)
    row = lambda g: g.astype(F32)[None, :]
    for l in range(norm_mix.shape[0]):
        lam_init = 0.8 - 0.6 * math.exp(-0.3 * l)
        p = _layer_params(l, seq_len, norm_mix, w_in, mla_q_latent_norm, w_mla_uq, mla_kv_latent_norm, w_mla_ukv,
                          mla_q_gain, mla_k_gain, diff_q_gain, diff_k_gain, w_mla_up, w_diff_up, w_out, norm_ffn,
                          w_router_group, b_router_group, w_router_expert, b_router_expert)
        qmt, km, vtm, qdt, kd, vtd, sgm, sgd = _proj_call(x2, seq_len, p)
        om = _mla_call(qmt, km, vtm, batch, seq_len)
        od = _diff_call(qdt, kd, vtd, row(lambda_q1[l]), row(lambda_k1[l]), row(lambda_q2[l]), row(lambda_k2[l]),
                        diff_subln[l].astype(F32)[:, None], lam_init, batch, seq_len)
        x1, h2, route, route_t, cnt = _merge_call(x2, om, od, sgm, sgd, p)
        sched, max_rows = _moe_schedule(cnt, x2.shape[0])
        xs = _sort_call(h2, route_t, sched, max_rows)
        ys = _expert_call(xs, w_expert_gate[l], w_expert_up[l], w_expert_down[l], sched)
        x2 = _combine_call(ys, route, x1, sched)
    return x2.reshape(batch, seq_len, d)
```
